```python
import math
import jax
import jax.numpy as jnp
from jax import lax
import numpy as np

D_MODEL = 1024
BATCH = 32
SEQ = 256
DEPTH = 2
DEC_BATCH = 4
DEC_SEQ = 1024
PAST_LEN = 512

GRID_W = 64
GDN_HEADS = 4
GDN_DK = 128
GDN_DV = 128
GDN_W = GDN_HEADS * GDN_DV
SHORT_CONV = 5
GDN_CHUNK = 64
GLA_HEADS = 4
GLA_DK = 64
GLA_DV = 128
GLA_KW = GLA_HEADS * GLA_DK
GLA_VW = GLA_HEADS * GLA_DV
GLA_RANK = 16
GLA_TAU = 16.0
GLA_CHUNK = 64
DIFF_HEADS = 4
DIFF_DH = 64
DIFF_VD = 2 * DIFF_DH
DIFF_QW = DIFF_HEADS * 2 * DIFF_DH
DIFF_VW = DIFF_HEADS * DIFF_VD
Q_BLOCK = 128
ROPE_THETA = 10000.0
MIX_W = 512
N_BRANCH = 3
N_EXPERTS = 16
N_GROUPS = 4
GROUP_SIZE = N_EXPERTS // N_GROUPS
TOP_K = 2
D_FF = 512
ALPHA = (2 * DEPTH) ** 0.25
BETA_INIT = (8 * DEPTH) ** -0.25
EPS = 1e-6
IN_SIZES = (3 * GDN_W, GDN_W, 2 * GDN_HEADS, 2 * GDN_HEADS,
            GLA_KW, GLA_KW, GLA_VW, GLA_VW, 2 * GLA_RANK,
            DIFF_QW, DIFF_QW, DIFF_VW, N_BRANCH * D_MODEL)
IN_SPLITS = tuple(sum(IN_SIZES[:i + 1]) for i in range(len(IN_SIZES) - 1))
D_IN = sum(IN_SIZES)

kernel_name = 'hybrid_diffusion_gdn_gla_diffattn_moe_step'


def _layernorm(x, g, b):
    xf = x.astype(jnp.float32)
    mu = jnp.mean(xf, -1, keepdims=True)
    var = jnp.mean(jnp.square(xf - mu), -1, keepdims=True)
    return ((xf - mu) * lax.rsqrt(var + EPS) * g + b).astype(x.dtype)


def _rmsnorm(x, g):
    xf = x.astype(jnp.float32)
    return xf * lax.rsqrt(jnp.mean(xf * xf, -1, keepdims=True) + EPS) * g.astype(jnp.float32)


def _l2norm(x):
    return x * lax.rsqrt(jnp.sum(x * x, -1, keepdims=True) + EPS)


def _flip(a):
    return jnp.flip(a, axis=2)


def _short_conv(x, w):
    K, C = w.shape
    return lax.conv_general_dilated(x, w[:, None, :], window_strides=(1,), padding=[(K // 2, K // 2)],
                                    dimension_numbers=('NWC', 'WIO', 'NWC'), feature_group_count=C)


def _gated_delta_chunked(q, k, v, beta, g, s0):
    B, H, T, dk = q.shape
    dv = v.shape[-1]
    C = GDN_CHUNK
    n = T // C
    q = q.reshape(B, H, n, C, dk)
    k = k.reshape(B, H, n, C, dk)
    v = v.reshape(B, H, n, C, dv)
    beta = beta.reshape(B, H, n, C)
    gc = jnp.cumsum(g.reshape(B, H, n, C), axis=-1)
    incl = jnp.tril(jnp.ones((C, C), dtype=bool))
    strict = jnp.tril(jnp.ones((C, C), dtype=bool), k=-1)
    rel = gc[..., :, None] - gc[..., None, :]
    decay = jnp.where(incl, jnp.exp(jnp.where(incl, rel, 0.0)), 0.0)
    k_beta = k * beta[..., None]
    lower = jnp.where(strict, jnp.einsum('bhnid,bhnjd->bhnij', k_beta, k) * decay, 0.0)
    rhs = jnp.concatenate([v * beta[..., None], k_beta * jnp.exp(gc)[..., None]], axis=-1)
    sol = lax.linalg.triangular_solve(lower + jnp.eye(C, dtype=q.dtype), rhs, left_side=True, lower=True)
    u, w = sol[..., :dv], sol[..., dv:]
    a_intra = jnp.where(incl, jnp.einsum('bhnid,bhnjd->bhnij', q, k) * decay, 0.0)
    q_dec = q * jnp.exp(gc)[..., None]
    k_tail = k * jnp.exp(gc[..., -1:] - gc)[..., None]
    g_tail = jnp.exp(gc[..., -1])

    def step(S, xs):
        u_c, w_c, a_c, qd_c, kt_c, gt_c = xs
        v_new = u_c - jnp.einsum('bhcd,bhde->bhce', w_c, S)
        o = jnp.einsum('bhcd,bhde->bhce', qd_c, S) + jnp.einsum('bhij,bhje->bhie', a_c, v_new)
        S = S * gt_c[..., None, None] + jnp.einsum('bhcd,bhce->bhde', kt_c, v_new)
        return S, o

    xs = tuple(jnp.moveaxis(a, 2, 0) for a in (u, w, a_intra, q_dec, k_tail, g_tail))
    s_fin, o = lax.scan(step, s0, xs)
    return jnp.moveaxis(o, 0, 2).reshape(B, H, T, dv), s_fin


def _gla_chunked(q, k, v, glog, s0):
    B, H, T, dk = q.shape
    dv = v.shape[-1]
    C = GLA_CHUNK
    n = T // C
    q = q.reshape(B, H, n, C, dk)
    k = k.reshape(B, H, n, C, dk)
    v = v.reshape(B, H, n, C, dv)
    b = jnp.cumsum(glog.reshape(B, H, n, C, dk), axis=3)
    b_ref = b[..., C // 2:C // 2 + 1, :]
    incl = jnp.tril(jnp.ones((C, C), dtype=bool))
    a_intra = jnp.where(incl, jnp.einsum('bhnid,bhnjd->bhnij', q * jnp.exp(b - b_ref), k * jnp.exp(b_ref - b)), 0.0)
    q_dec = q * jnp.exp(b)
    k_tail = k * jnp.exp(b[..., -1:, :] - b)
    g_tail = jnp.exp(b[..., -1, :])

    def step(S, xs):
        v_c, a_c, qd_c, kt_c, gt_c = xs
        o = jnp.einsum('bhcd,bhde->bhce', qd_c, S) + jnp.einsum('bhij,bhje->bhie', a_c, v_c)
        S = S * gt_c[..., :, None] + jnp.einsum('bhcd,bhce->bhde', kt_c, v_c)
        return S, o

    xs = tuple(jnp.moveaxis(a, 2, 0) for a in (v, a_intra, q_dec, k_tail, g_tail))
    s_fin, o = lax.scan(step, s0, xs)
    return jnp.moveaxis(o, 0, 2).reshape(B, H, T, dv), s_fin


def _axial_rope(x):
    n_tok = x.shape[1]
    rows = n_tok // GRID_W
    row_id = jnp.repeat(jnp.arange(rows), GRID_W).astype(jnp.float32)
    col_id = jnp.tile(jnp.arange(GRID_W), rows).astype(jnp.float32)
    half = DIFF_DH // 2
    inv = ROPE_THETA ** (-jnp.arange(0, half, 2, dtype=jnp.float32) / half)

    def rot(part, pos):
        ang = pos[:, None] * inv[None, :]
        cos = jnp.cos(ang)[None, :, None, None, :]
        sin = jnp.sin(ang)[None, :, None, None, :]
        p1, p2 = jnp.split(part, 2, axis=-1)
        return jnp.concatenate([p1 * cos - p2 * sin, p1 * sin + p2 * cos], axis=-1)

    xr, xc = jnp.split(x.astype(jnp.float32), 2, axis=-1)
    return jnp.concatenate([rot(xr, row_id), rot(xc, col_id)], axis=-1).astype(x.dtype)


def _diff_attention(q, k, v, lam):
    B, Tq, H = q.shape[:3]
    n_blk = Tq // Q_BLOCK
    q_blocks = jnp.moveaxis(q.astype(jnp.float32).reshape(B, n_blk, Q_BLOCK, H, 2, DIFF_DH), 1, 0)
    kf = k.astype(jnp.float32)
    vf = v.astype(jnp.float32)

    def block(qb):
        s = jnp.einsum('bqhsd,bkhsd->bshqk', qb, kf) * DIFF_DH ** -0.5
        p = jax.nn.softmax(s, axis=-1)
        w = p[:, 0] - lam * p[:, 1]
        return jnp.einsum('bhqk,bkhe->bqhe', w, vf)

    o = lax.map(block, q_blocks)
    return jnp.moveaxis(o, 0, 1).reshape(B, Tq, H, DIFF_VD)


def _mixer(u, w_in, conv_w, a_log, dt_bias, gdn_g, gla_wg, gla_bg, gla_g, lam_p, diff_g, w_branch, w_out,
           lam_init, gdn_s0, gla_s0, ctx_k, ctx_v):
    B, T, _ = u.shape
    f32 = jnp.float32
    (gdn_qkv, gdn_z, gdn_b, gdn_a, gla_q, gla_k, gla_v, gla_r, gla_lr,
     diff_q, diff_k, diff_v, merge_logit) = jnp.split(jnp.einsum('btd,dn->btn', u, w_in), IN_SPLITS, axis=-1)

    def heads(a, d):
        return a.reshape(B, T, -1, d).transpose(0, 2, 1, 3)

    qkv = jax.nn.silu(_short_conv(gdn_qkv, conv_w)).astype(f32)
    q, k, v = jnp.split(qkv, 3, axis=-1)
    q = _l2norm(heads(q, GDN_DK)) * GDN_DK ** -0.5
    k = _l2norm(heads(k, GDN_DK))
    v = heads(v, GDN_DV)
    beta = jax.nn.sigmoid(gdn_b.astype(f32)).reshape(B, T, 2, GDN_HEADS).transpose(2, 0, 3, 1)
    g = -jnp.exp(a_log.astype(f32))[:, None, :, None] * jax.nn.softplus(
        gdn_a.astype(f32).reshape(B, T, 2, GDN_HEADS).transpose(2, 0, 3, 1) + dt_bias.astype(f32)[:, None, :, None])
    s0 = gdn_s0.astype(f32)
    o_f, s_f = _gated_delta_chunked(q, k, v, beta[0], g[0], s0[:, 0])
    o_b, s_b = _gated_delta_chunked(_flip(q), _flip(k), _flip(v), _flip(beta[1]), _flip(g[1]), s0[:, 1])
    o_gdn = (o_f + _flip(o_b)).transpose(0, 2, 1, 3)
    o_gdn = (_rmsnorm(o_gdn, gdn_g) * jax.nn.silu(gdn_z.astype(f32)).reshape(B, T, GDN_HEADS, GDN_DV)).reshape(B, T, MIX_W)
    gdn_fin = jnp.stack([s_f, s_b], axis=1)

    q = heads(gla_q.astype(f32), GLA_DK) * GLA_DK ** -0.5
    k = heads(gla_k.astype(f32), GLA_DK)
    v = heads(gla_v.astype(f32), GLA_DV)
    lr = gla_lr.astype(f32).reshape(B, T, 2, GLA_RANK)
    logits = jnp.einsum('btsr,srk->sbtk', lr, gla_wg.astype(f32)) + gla_bg.astype(f32)[:, None, None, :]
    glog = (jax.nn.log_sigmoid(logits) / GLA_TAU).reshape(2, B, T, GLA_HEADS, GLA_DK).transpose(0, 1, 3, 2, 4)
    s0 = gla_s0.astype(f32)
    o_f, s_f = _gla_chunked(q, k, v, glog[0], s0[:, 0])
    o_b, s_b = _gla_chunked(_flip(q), _flip(k), _flip(v), _flip(glog[1]), s0[:, 1])
    o_gla = (o_f + _flip(o_b)).transpose(0, 2, 1, 3)
    o_gla = (_rmsnorm(o_gla, gla_g) * jax.nn.silu(gla_r.astype(f32)).reshape(B, T, GLA_HEADS, GLA_DV)).reshape(B, T, MIX_W)
    gla_fin = jnp.stack([s_f, s_b], axis=1)

    q = diff_q.reshape(B, T, DIFF_HEADS, 2, DIFF_DH)
    k = diff_k.reshape(B, T, DIFF_HEADS, 2, DIFF_DH)
    v = diff_v.reshape(B, T, DIFF_HEADS, DIFF_VD)
    k_own = k.reshape(B, T, DIFF_HEADS, 2 * DIFF_DH)
    v_own = v
    if ctx_k is not None:
        q = _axial_rope(q)
        k = jnp.concatenate([_axial_rope(k), ctx_k.reshape(B, -1, DIFF_HEADS, 2, DIFF_DH).astype(k.dtype)], axis=1)
        v = jnp.concatenate([v, ctx_v.astype(v.dtype)], axis=1)
    lp = lam_p.astype(f32)
    lam = jnp.exp(jnp.sum(lp[0] * lp[1])) - jnp.exp(jnp.sum(lp[2] * lp[3])) + lam_init
    o_diff = _diff_attention(q, k, v, lam)
    o_diff = (_rmsnorm(o_diff, diff_g) * (1.0 - lam_init)).reshape(B, T, MIX_W)

    branches = jnp.stack([o_gdn, o_gla, o_diff], axis=2).astype(u.dtype)
    proj = jnp.einsum('btsm,smd->btsd', branches, w_branch)
    gates = jax.nn.sigmoid(merge_logit.reshape(B, T, N_BRANCH, D_MODEL))
    out = jnp.einsum('btd,de->bte', jnp.sum(gates * proj, axis=2), w_out)
    return out, gdn_fin, gla_fin, k_own, v_own


def _moe(h, router_w, router_b, w_gate, w_up, w_down):
    B, T, D = h.shape
    tok = h.reshape(B * T, D)
    scores = jax.nn.sigmoid(jnp.dot(tok, router_w).astype(jnp.float32))
    biased = scores + router_b.astype(jnp.float32)
    grp_score = jnp.sum(lax.top_k(biased.reshape(-1, N_GROUPS, GROUP_SIZE), TOP_K)[0], axis=-1)
    best = jnp.argmax(grp_score, axis=-1)
    in_grp = (jnp.arange(N_EXPERTS) // GROUP_SIZE)[None, :] == best[:, None]
    _, idx = lax.top_k(jnp.where(in_grp, biased, -jnp.inf), TOP_K)
    w = jnp.take_along_axis(scores, idx, axis=-1)
    w = w / jnp.sum(w, axis=-1, keepdims=True)
    gates = jnp.sum(jax.nn.one_hot(idx, N_EXPERTS, dtype=jnp.float32) * w[..., None], axis=1)
    act = jax.nn.silu(jnp.einsum('nd,edf->nef', tok, w_gate)) * jnp.einsum('nd,edf->nef', tok, w_up)
    out = jnp.einsum('nef,efd->nd', act * gates[..., None].astype(act.dtype), w_down)
    return out.reshape(B, T, D).astype(h.dtype)


def setup_inputs(seed: int = 0) -> dict:
    key = jax.random.key(seed)
    ks = jax.random.split(key, 32)
    f32 = jnp.float32
    D = D_MODEL

    def nrm(k, shape, s):
        return jax.random.normal(k, shape, f32) * s

    dt = jnp.exp(jax.random.uniform(ks[13], (DEPTH, 2, GDN_HEADS), f32, math.log(1e-3), math.log(1e-1)))
    return {
        'x_prompt': nrm(ks[0], (BATCH, SEQ, D), 1.0),
        'x_sample': nrm(ks[1], (DEC_BATCH, DEC_SEQ, D), 1.0),
        'c': nrm(ks[2], (DEC_BATCH, D), 1.0),
        'state_gdn': nrm(ks[3], (DEC_BATCH, DEPTH, 2, GDN_HEADS, GDN_DK, GDN_DV), 0.1),
        'state_gla': nrm(ks[4], (DEC_BATCH, DEPTH, 2, GLA_HEADS, GLA_DK, GLA_DV), 0.5),
        'cache_k': nrm(ks[5], (DEC_BATCH, DEPTH, PAST_LEN, DIFF_HEADS, 2 * DIFF_DH), 1.0),
        'cache_v': nrm(ks[6], (DEC_BATCH, DEPTH, PAST_LEN, DIFF_HEADS, DIFF_VD), 1.0),
        'c_ctx': nrm(ks[7], (D,), 1.0),
        'w_mod': nrm(ks[8], (DEPTH, D, 6 * D), 0.5 * D ** -0.5),
        'b_mod': nrm(ks[9], (DEPTH, 6 * D), 0.02),
        'w_in': nrm(ks[10], (DEPTH, D, D_IN), D ** -0.5),
        'gdn_conv': nrm(ks[11], (DEPTH, SHORT_CONV, 3 * GDN_W), SHORT_CONV ** -0.5),
        'gdn_a_log': jnp.log(jax.random.uniform(ks[12], (DEPTH, 2, GDN_HEADS), f32, 1.0, 16.0)),
        'gdn_dt_bias': dt + jnp.log(-jnp.expm1(-dt)),
        'gdn_norm': 1.0 + nrm(ks[14], (DEPTH, GDN_DV), 0.02),
        'gla_w_gate': nrm(ks[15], (DEPTH, 2, GLA_RANK, GLA_KW), GLA_RANK ** -0.5),
        'gla_b_gate': nrm(ks[16], (DEPTH, 2, GLA_KW), 0.1),
        'gla_norm': 1.0 + nrm(ks[17], (DEPTH, GLA_DV), 0.02),
        'diff_lambda': nrm(ks[18], (DEPTH, 4, DIFF_DH), 0.1),
        'diff_norm': 1.0 + nrm(ks[19], (DEPTH, DIFF_VD), 0.02),
        'w_branch': nrm(ks[20], (DEPTH, N_BRANCH, MIX_W, D), MIX_W ** -0.5),
        'w_out': nrm(ks[21], (DEPTH, D, D), BETA_INIT * D ** -0.5),
        'ln_g': 1.0 + nrm(ks[22], (DEPTH, 2, D), 0.02),
        'ln_b': nrm(ks[23], (DEPTH, 2, D), 0.02),
        'router_w': nrm(ks[24], (D, N_EXPERTS), D ** -0.5),
        'router_b': nrm(ks[25], (N_EXPERTS,), 0.01),
        'exp_w_gate': nrm(ks[26], (DEPTH, N_EXPERTS, D, D_FF), D ** -0.5),
        'exp_w_up': nrm(ks[27], (DEPTH, N_EXPERTS, D, D_FF), D ** -0.5),
        'exp_w_down': nrm(ks[28], (DEPTH, N_EXPERTS, D_FF, D), BETA_INIT * D_FF ** -0.5),
    }


def reference(x_prompt, x_sample, c, state_gdn, state_gla, cache_k, cache_v, c_ctx, w_mod, b_mod, w_in,
              gdn_conv, gdn_a_log, gdn_dt_bias, gdn_norm, gla_w_gate, gla_b_gate, gla_norm, diff_lambda,
              diff_norm, w_branch, w_out, ln_g, ln_b, router_w, router_b, exp_w_gate, exp_w_up, exp_w_down):
    def layer(x, mod, l, gdn_s0, gla_s0, ctx_k, ctx_v):
        shift1, scale1, gate1, shift2, scale2, gate2 = jnp.split(mod.astype(x.dtype), 6, axis=-1)
        u = x * (1 + scale1) + shift1
        lam_init = 0.8 - 0.6 * math.exp(-0.3 * l)
        m, gdn_fin, gla_fin, k_own, v_own = _mixer(
            u, w_in[l], gdn_conv[l], gdn_a_log[l], gdn_dt_bias[l], gdn_norm[l], gla_w_gate[l], gla_b_gate[l],
            gla_norm[l], diff_lambda[l], diff_norm[l], w_branch[l], w_out[l], lam_init, gdn_s0, gla_s0, ctx_k, ctx_v)
        x = _layernorm(ALPHA * x + gate1 * m.astype(x.dtype), ln_g[l, 0], ln_b[l, 0])
        h = x * (1 + scale2) + shift2
        f = _moe(h, router_w, router_b, exp_w_gate[l], exp_w_up[l], exp_w_down[l])
        x = _layernorm(ALPHA * x + gate2 * f, ln_g[l, 1], ln_b[l, 1])
        return x, gdn_fin, gla_fin, k_own, v_own

    bp = x_prompt.shape[0]
    gdn_zero = jnp.zeros((bp, 2, GDN_HEADS, GDN_DK, GDN_DV), jnp.float32)
    gla_zero = jnp.zeros((bp, 2, GLA_HEADS, GLA_DK, GLA_DV), jnp.float32)
    hp = x_prompt
    gdn_states, gla_states, ks, vs = [], [], [], []
    for l in range(DEPTH):
        mod = (jax.nn.silu(c_ctx) @ w_mod[l] + b_mod[l])[None, None, :]
        hp, s_gdn, s_gla, k_l, v_l = layer(hp, mod, l, gdn_zero, gla_zero, None, None)
        gdn_states.append(s_gdn)
        gla_states.append(s_gla)
        ks.append(k_l)
        vs.append(v_l)
    y_prompt = hp

    hs = x_sample
    for l in range(DEPTH):
        mod = (jax.nn.silu(c) @ w_mod[l] + b_mod[l])[:, None, :]
        hs = layer(hs, mod, l, state_gdn[:, l], state_gla[:, l], cache_k[:, l], cache_v[:, l])[0]
    y_sample = hs

    return (y_prompt, y_sample, jnp.stack(gdn_states, axis=1), jnp.stack(gla_states, axis=1),
            jnp.stack(ks, axis=1), jnp.stack(vs, axis=1))
```

```python
import functools
import math

import numpy as np
import jax
import jax.numpy as jnp
from jax import lax
from jax.experimental import pallas as pl
from jax.experimental.pallas import tpu as pltpu

F32 = jnp.float32
BF16 = jnp.bfloat16

D_MODEL = 1024
DEPTH = 2
GRID_W = 64
GDN_HEADS = 4
GDN_DK = 128
GDN_DV = 128
GDN_W = GDN_HEADS * GDN_DV
SHORT_CONV = 5
GLA_HEADS = 4
GLA_DK = 64
GLA_DV = 128
GLA_KW = GLA_HEADS * GLA_DK
GLA_VW = GLA_HEADS * GLA_DV
GLA_RANK = 16
GLA_TAU = 16.0
CHUNK = 64
DIFF_HEADS = 4
DIFF_DH = 64
DIFF_VD = 2 * DIFF_DH
DIFF_QW = DIFF_HEADS * 2 * DIFF_DH
DIFF_VW = DIFF_HEADS * DIFF_VD
ROPE_THETA = 10000.0
MIX_W = 512
N_BRANCH = 3
N_EXPERTS = 16
N_GROUPS = 4
GROUP_SIZE = N_EXPERTS // N_GROUPS
D_FF = 512
ALPHA = (2 * DEPTH) ** 0.25
EPS = 1e-6

LANE = 128
MOD_ROWS = 8
VMEM_LIMIT = 48 * 1024 * 1024

OFF_GDN_QKV = 0
OFF_GDN_Z = 1536
OFF_GLA_Q = 2048
OFF_GLA_K = 2304
OFF_GLA_V = 2560
OFF_GLA_R = 3072
OFF_DIFF_Q = 3584
OFF_DIFF_K = 4096
OFF_DIFF_V = 4608
OFF_MERGE = 5120
OFF_MISC = 8192
P_COLS = 8320
MISC_B = 0
MISC_A = 8
MISC_LR = 16


def _cparams(sem):
    return pltpu.CompilerParams(dimension_semantics=sem, vmem_limit_bytes=VMEM_LIMIT)


def _mm(a, b):
    return jnp.dot(a.astype(BF16), b.astype(BF16), preferred_element_type=F32)


def _mm_nt(a, b):
    return lax.dot_general(a.astype(BF16), b.astype(BF16), (((1,), (1,)), ((), ())),
                           preferred_element_type=F32)


def _split3(x):
    hi = x.astype(BF16)
    r = x - hi.astype(F32)
    mid = r.astype(BF16)
    lo = (r - mid.astype(F32)).astype(BF16)
    return hi, mid, lo


def _mm_mask_lhs(mask_bf16, x):
    hi, mid, lo = _split3(x)
    dot = functools.partial(jnp.dot, preferred_element_type=F32)
    return dot(mask_bf16, hi) + dot(mask_bf16, mid) + dot(mask_bf16, lo)


def _mm_hi(a, b):
    ah = a.astype(BF16)
    al = (a - ah.astype(F32)).astype(BF16)
    bh = b.astype(BF16)
    bl = (b - bh.astype(F32)).astype(BF16)
    dot = functools.partial(jnp.dot, preferred_element_type=F32)
    return dot(ah, bh) + dot(ah, bl) + dot(al, bh)


def _mm_hi_nt(a, b):
    ah = a.astype(BF16)
    al = (a - ah.astype(F32)).astype(BF16)
    bh = b.astype(BF16)
    bl = (b - bh.astype(F32)).astype(BF16)
    dot = lambda x, y: lax.dot_general(x, y, (((1,), (1,)), ((), ())), preferred_element_type=F32)
    return dot(ah, bh) + dot(ah, bl) + dot(al, bh)


_mm_tri = _mm_hi


def _sigmoid(x):
    return 1.0 / (1.0 + jnp.exp(-x))


def _silu(x):
    return x * _sigmoid(x)


def _softplus(x):
    return jnp.maximum(x, 0.0) + jnp.log1p(jnp.exp(-jnp.abs(x)))


def _rms(x, g):
    return x * lax.rsqrt(jnp.mean(x * x, axis=-1, keepdims=True) + EPS) * g


def _layernorm(y, g, b):
    mu = jnp.mean(y, axis=-1, keepdims=True)
    d = y - mu
    var = jnp.mean(d * d, axis=-1, keepdims=True)
    return d * lax.rsqrt(var + EPS) * g + b


def _pick_lane(x, idx):
    lane = lax.broadcasted_iota(jnp.int32, x.shape, 1)
    return jnp.sum(jnp.where(lane == idx, x, 0.0), axis=1, keepdims=True)


def _tri(n, kind):
    r = lax.broadcasted_iota(jnp.int32, (n, n), 0)
    c = lax.broadcasted_iota(jnp.int32, (n, n), 1)
    return {"ge": r >= c, "gt": r > c, "le": r <= c, "lt": r < c}[kind]


def _mod_kernel(c_ref, w_ref, b_ref, o_ref):
    s = _silu(c_ref[...])
    o_ref[...] = _mm_hi(s, w_ref[...]) + b_ref[...]


def _modulation(cvecs, w_mod, b_mod):
    tn = 1536
    return pl.pallas_call(
        _mod_kernel,
        grid=(DEPTH, 6 * D_MODEL // tn),
        in_specs=[
            pl.BlockSpec((MOD_ROWS, D_MODEL), lambda l, j: (0, 0)),
            pl.BlockSpec((None, D_MODEL, tn), lambda l, j: (l, 0, j)),
            pl.BlockSpec((None, 1, tn), lambda l, j: (l, 0, j)),
        ],
        out_specs=pl.BlockSpec((None, MOD_ROWS, tn), lambda l, j: (l, 0, j)),
        out_shape=jax.ShapeDtypeStruct((DEPTH, MOD_ROWS, 6 * D_MODEL), F32),
        compiler_params=_cparams(("arbitrary", "arbitrary")),
    )(cvecs, w_mod, b_mod.reshape(DEPTH, 1, 6 * D_MODEL))


def _inproj_kernel(x_ref, mod_ref, w_ref, p_ref, u_sc):
    @pl.when(pl.program_id(1) == 0)
    def _():
        shift = mod_ref[:, 0:D_MODEL]
        scale = mod_ref[:, D_MODEL:2 * D_MODEL]
        u_sc[...] = (x_ref[...] * (1.0 + scale) + shift).astype(BF16)

    p_ref[...] = jnp.dot(u_sc[...], w_ref[...], preferred_element_type=F32)


def _inproj(x, mod, w, layer, row0, tokens_per_row, tm):
    n = x.shape[0]
    tn = 1664
    rows_per_block = lambda i: row0 + (i * tm) // tokens_per_row
    return pl.pallas_call(
        _inproj_kernel,
        grid=(n // tm, P_COLS // tn),
        in_specs=[
            pl.BlockSpec((tm, D_MODEL), lambda i, j: (i, 0)),
            pl.BlockSpec((None, None, 1, 6 * D_MODEL), lambda i, j: (layer, rows_per_block(i), 0, 0)),
            pl.BlockSpec((D_MODEL, tn), lambda i, j: (0, j)),
        ],
        out_specs=pl.BlockSpec((tm, tn), lambda i, j: (i, j)),
        out_shape=jax.ShapeDtypeStruct((n, P_COLS), F32),
        scratch_shapes=[pltpu.VMEM((tm, D_MODEL), BF16)],
        compiler_params=_cparams(("arbitrary", "arbitrary")),
    )(x, mod, w)


def _short_conv(x, w, t_len):
    row = lax.broadcasted_iota(jnp.int32, x.shape, 0)
    half = SHORT_CONV // 2
    acc = x * w[half:half + 1, :]
    for d in range(-half, half + 1):
        if d == 0:
            continue
        shifted = pltpu.roll(x, (-d) % t_len, axis=0)
        valid = jnp.logical_and(row + d >= 0, row + d < t_len)
        acc = acc + jnp.where(valid, shifted, 0.0) * w[half + d:half + d + 1, :]
    return acc


def _unit_lower_inverse(a):
    n = a.shape[0]
    eye = jnp.where(_tri(n, "ge") & _tri(n, "le"), 1.0, 0.0)
    p = -a
    inv = eye + p
    steps = int(math.log2(n)) - 1
    for _ in range(steps):
        p = _mm_tri(p, p)
        inv = inv + _mm_tri(inv, p)
    return inv


def _gdn_kernel(*refs, t_len, has_state, emit_state):
    it = iter(refs)
    q_ref, k_ref, v_ref, z_ref, misc_ref = (next(it) for _ in range(5))
    cwq_ref, cwk_ref, cwv_ref = (next(it) for _ in range(3))
    alog_ref, dtb_ref, g_ref = (next(it) for _ in range(3))
    s0_ref = next(it) if has_state else None
    o_ref = next(it)
    sfin_ref = next(it) if emit_state else None
    qs, ks, vs, kts, bet, gat, o_f, o_b, s_f, s_b = (next(it) for _ in range(10))

    h = pl.program_id(1)
    n_chunks = t_len // CHUNK
    c = CHUNK

    q = _silu(_short_conv(q_ref[...], cwq_ref[...], t_len))
    k = _silu(_short_conv(k_ref[...], cwk_ref[...], t_len))
    v = _silu(_short_conv(v_ref[...], cwv_ref[...], t_len))
    q = q * lax.rsqrt(jnp.sum(q * q, axis=-1, keepdims=True) + EPS) * (GDN_DK ** -0.5)
    k = k * lax.rsqrt(jnp.sum(k * k, axis=-1, keepdims=True) + EPS)
    qs[...] = q
    ks[...] = k
    vs[...] = v
    k_t = k.T
    for ci in range(n_chunks):
        kts[ci] = k_t[:, ci * c:(ci + 1) * c]
    misc = misc_ref[...]
    bet[...] = _sigmoid(misc)
    gat[...] = -jnp.exp(alog_ref[...]) * _softplus(misc + dtb_ref[...])
    if has_state:
        s_f[...] = s0_ref[0]
        s_b[...] = s0_ref[1]
    else:
        s_f[...] = jnp.zeros_like(s_f)
        s_b[...] = jnp.zeros_like(s_b)

    ones = jnp.ones((c, c), BF16)
    incl_lo = _tri(c, "ge")
    incl_up = _tri(c, "le")
    lo_bf = jnp.where(incl_lo, 1.0, 0.0).astype(BF16)
    up_bf = jnp.where(incl_up, 1.0, 0.0).astype(BF16)

    def one_direction(ci, backward, s_ref, out_ref, kk, qk, q_c, k_c, v_c, kt_c):
        r0 = pl.multiple_of(ci * c, c)
        g_blk = gat[pl.ds(r0, c), :]
        b_blk = bet[pl.ds(r0, c), :]
        if backward:
            cum_mask, row_mask, incl, strict, last = up_bf, lo_bf, incl_up, _tri(c, "lt"), 0
            beta = _pick_lane(b_blk, MISC_B + GDN_HEADS + h)
            lane_g = MISC_A + GDN_HEADS + h
        else:
            cum_mask, row_mask, incl, strict, last = lo_bf, up_bf, incl_lo, _tri(c, "gt"), c - 1
            beta = _pick_lane(b_blk, MISC_B + h)
            lane_g = MISC_A + h
        g_col = _pick_lane(g_blk, lane_g)
        gc = _pick_lane(_mm_mask_lhs(cum_mask, g_blk), lane_g)
        gc_row = _mm_mask_lhs(ones, g_col * row_mask.astype(F32))
        rel = gc - gc_row
        decay = jnp.where(incl, jnp.exp(jnp.where(incl, rel, 0.0)), 0.0)
        a = jnp.where(strict, beta * kk * decay, 0.0)
        inv = _unit_lower_inverse(a)
        e_gc = jnp.exp(gc)
        rhs = jnp.concatenate([v_c * beta, k_c * (beta * e_gc)], axis=1)
        sol = _mm_tri(inv, rhs)
        u = sol[:, :GDN_DV]
        w = sol[:, GDN_DV:]
        a_intra = jnp.where(incl, qk * decay, 0.0)
        s = s_ref[...]
        v_new = u - _mm(w, s)
        out_ref[pl.ds(r0, c), :] = _mm(q_c * e_gc, s) + _mm(a_intra, v_new)
        gc_last = gc[last:last + 1, :]
        s_ref[...] = s * jnp.exp(gc_last) + _mm(kt_c, v_new * jnp.exp(gc_last - gc))

    def chunk_inputs(ci):
        r0 = pl.multiple_of(ci * c, c)
        q_c = qs[pl.ds(r0, c), :]
        k_c = ks[pl.ds(r0, c), :]
        v_c = vs[pl.ds(r0, c), :]
        kt_c = kts[ci]
        return _mm(k_c, kt_c), _mm(q_c, kt_c), q_c, k_c, v_c, kt_c

    def body(i, carry):
        one_direction(i, False, s_f, o_f, *chunk_inputs(i))
        j = n_chunks - 1 - i
        one_direction(j, True, s_b, o_b, *chunk_inputs(j))
        return carry

    lax.fori_loop(0, n_chunks, body, 0)

    o = o_f[...] + o_b[...]
    o_ref[...] = _rms(o, g_ref[...]) * _silu(z_ref[...])
    if emit_state:
        sfin_ref[0] = s_f[...]
        sfin_ref[1] = s_b[...]


def _gdn(p3, conv_w, alog_row, dtb_row, norm_g, s0, layer, emit_state):
    b, t, _ = p3.shape
    cb = lambda off, width: off // width
    col = lambda base: (lambda bi, h: (bi, 0, base + h))
    in_specs = [
        pl.BlockSpec((None, t, LANE), col(cb(OFF_GDN_QKV, LANE))),
        pl.BlockSpec((None, t, LANE), col(cb(OFF_GDN_QKV + GDN_W, LANE))),
        pl.BlockSpec((None, t, LANE), col(cb(OFF_GDN_QKV + 2 * GDN_W, LANE))),
        pl.BlockSpec((None, t, LANE), col(cb(OFF_GDN_Z, LANE))),
        pl.BlockSpec((None, t, LANE), lambda bi, h: (bi, 0, cb(OFF_MISC, LANE))),
        pl.BlockSpec((SHORT_CONV, LANE), lambda bi, h: (0, h)),
        pl.BlockSpec((SHORT_CONV, LANE), lambda bi, h: (0, GDN_HEADS + h)),
        pl.BlockSpec((SHORT_CONV, LANE), lambda bi, h: (0, 2 * GDN_HEADS + h)),
        pl.BlockSpec((1, LANE), lambda bi, h: (0, 0)),
        pl.BlockSpec((1, LANE), lambda bi, h: (0, 0)),
        pl.BlockSpec((1, GDN_DV), lambda bi, h: (0, 0)),
    ]
    args = [p3, p3, p3, p3, p3, conv_w, conv_w, conv_w, alog_row, dtb_row, norm_g]
    if s0 is not None:
        in_specs.append(pl.BlockSpec((None, None, 2, None, GDN_DK, GDN_DV),
                                     lambda bi, h: (bi, layer, 0, h, 0, 0)))
        args.append(s0)
    out_specs = [pl.BlockSpec((None, t, GDN_DV), lambda bi, h: (bi, 0, h))]
    out_shape = [jax.ShapeDtypeStruct((b, t, GDN_W), F32)]
    if emit_state:
        out_specs.append(pl.BlockSpec((None, 2, None, GDN_DK, GDN_DV), lambda bi, h: (bi, 0, h, 0, 0)))
        out_shape.append(jax.ShapeDtypeStruct((b, 2, GDN_HEADS, GDN_DK, GDN_DV), F32))
    tv = pltpu.VMEM((t, LANE), F32)
    scratch = [tv, tv, tv, pltpu.VMEM((t // CHUNK, GDN_DK, CHUNK), F32), tv, tv, tv, tv,
               pltpu.VMEM((GDN_DK, GDN_DV), F32), pltpu.VMEM((GDN_DK, GDN_DV), F32)]
    outs = pl.pallas_call(
        functools.partial(_gdn_kernel, t_len=t, has_state=s0 is not None, emit_state=emit_state),
        grid=(b, GDN_HEADS),
        in_specs=in_specs,
        out_specs=out_specs,
        out_shape=out_shape,
        scratch_shapes=scratch,
        compiler_params=_cparams(("arbitrary", "arbitrary")),
    )(*args)
    return outs if emit_state else (outs[0], None)


GLA_PAIR = 2


def _gla_kernel(*refs, t_len, has_state, emit_state):
    it = iter(refs)
    q_ref, k_ref, v_ref, r_ref, misc_ref, wg_ref, bg_ref, g_ref = (next(it) for _ in range(8))
    s0_ref = next(it) if has_state else None
    o_ref = next(it)
    sfin_ref = next(it) if emit_state else None
    glog_f, glog_b, vts, o_f, o_b, st = (next(it) for _ in range(6))

    c = CHUNK
    n_chunks = t_len // c
    misc = misc_ref[...]
    for d, dst in enumerate((glog_f, glog_b)):
        lr = misc[:, MISC_LR + d * GLA_RANK:MISC_LR + (d + 1) * GLA_RANK]
        logits = _mm_hi(lr, wg_ref[d]) + bg_ref[d:d + 1, :]
        dst[...] = -_softplus(-logits) / GLA_TAU
    v_t = v_ref[...].T
    for hh in range(GLA_PAIR):
        for ci in range(n_chunks):
            vts[hh, ci] = v_t[hh * GLA_DV:(hh + 1) * GLA_DV, ci * c:(ci + 1) * c]
    for d in range(2):
        for hh in range(GLA_PAIR):
            if has_state:
                s0 = jnp.concatenate([s0_ref[d, hh], jnp.zeros((GLA_DV - GLA_DK, GLA_DV), F32)], axis=0)
                st[d, hh] = s0.T[:, :GLA_DK]
            else:
                st[d, hh] = jnp.zeros((GLA_DV, GLA_DK), F32)

    incl_lo = _tri(c, "ge")
    incl_up = _tri(c, "le")
    lo_bf = jnp.where(incl_lo, 1.0, 0.0).astype(BF16)
    up_bf = jnp.where(incl_up, 1.0, 0.0).astype(BF16)

    def one_direction(ci, backward):
        r0 = pl.multiple_of(ci * c, c)
        if backward:
            cum_mask, incl, last, mid, d, glog, out = up_bf, incl_up, 0, c - 1 - c // 2, 1, glog_b, o_b
        else:
            cum_mask, incl, last, mid, d, glog, out = lo_bf, incl_lo, c - 1, c // 2, 0, glog_f, o_f
        b_all = _mm_mask_lhs(cum_mask, glog[pl.ds(r0, c), :])
        q_all = q_ref[pl.ds(r0, c), :] * (GLA_DK ** -0.5)
        k_all = k_ref[pl.ds(r0, c), :]
        v_all = v_ref[pl.ds(r0, c), :]
        for hh in range(GLA_PAIR):
            sl = slice(hh * GLA_DK, (hh + 1) * GLA_DK)
            bq = b_all[:, sl]
            qh = q_all[:, sl]
            kh = k_all[:, sl]
            vh = v_all[:, hh * GLA_DV:(hh + 1) * GLA_DV]
            b_ref = bq[mid:mid + 1, :]
            b_last = bq[last:last + 1, :]
            a = jnp.where(incl, _mm_nt(qh * jnp.exp(bq - b_ref), kh * jnp.exp(b_ref - bq)), 0.0)
            s_t = st[d, hh]
            o = _mm_nt(qh * jnp.exp(bq), s_t) + _mm(a, vh)
            out[pl.ds(r0, c), hh * GLA_DV:(hh + 1) * GLA_DV] = o
            st[d, hh] = s_t * jnp.exp(b_last) + _mm(vts[hh, ci], kh * jnp.exp(b_last - bq))

    def body(i, carry):
        one_direction(i, False)
        one_direction(n_chunks - 1 - i, True)
        return carry

    lax.fori_loop(0, n_chunks, body, 0)

    o = o_f[...] + o_b[...]
    r = r_ref[...]
    for hh in range(GLA_PAIR):
        sl = slice(hh * GLA_DV, (hh + 1) * GLA_DV)
        o_ref[:, sl] = _rms(o[:, sl], g_ref[...]) * _silu(r[:, sl])
    if emit_state:
        for d in range(2):
            for hh in range(GLA_PAIR):
                s_pad = jnp.concatenate([st[d, hh], jnp.zeros((GLA_DV, GLA_DV - GLA_DK), F32)], axis=1)
                sfin_ref[d, hh] = s_pad.T[:GLA_DK, :]


def _gla(p3, wg, bg, norm_g, s0, layer, emit_state):
    b, t, _ = p3.shape
    kw = GLA_PAIR * GLA_DK
    vw = GLA_PAIR * GLA_DV
    in_specs = [
        pl.BlockSpec((None, t, kw), lambda bi, p: (bi, 0, OFF_GLA_Q // kw + p)),
        pl.BlockSpec((None, t, kw), lambda bi, p: (bi, 0, OFF_GLA_K // kw + p)),
        pl.BlockSpec((None, t, vw), lambda bi, p: (bi, 0, OFF_GLA_V // vw + p)),
        pl.BlockSpec((None, t, vw), lambda bi, p: (bi, 0, OFF_GLA_R // vw + p)),
        pl.BlockSpec((None, t, LANE), lambda bi, p: (bi, 0, OFF_MISC // LANE)),
        pl.BlockSpec((2, GLA_RANK, kw), lambda bi, p: (0, 0, p)),
        pl.BlockSpec((2, kw), lambda bi, p: (0, p)),
        pl.BlockSpec((1, GLA_DV), lambda bi, p: (0, 0)),
    ]
    args = [p3, p3, p3, p3, p3, wg, bg, norm_g]
    if s0 is not None:
        in_specs.append(pl.BlockSpec((None, None, 2, GLA_PAIR, GLA_DK, GLA_DV),
                                     lambda bi, p: (bi, layer, 0, p, 0, 0)))
        args.append(s0)
    out_specs = [pl.BlockSpec((None, t, vw), lambda bi, p: (bi, 0, p))]
    out_shape = [jax.ShapeDtypeStruct((b, t, GLA_VW), F32)]
    if emit_state:
        out_specs.append(pl.BlockSpec((None, 2, GLA_PAIR, GLA_DK, GLA_DV), lambda bi, p: (bi, 0, p, 0, 0)))
        out_shape.append(jax.ShapeDtypeStruct((b, 2, GLA_HEADS, GLA_DK, GLA_DV), F32))
    scratch = [pltpu.VMEM((t, kw), F32), pltpu.VMEM((t, kw), F32),
               pltpu.VMEM((GLA_PAIR, t // CHUNK, GLA_DV, CHUNK), F32),
               pltpu.VMEM((t, vw), F32), pltpu.VMEM((t, vw), F32),
               pltpu.VMEM((2, GLA_PAIR, GLA_DV, GLA_DK), F32)]
    outs = pl.pallas_call(
        functools.partial(_gla_kernel, t_len=t, has_state=s0 is not None, emit_state=emit_state),
        grid=(b, GLA_HEADS // GLA_PAIR),
        in_specs=in_specs,
        out_specs=out_specs,
        out_shape=out_shape,
        scratch_shapes=scratch,
        compiler_params=_cparams(("arbitrary", "arbitrary")),
    )(*args)
    return outs if emit_state else (outs[0], None)


ATT_Q_BLOCK = 256


def _rope_tables(t_len):
    half = DIFF_DH // 2
    quarter = half // 2
    inv = ROPE_THETA ** (-np.arange(0, half, 2, dtype=np.float64) / half)
    tok = np.arange(t_len)
    pos = np.stack([tok // GRID_W, tok % GRID_W], axis=1).astype(np.float64)
    ang = pos[:, :, None] * inv[None, None, :]
    cos = np.concatenate([np.cos(ang), np.cos(ang)], axis=-1).reshape(t_len, DIFF_DH)
    sin = np.concatenate([-np.sin(ang), np.sin(ang)], axis=-1).reshape(t_len, DIFF_DH)
    cos = np.concatenate([cos, cos], axis=-1).astype(np.float32)
    sin = np.concatenate([sin, sin], axis=-1).astype(np.float32)
    first = ((np.arange(2 * DIFF_DH) % half) < quarter).astype(np.float32)[None, :]
    return jnp.asarray(cos), jnp.asarray(sin), jnp.asarray(first), quarter


def _rope(x, cos, sin, first, quarter):
    width = x.shape[-1]
    ahead = pltpu.roll(x, width - quarter, axis=1)
    behind = pltpu.roll(x, quarter, axis=1)
    partner = jnp.where(first > 0.5, ahead, behind)
    return x * cos + partner * sin


def _diff_kernel(*refs, t_len, ctx_len, lam_init):
    it = iter(refs)
    q_ref, k_ref, v_ref, lam_ref, g_ref = (next(it) for _ in range(5))
    if ctx_len:
        ck_ref, cv_ref, cos_ref, sin_ref, first_ref = (next(it) for _ in range(5))
    o_ref = next(it)
    q_sc, k_sc, v_sc = (next(it) for _ in range(3))

    scale = DIFF_DH ** -0.5
    q = q_ref[...]
    k = k_ref[...]
    if ctx_len:
        quarter = DIFF_DH // 4
        q = _rope(q, cos_ref[...], sin_ref[...], first_ref[...], quarter)
        k = _rope(k, cos_ref[...], sin_ref[...], first_ref[...], quarter)
        k_sc[t_len:t_len + ctx_len, :] = ck_ref[...].astype(BF16)
        v_sc[t_len:t_len + ctx_len, :] = cv_ref[...].astype(BF16)
    q_sc[...] = (q * scale).astype(BF16)
    k_sc[0:t_len, :] = k.astype(BF16)
    v_sc[0:t_len, :] = v_ref[...].astype(BF16)

    lp = lam_ref[...]
    lam = (jnp.exp(jnp.sum(lp[0:1, :] * lp[1:2, :], axis=1, keepdims=True))
           - jnp.exp(jnp.sum(lp[2:3, :] * lp[3:4, :], axis=1, keepdims=True)) + lam_init)

    def softmax(s):
        e = jnp.exp(s - jnp.max(s, axis=-1, keepdims=True))
        return e / jnp.sum(e, axis=-1, keepdims=True)

    tq = min(ATT_Q_BLOCK, t_len)

    def body(i, carry):
        r0 = pl.multiple_of(i * tq, tq)
        qb = q_sc[pl.ds(r0, tq), :]
        kk = k_sc[...]
        p1 = softmax(_mm_nt(qb[:, :DIFF_DH], kk[:, :DIFF_DH]))
        p2 = softmax(_mm_nt(qb[:, DIFF_DH:], kk[:, DIFF_DH:]))
        o = _mm(p1 - lam * p2, v_sc[...])
        o_ref[pl.ds(r0, tq), :] = _rms(o, g_ref[...]) * (1.0 - lam_init)
        return carry

    lax.fori_loop(0, t_len // tq, body, 0)


def _diff(p3, lam_p, norm_g, ctx_k, ctx_v, layer, lam_init):
    b, t, _ = p3.shape
    ctx_len = 0 if ctx_k is None else ctx_k.shape[2]
    col = lambda base: (lambda bi, h: (bi, 0, base // LANE + h))
    in_specs = [
        pl.BlockSpec((None, t, LANE), col(OFF_DIFF_Q)),
        pl.BlockSpec((None, t, LANE), col(OFF_DIFF_K)),
        pl.BlockSpec((None, t, LANE), col(OFF_DIFF_V)),
        pl.BlockSpec((4, DIFF_DH), lambda bi, h: (0, 0)),
        pl.BlockSpec((1, DIFF_VD), lambda bi, h: (0, 0)),
    ]
    args = [p3, p3, p3, lam_p, norm_g]
    if ctx_len:
        cos, sin, first, _ = _rope_tables(t)
        in_specs += [
            pl.BlockSpec((None, None, ctx_len, LANE), lambda bi, h: (bi, layer, 0, h)),
            pl.BlockSpec((None, None, ctx_len, LANE), lambda bi, h: (bi, layer, 0, h)),
            pl.BlockSpec((t, LANE), lambda bi, h: (0, 0)),
            pl.BlockSpec((t, LANE), lambda bi, h: (0, 0)),
            pl.BlockSpec((1, LANE), lambda bi, h: (0, 0)),
        ]
        args += [ctx_k, ctx_v, cos, sin, first]
    tk = t + ctx_len
    return pl.pallas_call(
        functools.partial(_diff_kernel, t_len=t, ctx_len=ctx_len, lam_init=lam_init),
        grid=(b, DIFF_HEADS),
        in_specs=in_specs,
        out_specs=pl.BlockSpec((None, t, DIFF_VD), lambda bi, h: (bi, 0, h)),
        out_shape=jax.ShapeDtypeStruct((b, t, DIFF_VW), F32),
        scratch_shapes=[pltpu.VMEM((t, LANE), BF16), pltpu.VMEM((tk, LANE), BF16),
                        pltpu.VMEM((tk, LANE), BF16)],
        compiler_params=_cparams(("arbitrary", "arbitrary")),
    )(*args)


def _route(logits_t, bias_col):
    scores = _sigmoid(logits_t)
    biased = scores + bias_col
    rows = [biased[e:e + 1, :] for e in range(N_EXPERTS)]
    grp = []
    for g in range(N_GROUPS):
        a0, a1, a2, a3 = rows[g * GROUP_SIZE:(g + 1) * GROUP_SIZE]
        hi01, lo01 = jnp.maximum(a0, a1), jnp.minimum(a0, a1)
        hi23, lo23 = jnp.maximum(a2, a3), jnp.minimum(a2, a3)
        top1 = jnp.maximum(hi01, hi23)
        top2 = jnp.maximum(jnp.minimum(hi01, hi23), jnp.maximum(lo01, lo23))
        grp.append(top1 + top2)
    best = []
    for g in range(N_GROUPS):
        win = None
        for o in range(N_GROUPS):
            if o == g:
                continue
            cond = grp[g] > grp[o] if o < g else grp[g] >= grp[o]
            win = cond if win is None else jnp.logical_and(win, cond)
        best.append(win)
    sel_rows = []
    for e in range(N_EXPERTS):
        g = e // GROUP_SIZE
        beaten = jnp.zeros_like(rows[e])
        for o in range(g * GROUP_SIZE, (g + 1) * GROUP_SIZE):
            if o == e:
                continue
            ahead = rows[o] >= rows[e] if o < e else rows[o] > rows[e]
            beaten = beaten + jnp.where(ahead, 1.0, 0.0)
        sel = jnp.logical_and(best[g], beaten < 1.5)
        sel_rows.append(jnp.where(sel, scores[e:e + 1, :], 0.0))
    picked = jnp.concatenate(sel_rows, axis=0)
    return picked / jnp.sum(picked, axis=0, keepdims=True)


def _merge_kernel(og_ref, ol_ref, od_ref, lg0_ref, lg1_ref, lg2_ref, x_ref, mod_ref, wb_ref, wo_ref,
                  lng_ref, lnb_ref, rw_ref, rb_ref, x1_ref, h_ref, gates_ref):
    acc = None
    for s, (o_ref, lg_ref) in enumerate(((og_ref, lg0_ref), (ol_ref, lg1_ref), (od_ref, lg2_ref))):
        proj = _mm(o_ref[...], wb_ref[s])
        term = _sigmoid(lg_ref[...]) * proj
        acc = term if acc is None else acc + term
    m = _mm(acc, wo_ref[...])
    gate1 = mod_ref[:, 2 * D_MODEL:3 * D_MODEL]
    shift2 = mod_ref[:, 3 * D_MODEL:4 * D_MODEL]
    scale2 = mod_ref[:, 4 * D_MODEL:5 * D_MODEL]
    x1 = _layernorm(ALPHA * x_ref[...] + gate1 * m, lng_ref[...], lnb_ref[...])
    x1_ref[...] = x1
    h = x1 * (1.0 + scale2) + shift2
    h_ref[...] = h.astype(BF16)
    gates_t = _route(_mm_hi_nt(rw_ref[...], h), rb_ref[...])
    pad = jnp.zeros((LANE - N_EXPERTS, gates_t.shape[1]), F32)
    gates_ref[...] = jnp.concatenate([gates_t, pad], axis=0).T


def _merge(o_gdn, o_gla, o_diff, p, x, mod, wb, wo, ln_g, ln_b, rw_t, rb_col, layer, row0,
           tokens_per_row, tm):
    n = x.shape[0]
    mod_row = lambda i: (layer, row0 + (i * tm) // tokens_per_row, 0, 0)
    full = lambda shape: pl.BlockSpec(shape, lambda i: (0,) * len(shape))
    return pl.pallas_call(
        _merge_kernel,
        grid=(n // tm,),
        in_specs=[
            pl.BlockSpec((tm, MIX_W), lambda i: (i, 0)),
            pl.BlockSpec((tm, MIX_W), lambda i: (i, 0)),
            pl.BlockSpec((tm, MIX_W), lambda i: (i, 0)),
            pl.BlockSpec((tm, D_MODEL), lambda i: (i, OFF_MERGE // D_MODEL)),
            pl.BlockSpec((tm, D_MODEL), lambda i: (i, OFF_MERGE // D_MODEL + 1)),
            pl.BlockSpec((tm, D_MODEL), lambda i: (i, OFF_MERGE // D_MODEL + 2)),
            pl.BlockSpec((tm, D_MODEL), lambda i: (i, 0)),
            pl.BlockSpec((None, None, 1, 6 * D_MODEL), mod_row),
            full((N_BRANCH, MIX_W, D_MODEL)),
            full((D_MODEL, D_MODEL)),
            full((1, D_MODEL)),
            full((1, D_MODEL)),
            full((N_EXPERTS, D_MODEL)),
            full((N_EXPERTS, 1)),
        ],
        out_specs=[
            pl.BlockSpec((tm, D_MODEL), lambda i: (i, 0)),
            pl.BlockSpec((tm, D_MODEL), lambda i: (i, 0)),
            pl.BlockSpec((tm, LANE), lambda i: (i, 0)),
        ],
        out_shape=[
            jax.ShapeDtypeStruct((n, D_MODEL), F32),
            jax.ShapeDtypeStruct((n, D_MODEL), BF16),
            jax.ShapeDtypeStruct((n, LANE), F32),
        ],
        compiler_params=_cparams(("arbitrary",)),
    )(o_gdn, o_gla, o_diff, p, p, p, x, mod, wb, wo, ln_g, ln_b, rw_t, rb_col)


def _moe_kernel(h_ref, gates_ref, wg_ref, wu_ref, wd_ref, x1_ref, mod_ref, lng_ref, lnb_ref,
                o_ref, acc):
    e = pl.program_id(1)

    @pl.when(e == 0)
    def _():
        acc[...] = jnp.zeros_like(acc)

    h = h_ref[...]
    act = _silu(jnp.dot(h, wg_ref[...], preferred_element_type=F32)) * jnp.dot(
        h, wu_ref[...], preferred_element_type=F32)
    gate = _pick_lane(gates_ref[...], e)
    acc[...] += _mm(act * gate, wd_ref[...])

    @pl.when(e == N_EXPERTS - 1)
    def _():
        gate2 = mod_ref[:, 5 * D_MODEL:6 * D_MODEL]
        o_ref[...] = _layernorm(ALPHA * x1_ref[...] + gate2 * acc[...], lng_ref[...], lnb_ref[...])


def _moe(h, gates, wg, wu, wd, x1, mod, ln_g, ln_b, layer, row0, tokens_per_row, tm):
    n = h.shape[0]
    mod_row = lambda i, e: (layer, row0 + (i * tm) // tokens_per_row, 0, 0)
    return pl.pallas_call(
        _moe_kernel,
        grid=(n // tm, N_EXPERTS),
        in_specs=[
            pl.BlockSpec((tm, D_MODEL), lambda i, e: (i, 0)),
            pl.BlockSpec((tm, LANE), lambda i, e: (i, 0)),
            pl.BlockSpec((None, D_MODEL, D_FF), lambda i, e: (e, 0, 0)),
            pl.BlockSpec((None, D_MODEL, D_FF), lambda i, e: (e, 0, 0)),
            pl.BlockSpec((None, D_FF, D_MODEL), lambda i, e: (e, 0, 0)),
            pl.BlockSpec((tm, D_MODEL), lambda i, e: (i, 0)),
            pl.BlockSpec((None, None, 1, 6 * D_MODEL), mod_row),
            pl.BlockSpec((1, D_MODEL), lambda i, e: (0, 0)),
            pl.BlockSpec((1, D_MODEL), lambda i, e: (0, 0)),
        ],
        out_specs=pl.BlockSpec((tm, D_MODEL), lambda i, e: (i, 0)),
        out_shape=jax.ShapeDtypeStruct((n, D_MODEL), F32),
        scratch_shapes=[pltpu.VMEM((tm, D_MODEL), F32)],
        compiler_params=_cparams(("arbitrary", "arbitrary")),
    )(h, gates, wg, wu, wd, x1, mod, ln_g, ln_b)


def _reorder_w_in(w):
    sizes = (3 * GDN_W, GDN_W, 2 * GDN_HEADS, 2 * GDN_HEADS, GLA_KW, GLA_KW, GLA_VW, GLA_VW,
             2 * GLA_RANK, DIFF_QW, DIFF_QW, DIFF_VW, N_BRANCH * D_MODEL)
    offs = np.concatenate([[0], np.cumsum(sizes)])
    seg = lambda i: w[:, offs[i]:offs[i + 1]]
    used = 2 * GDN_HEADS * 2 + 2 * GLA_RANK
    parts = [seg(0), seg(1), seg(4), seg(5), seg(6), seg(7), seg(9), seg(10), seg(11), seg(12),
             seg(2), seg(3), seg(8), jnp.zeros((w.shape[0], LANE - used), w.dtype)]
    return jnp.concatenate(parts, axis=1).astype(BF16)


def _lane_row(vals, offset):
    return jnp.zeros((1, LANE), F32).at[0, offset:offset + vals.shape[0]].set(vals)


def kernel(x_prompt, x_sample, c, state_gdn, state_gla, cache_k, cache_v, c_ctx, w_mod, b_mod, w_in,
           gdn_conv, gdn_a_log, gdn_dt_bias, gdn_norm, gla_w_gate, gla_b_gate, gla_norm, diff_lambda,
           diff_norm, w_branch, w_out, ln_g, ln_b, router_w, router_b, exp_w_gate, exp_w_up, exp_w_down):
    bp, tp, d = x_prompt.shape
    bs, ts, _ = x_sample.shape
    pad_rows = MOD_ROWS - 1 - bs
    cvecs = jnp.concatenate([c_ctx[None, :], c, jnp.zeros((pad_rows, d), F32)], axis=0)
    mod = _modulation(cvecs, w_mod, b_mod).reshape(DEPTH, MOD_ROWS, 1, 6 * d)

    rw_t = router_w.T
    rb_col = router_b.reshape(N_EXPERTS, 1)
    ck = cache_k.reshape(bs, DEPTH, cache_k.shape[2], DIFF_QW)
    cv = cache_v.reshape(bs, DEPTH, cache_v.shape[2], DIFF_VW)

    layer_w = []
    for l in range(DEPTH):
        layer_w.append(dict(
            w_in=_reorder_w_in(w_in[l]),
            alog=_lane_row(gdn_a_log[l].reshape(-1), MISC_A),
            dtb=_lane_row(gdn_dt_bias[l].reshape(-1), MISC_A),
            wb=w_branch[l].astype(BF16),
            wo=w_out[l].astype(BF16),
            wg=exp_w_gate[l].astype(BF16),
            wu=exp_w_up[l].astype(BF16),
            wd=exp_w_down[l].astype(BF16),
        ))

    def layer(x, l, row0, tokens_per_row, s_gdn, s_gla, ctx_k, ctx_v, emit_state, tm):
        b, t, _ = x.shape
        n = b * t
        lw = layer_w[l]
        xf = x.reshape(n, d)
        p = _inproj(xf, mod, lw["w_in"], l, row0, tokens_per_row, tm)
        p3 = p.reshape(b, t, P_COLS)
        o_gdn, gdn_fin = _gdn(p3, gdn_conv[l], lw["alog"], lw["dtb"], gdn_norm[l][None, :], s_gdn, l,
                              emit_state)
        o_gla, gla_fin = _gla(p3, gla_w_gate[l], gla_b_gate[l], gla_norm[l][None, :], s_gla, l,
                              emit_state)
        lam_init = 0.8 - 0.6 * math.exp(-0.3 * l)
        o_diff = _diff(p3, diff_lambda[l], diff_norm[l][None, :], ctx_k, ctx_v, l, lam_init)
        x1, h, gates = _merge(o_gdn.reshape(n, MIX_W), o_gla.reshape(n, MIX_W), o_diff.reshape(n, MIX_W),
                              p, xf, mod, lw["wb"], lw["wo"], ln_g[l, 0][None, :], ln_b[l, 0][None, :],
                              rw_t, rb_col, l, row0, tokens_per_row, tm)
        x2 = _moe(h, gates, lw["wg"], lw["wu"], lw["wd"], x1, mod, ln_g[l, 1][None, :],
                  ln_b[l, 1][None, :], l, row0, tokens_per_row, tm)
        k_own = p3[:, :, OFF_DIFF_K:OFF_DIFF_K + DIFF_QW].reshape(b, t, DIFF_HEADS, 2 * DIFF_DH)
        v_own = p3[:, :, OFF_DIFF_V:OFF_DIFF_V + DIFF_VW].reshape(b, t, DIFF_HEADS, DIFF_VD)
        return x2.reshape(b, t, d), gdn_fin, gla_fin, k_own, v_own

    hp = x_prompt
    gdn_states, gla_states, ks, vs = [], [], [], []
    for l in range(DEPTH):
        hp, s_gdn, s_gla, k_l, v_l = layer(hp, l, 0, bp * tp, None, None, None, None, True, 512)
        gdn_states.append(s_gdn)
        gla_states.append(s_gla)
        ks.append(k_l)
        vs.append(v_l)

    hs = x_sample
    for l in range(DEPTH):
        hs = layer(hs, l, 1, ts, state_gdn, state_gla, ck, cv, False, 512)[0]

    return (hp, hs, jnp.stack(gdn_states, axis=1), jnp.stack(gla_states, axis=1),
            jnp.stack(ks, axis=1), jnp.stack(vs, axis=1))
```

```python
import functools
import math

import numpy as np
import jax
import jax.numpy as jnp
from jax import lax
from jax.experimental import pallas as pl
from jax.experimental.pallas import tpu as pltpu

F32 = jnp.float32
BF16 = jnp.bfloat16

D_MODEL = 1024
DEPTH = 2
GRID_W = 64
GDN_HEADS = 4
GDN_DK = 128
GDN_DV = 128
GDN_W = GDN_HEADS * GDN_DV
SHORT_CONV = 5
GLA_HEADS = 4
GLA_DK = 64
GLA_DV = 128
GLA_KW = GLA_HEADS * GLA_DK
GLA_VW = GLA_HEADS * GLA_DV
GLA_RANK = 16
GLA_TAU = 16.0
CHUNK = 64
DIFF_HEADS = 4
DIFF_DH = 64
DIFF_VD = 2 * DIFF_DH
DIFF_QW = DIFF_HEADS * 2 * DIFF_DH
DIFF_VW = DIFF_HEADS * DIFF_VD
ROPE_THETA = 10000.0
MIX_W = 512
N_BRANCH = 3
N_EXPERTS = 16
N_GROUPS = 4
GROUP_SIZE = N_EXPERTS // N_GROUPS
D_FF = 512
ALPHA = (2 * DEPTH) ** 0.25
EPS = 1e-6

LANE = 128
MOD_ROWS = 8
VMEM_LIMIT = 56 * 1024 * 1024

A_GDN_QKV = 0
A_GDN_Z = 1536
A_GLA_Q = 2048
A_GLA_K = 2304
A_GLA_V = 2560
A_GLA_R = 3072
A_DIFF_Q = 3584
A_MERGE = 4096
A_COLS = 7168
F_DIFF_K = 0
F_DIFF_V = 512
F_MISC = 1024
F_COLS = 1152
MISC_B = 0
MISC_A = 8
MISC_LR = 16


def _cparams(sem):
    return pltpu.CompilerParams(dimension_semantics=sem, vmem_limit_bytes=VMEM_LIMIT)


def _dot(a, b):
    return jnp.dot(a, b, preferred_element_type=F32)


def _mm(a, b):
    return _dot(a.astype(BF16), b.astype(BF16))


def _mm_nt(a, b):
    return lax.dot_general(a.astype(BF16), b.astype(BF16), (((1,), (1,)), ((), ())),
                           preferred_element_type=F32)


def _split3(x):
    hi = x.astype(BF16)
    r = x - hi.astype(F32)
    mid = r.astype(BF16)
    lo = (r - mid.astype(F32)).astype(BF16)
    return hi, mid, lo


def _mm_mask_lhs(mask_bf16, x):
    hi, mid, lo = _split3(x)
    return _dot(mask_bf16, hi) + _dot(mask_bf16, mid) + _dot(mask_bf16, lo)


def _mm_mask_rhs(x, mask_bf16):
    hi, mid, lo = _split3(x)
    return _dot(hi, mask_bf16) + _dot(mid, mask_bf16) + _dot(lo, mask_bf16)


def _mm_hi(a, b):
    ah = a.astype(BF16)
    al = (a - ah.astype(F32)).astype(BF16)
    bh = b.astype(BF16)
    bl = (b - bh.astype(F32)).astype(BF16)
    return _dot(ah, bh) + _dot(ah, bl) + _dot(al, bh)


def _mm_hi_nt(a, b):
    ah = a.astype(BF16)
    al = (a - ah.astype(F32)).astype(BF16)
    bh = b.astype(BF16)
    bl = (b - bh.astype(F32)).astype(BF16)
    dot = lambda x, y: lax.dot_general(x, y, (((1,), (1,)), ((), ())), preferred_element_type=F32)
    return dot(ah, bh) + dot(ah, bl) + dot(al, bh)


def _sigmoid(x):
    return 1.0 / (1.0 + jnp.exp(-x))


def _silu(x):
    return x * _sigmoid(x)


def _softplus(x):
    return jnp.maximum(x, 0.0) + jnp.log1p(jnp.exp(-jnp.abs(x)))


def _rms(x, g):
    return x * lax.rsqrt(jnp.mean(x * x, axis=-1, keepdims=True) + EPS) * g


def _layernorm(y, g, b):
    mu = jnp.mean(y, axis=-1, keepdims=True)
    d = y - mu
    var = jnp.mean(d * d, axis=-1, keepdims=True)
    return d * lax.rsqrt(var + EPS) * g + b


def _pick_lane(x, idx):
    lane = lax.broadcasted_iota(jnp.int32, x.shape, 1)
    return jnp.sum(jnp.where(lane == idx, x, 0.0), axis=1, keepdims=True)


def _tri(n, kind):
    r = lax.broadcasted_iota(jnp.int32, (n, n), 0)
    c = lax.broadcasted_iota(jnp.int32, (n, n), 1)
    return {"ge": r >= c, "gt": r > c, "le": r <= c, "lt": r < c}[kind]


def _mod_kernel(c_ref, w_ref, b_ref, o_ref):
    s = _silu(c_ref[...])
    o_ref[...] = _mm_hi(s, w_ref[...]) + b_ref[...]


def _modulation(cvecs, w_mod, b_mod):
    tn = 1536
    return pl.pallas_call(
        _mod_kernel,
        grid=(DEPTH, 6 * D_MODEL // tn),
        in_specs=[
            pl.BlockSpec((MOD_ROWS, D_MODEL), lambda l, j: (0, 0)),
            pl.BlockSpec((None, D_MODEL, tn), lambda l, j: (l, 0, j)),
            pl.BlockSpec((None, 1, tn), lambda l, j: (l, 0, j)),
        ],
        out_specs=pl.BlockSpec((None, MOD_ROWS, tn), lambda l, j: (l, 0, j)),
        out_shape=jax.ShapeDtypeStruct((DEPTH, MOD_ROWS, 6 * D_MODEL), F32),
        compiler_params=_cparams(("arbitrary", "arbitrary")),
    )(cvecs, w_mod, b_mod.reshape(DEPTH, 1, 6 * D_MODEL))


def _inproj_kernel(x_ref, mod_ref, wa_ref, wf_ref, pa_ref, pf_ref, u_sc):
    @pl.when(pl.program_id(1) == 0)
    def _():
        shift = mod_ref[:, 0:D_MODEL]
        scale = mod_ref[:, D_MODEL:2 * D_MODEL]
        u = (x_ref[...] * (1.0 + scale) + shift).astype(BF16)
        u_sc[...] = u
        pf_ref[...] = _dot(u, wf_ref[...])

    pa_ref[...] = _dot(u_sc[...], wa_ref[...]).astype(BF16)


def _inproj(x, mod, wa, wf, layer, row0, tokens_per_row, tm):
    n = x.shape[0]
    tn = 1792
    mod_row = lambda i, j: (layer, row0 + (i * tm) // tokens_per_row, 0, 0)
    return pl.pallas_call(
        _inproj_kernel,
        grid=(n // tm, A_COLS // tn),
        in_specs=[
            pl.BlockSpec((tm, D_MODEL), lambda i, j: (i, 0)),
            pl.BlockSpec((None, None, 1, 6 * D_MODEL), mod_row),
            pl.BlockSpec((D_MODEL, tn), lambda i, j: (0, j)),
            pl.BlockSpec((D_MODEL, F_COLS), lambda i, j: (0, 0)),
        ],
        out_specs=[
            pl.BlockSpec((tm, tn), lambda i, j: (i, j)),
            pl.BlockSpec((tm, F_COLS), lambda i, j: (i, 0)),
        ],
        out_shape=[
            jax.ShapeDtypeStruct((n, A_COLS), BF16),
            jax.ShapeDtypeStruct((n, F_COLS), F32),
        ],
        scratch_shapes=[pltpu.VMEM((tm, D_MODEL), BF16)],
        compiler_params=_cparams(("arbitrary", "arbitrary")),
    )(x, mod, wa, wf)


def _short_conv(x, w, t_len):
    row = lax.broadcasted_iota(jnp.int32, x.shape, 0)
    half = SHORT_CONV // 2
    acc = x * w[half:half + 1, :]
    for d in range(-half, half + 1):
        if d == 0:
            continue
        shifted = pltpu.roll(x, (-d) % t_len, axis=0)
        valid = jnp.logical_and(row + d >= 0, row + d < t_len)
        acc = acc + jnp.where(valid, shifted, 0.0) * w[half + d:half + d + 1, :]
    return acc


def _solve_unit_tri(a_list, rhs_list):
    n = a_list[0].shape[0]
    eye = jnp.where(_tri(n, "ge") & _tri(n, "le"), 1.0, 0.0)
    a_b = [a.astype(BF16) for a in a_list]
    p_b = [-a for a in a_b]
    inv = [eye - a for a in a_list]
    for _ in range(int(math.log2(n)) - 1):
        p_b = [_dot(p, p).astype(BF16) for p in p_b]
        inv = [x + _dot(x.astype(BF16), p) for x, p in zip(inv, p_b)]
    inv_b = [x.astype(BF16) for x in inv]
    sol = [_dot(x, r.astype(BF16)) for x, r in zip(inv_b, rhs_list)]
    resid = [r - s - _dot(a, s.astype(BF16)) for r, s, a in zip(rhs_list, sol, a_b)]
    return [s + _dot(x, r.astype(BF16)) for s, x, r in zip(sol, inv_b, resid)]


def _gdn_kernel(*refs, t_len, has_state, emit_state):
    it = iter(refs)
    qkv_ref, z_ref, misc_ref, cw_ref, alog_ref, dtb_ref, g_ref = (next(it) for _ in range(7))
    s0_ref = next(it) if has_state else None
    o_ref = next(it)
    sfin_ref = next(it) if emit_state else None
    (q_s, k_s, v_s, kt_s, bet, gat, gat_t, u_s, w_s, qd_s, a_s, ktl_s, gts, o_f, o_b,
     s_s) = (next(it) for _ in range(16))

    c = CHUNK
    n_chunks = t_len // c
    nh = GDN_HEADS
    dk = GDN_DK

    for h in range(nh):
        hs = slice(h * dk, (h + 1) * dk)
        q, k, v = (
            _silu(_short_conv(qkv_ref[:, j * GDN_W + h * dk:j * GDN_W + (h + 1) * dk].astype(F32),
                              cw_ref[:, j * GDN_W + h * dk:j * GDN_W + (h + 1) * dk], t_len))
            for j in range(3))
        q = q * lax.rsqrt(jnp.sum(q * q, axis=-1, keepdims=True) + EPS) * (dk ** -0.5)
        k = k * lax.rsqrt(jnp.sum(k * k, axis=-1, keepdims=True) + EPS)
        q_s[:, hs] = q
        k_s[:, hs] = k
        v_s[:, hs] = v
        kt_s[hs, :] = k.T.astype(BF16)
    misc = misc_ref[...]
    bet[...] = _sigmoid(misc)
    g_all = -jnp.exp(alog_ref[...]) * _softplus(misc + dtb_ref[...])
    gat[...] = g_all
    gat_t[...] = g_all.T[MISC_A:MISC_A + 2 * nh, :]
    for d in range(2):
        for h in range(nh):
            s_s[d * nh + h] = s0_ref[d, h] if has_state else jnp.zeros((dk, GDN_DV), F32)

    two = 2 * c
    r2 = lax.broadcasted_iota(jnp.int32, (two, two), 0)
    c2 = lax.broadcasted_iota(jnp.int32, (two, two), 1)
    same = (r2 >= c) == (c2 >= c)
    lo2 = jnp.where(jnp.logical_and(same, r2 >= c2), 1.0, 0.0).astype(BF16)
    up2 = jnp.where(jnp.logical_and(same, r2 <= c2), 1.0, 0.0).astype(BF16)
    incl = (_tri(c, "ge"), _tri(c, "le"))
    strict = (_tri(c, "gt"), _tri(c, "lt"))
    last = (c - 1, 0)

    def phase1(cp, carry):
        r0 = pl.multiple_of(cp * two, two)
        g_blk = gat[pl.ds(r0, two), :]
        b_blk = bet[pl.ds(r0, two), :]
        gt_blk = gat_t[:, pl.ds(r0, two)]
        cum_col = (_mm_mask_lhs(lo2, g_blk), _mm_mask_lhs(up2, g_blk))
        cum_row = (_mm_mask_rhs(gt_blk, up2), _mm_mask_rhs(gt_blk, lo2))
        chains, a_list, rhs_list = [], [], []
        for h in range(nh):
            hs = slice(h * dk, (h + 1) * dk)
            kt_pair = kt_s[hs, pl.ds(r0, two)]
            for s in range(2):
                rows = pl.ds(pl.multiple_of(r0 + s * c, c), c)
                q_c = q_s[rows, hs]
                k_c = k_s[rows, hs]
                v_c = v_s[rows, hs]
                kt_c = kt_pair[:, s * c:(s + 1) * c]
                kk = _dot(k_c.astype(BF16), kt_c)
                qk = _dot(q_c.astype(BF16), kt_c)
                for d in range(2):
                    idx = d * nh + h
                    gc = cum_col[d][s * c:(s + 1) * c, MISC_A + idx:MISC_A + idx + 1]
                    gcr = cum_row[d][idx:idx + 1, s * c:(s + 1) * c]
                    beta = b_blk[s * c:(s + 1) * c, MISC_B + idx:MISC_B + idx + 1]
                    decay = jnp.where(incl[d], jnp.exp(jnp.where(incl[d], gc - gcr, 0.0)), 0.0)
                    e_gc = jnp.exp(gc)
                    a_list.append(jnp.where(strict[d], beta * kk * decay, 0.0))
                    rhs_list.append(jnp.concatenate([v_c * beta, k_c * (beta * e_gc)], axis=1))
                    a_s[idx, rows, :] = jnp.where(incl[d], qk * decay, 0.0).astype(BF16)
                    qd_s[idx, rows, :] = (q_c * e_gc).astype(BF16)
                    g_last = gcr[:, last[d]:last[d] + 1]
                    ktl_s[idx, cp * 2 + s] = (kt_c.astype(F32) * jnp.exp(g_last - gcr)).astype(BF16)
                    gts[cp * 2 + s, idx:idx + 1, :] = jnp.broadcast_to(jnp.exp(g_last), (1, LANE))
                    chains.append((idx, rows))
        for (idx, rows), sol in zip(chains, _solve_unit_tri(a_list, rhs_list)):
            u_s[idx, rows, :] = sol[:, :GDN_DV]
            w_s[idx, rows, :] = sol[:, GDN_DV:].astype(BF16)
        return carry

    lax.fori_loop(0, n_chunks // 2, phase1, 0)

    def phase2(i, carry):
        chains = []
        for d, out in ((0, o_f), (1, o_b)):
            ci = i if d == 0 else n_chunks - 1 - i
            rows = pl.ds(pl.multiple_of(ci * c, c), c)
            g_tail = gts[ci]
            for h in range(nh):
                chains.append((d * nh + h, h, ci, rows, out, g_tail))
        s_f32 = [s_s[idx] for idx, *_ in chains]
        s_b = [s.astype(BF16) for s in s_f32]
        v_b = [(u_s[idx, rows, :] - _dot(w_s[idx, rows, :], sb)).astype(BF16)
               for (idx, _, _, rows, _, _), sb in zip(chains, s_b)]
        for (idx, h, ci, rows, out, g_tail), s, sb, vb in zip(chains, s_f32, s_b, v_b):
            s_s[idx] = s * g_tail[idx:idx + 1, :] + _dot(ktl_s[idx, ci], vb)
            out[rows, h * GDN_DV:(h + 1) * GDN_DV] = _dot(qd_s[idx, rows, :], sb) + _dot(
                a_s[idx, rows, :], vb)
        return carry

    lax.fori_loop(0, n_chunks, phase2, 0)

    for h in range(nh):
        hs = slice(h * GDN_DV, (h + 1) * GDN_DV)
        o = o_f[:, hs] + o_b[:, hs]
        o_ref[:, hs] = _rms(o, g_ref[...]) * _silu(z_ref[:, hs].astype(F32))
    if emit_state:
        for d in range(2):
            for h in range(nh):
                sfin_ref[d, h] = s_s[d * nh + h]


def _gdn(pa3, pf3, conv_w, alog_row, dtb_row, norm_g, s0, layer, emit_state):
    b, t, _ = pa3.shape
    full = lambda shape: pl.BlockSpec(shape, lambda bi: (0,) * len(shape))
    in_specs = [
        pl.BlockSpec((None, t, 3 * GDN_W), lambda bi: (bi, 0, A_GDN_QKV // (3 * GDN_W))),
        pl.BlockSpec((None, t, GDN_W), lambda bi: (bi, 0, A_GDN_Z // GDN_W)),
        pl.BlockSpec((None, t, LANE), lambda bi: (bi, 0, F_MISC // LANE)),
        full((SHORT_CONV, 3 * GDN_W)),
        full((1, LANE)),
        full((1, LANE)),
        full((1, GDN_DV)),
    ]
    args = [pa3, pa3, pf3, conv_w, alog_row, dtb_row, norm_g]
    if s0 is not None:
        in_specs.append(pl.BlockSpec((None, None, 2, GDN_HEADS, GDN_DK, GDN_DV),
                                     lambda bi: (bi, layer, 0, 0, 0, 0)))
        args.append(s0)
    out_specs = [pl.BlockSpec((None, t, GDN_W), lambda bi: (bi, 0, 0))]
    out_shape = [jax.ShapeDtypeStruct((b, t, GDN_W), F32)]
    if emit_state:
        out_specs.append(pl.BlockSpec((None, 2, GDN_HEADS, GDN_DK, GDN_DV), lambda bi: (bi, 0, 0, 0, 0)))
        out_shape.append(jax.ShapeDtypeStruct((b, 2, GDN_HEADS, GDN_DK, GDN_DV), F32))
    nc = t // CHUNK
    nd = 2 * GDN_HEADS
    wide = pltpu.VMEM((t, GDN_W), F32)
    scratch = [
        wide, wide, wide,
        pltpu.VMEM((GDN_W, t), BF16),
        pltpu.VMEM((t, LANE), F32), pltpu.VMEM((t, LANE), F32),
        pltpu.VMEM((nd, t), F32),
        pltpu.VMEM((nd, t, GDN_DV), F32),
        pltpu.VMEM((nd, t, GDN_DK), BF16),
        pltpu.VMEM((nd, t, GDN_DK), BF16),
        pltpu.VMEM((nd, t, CHUNK), BF16),
        pltpu.VMEM((nd, nc, GDN_DK, CHUNK), BF16),
        pltpu.VMEM((nc, nd, LANE), F32),
        wide, wide,
        pltpu.VMEM((nd, GDN_DK, GDN_DV), F32),
    ]
    outs = pl.pallas_call(
        functools.partial(_gdn_kernel, t_len=t, has_state=s0 is not None, emit_state=emit_state),
        grid=(b,),
        in_specs=in_specs,
        out_specs=out_specs,
        out_shape=out_shape,
        scratch_shapes=scratch,
        compiler_params=_cparams(("arbitrary",)),
    )(*args)
    return outs if emit_state else (outs[0], None)


GLA_PAIR = 2


def _gla_kernel(*refs, t_len, has_state, emit_state):
    it = iter(refs)
    q_ref, k_ref, v_ref, r_ref, misc_ref, wg_ref, bg_ref, g_ref = (next(it) for _ in range(8))
    s0_ref = next(it) if has_state else None
    o_ref = next(it)
    sfin_ref = next(it) if emit_state else None
    glog_f, glog_b, vts, o_f, o_b, st = (next(it) for _ in range(6))

    c = CHUNK
    n_chunks = t_len // c
    misc = misc_ref[...]
    for d, dst in enumerate((glog_f, glog_b)):
        lr = misc[:, MISC_LR + d * GLA_RANK:MISC_LR + (d + 1) * GLA_RANK]
        logits = _mm_hi(lr, wg_ref[d]) + bg_ref[d:d + 1, :]
        dst[...] = -_softplus(-logits) / GLA_TAU
    v_t = v_ref[...].astype(F32).T
    for hh in range(GLA_PAIR):
        for ci in range(n_chunks):
            vts[hh, ci] = v_t[hh * GLA_DV:(hh + 1) * GLA_DV, ci * c:(ci + 1) * c]
    for d in range(2):
        for hh in range(GLA_PAIR):
            if has_state:
                s0 = jnp.concatenate([s0_ref[d, hh], jnp.zeros((GLA_DV - GLA_DK, GLA_DV), F32)], axis=0)
                st[d, hh] = s0.T[:, :GLA_DK]
            else:
                st[d, hh] = jnp.zeros((GLA_DV, GLA_DK), F32)

    incl_lo = _tri(c, "ge")
    incl_up = _tri(c, "le")
    lo_bf = jnp.where(incl_lo, 1.0, 0.0).astype(BF16)
    up_bf = jnp.where(incl_up, 1.0, 0.0).astype(BF16)

    def one_direction(ci, backward):
        r0 = pl.multiple_of(ci * c, c)
        if backward:
            cum_mask, incl, last, mid, d, glog, out = up_bf, incl_up, 0, c - 1 - c // 2, 1, glog_b, o_b
        else:
            cum_mask, incl, last, mid, d, glog, out = lo_bf, incl_lo, c - 1, c // 2, 0, glog_f, o_f
        b_all = _mm_mask_lhs(cum_mask, glog[pl.ds(r0, c), :])
        q_all = q_ref[pl.ds(r0, c), :].astype(F32) * (GLA_DK ** -0.5)
        k_all = k_ref[pl.ds(r0, c), :].astype(F32)
        v_all = v_ref[pl.ds(r0, c), :]
        for hh in range(GLA_PAIR):
            sl = slice(hh * GLA_DK, (hh + 1) * GLA_DK)
            bq = b_all[:, sl]
            qh = q_all[:, sl]
            kh = k_all[:, sl]
            vh = v_all[:, hh * GLA_DV:(hh + 1) * GLA_DV]
            b_ref = bq[mid:mid + 1, :]
            b_last = bq[last:last + 1, :]
            a = jnp.where(incl, _mm_nt(qh * jnp.exp(bq - b_ref), kh * jnp.exp(b_ref - bq)), 0.0)
            s_t = st[d, hh]
            o = _mm_nt(qh * jnp.exp(bq), s_t) + _mm(a, vh)
            out[pl.ds(r0, c), hh * GLA_DV:(hh + 1) * GLA_DV] = o
            st[d, hh] = s_t * jnp.exp(b_last) + _mm(vts[hh, ci], kh * jnp.exp(b_last - bq))

    def body(i, carry):
        one_direction(i, False)
        one_direction(n_chunks - 1 - i, True)
        return carry

    lax.fori_loop(0, n_chunks, body, 0)

    o = o_f[...] + o_b[...]
    r = r_ref[...].astype(F32)
    for hh in range(GLA_PAIR):
        sl = slice(hh * GLA_DV, (hh + 1) * GLA_DV)
        o_ref[:, sl] = _rms(o[:, sl], g_ref[...]) * _silu(r[:, sl])
    if emit_state:
        for d in range(2):
            for hh in range(GLA_PAIR):
                s_pad = jnp.concatenate([st[d, hh], jnp.zeros((GLA_DV, GLA_DV - GLA_DK), F32)], axis=1)
                sfin_ref[d, hh] = s_pad.T[:GLA_DK, :]


def _gla(pa3, pf3, wg, bg, norm_g, s0, layer, emit_state):
    b, t, _ = pa3.shape
    kw = GLA_PAIR * GLA_DK
    vw = GLA_PAIR * GLA_DV
    in_specs = [
        pl.BlockSpec((None, t, kw), lambda bi, p: (bi, 0, A_GLA_Q // kw + p)),
        pl.BlockSpec((None, t, kw), lambda bi, p: (bi, 0, A_GLA_K // kw + p)),
        pl.BlockSpec((None, t, vw), lambda bi, p: (bi, 0, A_GLA_V // vw + p)),
        pl.BlockSpec((None, t, vw), lambda bi, p: (bi, 0, A_GLA_R // vw + p)),
        pl.BlockSpec((None, t, LANE), lambda bi, p: (bi, 0, F_MISC // LANE)),
        pl.BlockSpec((2, GLA_RANK, kw), lambda bi, p: (0, 0, p)),
        pl.BlockSpec((2, kw), lambda bi, p: (0, p)),
        pl.BlockSpec((1, GLA_DV), lambda bi, p: (0, 0)),
    ]
    args = [pa3, pa3, pa3, pa3, pf3, wg, bg, norm_g]
    if s0 is not None:
        in_specs.append(pl.BlockSpec((None, None, 2, GLA_PAIR, GLA_DK, GLA_DV),
                                     lambda bi, p: (bi, layer, 0, p, 0, 0)))
        args.append(s0)
    out_specs = [pl.BlockSpec((None, t, vw), lambda bi, p: (bi, 0, p))]
    out_shape = [jax.ShapeDtypeStruct((b, t, GLA_VW), F32)]
    if emit_state:
        out_specs.append(pl.BlockSpec((None, 2, GLA_PAIR, GLA_DK, GLA_DV), lambda bi, p: (bi, 0, p, 0, 0)))
        out_shape.append(jax.ShapeDtypeStruct((b, 2, GLA_HEADS, GLA_DK, GLA_DV), F32))
    scratch = [pltpu.VMEM((t, kw), F32), pltpu.VMEM((t, kw), F32),
               pltpu.VMEM((GLA_PAIR, t // CHUNK, GLA_DV, CHUNK), F32),
               pltpu.VMEM((t, vw), F32), pltpu.VMEM((t, vw), F32),
               pltpu.VMEM((2, GLA_PAIR, GLA_DV, GLA_DK), F32)]
    outs = pl.pallas_call(
        functools.partial(_gla_kernel, t_len=t, has_state=s0 is not None, emit_state=emit_state),
        grid=(b, GLA_HEADS // GLA_PAIR),
        in_specs=in_specs,
        out_specs=out_specs,
        out_shape=out_shape,
        scratch_shapes=scratch,
        compiler_params=_cparams(("arbitrary", "arbitrary")),
    )(*args)
    return outs if emit_state else (outs[0], None)


ATT_Q_BLOCK = 256


def _rope_tables(t_len):
    half = DIFF_DH // 2
    quarter = half // 2
    inv = ROPE_THETA ** (-np.arange(0, half, 2, dtype=np.float64) / half)
    tok = np.arange(t_len)
    pos = np.stack([tok // GRID_W, tok % GRID_W], axis=1).astype(np.float64)
    ang = pos[:, :, None] * inv[None, None, :]
    cos = np.concatenate([np.cos(ang), np.cos(ang)], axis=-1).reshape(t_len, DIFF_DH)
    sin = np.concatenate([-np.sin(ang), np.sin(ang)], axis=-1).reshape(t_len, DIFF_DH)
    cos = np.concatenate([cos, cos], axis=-1).astype(np.float32)
    sin = np.concatenate([sin, sin], axis=-1).astype(np.float32)
    first = ((np.arange(2 * DIFF_DH) % half) < quarter).astype(np.float32)[None, :]
    return jnp.asarray(cos), jnp.asarray(sin), jnp.asarray(first), quarter


def _rope(x, cos, sin, first, quarter):
    width = x.shape[-1]
    ahead = pltpu.roll(x, width - quarter, axis=1)
    behind = pltpu.roll(x, quarter, axis=1)
    partner = jnp.where(first > 0.5, ahead, behind)
    return x * cos + partner * sin


def _diff_kernel(*refs, t_len, ctx_len, lam_init):
    it = iter(refs)
    q_ref, k_ref, v_ref, lam_ref, g_ref = (next(it) for _ in range(5))
    if ctx_len:
        ck_ref, cv_ref, cos_ref, sin_ref, first_ref = (next(it) for _ in range(5))
    o_ref = next(it)
    q_sc, k_sc, v_sc = (next(it) for _ in range(3))

    scale = DIFF_DH ** -0.5
    q = q_ref[...].astype(F32)
    k = k_ref[...]
    if ctx_len:
        quarter = DIFF_DH // 4
        q = _rope(q, cos_ref[...], sin_ref[...], first_ref[...], quarter)
        k = _rope(k, cos_ref[...], sin_ref[...], first_ref[...], quarter)
        k_sc[t_len:t_len + ctx_len, :] = ck_ref[...].astype(BF16)
        v_sc[t_len:t_len + ctx_len, :] = cv_ref[...].astype(BF16)
    q_sc[...] = (q * scale).astype(BF16)
    k_sc[0:t_len, :] = k.astype(BF16)
    v_sc[0:t_len, :] = v_ref[...].astype(BF16)

    lp = lam_ref[...]
    lam = (jnp.exp(jnp.sum(lp[0:1, :] * lp[1:2, :], axis=1, keepdims=True))
           - jnp.exp(jnp.sum(lp[2:3, :] * lp[3:4, :], axis=1, keepdims=True)) + lam_init)

    def softmax(s):
        e = jnp.exp(s - jnp.max(s, axis=-1, keepdims=True))
        return e / jnp.sum(e, axis=-1, keepdims=True)

    tq = min(ATT_Q_BLOCK, t_len)

    def body(i, carry):
        r0 = pl.multiple_of(i * tq, tq)
        qb = q_sc[pl.ds(r0, tq), :]
        kk = k_sc[...]
        p1 = softmax(_mm_nt(qb[:, :DIFF_DH], kk[:, :DIFF_DH]))
        p2 = softmax(_mm_nt(qb[:, DIFF_DH:], kk[:, DIFF_DH:]))
        o = _mm(p1 - lam * p2, v_sc[...])
        o_ref[pl.ds(r0, tq), :] = _rms(o, g_ref[...]) * (1.0 - lam_init)
        return carry

    lax.fori_loop(0, t_len // tq, body, 0)


def _diff(pa3, pf3, lam_p, norm_g, ctx_k, ctx_v, layer, lam_init):
    b, t, _ = pa3.shape
    ctx_len = 0 if ctx_k is None else ctx_k.shape[2]
    col = lambda base: (lambda bi, h: (bi, 0, base // LANE + h))
    in_specs = [
        pl.BlockSpec((None, t, LANE), col(A_DIFF_Q)),
        pl.BlockSpec((None, t, LANE), col(F_DIFF_K)),
        pl.BlockSpec((None, t, LANE), col(F_DIFF_V)),
        pl.BlockSpec((4, DIFF_DH), lambda bi, h: (0, 0)),
        pl.BlockSpec((1, DIFF_VD), lambda bi, h: (0, 0)),
    ]
    args = [pa3, pf3, pf3, lam_p, norm_g]
    if ctx_len:
        cos, sin, first, _ = _rope_tables(t)
        in_specs += [
            pl.BlockSpec((None, None, ctx_len, LANE), lambda bi, h: (bi, layer, 0, h)),
            pl.BlockSpec((None, None, ctx_len, LANE), lambda bi, h: (bi, layer, 0, h)),
            pl.BlockSpec((t, LANE), lambda bi, h: (0, 0)),
            pl.BlockSpec((t, LANE), lambda bi, h: (0, 0)),
            pl.BlockSpec((1, LANE), lambda bi, h: (0, 0)),
        ]
        args += [ctx_k, ctx_v, cos, sin, first]
    tk = t + ctx_len
    return pl.pallas_call(
        functools.partial(_diff_kernel, t_len=t, ctx_len=ctx_len, lam_init=lam_init),
        grid=(b, DIFF_HEADS),
        in_specs=in_specs,
        out_specs=pl.BlockSpec((None, t, DIFF_VD), lambda bi, h: (bi, 0, h)),
        out_shape=jax.ShapeDtypeStruct((b, t, DIFF_VW), F32),
        scratch_shapes=[pltpu.VMEM((t, LANE), BF16), pltpu.VMEM((tk, LANE), BF16),
                        pltpu.VMEM((tk, LANE), BF16)],
        compiler_params=_cparams(("arbitrary", "arbitrary")),
    )(*args)


def _route(logits_t, bias_col):
    scores = _sigmoid(logits_t)
    biased = scores + bias_col
    rows = [biased[e:e + 1, :] for e in range(N_EXPERTS)]
    grp = []
    for g in range(N_GROUPS):
        a0, a1, a2, a3 = rows[g * GROUP_SIZE:(g + 1) * GROUP_SIZE]
        hi01, lo01 = jnp.maximum(a0, a1), jnp.minimum(a0, a1)
        hi23, lo23 = jnp.maximum(a2, a3), jnp.minimum(a2, a3)
        top1 = jnp.maximum(hi01, hi23)
        top2 = jnp.maximum(jnp.minimum(hi01, hi23), jnp.maximum(lo01, lo23))
        grp.append(top1 + top2)
    best = []
    for g in range(N_GROUPS):
        win = None
        for o in range(N_GROUPS):
            if o == g:
                continue
            cond = grp[g] > grp[o] if o < g else grp[g] >= grp[o]
            win = cond if win is None else jnp.logical_and(win, cond)
        best.append(win)
    sel_rows = []
    for e in range(N_EXPERTS):
        g = e // GROUP_SIZE
        beaten = jnp.zeros_like(rows[e])
        for o in range(g * GROUP_SIZE, (g + 1) * GROUP_SIZE):
            if o == e:
                continue
            ahead = rows[o] >= rows[e] if o < e else rows[o] > rows[e]
            beaten = beaten + jnp.where(ahead, 1.0, 0.0)
        sel = jnp.logical_and(best[g], beaten < 1.5)
        sel_rows.append(jnp.where(sel, scores[e:e + 1, :], 0.0))
    picked = jnp.concatenate(sel_rows, axis=0)
    return picked / jnp.sum(picked, axis=0, keepdims=True)


def _merge_kernel(og_ref, ol_ref, od_ref, lg0_ref, lg1_ref, lg2_ref, x_ref, mod_ref, wb_ref, wo_ref,
                  lng_ref, lnb_ref, rw_ref, rb_ref, x1_ref, h_ref, gates_ref):
    acc = None
    for s, (o_ref, lg_ref) in enumerate(((og_ref, lg0_ref), (ol_ref, lg1_ref), (od_ref, lg2_ref))):
        proj = _mm(o_ref[...], wb_ref[s])
        term = _sigmoid(lg_ref[...].astype(F32)) * proj
        acc = term if acc is None else acc + term
    m = _mm(acc, wo_ref[...])
    gate1 = mod_ref[:, 2 * D_MODEL:3 * D_MODEL]
    shift2 = mod_ref[:, 3 * D_MODEL:4 * D_MODEL]
    scale2 = mod_ref[:, 4 * D_MODEL:5 * D_MODEL]
    x1 = _layernorm(ALPHA * x_ref[...] + gate1 * m, lng_ref[...], lnb_ref[...])
    x1_ref[...] = x1
    h = x1 * (1.0 + scale2) + shift2
    h_ref[...] = h.astype(BF16)
    gates_t = _route(_mm_hi_nt(rw_ref[...], h), rb_ref[...])
    pad = jnp.zeros((LANE - N_EXPERTS, gates_t.shape[1]), F32)
    gates_ref[...] = jnp.concatenate([gates_t, pad], axis=0).T


def _merge(o_gdn, o_gla, o_diff, pa, x, mod, wb, wo, ln_g, ln_b, rw_t, rb_col, layer, row0,
           tokens_per_row, tm):
    n = x.shape[0]
    mod_row = lambda i: (layer, row0 + (i * tm) // tokens_per_row, 0, 0)
    full = lambda shape: pl.BlockSpec(shape, lambda i: (0,) * len(shape))
    return pl.pallas_call(
        _merge_kernel,
        grid=(n // tm,),
        in_specs=[
            pl.BlockSpec((tm, MIX_W), lambda i: (i, 0)),
            pl.BlockSpec((tm, MIX_W), lambda i: (i, 0)),
            pl.BlockSpec((tm, MIX_W), lambda i: (i, 0)),
            pl.BlockSpec((tm, D_MODEL), lambda i: (i, A_MERGE // D_MODEL)),
            pl.BlockSpec((tm, D_MODEL), lambda i: (i, A_MERGE // D_MODEL + 1)),
            pl.BlockSpec((tm, D_MODEL), lambda i: (i, A_MERGE // D_MODEL + 2)),
            pl.BlockSpec((tm, D_MODEL), lambda i: (i, 0)),
            pl.BlockSpec((None, None, 1, 6 * D_MODEL), mod_row),
            full((N_BRANCH, MIX_W, D_MODEL)),
            full((D_MODEL, D_MODEL)),
            full((1, D_MODEL)),
            full((1, D_MODEL)),
            full((N_EXPERTS, D_MODEL)),
            full((N_EXPERTS, 1)),
        ],
        out_specs=[
            pl.BlockSpec((tm, D_MODEL), lambda i: (i, 0)),
            pl.BlockSpec((tm, D_MODEL), lambda i: (i, 0)),
            pl.BlockSpec((tm, LANE), lambda i: (i, 0)),
        ],
        out_shape=[
            jax.ShapeDtypeStruct((n, D_MODEL), F32),
            jax.ShapeDtypeStruct((n, D_MODEL), BF16),
            jax.ShapeDtypeStruct((n, LANE), F32),
        ],
        compiler_params=_cparams(("arbitrary",)),
    )(o_gdn, o_gla, o_diff, pa, pa, pa, x, mod, wb, wo, ln_g, ln_b, rw_t, rb_col)


def _moe_kernel(h_ref, gates_ref, wg_ref, wu_ref, wd_ref, x1_ref, mod_ref, lng_ref, lnb_ref,
                o_ref, acc):
    e = pl.program_id(1)

    @pl.when(e == 0)
    def _():
        acc[...] = jnp.zeros_like(acc)

    h = h_ref[...]
    act = _silu(_dot(h, wg_ref[...])) * _dot(h, wu_ref[...])
    gate = _pick_lane(gates_ref[...], e)
    acc[...] += _mm(act * gate, wd_ref[...])

    @pl.when(e == N_EXPERTS - 1)
    def _():
        gate2 = mod_ref[:, 5 * D_MODEL:6 * D_MODEL]
        o_ref[...] = _layernorm(ALPHA * x1_ref[...] + gate2 * acc[...], lng_ref[...], lnb_ref[...])


def _moe(h, gates, wg, wu, wd, x1, mod, ln_g, ln_b, layer, row0, tokens_per_row, tm):
    n = h.shape[0]
    mod_row = lambda i, e: (layer, row0 + (i * tm) // tokens_per_row, 0, 0)
    return pl.pallas_call(
        _moe_kernel,
        grid=(n // tm, N_EXPERTS),
        in_specs=[
            pl.BlockSpec((tm, D_MODEL), lambda i, e: (i, 0)),
            pl.BlockSpec((tm, LANE), lambda i, e: (i, 0)),
            pl.BlockSpec((None, D_MODEL, D_FF), lambda i, e: (e, 0, 0)),
            pl.BlockSpec((None, D_MODEL, D_FF), lambda i, e: (e, 0, 0)),
            pl.BlockSpec((None, D_FF, D_MODEL), lambda i, e: (e, 0, 0)),
            pl.BlockSpec((tm, D_MODEL), lambda i, e: (i, 0)),
            pl.BlockSpec((None, None, 1, 6 * D_MODEL), mod_row),
            pl.BlockSpec((1, D_MODEL), lambda i, e: (0, 0)),
            pl.BlockSpec((1, D_MODEL), lambda i, e: (0, 0)),
        ],
        out_specs=pl.BlockSpec((tm, D_MODEL), lambda i, e: (i, 0)),
        out_shape=jax.ShapeDtypeStruct((n, D_MODEL), F32),
        scratch_shapes=[pltpu.VMEM((tm, D_MODEL), F32)],
        compiler_params=_cparams(("arbitrary", "arbitrary")),
    )(h, gates, wg, wu, wd, x1, mod, ln_g, ln_b)


def _reorder_w_in(w):
    sizes = (3 * GDN_W, GDN_W, 2 * GDN_HEADS, 2 * GDN_HEADS, GLA_KW, GLA_KW, GLA_VW, GLA_VW,
             2 * GLA_RANK, DIFF_QW, DIFF_QW, DIFF_VW, N_BRANCH * D_MODEL)
    offs = np.concatenate([[0], np.cumsum(sizes)])
    seg = lambda i: w[:, offs[i]:offs[i + 1]]
    used = 2 * GDN_HEADS * 2 + 2 * GLA_RANK
    wa = jnp.concatenate([seg(0), seg(1), seg(4), seg(5), seg(6), seg(7), seg(9), seg(12)], axis=1)
    wf = jnp.concatenate([seg(10), seg(11), seg(2), seg(3), seg(8),
                          jnp.zeros((w.shape[0], LANE - used), w.dtype)], axis=1)
    return wa.astype(BF16), wf.astype(BF16)


def _lane_row(vals, offset):
    return jnp.zeros((1, LANE), F32).at[0, offset:offset + vals.shape[0]].set(vals)


def kernel(x_prompt, x_sample, c, state_gdn, state_gla, cache_k, cache_v, c_ctx, w_mod, b_mod, w_in,
           gdn_conv, gdn_a_log, gdn_dt_bias, gdn_norm, gla_w_gate, gla_b_gate, gla_norm, diff_lambda,
           diff_norm, w_branch, w_out, ln_g, ln_b, router_w, router_b, exp_w_gate, exp_w_up, exp_w_down):
    bp, tp, d = x_prompt.shape
    bs, ts, _ = x_sample.shape
    pad_rows = MOD_ROWS - 1 - bs
    cvecs = jnp.concatenate([c_ctx[None, :], c, jnp.zeros((pad_rows, d), F32)], axis=0)
    mod = _modulation(cvecs, w_mod, b_mod).reshape(DEPTH, MOD_ROWS, 1, 6 * d)

    rw_t = router_w.T
    rb_col = router_b.reshape(N_EXPERTS, 1)
    ck = cache_k.reshape(bs, DEPTH, cache_k.shape[2], DIFF_QW)
    cv = cache_v.reshape(bs, DEPTH, cache_v.shape[2], DIFF_VW)

    layer_w = []
    for l in range(DEPTH):
        wa, wf = _reorder_w_in(w_in[l])
        layer_w.append(dict(
            wa=wa, wf=wf,
            alog=_lane_row(gdn_a_log[l].reshape(-1), MISC_A),
            dtb=_lane_row(gdn_dt_bias[l].reshape(-1), MISC_A),
            wb=w_branch[l].astype(BF16),
            wo=w_out[l].astype(BF16),
            wg=exp_w_gate[l].astype(BF16),
            wu=exp_w_up[l].astype(BF16),
            wd=exp_w_down[l].astype(BF16),
        ))

    def layer(x, l, row0, tokens_per_row, s_gdn, s_gla, ctx_k, ctx_v, emit_state, tm):
        b, t, _ = x.shape
        n = b * t
        lw = layer_w[l]
        xf = x.reshape(n, d)
        pa, pf = _inproj(xf, mod, lw["wa"], lw["wf"], l, row0, tokens_per_row, tm)
        pa3 = pa.reshape(b, t, A_COLS)
        pf3 = pf.reshape(b, t, F_COLS)
        o_gdn, gdn_fin = _gdn(pa3, pf3, gdn_conv[l], lw["alog"], lw["dtb"], gdn_norm[l][None, :], s_gdn, l,
                              emit_state)
        o_gla, gla_fin = _gla(pa3, pf3, gla_w_gate[l], gla_b_gate[l], gla_norm[l][None, :], s_gla, l,
                              emit_state)
        lam_init = 0.8 - 0.6 * math.exp(-0.3 * l)
        o_diff = _diff(pa3, pf3, diff_lambda[l], diff_norm[l][None, :], ctx_k, ctx_v, l, lam_init)
        x1, h, gates = _merge(o_gdn.reshape(n, MIX_W), o_gla.reshape(n, MIX_W), o_diff.reshape(n, MIX_W),
                              pa, xf, mod, lw["wb"], lw["wo"], ln_g[l, 0][None, :], ln_b[l, 0][None, :],
                              rw_t, rb_col, l, row0, tokens_per_row, tm)
        x2 = _moe(h, gates, lw["wg"], lw["wu"], lw["wd"], x1, mod, ln_g[l, 1][None, :],
                  ln_b[l, 1][None, :], l, row0, tokens_per_row, tm)
        k_own = pf3[:, :, F_DIFF_K:F_DIFF_K + DIFF_QW].reshape(b, t, DIFF_HEADS, 2 * DIFF_DH)
        v_own = pf3[:, :, F_DIFF_V:F_DIFF_V + DIFF_VW].reshape(b, t, DIFF_HEADS, DIFF_VD)
        return x2.reshape(b, t, d), gdn_fin, gla_fin, k_own, v_own

    hp = x_prompt
    gdn_states, gla_states, ks, vs = [], [], [], []
    for l in range(DEPTH):
        hp, s_gdn, s_gla, k_l, v_l = layer(hp, l, 0, bp * tp, None, None, None, None, True, 512)
        gdn_states.append(s_gdn)
        gla_states.append(s_gla)
        ks.append(k_l)
        vs.append(v_l)

    hs = x_sample
    for l in range(DEPTH):
        hs = layer(hs, l, 1, ts, state_gdn, state_gla, ck, cv, False, 512)[0]

    return (hp, hs, jnp.stack(gdn_states, axis=1), jnp.stack(gla_states, axis=1),
            jnp.stack(ks, axis=1), jnp.stack(vs, axis=1))
```

```python
import functools
import math

import numpy as np
import jax
import jax.numpy as jnp
from jax import lax
from jax.experimental import pallas as pl
from jax.experimental.pallas import tpu as pltpu

F32 = jnp.float32
BF16 = jnp.bfloat16

D_MODEL = 1024
DEPTH = 2
GRID_W = 64
GDN_HEADS = 4
GDN_DK = 128
GDN_DV = 128
GDN_W = GDN_HEADS * GDN_DV
SHORT_CONV = 5
GLA_HEADS = 4
GLA_DK = 64
GLA_DV = 128
GLA_KW = GLA_HEADS * GLA_DK
GLA_VW = GLA_HEADS * GLA_DV
GLA_RANK = 16
GLA_TAU = 16.0
CHUNK = 64
DIFF_HEADS = 4
DIFF_DH = 64
DIFF_VD = 2 * DIFF_DH
DIFF_QW = DIFF_HEADS * 2 * DIFF_DH
DIFF_VW = DIFF_HEADS * DIFF_VD
ROPE_THETA = 10000.0
MIX_W = 512
N_BRANCH = 3
N_EXPERTS = 16
N_GROUPS = 4
GROUP_SIZE = N_EXPERTS // N_GROUPS
D_FF = 512
ALPHA = (2 * DEPTH) ** 0.25
EPS = 1e-6

LANE = 128
MOD_ROWS = 8
VMEM_LIMIT = 56 * 1024 * 1024

A_GDN_QKV = 0
A_GDN_Z = 1536
A_GLA_Q = 2048
A_GLA_K = 2304
A_GLA_V = 2560
A_GLA_R = 3072
A_DIFF_Q = 3584
A_MERGE = 4096
A_COLS = 7168
F_DIFF_K = 0
F_DIFF_V = 512
F_MISC = 1024
F_COLS = 1152
MISC_B = 0
MISC_A = 8
MISC_LR = 16


def _cparams(sem):
    return pltpu.CompilerParams(dimension_semantics=sem, vmem_limit_bytes=VMEM_LIMIT)


def _dot(a, b):
    return jnp.dot(a, b, preferred_element_type=F32)


def _mm(a, b):
    return _dot(a.astype(BF16), b.astype(BF16))


def _mm_nt(a, b):
    return lax.dot_general(a.astype(BF16), b.astype(BF16), (((1,), (1,)), ((), ())),
                           preferred_element_type=F32)


def _split3(x):
    hi = x.astype(BF16)
    r = x - hi.astype(F32)
    mid = r.astype(BF16)
    lo = (r - mid.astype(F32)).astype(BF16)
    return hi, mid, lo


def _mm_mask_lhs(mask_bf16, x):
    hi, mid, lo = _split3(x)
    return _dot(mask_bf16, hi) + _dot(mask_bf16, mid) + _dot(mask_bf16, lo)


def _mm_mask_rhs(x, mask_bf16):
    hi, mid, lo = _split3(x)
    return _dot(hi, mask_bf16) + _dot(mid, mask_bf16) + _dot(lo, mask_bf16)


def _mm_hi(a, b):
    ah = a.astype(BF16)
    al = (a - ah.astype(F32)).astype(BF16)
    bh = b.astype(BF16)
    bl = (b - bh.astype(F32)).astype(BF16)
    return _dot(ah, bh) + _dot(ah, bl) + _dot(al, bh)


def _mm_hi_nt(a, b):
    ah = a.astype(BF16)
    al = (a - ah.astype(F32)).astype(BF16)
    bh = b.astype(BF16)
    bl = (b - bh.astype(F32)).astype(BF16)
    dot = lambda x, y: lax.dot_general(x, y, (((1,), (1,)), ((), ())), preferred_element_type=F32)
    return dot(ah, bh) + dot(ah, bl) + dot(al, bh)


def _sigmoid(x):
    return 1.0 / (1.0 + jnp.exp(-x))


def _silu(x):
    return x * _sigmoid(x)


def _softplus(x):
    return jnp.maximum(x, 0.0) + jnp.log1p(jnp.exp(-jnp.abs(x)))


def _rms(x, g):
    return x * lax.rsqrt(jnp.mean(x * x, axis=-1, keepdims=True) + EPS) * g


def _layernorm(y, g, b):
    mu = jnp.mean(y, axis=-1, keepdims=True)
    d = y - mu
    var = jnp.mean(d * d, axis=-1, keepdims=True)
    return d * lax.rsqrt(var + EPS) * g + b


def _pick_lane(x, idx):
    lane = lax.broadcasted_iota(jnp.int32, x.shape, 1)
    return jnp.sum(jnp.where(lane == idx, x, 0.0), axis=1, keepdims=True)


def _tri(n, kind):
    r = lax.broadcasted_iota(jnp.int32, (n, n), 0)
    c = lax.broadcasted_iota(jnp.int32, (n, n), 1)
    return {"ge": r >= c, "gt": r > c, "le": r <= c, "lt": r < c}[kind]


def _mod_kernel(c_ref, w_ref, b_ref, o_ref):
    s = _silu(c_ref[...])
    o_ref[...] = _mm_hi(s, w_ref[...]) + b_ref[...]


def _modulation(cvecs, w_mod, b_mod):
    tn = 1536
    return pl.pallas_call(
        _mod_kernel,
        grid=(DEPTH, 6 * D_MODEL // tn),
        in_specs=[
            pl.BlockSpec((MOD_ROWS, D_MODEL), lambda l, j: (0, 0)),
            pl.BlockSpec((None, D_MODEL, tn), lambda l, j: (l, 0, j)),
            pl.BlockSpec((None, 1, tn), lambda l, j: (l, 0, j)),
        ],
        out_specs=pl.BlockSpec((None, MOD_ROWS, tn), lambda l, j: (l, 0, j)),
        out_shape=jax.ShapeDtypeStruct((DEPTH, MOD_ROWS, 6 * D_MODEL), F32),
        compiler_params=_cparams(("arbitrary", "arbitrary")),
    )(cvecs, w_mod, b_mod.reshape(DEPTH, 1, 6 * D_MODEL))


def _inproj_kernel(x_ref, mod_ref, wa_ref, wf_ref, pa_ref, pf_ref, u_sc):
    @pl.when(pl.program_id(1) == 0)
    def _():
        shift = mod_ref[:, 0:D_MODEL]
        scale = mod_ref[:, D_MODEL:2 * D_MODEL]
        u = (x_ref[...] * (1.0 + scale) + shift).astype(BF16)
        u_sc[...] = u
        pf_ref[...] = _dot(u, wf_ref[...])

    pa_ref[...] = _dot(u_sc[...], wa_ref[...]).astype(BF16)


def _inproj(x, mod, wa, wf, layer, row0, tokens_per_row, tm):
    n = x.shape[0]
    tn = 1792
    mod_row = lambda i, j: (layer, row0 + (i * tm) // tokens_per_row, 0, 0)
    return pl.pallas_call(
        _inproj_kernel,
        grid=(n // tm, A_COLS // tn),
        in_specs=[
            pl.BlockSpec((tm, D_MODEL), lambda i, j: (i, 0)),
            pl.BlockSpec((None, None, 1, 6 * D_MODEL), mod_row),
            pl.BlockSpec((D_MODEL, tn), lambda i, j: (0, j)),
            pl.BlockSpec((D_MODEL, F_COLS), lambda i, j: (0, 0)),
        ],
        out_specs=[
            pl.BlockSpec((tm, tn), lambda i, j: (i, j)),
            pl.BlockSpec((tm, F_COLS), lambda i, j: (i, 0)),
        ],
        out_shape=[
            jax.ShapeDtypeStruct((n, A_COLS), BF16),
            jax.ShapeDtypeStruct((n, F_COLS), F32),
        ],
        scratch_shapes=[pltpu.VMEM((tm, D_MODEL), BF16)],
        compiler_params=_cparams(("arbitrary", "arbitrary")),
    )(x, mod, wa, wf)


def _short_conv(x, w, t_len):
    row = lax.broadcasted_iota(jnp.int32, x.shape, 0)
    half = SHORT_CONV // 2
    acc = x * w[half:half + 1, :]
    for d in range(-half, half + 1):
        if d == 0:
            continue
        shifted = pltpu.roll(x, (-d) % t_len, axis=0)
        valid = jnp.logical_and(row + d >= 0, row + d < t_len)
        acc = acc + jnp.where(valid, shifted, 0.0) * w[half + d:half + d + 1, :]
    return acc


def _solve_unit_tri(a_list, rhs_list):
    n = a_list[0].shape[0]
    eye = jnp.where(_tri(n, "ge") & _tri(n, "le"), 1.0, 0.0)
    a_b = [a.astype(BF16) for a in a_list]
    p_b = [-a for a in a_b]
    inv = [eye - a for a in a_list]
    for _ in range(int(math.log2(n)) - 1):
        p_b = [_dot(p, p).astype(BF16) for p in p_b]
        inv = [x + _dot(x.astype(BF16), p) for x, p in zip(inv, p_b)]
    inv_b = [x.astype(BF16) for x in inv]
    sol = [_dot(x, r.astype(BF16)) for x, r in zip(inv_b, rhs_list)]
    resid = [r - s - _dot(a, s.astype(BF16)) for r, s, a in zip(rhs_list, sol, a_b)]
    return [s + _dot(x, r.astype(BF16)) for s, x, r in zip(sol, inv_b, resid)]


def _gdn_kernel(*refs, t_len, has_state, emit_state):
    it = iter(refs)
    qkv_ref, z_ref, misc_ref, cw_ref, alog_ref, dtb_ref, g_ref = (next(it) for _ in range(7))
    s0_ref = next(it) if has_state else None
    o_ref = next(it)
    sfin_ref = next(it) if emit_state else None
    (q_s, k_s, v_s, kt_s, bet, gat, gat_t, u_s, w_s, qd_s, a_s, ktl_s, gts, o_f, o_b,
     s_s) = (next(it) for _ in range(16))

    c = CHUNK
    n_chunks = t_len // c
    nh = GDN_HEADS
    dk = GDN_DK

    for h in range(nh):
        hs = slice(h * dk, (h + 1) * dk)
        q, k, v = (
            _silu(_short_conv(qkv_ref[:, j * GDN_W + h * dk:j * GDN_W + (h + 1) * dk].astype(F32),
                              cw_ref[:, j * GDN_W + h * dk:j * GDN_W + (h + 1) * dk], t_len))
            for j in range(3))
        q = q * lax.rsqrt(jnp.sum(q * q, axis=-1, keepdims=True) + EPS) * (dk ** -0.5)
        k = k * lax.rsqrt(jnp.sum(k * k, axis=-1, keepdims=True) + EPS)
        q_s[:, hs] = q
        k_s[:, hs] = k
        v_s[:, hs] = v
        kt_s[hs, :] = k.T.astype(BF16)
    misc = misc_ref[...]
    bet[...] = _sigmoid(misc)
    g_all = -jnp.exp(alog_ref[...]) * _softplus(misc + dtb_ref[...])
    gat[...] = g_all
    gat_t[...] = g_all.T[MISC_A:MISC_A + 2 * nh, :]
    for d in range(2):
        for h in range(nh):
            s_s[d * nh + h] = s0_ref[d, h] if has_state else jnp.zeros((dk, GDN_DV), F32)

    two = 2 * c
    r2 = lax.broadcasted_iota(jnp.int32, (two, two), 0)
    c2 = lax.broadcasted_iota(jnp.int32, (two, two), 1)
    same = (r2 >= c) == (c2 >= c)
    lo2 = jnp.where(jnp.logical_and(same, r2 >= c2), 1.0, 0.0).astype(BF16)
    up2 = jnp.where(jnp.logical_and(same, r2 <= c2), 1.0, 0.0).astype(BF16)
    incl = (_tri(c, "ge"), _tri(c, "le"))
    strict = (_tri(c, "gt"), _tri(c, "lt"))
    last = (c - 1, 0)

    def phase1(cp, carry):
        r0 = pl.multiple_of(cp * two, two)
        g_blk = gat[pl.ds(r0, two), :]
        b_blk = bet[pl.ds(r0, two), :]
        gt_blk = gat_t[:, pl.ds(r0, two)]
        cum_col = (_mm_mask_lhs(lo2, g_blk), _mm_mask_lhs(up2, g_blk))
        cum_row = (_mm_mask_rhs(gt_blk, up2), _mm_mask_rhs(gt_blk, lo2))
        chains, a_list, rhs_list = [], [], []
        for h in range(nh):
            hs = slice(h * dk, (h + 1) * dk)
            kt_pair = kt_s[hs, pl.ds(r0, two)]
            for s in range(2):
                rows = pl.ds(pl.multiple_of(r0 + s * c, c), c)
                q_c = q_s[rows, hs]
                k_c = k_s[rows, hs]
                v_c = v_s[rows, hs]
                kt_c = kt_pair[:, s * c:(s + 1) * c]
                kk = _dot(k_c.astype(BF16), kt_c)
                qk = _dot(q_c.astype(BF16), kt_c)
                for d in range(2):
                    idx = d * nh + h
                    gc = cum_col[d][s * c:(s + 1) * c, MISC_A + idx:MISC_A + idx + 1]
                    gcr = cum_row[d][idx:idx + 1, s * c:(s + 1) * c]
                    beta = b_blk[s * c:(s + 1) * c, MISC_B + idx:MISC_B + idx + 1]
                    decay = jnp.where(incl[d], jnp.exp(jnp.where(incl[d], gc - gcr, 0.0)), 0.0)
                    e_gc = jnp.exp(gc)
                    a_list.append(jnp.where(strict[d], beta * kk * decay, 0.0))
                    rhs_list.append(jnp.concatenate([v_c * beta, k_c * (beta * e_gc)], axis=1))
                    a_s[idx, rows, :] = jnp.where(incl[d], qk * decay, 0.0).astype(BF16)
                    qd_s[idx, rows, :] = (q_c * e_gc).astype(BF16)
                    g_last = gcr[:, last[d]:last[d] + 1]
                    ktl_s[idx, cp * 2 + s] = (kt_c.astype(F32) * jnp.exp(g_last - gcr)).astype(BF16)
                    gts[cp * 2 + s, idx:idx + 1, :] = jnp.broadcast_to(jnp.exp(g_last), (1, LANE))
                    chains.append((idx, rows))
        for (idx, rows), sol in zip(chains, _solve_unit_tri(a_list, rhs_list)):
            u_s[idx, rows, :] = sol[:, :GDN_DV]
            w_s[idx, rows, :] = sol[:, GDN_DV:].astype(BF16)
        return carry

    lax.fori_loop(0, n_chunks // 2, phase1, 0)

    def phase2(i, carry):
        chains = []
        for d, out in ((0, o_f), (1, o_b)):
            ci = i if d == 0 else n_chunks - 1 - i
            rows = pl.ds(pl.multiple_of(ci * c, c), c)
            g_tail = gts[ci]
            for h in range(nh):
                chains.append((d * nh + h, h, ci, rows, out, g_tail))
        s_f32 = [s_s[idx] for idx, *_ in chains]
        s_b = [s.astype(BF16) for s in s_f32]
        v_b = [(u_s[idx, rows, :] - _dot(w_s[idx, rows, :], sb)).astype(BF16)
               for (idx, _, _, rows, _, _), sb in zip(chains, s_b)]
        for (idx, h, ci, rows, out, g_tail), s, sb, vb in zip(chains, s_f32, s_b, v_b):
            s_s[idx] = s * g_tail[idx:idx + 1, :] + _dot(ktl_s[idx, ci], vb)
            out[rows, h * GDN_DV:(h + 1) * GDN_DV] = _dot(qd_s[idx, rows, :], sb) + _dot(
                a_s[idx, rows, :], vb)
        return carry

    lax.fori_loop(0, n_chunks, phase2, 0)

    for h in range(nh):
        hs = slice(h * GDN_DV, (h + 1) * GDN_DV)
        o = o_f[:, hs] + o_b[:, hs]
        o_ref[:, hs] = _rms(o, g_ref[...]) * _silu(z_ref[:, hs].astype(F32))
    if emit_state:
        for d in range(2):
            for h in range(nh):
                sfin_ref[d, h] = s_s[d * nh + h]


def _gdn(pa3, pf3, conv_w, alog_row, dtb_row, norm_g, s0, layer, emit_state):
    b, t, _ = pa3.shape
    full = lambda shape: pl.BlockSpec(shape, lambda bi: (0,) * len(shape))
    in_specs = [
        pl.BlockSpec((None, t, 3 * GDN_W), lambda bi: (bi, 0, A_GDN_QKV // (3 * GDN_W))),
        pl.BlockSpec((None, t, GDN_W), lambda bi: (bi, 0, A_GDN_Z // GDN_W)),
        pl.BlockSpec((None, t, LANE), lambda bi: (bi, 0, F_MISC // LANE)),
        full((SHORT_CONV, 3 * GDN_W)),
        full((1, LANE)),
        full((1, LANE)),
        full((1, GDN_DV)),
    ]
    args = [pa3, pa3, pf3, conv_w, alog_row, dtb_row, norm_g]
    if s0 is not None:
        in_specs.append(pl.BlockSpec((None, None, 2, GDN_HEADS, GDN_DK, GDN_DV),
                                     lambda bi: (bi, layer, 0, 0, 0, 0)))
        args.append(s0)
    out_specs = [pl.BlockSpec((None, t, GDN_W), lambda bi: (bi, 0, 0))]
    out_shape = [jax.ShapeDtypeStruct((b, t, GDN_W), F32)]
    if emit_state:
        out_specs.append(pl.BlockSpec((None, 2, GDN_HEADS, GDN_DK, GDN_DV), lambda bi: (bi, 0, 0, 0, 0)))
        out_shape.append(jax.ShapeDtypeStruct((b, 2, GDN_HEADS, GDN_DK, GDN_DV), F32))
    nc = t // CHUNK
    nd = 2 * GDN_HEADS
    wide = pltpu.VMEM((t, GDN_W), F32)
    scratch = [
        wide, wide, wide,
        pltpu.VMEM((GDN_W, t), BF16),
        pltpu.VMEM((t, LANE), F32), pltpu.VMEM((t, LANE), F32),
        pltpu.VMEM((nd, t), F32),
        pltpu.VMEM((nd, t, GDN_DV), F32),
        pltpu.VMEM((nd, t, GDN_DK), BF16),
        pltpu.VMEM((nd, t, GDN_DK), BF16),
        pltpu.VMEM((nd, t, CHUNK), BF16),
        pltpu.VMEM((nd, nc, GDN_DK, CHUNK), BF16),
        pltpu.VMEM((nc, nd, LANE), F32),
        wide, wide,
        pltpu.VMEM((nd, GDN_DK, GDN_DV), F32),
    ]
    outs = pl.pallas_call(
        functools.partial(_gdn_kernel, t_len=t, has_state=s0 is not None, emit_state=emit_state),
        grid=(b,),
        in_specs=in_specs,
        out_specs=out_specs,
        out_shape=out_shape,
        scratch_shapes=scratch,
        compiler_params=_cparams(("arbitrary",)),
    )(*args)
    return outs if emit_state else (outs[0], None)


def _gla_kernel(*refs, t_len, has_state, emit_state):
    it = iter(refs)
    q_ref, k_ref, v_ref, r_ref, misc_ref, wg_ref, bg_ref, g_ref = (next(it) for _ in range(8))
    s0_ref = next(it) if has_state else None
    o_ref = next(it)
    sfin_ref = next(it) if emit_state else None
    glog, vt_s, o_f, o_b, st = (next(it) for _ in range(5))

    c = CHUNK
    two = 2 * c
    n_pairs = t_len // two
    nh = GLA_HEADS
    misc = misc_ref[...]
    for d in range(2):
        lr = misc[:, MISC_LR + d * GLA_RANK:MISC_LR + (d + 1) * GLA_RANK]
        logits = _mm_hi(lr, wg_ref[d]) + bg_ref[d:d + 1, :]
        glog[d] = -_softplus(-logits) / GLA_TAU
    for h in range(nh):
        hs = slice(h * GLA_DV, (h + 1) * GLA_DV)
        vt_s[hs, :] = v_ref[:, hs].astype(F32).T.astype(BF16)
    for d in range(2):
        for h in range(nh):
            if has_state:
                s0 = jnp.concatenate([s0_ref[d, h], jnp.zeros((GLA_DV - GLA_DK, GLA_DV), F32)], axis=0)
                st[d * nh + h] = s0.T[:, :GLA_DK]
            else:
                st[d * nh + h] = jnp.zeros((GLA_DV, GLA_DK), F32)

    r2 = lax.broadcasted_iota(jnp.int32, (two, two), 0)
    c2 = lax.broadcasted_iota(jnp.int32, (two, two), 1)
    same = (r2 >= c) == (c2 >= c)
    cum_mask = (jnp.where(jnp.logical_and(same, r2 >= c2), 1.0, 0.0).astype(BF16),
                jnp.where(jnp.logical_and(same, r2 <= c2), 1.0, 0.0).astype(BF16))
    incl = (_tri(c, "ge"), _tri(c, "le"))
    last = (c - 1, 0)
    mid = (c // 2, c - 1 - c // 2)
    order = ((0, 1), (1, 0))

    def body(i, carry):
        pieces = []
        for d in range(2):
            r0 = pl.multiple_of((i if d == 0 else n_pairs - 1 - i) * two, two)
            b_all = _mm_mask_lhs(cum_mask[d], glog[d, pl.ds(r0, two), :])
            q_all = q_ref[pl.ds(r0, two), :].astype(F32) * (GLA_DK ** -0.5)
            k_all = k_ref[pl.ds(r0, two), :].astype(F32)
            vt_pair = [vt_s[h * GLA_DV:(h + 1) * GLA_DV, pl.ds(r0, two)] for h in range(nh)]
            for s in range(2):
                rs = slice(s * c, (s + 1) * c)
                bq = b_all[rs]
                b_mid = bq[mid[d]:mid[d] + 1, :]
                b_last = bq[last[d]:last[d] + 1, :]
                qe = (q_all[rs] * jnp.exp(bq - b_mid)).astype(BF16)
                ke = (k_all[rs] * jnp.exp(b_mid - bq)).astype(BF16)
                qd = (q_all[rs] * jnp.exp(bq)).astype(BF16)
                kt = (k_all[rs] * jnp.exp(b_last - bq)).astype(BF16)
                g_last = jnp.exp(b_last)
                rows = pl.ds(pl.multiple_of(r0 + s * c, c), c)
                for h in range(nh):
                    ks = slice(h * GLA_DK, (h + 1) * GLA_DK)
                    pieces.append((d, s, h, rows, qe[:, ks], ke[:, ks], qd[:, ks], kt[:, ks], g_last[:, ks],
                                   vt_pair[h][:, rs], v_ref[rows, h * GLA_DV:(h + 1) * GLA_DV]))
        a_intra = [jnp.where(incl[p[0]], _mm_nt(p[4], p[5]), 0.0).astype(BF16) for p in pieces]
        upd = [_dot(p[9], p[7]) for p in pieces]
        intra = [_dot(a, p[10]) for a, p in zip(a_intra, pieces)]
        for step in range(2):
            for p, m, o_in in zip(pieces, upd, intra):
                d, s, h, rows = p[0], p[1], p[2], p[3]
                if s != order[d][step]:
                    continue
                s_t = st[d * nh + h]
                out = o_f if d == 0 else o_b
                out[rows, h * GLA_DV:(h + 1) * GLA_DV] = _mm_nt(p[6], s_t) + o_in
                st[d * nh + h] = s_t * p[8] + m
        return carry

    lax.fori_loop(0, n_pairs, body, 0)

    for h in range(nh):
        hs = slice(h * GLA_DV, (h + 1) * GLA_DV)
        o = o_f[:, hs] + o_b[:, hs]
        o_ref[:, hs] = _rms(o, g_ref[...]) * _silu(r_ref[:, hs].astype(F32))
    if emit_state:
        for d in range(2):
            for h in range(nh):
                s_pad = jnp.concatenate([st[d * nh + h], jnp.zeros((GLA_DV, GLA_DV - GLA_DK), F32)], axis=1)
                sfin_ref[d, h] = s_pad.T[:GLA_DK, :]


def _gla(pa3, pf3, wg, bg, norm_g, s0, layer, emit_state):
    b, t, _ = pa3.shape
    full = lambda shape: pl.BlockSpec(shape, lambda bi: (0,) * len(shape))
    in_specs = [
        pl.BlockSpec((None, t, GLA_KW), lambda bi: (bi, 0, A_GLA_Q // GLA_KW)),
        pl.BlockSpec((None, t, GLA_KW), lambda bi: (bi, 0, A_GLA_K // GLA_KW)),
        pl.BlockSpec((None, t, GLA_VW), lambda bi: (bi, 0, A_GLA_V // GLA_VW)),
        pl.BlockSpec((None, t, GLA_VW), lambda bi: (bi, 0, A_GLA_R // GLA_VW)),
        pl.BlockSpec((None, t, LANE), lambda bi: (bi, 0, F_MISC // LANE)),
        full((2, GLA_RANK, GLA_KW)),
        full((2, GLA_KW)),
        full((1, GLA_DV)),
    ]
    args = [pa3, pa3, pa3, pa3, pf3, wg, bg, norm_g]
    if s0 is not None:
        in_specs.append(pl.BlockSpec((None, None, 2, GLA_HEADS, GLA_DK, GLA_DV),
                                     lambda bi: (bi, layer, 0, 0, 0, 0)))
        args.append(s0)
    out_specs = [pl.BlockSpec((None, t, GLA_VW), lambda bi: (bi, 0, 0))]
    out_shape = [jax.ShapeDtypeStruct((b, t, GLA_VW), F32)]
    if emit_state:
        out_specs.append(pl.BlockSpec((None, 2, GLA_HEADS, GLA_DK, GLA_DV), lambda bi: (bi, 0, 0, 0, 0)))
        out_shape.append(jax.ShapeDtypeStruct((b, 2, GLA_HEADS, GLA_DK, GLA_DV), F32))
    scratch = [pltpu.VMEM((2, t, GLA_KW), F32),
               pltpu.VMEM((GLA_VW, t), BF16),
               pltpu.VMEM((t, GLA_VW), F32), pltpu.VMEM((t, GLA_VW), F32),
               pltpu.VMEM((2 * GLA_HEADS, GLA_DV, GLA_DK), F32)]
    outs = pl.pallas_call(
        functools.partial(_gla_kernel, t_len=t, has_state=s0 is not None, emit_state=emit_state),
        grid=(b,),
        in_specs=in_specs,
        out_specs=out_specs,
        out_shape=out_shape,
        scratch_shapes=scratch,
        compiler_params=_cparams(("arbitrary",)),
    )(*args)
    return outs if emit_state else (outs[0], None)


ATT_Q_BLOCK = 256


def _rope_tables(t_len):
    half = DIFF_DH // 2
    quarter = half // 2
    inv = ROPE_THETA ** (-np.arange(0, half, 2, dtype=np.float64) / half)
    tok = np.arange(t_len)
    pos = np.stack([tok // GRID_W, tok % GRID_W], axis=1).astype(np.float64)
    ang = pos[:, :, None] * inv[None, None, :]
    cos = np.concatenate([np.cos(ang), np.cos(ang)], axis=-1).reshape(t_len, DIFF_DH)
    sin = np.concatenate([-np.sin(ang), np.sin(ang)], axis=-1).reshape(t_len, DIFF_DH)
    cos = np.concatenate([cos, cos], axis=-1).astype(np.float32)
    sin = np.concatenate([sin, sin], axis=-1).astype(np.float32)
    first = ((np.arange(2 * DIFF_DH) % half) < quarter).astype(np.float32)[None, :]
    return jnp.asarray(cos), jnp.asarray(sin), jnp.asarray(first), quarter


def _rope(x, cos, sin, first, quarter):
    width = x.shape[-1]
    ahead = pltpu.roll(x, width - quarter, axis=1)
    behind = pltpu.roll(x, quarter, axis=1)
    partner = jnp.where(first > 0.5, ahead, behind)
    return x * cos + partner * sin


def _diff_kernel(*refs, t_len, ctx_len, lam_init):
    it = iter(refs)
    q_ref, k_ref, v_ref, lam_ref, g_ref = (next(it) for _ in range(5))
    if ctx_len:
        ck_ref, cv_ref, cos_ref, sin_ref, first_ref = (next(it) for _ in range(5))
    o_ref = next(it)
    q_sc, k_sc, v_sc = (next(it) for _ in range(3))

    scale = DIFF_DH ** -0.5
    q = q_ref[...].astype(F32)
    k = k_ref[...]
    if ctx_len:
        quarter = DIFF_DH // 4
        q = _rope(q, cos_ref[...], sin_ref[...], first_ref[...], quarter)
        k = _rope(k, cos_ref[...], sin_ref[...], first_ref[...], quarter)
        k_sc[t_len:t_len + ctx_len, :] = ck_ref[...].astype(BF16)
        v_sc[t_len:t_len + ctx_len, :] = cv_ref[...].astype(BF16)
    q_sc[...] = (q * scale).astype(BF16)
    k_sc[0:t_len, :] = k.astype(BF16)
    v_sc[0:t_len, :] = v_ref[...].astype(BF16)

    lp = lam_ref[...]
    lam = (jnp.exp(jnp.sum(lp[0:1, :] * lp[1:2, :], axis=1, keepdims=True))
           - jnp.exp(jnp.sum(lp[2:3, :] * lp[3:4, :], axis=1, keepdims=True)) + lam_init)

    def softmax(s):
        e = jnp.exp(s - jnp.max(s, axis=-1, keepdims=True))
        return e / jnp.sum(e, axis=-1, keepdims=True)

    tq = min(ATT_Q_BLOCK, t_len)

    def body(i, carry):
        r0 = pl.multiple_of(i * tq, tq)
        qb = q_sc[pl.ds(r0, tq), :]
        kk = k_sc[...]
        p1 = softmax(_mm_nt(qb[:, :DIFF_DH], kk[:, :DIFF_DH]))
        p2 = softmax(_mm_nt(qb[:, DIFF_DH:], kk[:, DIFF_DH:]))
        o = _mm(p1 - lam * p2, v_sc[...])
        o_ref[pl.ds(r0, tq), :] = _rms(o, g_ref[...]) * (1.0 - lam_init)
        return carry

    lax.fori_loop(0, t_len // tq, body, 0)


def _diff(pa3, pf3, lam_p, norm_g, ctx_k, ctx_v, layer, lam_init):
    b, t, _ = pa3.shape
    ctx_len = 0 if ctx_k is None else ctx_k.shape[2]
    col = lambda base: (lambda bi, h: (bi, 0, base // LANE + h))
    in_specs = [
        pl.BlockSpec((None, t, LANE), col(A_DIFF_Q)),
        pl.BlockSpec((None, t, LANE), col(F_DIFF_K)),
        pl.BlockSpec((None, t, LANE), col(F_DIFF_V)),
        pl.BlockSpec((4, DIFF_DH), lambda bi, h: (0, 0)),
        pl.BlockSpec((1, DIFF_VD), lambda bi, h: (0, 0)),
    ]
    args = [pa3, pf3, pf3, lam_p, norm_g]
    if ctx_len:
        cos, sin, first, _ = _rope_tables(t)
        in_specs += [
            pl.BlockSpec((None, None, ctx_len, LANE), lambda bi, h: (bi, layer, 0, h)),
            pl.BlockSpec((None, None, ctx_len, LANE), lambda bi, h: (bi, layer, 0, h)),
            pl.BlockSpec((t, LANE), lambda bi, h: (0, 0)),
            pl.BlockSpec((t, LANE), lambda bi, h: (0, 0)),
            pl.BlockSpec((1, LANE), lambda bi, h: (0, 0)),
        ]
        args += [ctx_k, ctx_v, cos, sin, first]
    tk = t + ctx_len
    return pl.pallas_call(
        functools.partial(_diff_kernel, t_len=t, ctx_len=ctx_len, lam_init=lam_init),
        grid=(b, DIFF_HEADS),
        in_specs=in_specs,
        out_specs=pl.BlockSpec((None, t, DIFF_VD), lambda bi, h: (bi, 0, h)),
        out_shape=jax.ShapeDtypeStruct((b, t, DIFF_VW), F32),
        scratch_shapes=[pltpu.VMEM((t, LANE), BF16), pltpu.VMEM((tk, LANE), BF16),
                        pltpu.VMEM((tk, LANE), BF16)],
        compiler_params=_cparams(("arbitrary", "arbitrary")),
    )(*args)


def _route(logits_t, bias_col):
    scores = _sigmoid(logits_t)
    biased = scores + bias_col
    rows = [biased[e:e + 1, :] for e in range(N_EXPERTS)]
    grp = []
    for g in range(N_GROUPS):
        a0, a1, a2, a3 = rows[g * GROUP_SIZE:(g + 1) * GROUP_SIZE]
        hi01, lo01 = jnp.maximum(a0, a1), jnp.minimum(a0, a1)
        hi23, lo23 = jnp.maximum(a2, a3), jnp.minimum(a2, a3)
        top1 = jnp.maximum(hi01, hi23)
        top2 = jnp.maximum(jnp.minimum(hi01, hi23), jnp.maximum(lo01, lo23))
        grp.append(top1 + top2)
    best = []
    for g in range(N_GROUPS):
        win = None
        for o in range(N_GROUPS):
            if o == g:
                continue
            cond = grp[g] > grp[o] if o < g else grp[g] >= grp[o]
            win = cond if win is None else jnp.logical_and(win, cond)
        best.append(win)
    sel_rows = []
    for e in range(N_EXPERTS):
        g = e // GROUP_SIZE
        beaten = jnp.zeros_like(rows[e])
        for o in range(g * GROUP_SIZE, (g + 1) * GROUP_SIZE):
            if o == e:
                continue
            ahead = rows[o] >= rows[e] if o < e else rows[o] > rows[e]
            beaten = beaten + jnp.where(ahead, 1.0, 0.0)
        sel = jnp.logical_and(best[g], beaten < 1.5)
        sel_rows.append(jnp.where(sel, scores[e:e + 1, :], 0.0))
    picked = jnp.concatenate(sel_rows, axis=0)
    return picked / jnp.sum(picked, axis=0, keepdims=True)


def _merge_kernel(og_ref, ol_ref, od_ref, lg0_ref, lg1_ref, lg2_ref, x_ref, mod_ref, wb_ref, wo_ref,
                  lng_ref, lnb_ref, rw_ref, rb_ref, x1_ref, h_ref, gates_ref):
    acc = None
    for s, (o_ref, lg_ref) in enumerate(((og_ref, lg0_ref), (ol_ref, lg1_ref), (od_ref, lg2_ref))):
        proj = _mm(o_ref[...], wb_ref[s])
        term = _sigmoid(lg_ref[...].astype(F32)) * proj
        acc = term if acc is None else acc + term
    m = _mm(acc, wo_ref[...])
    gate1 = mod_ref[:, 2 * D_MODEL:3 * D_MODEL]
    shift2 = mod_ref[:, 3 * D_MODEL:4 * D_MODEL]
    scale2 = mod_ref[:, 4 * D_MODEL:5 * D_MODEL]
    x1 = _layernorm(ALPHA * x_ref[...] + gate1 * m, lng_ref[...], lnb_ref[...])
    x1_ref[...] = x1
    h = x1 * (1.0 + scale2) + shift2
    h_ref[...] = h.astype(BF16)
    gates_t = _route(_mm_hi_nt(rw_ref[...], h), rb_ref[...])
    pad = jnp.zeros((LANE - N_EXPERTS, gates_t.shape[1]), F32)
    gates_ref[...] = jnp.concatenate([gates_t, pad], axis=0).T


def _merge(o_gdn, o_gla, o_diff, pa, x, mod, wb, wo, ln_g, ln_b, rw_t, rb_col, layer, row0,
           tokens_per_row, tm):
    n = x.shape[0]
    mod_row = lambda i: (layer, row0 + (i * tm) // tokens_per_row, 0, 0)
    full = lambda shape: pl.BlockSpec(shape, lambda i: (0,) * len(shape))
    return pl.pallas_call(
        _merge_kernel,
        grid=(n // tm,),
        in_specs=[
            pl.BlockSpec((tm, MIX_W), lambda i: (i, 0)),
            pl.BlockSpec((tm, MIX_W), lambda i: (i, 0)),
            pl.BlockSpec((tm, MIX_W), lambda i: (i, 0)),
            pl.BlockSpec((tm, D_MODEL), lambda i: (i, A_MERGE // D_MODEL)),
            pl.BlockSpec((tm, D_MODEL), lambda i: (i, A_MERGE // D_MODEL + 1)),
            pl.BlockSpec((tm, D_MODEL), lambda i: (i, A_MERGE // D_MODEL + 2)),
            pl.BlockSpec((tm, D_MODEL), lambda i: (i, 0)),
            pl.BlockSpec((None, None, 1, 6 * D_MODEL), mod_row),
            full((N_BRANCH, MIX_W, D_MODEL)),
            full((D_MODEL, D_MODEL)),
            full((1, D_MODEL)),
            full((1, D_MODEL)),
            full((N_EXPERTS, D_MODEL)),
            full((N_EXPERTS, 1)),
        ],
        out_specs=[
            pl.BlockSpec((tm, D_MODEL), lambda i: (i, 0)),
            pl.BlockSpec((tm, D_MODEL), lambda i: (i, 0)),
            pl.BlockSpec((tm, LANE), lambda i: (i, 0)),
        ],
        out_shape=[
            jax.ShapeDtypeStruct((n, D_MODEL), F32),
            jax.ShapeDtypeStruct((n, D_MODEL), BF16),
            jax.ShapeDtypeStruct((n, LANE), F32),
        ],
        compiler_params=_cparams(("arbitrary",)),
    )(o_gdn, o_gla, o_diff, pa, pa, pa, x, mod, wb, wo, ln_g, ln_b, rw_t, rb_col)


def _moe_kernel(h_ref, gates_ref, wg_ref, wu_ref, wd_ref, x1_ref, mod_ref, lng_ref, lnb_ref,
                o_ref, acc):
    e = pl.program_id(1)

    @pl.when(e == 0)
    def _():
        acc[...] = jnp.zeros_like(acc)

    h = h_ref[...]
    act = _silu(_dot(h, wg_ref[...])) * _dot(h, wu_ref[...])
    gate = _pick_lane(gates_ref[...], e)
    acc[...] += _mm(act * gate, wd_ref[...])

    @pl.when(e == N_EXPERTS - 1)
    def _():
        gate2 = mod_ref[:, 5 * D_MODEL:6 * D_MODEL]
        o_ref[...] = _layernorm(ALPHA * x1_ref[...] + gate2 * acc[...], lng_ref[...], lnb_ref[...])


def _moe(h, gates, wg, wu, wd, x1, mod, ln_g, ln_b, layer, row0, tokens_per_row, tm):
    n = h.shape[0]
    mod_row = lambda i, e: (layer, row0 + (i * tm) // tokens_per_row, 0, 0)
    return pl.pallas_call(
        _moe_kernel,
        grid=(n // tm, N_EXPERTS),
        in_specs=[
            pl.BlockSpec((tm, D_MODEL), lambda i, e: (i, 0)),
            pl.BlockSpec((tm, LANE), lambda i, e: (i, 0)),
            pl.BlockSpec((None, D_MODEL, D_FF), lambda i, e: (e, 0, 0)),
            pl.BlockSpec((None, D_MODEL, D_FF), lambda i, e: (e, 0, 0)),
            pl.BlockSpec((None, D_FF, D_MODEL), lambda i, e: (e, 0, 0)),
            pl.BlockSpec((tm, D_MODEL), lambda i, e: (i, 0)),
            pl.BlockSpec((None, None, 1, 6 * D_MODEL), mod_row),
            pl.BlockSpec((1, D_MODEL), lambda i, e: (0, 0)),
            pl.BlockSpec((1, D_MODEL), lambda i, e: (0, 0)),
        ],
        out_specs=pl.BlockSpec((tm, D_MODEL), lambda i, e: (i, 0)),
        out_shape=jax.ShapeDtypeStruct((n, D_MODEL), F32),
        scratch_shapes=[pltpu.VMEM((tm, D_MODEL), F32)],
        compiler_params=_cparams(("arbitrary", "arbitrary")),
    )(h, gates, wg, wu, wd, x1, mod, ln_g, ln_b)


def _reorder_w_in(w):
    sizes = (3 * GDN_W, GDN_W, 2 * GDN_HEADS, 2 * GDN_HEADS, GLA_KW, GLA_KW, GLA_VW, GLA_VW,
             2 * GLA_RANK, DIFF_QW, DIFF_QW, DIFF_VW, N_BRANCH * D_MODEL)
    offs = np.concatenate([[0], np.cumsum(sizes)])
    seg = lambda i: w[:, offs[i]:offs[i + 1]]
    used = 2 * GDN_HEADS * 2 + 2 * GLA_RANK
    wa = jnp.concatenate([seg(0), seg(1), seg(4), seg(5), seg(6), seg(7), seg(9), seg(12)], axis=1)
    wf = jnp.concatenate([seg(10), seg(11), seg(2), seg(3), seg(8),
                          jnp.zeros((w.shape[0], LANE - used), w.dtype)], axis=1)
    return wa.astype(BF16), wf.astype(BF16)


def _lane_row(vals, offset):
    return jnp.zeros((1, LANE), F32).at[0, offset:offset + vals.shape[0]].set(vals)


def kernel(x_prompt, x_sample, c, state_gdn, state_gla, cache_k, cache_v, c_ctx, w_mod, b_mod, w_in,
           gdn_conv, gdn_a_log, gdn_dt_bias, gdn_norm, gla_w_gate, gla_b_gate, gla_norm, diff_lambda,
           diff_norm, w_branch, w_out, ln_g, ln_b, router_w, router_b, exp_w_gate, exp_w_up, exp_w_down):
    bp, tp, d = x_prompt.shape
    bs, ts, _ = x_sample.shape
    pad_rows = MOD_ROWS - 1 - bs
    cvecs = jnp.concatenate([c_ctx[None, :], c, jnp.zeros((pad_rows, d), F32)], axis=0)
    mod = _modulation(cvecs, w_mod, b_mod).reshape(DEPTH, MOD_ROWS, 1, 6 * d)

    rw_t = router_w.T
    rb_col = router_b.reshape(N_EXPERTS, 1)
    ck = cache_k.reshape(bs, DEPTH, cache_k.shape[2], DIFF_QW)
    cv = cache_v.reshape(bs, DEPTH, cache_v.shape[2], DIFF_VW)

    layer_w = []
    for l in range(DEPTH):
        wa, wf = _reorder_w_in(w_in[l])
        layer_w.append(dict(
            wa=wa, wf=wf,
            alog=_lane_row(gdn_a_log[l].reshape(-1), MISC_A),
            dtb=_lane_row(gdn_dt_bias[l].reshape(-1), MISC_A),
            wb=w_branch[l].astype(BF16),
            wo=w_out[l].astype(BF16),
            wg=exp_w_gate[l].astype(BF16),
            wu=exp_w_up[l].astype(BF16),
            wd=exp_w_down[l].astype(BF16),
        ))

    def layer(x, l, row0, tokens_per_row, s_gdn, s_gla, ctx_k, ctx_v, emit_state, tm):
        b, t, _ = x.shape
        n = b * t
        lw = layer_w[l]
        xf = x.reshape(n, d)
        pa, pf = _inproj(xf, mod, lw["wa"], lw["wf"], l, row0, tokens_per_row, tm)
        pa3 = pa.reshape(b, t, A_COLS)
        pf3 = pf.reshape(b, t, F_COLS)
        o_gdn, gdn_fin = _gdn(pa3, pf3, gdn_conv[l], lw["alog"], lw["dtb"], gdn_norm[l][None, :], s_gdn, l,
                              emit_state)
        o_gla, gla_fin = _gla(pa3, pf3, gla_w_gate[l], gla_b_gate[l], gla_norm[l][None, :], s_gla, l,
                              emit_state)
        lam_init = 0.8 - 0.6 * math.exp(-0.3 * l)
        o_diff = _diff(pa3, pf3, diff_lambda[l], diff_norm[l][None, :], ctx_k, ctx_v, l, lam_init)
        x1, h, gates = _merge(o_gdn.reshape(n, MIX_W), o_gla.reshape(n, MIX_W), o_diff.reshape(n, MIX_W),
                              pa, xf, mod, lw["wb"], lw["wo"], ln_g[l, 0][None, :], ln_b[l, 0][None, :],
                              rw_t, rb_col, l, row0, tokens_per_row, tm)
        x2 = _moe(h, gates, lw["wg"], lw["wu"], lw["wd"], x1, mod, ln_g[l, 1][None, :],
                  ln_b[l, 1][None, :], l, row0, tokens_per_row, tm)
        k_own = pf3[:, :, F_DIFF_K:F_DIFF_K + DIFF_QW].reshape(b, t, DIFF_HEADS, 2 * DIFF_DH)
        v_own = pf3[:, :, F_DIFF_V:F_DIFF_V + DIFF_VW].reshape(b, t, DIFF_HEADS, DIFF_VD)
        return x2.reshape(b, t, d), gdn_fin, gla_fin, k_own, v_own

    hp = x_prompt
    gdn_states, gla_states, ks, vs = [], [], [], []
    for l in range(DEPTH):
        hp, s_gdn, s_gla, k_l, v_l = layer(hp, l, 0, bp * tp, None, None, None, None, True, 512)
        gdn_states.append(s_gdn)
        gla_states.append(s_gla)
        ks.append(k_l)
        vs.append(v_l)

    hs = x_sample
    for l in range(DEPTH):
        hs = layer(hs, l, 1, ts, state_gdn, state_gla, ck, cv, False, 512)[0]

    return (hp, hs, jnp.stack(gdn_states, axis=1), jnp.stack(gla_states, axis=1),
            jnp.stack(ks, axis=1), jnp.stack(vs, axis=1))
```

```python
import functools
import math

import numpy as np
import jax
import jax.numpy as jnp
from jax import lax
from jax.experimental import pallas as pl
from jax.experimental.pallas import tpu as pltpu

F32 = jnp.float32
BF16 = jnp.bfloat16

D_MODEL = 1024
DEPTH = 2
GRID_W = 64
GDN_HEADS = 4
GDN_DK = 128
GDN_DV = 128
GDN_W = GDN_HEADS * GDN_DV
SHORT_CONV = 5
GLA_HEADS = 4
GLA_DK = 64
GLA_DV = 128
GLA_KW = GLA_HEADS * GLA_DK
GLA_VW = GLA_HEADS * GLA_DV
GLA_RANK = 16
GLA_TAU = 16.0
CHUNK = 64
DIFF_HEADS = 4
DIFF_DH = 64
DIFF_VD = 2 * DIFF_DH
DIFF_QW = DIFF_HEADS * 2 * DIFF_DH
DIFF_VW = DIFF_HEADS * DIFF_VD
ROPE_THETA = 10000.0
MIX_W = 512
N_BRANCH = 3
N_EXPERTS = 16
N_GROUPS = 4
GROUP_SIZE = N_EXPERTS // N_GROUPS
D_FF = 512
ALPHA = (2 * DEPTH) ** 0.25
EPS = 1e-6

LANE = 128
MOD_ROWS = 8
VMEM_LIMIT = 56 * 1024 * 1024

A_GDN_QKV = 0
A_GDN_Z = 1536
A_GLA_Q = 2048
A_GLA_K = 2304
A_GLA_V = 2560
A_GLA_R = 3072
A_DIFF_Q = 3584
A_MERGE = 4096
A_COLS = 7168
F_DIFF_K = 0
F_DIFF_V = 512
F_MISC = 1024
F_COLS = 1152
MISC_B = 0
MISC_A = 8
MISC_LR = 16


def _cparams(sem):
    return pltpu.CompilerParams(dimension_semantics=sem, vmem_limit_bytes=VMEM_LIMIT)


def _dot(a, b):
    return jnp.dot(a, b, preferred_element_type=F32)


def _mm(a, b):
    return _dot(a.astype(BF16), b.astype(BF16))


def _mm_nt(a, b):
    return lax.dot_general(a.astype(BF16), b.astype(BF16), (((1,), (1,)), ((), ())),
                           preferred_element_type=F32)


def _split3(x):
    hi = x.astype(BF16)
    r = x - hi.astype(F32)
    mid = r.astype(BF16)
    lo = (r - mid.astype(F32)).astype(BF16)
    return hi, mid, lo


def _mm_mask_lhs(mask_bf16, x):
    hi, mid, lo = _split3(x)
    return _dot(mask_bf16, hi) + _dot(mask_bf16, mid) + _dot(mask_bf16, lo)


def _mm_mask_rhs(x, mask_bf16):
    hi, mid, lo = _split3(x)
    return _dot(hi, mask_bf16) + _dot(mid, mask_bf16) + _dot(lo, mask_bf16)


def _mm_hi(a, b):
    ah = a.astype(BF16)
    al = (a - ah.astype(F32)).astype(BF16)
    bh = b.astype(BF16)
    bl = (b - bh.astype(F32)).astype(BF16)
    return _dot(ah, bh) + _dot(ah, bl) + _dot(al, bh)


def _mm_hi_nt(a, b):
    ah = a.astype(BF16)
    al = (a - ah.astype(F32)).astype(BF16)
    bh = b.astype(BF16)
    bl = (b - bh.astype(F32)).astype(BF16)
    dot = lambda x, y: lax.dot_general(x, y, (((1,), (1,)), ((), ())), preferred_element_type=F32)
    return dot(ah, bh) + dot(ah, bl) + dot(al, bh)


def _sigmoid(x):
    return 1.0 / (1.0 + jnp.exp(-x))


def _silu(x):
    return x * _sigmoid(x)


def _softplus(x):
    return jnp.maximum(x, 0.0) + jnp.log1p(jnp.exp(-jnp.abs(x)))


def _rms(x, g):
    return x * lax.rsqrt(jnp.mean(x * x, axis=-1, keepdims=True) + EPS) * g


def _layernorm(y, g, b):
    mu = jnp.mean(y, axis=-1, keepdims=True)
    d = y - mu
    var = jnp.mean(d * d, axis=-1, keepdims=True)
    return d * lax.rsqrt(var + EPS) * g + b


def _pick_lane(x, idx):
    lane = lax.broadcasted_iota(jnp.int32, x.shape, 1)
    return jnp.sum(jnp.where(lane == idx, x, 0.0), axis=1, keepdims=True)


def _tri(n, kind):
    r = lax.broadcasted_iota(jnp.int32, (n, n), 0)
    c = lax.broadcasted_iota(jnp.int32, (n, n), 1)
    return {"ge": r >= c, "gt": r > c, "le": r <= c, "lt": r < c}[kind]


def _mod_kernel(c_ref, w_ref, b_ref, o_ref):
    s = _silu(c_ref[...])
    o_ref[...] = _mm_hi(s, w_ref[...]) + b_ref[...]


def _modulation(cvecs, w_mod, b_mod):
    tn = 1536
    return pl.pallas_call(
        _mod_kernel,
        grid=(DEPTH, 6 * D_MODEL // tn),
        in_specs=[
            pl.BlockSpec((MOD_ROWS, D_MODEL), lambda l, j: (0, 0)),
            pl.BlockSpec((None, D_MODEL, tn), lambda l, j: (l, 0, j)),
            pl.BlockSpec((None, 1, tn), lambda l, j: (l, 0, j)),
        ],
        out_specs=pl.BlockSpec((None, MOD_ROWS, tn), lambda l, j: (l, 0, j)),
        out_shape=jax.ShapeDtypeStruct((DEPTH, MOD_ROWS, 6 * D_MODEL), F32),
        compiler_params=_cparams(("arbitrary", "arbitrary")),
    )(cvecs, w_mod, b_mod.reshape(DEPTH, 1, 6 * D_MODEL))


def _inproj_kernel(x_ref, mod_ref, wa_ref, wf_ref, pa_ref, pf_ref, u_sc):
    @pl.when(pl.program_id(1) == 0)
    def _():
        shift = mod_ref[:, 0:D_MODEL]
        scale = mod_ref[:, D_MODEL:2 * D_MODEL]
        u = (x_ref[...] * (1.0 + scale) + shift).astype(BF16)
        u_sc[...] = u
        pf_ref[...] = _dot(u, wf_ref[...])

    pa_ref[...] = _dot(u_sc[...], wa_ref[...]).astype(BF16)


def _inproj(x, mod, wa, wf, layer, row0, tokens_per_row, tm):
    n = x.shape[0]
    tn = 1792
    mod_row = lambda i, j: (layer, row0 + (i * tm) // tokens_per_row, 0, 0)
    return pl.pallas_call(
        _inproj_kernel,
        grid=(n // tm, A_COLS // tn),
        in_specs=[
            pl.BlockSpec((tm, D_MODEL), lambda i, j: (i, 0)),
            pl.BlockSpec((None, None, 1, 6 * D_MODEL), mod_row),
            pl.BlockSpec((D_MODEL, tn), lambda i, j: (0, j)),
            pl.BlockSpec((D_MODEL, F_COLS), lambda i, j: (0, 0)),
        ],
        out_specs=[
            pl.BlockSpec((tm, tn), lambda i, j: (i, j)),
            pl.BlockSpec((tm, F_COLS), lambda i, j: (i, 0)),
        ],
        out_shape=[
            jax.ShapeDtypeStruct((n, A_COLS), BF16),
            jax.ShapeDtypeStruct((n, F_COLS), F32),
        ],
        scratch_shapes=[pltpu.VMEM((tm, D_MODEL), BF16)],
        compiler_params=_cparams(("arbitrary", "arbitrary")),
    )(x, mod, wa, wf)


def _short_conv(x, w, t_len):
    row = lax.broadcasted_iota(jnp.int32, x.shape, 0)
    half = SHORT_CONV // 2
    acc = x * w[half:half + 1, :]
    for d in range(-half, half + 1):
        if d == 0:
            continue
        shifted = pltpu.roll(x, (-d) % t_len, axis=0)
        valid = jnp.logical_and(row + d >= 0, row + d < t_len)
        acc = acc + jnp.where(valid, shifted, 0.0) * w[half + d:half + d + 1, :]
    return acc


def _solve_unit_tri(a_list, rhs_list):
    n = a_list[0].shape[0]
    eye = jnp.where(_tri(n, "ge") & _tri(n, "le"), 1.0, 0.0)
    a_b = [a.astype(BF16) for a in a_list]
    p_b = [-a for a in a_b]
    inv = [eye - a for a in a_list]
    for _ in range(int(math.log2(n)) - 1):
        p_b = [_dot(p, p).astype(BF16) for p in p_b]
        inv = [x + _dot(x.astype(BF16), p) for x, p in zip(inv, p_b)]
    inv_b = [x.astype(BF16) for x in inv]
    sol = [_dot(x, r.astype(BF16)) for x, r in zip(inv_b, rhs_list)]
    resid = [r - s - _dot(a, s.astype(BF16)) for r, s, a in zip(rhs_list, sol, a_b)]
    return [s + _dot(x, r.astype(BF16)) for s, x, r in zip(sol, inv_b, resid)]


def _gdn_kernel(*refs, t_len, has_state, emit_state):
    it = iter(refs)
    qkv_ref, z_ref, misc_ref, cw_ref, alog_ref, dtb_ref, g_ref = (next(it) for _ in range(7))
    s0_ref = next(it) if has_state else None
    o_ref = next(it)
    sfin_ref = next(it) if emit_state else None
    (q_s, k_s, v_s, kt_s, bet, gat, gat_t, u_s, w_s, qd_s, a_s, ktl_s, gts, o_f, o_b,
     s_s) = (next(it) for _ in range(16))

    c = CHUNK
    n_chunks = t_len // c
    nh = GDN_HEADS
    dk = GDN_DK

    for h in range(nh):
        hs = slice(h * dk, (h + 1) * dk)
        q, k, v = (
            _silu(_short_conv(qkv_ref[:, j * GDN_W + h * dk:j * GDN_W + (h + 1) * dk].astype(F32),
                              cw_ref[:, j * GDN_W + h * dk:j * GDN_W + (h + 1) * dk], t_len))
            for j in range(3))
        q = q * lax.rsqrt(jnp.sum(q * q, axis=-1, keepdims=True) + EPS) * (dk ** -0.5)
        k = k * lax.rsqrt(jnp.sum(k * k, axis=-1, keepdims=True) + EPS)
        q_s[:, hs] = q
        k_s[:, hs] = k
        v_s[:, hs] = v
        kt_s[hs, :] = k.T.astype(BF16)
    misc = misc_ref[...]
    bet[...] = _sigmoid(misc)
    g_all = -jnp.exp(alog_ref[...]) * _softplus(misc + dtb_ref[...])
    gat[...] = g_all
    gat_t[...] = g_all.T[MISC_A:MISC_A + 2 * nh, :]
    for d in range(2):
        for h in range(nh):
            s_s[d * nh + h] = s0_ref[d, h] if has_state else jnp.zeros((dk, GDN_DV), F32)

    two = 2 * c
    r2 = lax.broadcasted_iota(jnp.int32, (two, two), 0)
    c2 = lax.broadcasted_iota(jnp.int32, (two, two), 1)
    same = (r2 >= c) == (c2 >= c)
    lo2 = jnp.where(jnp.logical_and(same, r2 >= c2), 1.0, 0.0).astype(BF16)
    up2 = jnp.where(jnp.logical_and(same, r2 <= c2), 1.0, 0.0).astype(BF16)
    incl = (_tri(c, "ge"), _tri(c, "le"))
    strict = (_tri(c, "gt"), _tri(c, "lt"))
    last = (c - 1, 0)

    def phase1(cp, carry):
        r0 = pl.multiple_of(cp * two, two)
        g_blk = gat[pl.ds(r0, two), :]
        b_blk = bet[pl.ds(r0, two), :]
        gt_blk = gat_t[:, pl.ds(r0, two)]
        cum_col = (_mm_mask_lhs(lo2, g_blk), _mm_mask_lhs(up2, g_blk))
        cum_row = (_mm_mask_rhs(gt_blk, up2), _mm_mask_rhs(gt_blk, lo2))
        chains, a_list, rhs_list = [], [], []
        for h in range(nh):
            hs = slice(h * dk, (h + 1) * dk)
            kt_pair = kt_s[hs, pl.ds(r0, two)]
            for s in range(2):
                rows = pl.ds(pl.multiple_of(r0 + s * c, c), c)
                q_c = q_s[rows, hs]
                k_c = k_s[rows, hs]
                v_c = v_s[rows, hs]
                kt_c = kt_pair[:, s * c:(s + 1) * c]
                kk = _dot(k_c.astype(BF16), kt_c)
                qk = _dot(q_c.astype(BF16), kt_c)
                for d in range(2):
                    idx = d * nh + h
                    gc = cum_col[d][s * c:(s + 1) * c, MISC_A + idx:MISC_A + idx + 1]
                    gcr = cum_row[d][idx:idx + 1, s * c:(s + 1) * c]
                    beta = b_blk[s * c:(s + 1) * c, MISC_B + idx:MISC_B + idx + 1]
                    decay = jnp.where(incl[d], jnp.exp(jnp.where(incl[d], gc - gcr, 0.0)), 0.0)
                    e_gc = jnp.exp(gc)
                    a_list.append(jnp.where(strict[d], beta * kk * decay, 0.0))
                    rhs_list.append(jnp.concatenate([v_c * beta, k_c * (beta * e_gc)], axis=1))
                    a_s[idx, rows, :] = jnp.where(incl[d], qk * decay, 0.0).astype(BF16)
                    qd_s[idx, rows, :] = (q_c * e_gc).astype(BF16)
                    g_last = gcr[:, last[d]:last[d] + 1]
                    ktl_s[idx, cp * 2 + s] = (kt_c.astype(F32) * jnp.exp(g_last - gcr)).astype(BF16)
                    gts[cp * 2 + s, idx:idx + 1, :] = jnp.broadcast_to(jnp.exp(g_last), (1, LANE))
                    chains.append((idx, rows))
        for (idx, rows), sol in zip(chains, _solve_unit_tri(a_list, rhs_list)):
            u_s[idx, rows, :] = sol[:, :GDN_DV]
            w_s[idx, rows, :] = sol[:, GDN_DV:].astype(BF16)
        return carry

    lax.fori_loop(0, n_chunks // 2, phase1, 0)

    def phase2(i, carry):
        chains = []
        for d, out in ((0, o_f), (1, o_b)):
            ci = i if d == 0 else n_chunks - 1 - i
            rows = pl.ds(pl.multiple_of(ci * c, c), c)
            g_tail = gts[ci]
            for h in range(nh):
                chains.append((d * nh + h, h, ci, rows, out, g_tail))
        s_f32 = [s_s[idx] for idx, *_ in chains]
        s_b = [s.astype(BF16) for s in s_f32]
        v_b = [(u_s[idx, rows, :] - _dot(w_s[idx, rows, :], sb)).astype(BF16)
               for (idx, _, _, rows, _, _), sb in zip(chains, s_b)]
        for (idx, h, ci, rows, out, g_tail), s, sb, vb in zip(chains, s_f32, s_b, v_b):
            s_s[idx] = s * g_tail[idx:idx + 1, :] + _dot(ktl_s[idx, ci], vb)
            out[rows, h * GDN_DV:(h + 1) * GDN_DV] = _dot(qd_s[idx, rows, :], sb) + _dot(
                a_s[idx, rows, :], vb)
        return carry

    lax.fori_loop(0, n_chunks, phase2, 0)

    for h in range(nh):
        hs = slice(h * GDN_DV, (h + 1) * GDN_DV)
        o = o_f[:, hs] + o_b[:, hs]
        o_ref[:, hs] = _rms(o, g_ref[...]) * _silu(z_ref[:, hs].astype(F32))
    if emit_state:
        for d in range(2):
            for h in range(nh):
                sfin_ref[d, h] = s_s[d * nh + h]


def _gdn(pa3, pf3, conv_w, alog_row, dtb_row, norm_g, s0, layer, emit_state):
    b, t, _ = pa3.shape
    full = lambda shape: pl.BlockSpec(shape, lambda bi: (0,) * len(shape))
    in_specs = [
        pl.BlockSpec((None, t, 3 * GDN_W), lambda bi: (bi, 0, A_GDN_QKV // (3 * GDN_W))),
        pl.BlockSpec((None, t, GDN_W), lambda bi: (bi, 0, A_GDN_Z // GDN_W)),
        pl.BlockSpec((None, t, LANE), lambda bi: (bi, 0, F_MISC // LANE)),
        full((SHORT_CONV, 3 * GDN_W)),
        full((1, LANE)),
        full((1, LANE)),
        full((1, GDN_DV)),
    ]
    args = [pa3, pa3, pf3, conv_w, alog_row, dtb_row, norm_g]
    if s0 is not None:
        in_specs.append(pl.BlockSpec((None, None, 2, GDN_HEADS, GDN_DK, GDN_DV),
                                     lambda bi: (bi, layer, 0, 0, 0, 0)))
        args.append(s0)
    out_specs = [pl.BlockSpec((None, t, GDN_W), lambda bi: (bi, 0, 0))]
    out_shape = [jax.ShapeDtypeStruct((b, t, GDN_W), F32)]
    if emit_state:
        out_specs.append(pl.BlockSpec((None, 2, GDN_HEADS, GDN_DK, GDN_DV), lambda bi: (bi, 0, 0, 0, 0)))
        out_shape.append(jax.ShapeDtypeStruct((b, 2, GDN_HEADS, GDN_DK, GDN_DV), F32))
    nc = t // CHUNK
    nd = 2 * GDN_HEADS
    wide = pltpu.VMEM((t, GDN_W), F32)
    scratch = [
        wide, wide, wide,
        pltpu.VMEM((GDN_W, t), BF16),
        pltpu.VMEM((t, LANE), F32), pltpu.VMEM((t, LANE), F32),
        pltpu.VMEM((nd, t), F32),
        pltpu.VMEM((nd, t, GDN_DV), F32),
        pltpu.VMEM((nd, t, GDN_DK), BF16),
        pltpu.VMEM((nd, t, GDN_DK), BF16),
        pltpu.VMEM((nd, t, CHUNK), BF16),
        pltpu.VMEM((nd, nc, GDN_DK, CHUNK), BF16),
        pltpu.VMEM((nc, nd, LANE), F32),
        wide, wide,
        pltpu.VMEM((nd, GDN_DK, GDN_DV), F32),
    ]
    outs = pl.pallas_call(
        functools.partial(_gdn_kernel, t_len=t, has_state=s0 is not None, emit_state=emit_state),
        grid=(b,),
        in_specs=in_specs,
        out_specs=out_specs,
        out_shape=out_shape,
        scratch_shapes=scratch,
        compiler_params=_cparams(("arbitrary",)),
    )(*args)
    return outs if emit_state else (outs[0], None)


def _gla_kernel(*refs, t_len, has_state, emit_state):
    it = iter(refs)
    q_ref, k_ref, v_ref, r_ref, misc_ref, wg_ref, bg_ref, g_ref = (next(it) for _ in range(8))
    s0_ref = next(it) if has_state else None
    o_ref = next(it)
    sfin_ref = next(it) if emit_state else None
    glog, vt_s, o_f, o_b, st = (next(it) for _ in range(5))

    c = CHUNK
    two = 2 * c
    n_pairs = t_len // two
    nh = GLA_HEADS
    misc = misc_ref[...]
    for d in range(2):
        lr = misc[:, MISC_LR + d * GLA_RANK:MISC_LR + (d + 1) * GLA_RANK]
        logits = _mm_hi(lr, wg_ref[d]) + bg_ref[d:d + 1, :]
        glog[d] = -_softplus(-logits) / GLA_TAU
    for h in range(nh):
        hs = slice(h * GLA_DV, (h + 1) * GLA_DV)
        vt_s[hs, :] = v_ref[:, hs].astype(F32).T.astype(BF16)
    for d in range(2):
        for h in range(nh):
            if has_state:
                s0 = jnp.concatenate([s0_ref[d, h], jnp.zeros((GLA_DV - GLA_DK, GLA_DV), F32)], axis=0)
                st[d * nh + h] = s0.T[:, :GLA_DK]
            else:
                st[d * nh + h] = jnp.zeros((GLA_DV, GLA_DK), F32)

    r2 = lax.broadcasted_iota(jnp.int32, (two, two), 0)
    c2 = lax.broadcasted_iota(jnp.int32, (two, two), 1)
    same = (r2 >= c) == (c2 >= c)
    cum_mask = (jnp.where(jnp.logical_and(same, r2 >= c2), 1.0, 0.0).astype(BF16),
                jnp.where(jnp.logical_and(same, r2 <= c2), 1.0, 0.0).astype(BF16))
    incl = (_tri(c, "ge"), _tri(c, "le"))
    last = (c - 1, 0)
    mid = (c // 2, c - 1 - c // 2)
    order = ((0, 1), (1, 0))

    def body(i, carry):
        pieces = []
        for d in range(2):
            r0 = pl.multiple_of((i if d == 0 else n_pairs - 1 - i) * two, two)
            b_all = _mm_mask_lhs(cum_mask[d], glog[d, pl.ds(r0, two), :])
            q_all = q_ref[pl.ds(r0, two), :].astype(F32) * (GLA_DK ** -0.5)
            k_all = k_ref[pl.ds(r0, two), :].astype(F32)
            vt_pair = [vt_s[h * GLA_DV:(h + 1) * GLA_DV, pl.ds(r0, two)] for h in range(nh)]
            for s in range(2):
                rs = slice(s * c, (s + 1) * c)
                bq = b_all[rs]
                b_mid = bq[mid[d]:mid[d] + 1, :]
                b_last = bq[last[d]:last[d] + 1, :]
                qe = (q_all[rs] * jnp.exp(bq - b_mid)).astype(BF16)
                ke = (k_all[rs] * jnp.exp(b_mid - bq)).astype(BF16)
                qd = (q_all[rs] * jnp.exp(bq)).astype(BF16)
                kt = (k_all[rs] * jnp.exp(b_last - bq)).astype(BF16)
                g_last = jnp.exp(b_last)
                rows = pl.ds(pl.multiple_of(r0 + s * c, c), c)
                for h in range(nh):
                    ks = slice(h * GLA_DK, (h + 1) * GLA_DK)
                    pieces.append((d, s, h, rows, qe[:, ks], ke[:, ks], qd[:, ks], kt[:, ks], g_last[:, ks],
                                   vt_pair[h][:, rs], v_ref[rows, h * GLA_DV:(h + 1) * GLA_DV]))
        a_intra = [jnp.where(incl[p[0]], _mm_nt(p[4], p[5]), 0.0).astype(BF16) for p in pieces]
        upd = [_dot(p[9], p[7]) for p in pieces]
        intra = [_dot(a, p[10]) for a, p in zip(a_intra, pieces)]
        for step in range(2):
            for p, m, o_in in zip(pieces, upd, intra):
                d, s, h, rows = p[0], p[1], p[2], p[3]
                if s != order[d][step]:
                    continue
                s_t = st[d * nh + h]
                out = o_f if d == 0 else o_b
                out[rows, h * GLA_DV:(h + 1) * GLA_DV] = _mm_nt(p[6], s_t) + o_in
                st[d * nh + h] = s_t * p[8] + m
        return carry

    lax.fori_loop(0, n_pairs, body, 0)

    for h in range(nh):
        hs = slice(h * GLA_DV, (h + 1) * GLA_DV)
        o = o_f[:, hs] + o_b[:, hs]
        o_ref[:, hs] = _rms(o, g_ref[...]) * _silu(r_ref[:, hs].astype(F32))
    if emit_state:
        for d in range(2):
            for h in range(nh):
                s_pad = jnp.concatenate([st[d * nh + h], jnp.zeros((GLA_DV, GLA_DV - GLA_DK), F32)], axis=1)
                sfin_ref[d, h] = s_pad.T[:GLA_DK, :]


def _gla(pa3, pf3, wg, bg, norm_g, s0, layer, emit_state):
    b, t, _ = pa3.shape
    full = lambda shape: pl.BlockSpec(shape, lambda bi: (0,) * len(shape))
    in_specs = [
        pl.BlockSpec((None, t, GLA_KW), lambda bi: (bi, 0, A_GLA_Q // GLA_KW)),
        pl.BlockSpec((None, t, GLA_KW), lambda bi: (bi, 0, A_GLA_K // GLA_KW)),
        pl.BlockSpec((None, t, GLA_VW), lambda bi: (bi, 0, A_GLA_V // GLA_VW)),
        pl.BlockSpec((None, t, GLA_VW), lambda bi: (bi, 0, A_GLA_R // GLA_VW)),
        pl.BlockSpec((None, t, LANE), lambda bi: (bi, 0, F_MISC // LANE)),
        full((2, GLA_RANK, GLA_KW)),
        full((2, GLA_KW)),
        full((1, GLA_DV)),
    ]
    args = [pa3, pa3, pa3, pa3, pf3, wg, bg, norm_g]
    if s0 is not None:
        in_specs.append(pl.BlockSpec((None, None, 2, GLA_HEADS, GLA_DK, GLA_DV),
                                     lambda bi: (bi, layer, 0, 0, 0, 0)))
        args.append(s0)
    out_specs = [pl.BlockSpec((None, t, GLA_VW), lambda bi: (bi, 0, 0))]
    out_shape = [jax.ShapeDtypeStruct((b, t, GLA_VW), F32)]
    if emit_state:
        out_specs.append(pl.BlockSpec((None, 2, GLA_HEADS, GLA_DK, GLA_DV), lambda bi: (bi, 0, 0, 0, 0)))
        out_shape.append(jax.ShapeDtypeStruct((b, 2, GLA_HEADS, GLA_DK, GLA_DV), F32))
    scratch = [pltpu.VMEM((2, t, GLA_KW), F32),
               pltpu.VMEM((GLA_VW, t), BF16),
               pltpu.VMEM((t, GLA_VW), F32), pltpu.VMEM((t, GLA_VW), F32),
               pltpu.VMEM((2 * GLA_HEADS, GLA_DV, GLA_DK), F32)]
    outs = pl.pallas_call(
        functools.partial(_gla_kernel, t_len=t, has_state=s0 is not None, emit_state=emit_state),
        grid=(b,),
        in_specs=in_specs,
        out_specs=out_specs,
        out_shape=out_shape,
        scratch_shapes=scratch,
        compiler_params=_cparams(("arbitrary",)),
    )(*args)
    return outs if emit_state else (outs[0], None)


ATT_Q_BLOCK = 256


def _rope_tables(t_len):
    half = DIFF_DH // 2
    quarter = half // 2
    inv = ROPE_THETA ** (-np.arange(0, half, 2, dtype=np.float64) / half)
    tok = np.arange(t_len)
    pos = np.stack([tok // GRID_W, tok % GRID_W], axis=1).astype(np.float64)
    ang = pos[:, :, None] * inv[None, None, :]
    cos = np.concatenate([np.cos(ang), np.cos(ang)], axis=-1).reshape(t_len, DIFF_DH)
    sin = np.concatenate([-np.sin(ang), np.sin(ang)], axis=-1).reshape(t_len, DIFF_DH)
    cos = np.concatenate([cos, cos], axis=-1).astype(np.float32)
    sin = np.concatenate([sin, sin], axis=-1).astype(np.float32)
    first = ((np.arange(2 * DIFF_DH) % half) < quarter).astype(np.float32)[None, :]
    return jnp.asarray(cos), jnp.asarray(sin), jnp.asarray(first), quarter


def _rope(x, cos, sin, first, quarter):
    width = x.shape[-1]
    ahead = pltpu.roll(x, width - quarter, axis=1)
    behind = pltpu.roll(x, quarter, axis=1)
    partner = jnp.where(first > 0.5, ahead, behind)
    return x * cos + partner * sin


def _diff_kernel(*refs, t_len, ctx_len, lam_init):
    it = iter(refs)
    q_ref, k_ref, v_ref, lam_ref, g_ref = (next(it) for _ in range(5))
    if ctx_len:
        ck_ref, cv_ref, cos_ref, sin_ref, first_ref = (next(it) for _ in range(5))
    o_ref = next(it)
    q_sc, k_sc, v_sc = (next(it) for _ in range(3))

    scale = DIFF_DH ** -0.5
    q = q_ref[...].astype(F32)
    k = k_ref[...]
    if ctx_len:
        quarter = DIFF_DH // 4
        q = _rope(q, cos_ref[...], sin_ref[...], first_ref[...], quarter)
        k = _rope(k, cos_ref[...], sin_ref[...], first_ref[...], quarter)
        k_sc[t_len:t_len + ctx_len, :] = ck_ref[...].astype(BF16)
        v_sc[t_len:t_len + ctx_len, :] = cv_ref[...].astype(BF16)
    q_sc[...] = (q * scale).astype(BF16)
    k_sc[0:t_len, :] = k.astype(BF16)
    v_sc[0:t_len, :] = v_ref[...].astype(BF16)

    lp = lam_ref[...]
    lam = (jnp.exp(jnp.sum(lp[0:1, :] * lp[1:2, :], axis=1, keepdims=True))
           - jnp.exp(jnp.sum(lp[2:3, :] * lp[3:4, :], axis=1, keepdims=True)) + lam_init)

    def softmax(s):
        e = jnp.exp(s - jnp.max(s, axis=-1, keepdims=True))
        return e / jnp.sum(e, axis=-1, keepdims=True)

    tq = min(ATT_Q_BLOCK, t_len)

    def body(i, carry):
        r0 = pl.multiple_of(i * tq, tq)
        qb = q_sc[pl.ds(r0, tq), :]
        kk = k_sc[...]
        p1 = softmax(_mm_nt(qb[:, :DIFF_DH], kk[:, :DIFF_DH]))
        p2 = softmax(_mm_nt(qb[:, DIFF_DH:], kk[:, DIFF_DH:]))
        o = _mm(p1 - lam * p2, v_sc[...])
        o_ref[pl.ds(r0, tq), :] = _rms(o, g_ref[...]) * (1.0 - lam_init)
        return carry

    lax.fori_loop(0, t_len // tq, body, 0)


def _diff(pa3, pf3, lam_p, norm_g, ctx_k, ctx_v, layer, lam_init):
    b, t, _ = pa3.shape
    ctx_len = 0 if ctx_k is None else ctx_k.shape[2]
    col = lambda base: (lambda bi, h: (bi, 0, base // LANE + h))
    in_specs = [
        pl.BlockSpec((None, t, LANE), col(A_DIFF_Q)),
        pl.BlockSpec((None, t, LANE), col(F_DIFF_K)),
        pl.BlockSpec((None, t, LANE), col(F_DIFF_V)),
        pl.BlockSpec((4, DIFF_DH), lambda bi, h: (0, 0)),
        pl.BlockSpec((1, DIFF_VD), lambda bi, h: (0, 0)),
    ]
    args = [pa3, pf3, pf3, lam_p, norm_g]
    if ctx_len:
        cos, sin, first, _ = _rope_tables(t)
        in_specs += [
            pl.BlockSpec((None, None, ctx_len, LANE), lambda bi, h: (bi, layer, 0, h)),
            pl.BlockSpec((None, None, ctx_len, LANE), lambda bi, h: (bi, layer, 0, h)),
            pl.BlockSpec((t, LANE), lambda bi, h: (0, 0)),
            pl.BlockSpec((t, LANE), lambda bi, h: (0, 0)),
            pl.BlockSpec((1, LANE), lambda bi, h: (0, 0)),
        ]
        args += [ctx_k, ctx_v, cos, sin, first]
    tk = t + ctx_len
    return pl.pallas_call(
        functools.partial(_diff_kernel, t_len=t, ctx_len=ctx_len, lam_init=lam_init),
        grid=(b, DIFF_HEADS),
        in_specs=in_specs,
        out_specs=pl.BlockSpec((None, t, DIFF_VD), lambda bi, h: (bi, 0, h)),
        out_shape=jax.ShapeDtypeStruct((b, t, DIFF_VW), F32),
        scratch_shapes=[pltpu.VMEM((t, LANE), BF16), pltpu.VMEM((tk, LANE), BF16),
                        pltpu.VMEM((tk, LANE), BF16)],
        compiler_params=_cparams(("arbitrary", "arbitrary")),
    )(*args)


def _route(logits_t, bias_col):
    scores = _sigmoid(logits_t)
    biased = scores + bias_col
    rows = [biased[e:e + 1, :] for e in range(N_EXPERTS)]
    grp = []
    for g in range(N_GROUPS):
        a0, a1, a2, a3 = rows[g * GROUP_SIZE:(g + 1) * GROUP_SIZE]
        hi01, lo01 = jnp.maximum(a0, a1), jnp.minimum(a0, a1)
        hi23, lo23 = jnp.maximum(a2, a3), jnp.minimum(a2, a3)
        top1 = jnp.maximum(hi01, hi23)
        top2 = jnp.maximum(jnp.minimum(hi01, hi23), jnp.maximum(lo01, lo23))
        grp.append(top1 + top2)
    best = []
    for g in range(N_GROUPS):
        win = None
        for o in range(N_GROUPS):
            if o == g:
                continue
            cond = grp[g] > grp[o] if o < g else grp[g] >= grp[o]
            win = cond if win is None else jnp.logical_and(win, cond)
        best.append(win)
    sel_rows = []
    for e in range(N_EXPERTS):
        g = e // GROUP_SIZE
        beaten = jnp.zeros_like(rows[e])
        for o in range(g * GROUP_SIZE, (g + 1) * GROUP_SIZE):
            if o == e:
                continue
            ahead = rows[o] >= rows[e] if o < e else rows[o] > rows[e]
            beaten = beaten + jnp.where(ahead, 1.0, 0.0)
        sel_rows.append(jnp.where(jnp.logical_and(best[g], beaten < 1.5), 1.0, 0.0))
    sel = jnp.concatenate(sel_rows, axis=0)
    picked = sel * scores
    return sel, picked / jnp.sum(picked, axis=0, keepdims=True)


PIECE = 16
ROW_TILE = 256
SLOT_PAD = N_EXPERTS * PIECE
HS_COLS = D_MODEL + LANE


def _route_meta(sel, wts):
    n_e, t = sel.shape
    e_col = lax.broadcasted_iota(jnp.int32, (n_e, 1), 0).astype(F32)
    before = jnp.where(_tri(t, "lt"), 1.0, 0.0).astype(BF16)
    rank = _dot(sel.astype(BF16), before)
    cnt = jnp.sum(sel, axis=1, keepdims=True)
    plen = jnp.floor((cnt + (PIECE - 1.0)) * (1.0 / PIECE)) * PIECE
    lower = jnp.where(_tri(n_e, "gt"), 1.0, 0.0).astype(BF16)
    loff = _dot(lower, jnp.broadcast_to(plen, (n_e, LANE)).astype(BF16))[:, 0:1]
    slot = loff + rank
    chosen = sel > 0.5
    e_a = jnp.min(jnp.where(chosen, e_col, 2.0 * n_e), axis=0, keepdims=True)
    e_b = jnp.max(jnp.where(chosen, e_col, -1.0), axis=0, keepdims=True)
    take = lambda e_row, x: jnp.sum(jnp.where(e_col == e_row, x, 0.0), axis=0, keepdims=True)
    rows = jnp.concatenate([take(e_a, slot), take(e_b, slot), take(e_a, wts), take(e_b, wts), e_a, e_b,
                            jnp.zeros((2, t), F32)], axis=0)
    return rows, plen, loff


def _merge_kernel(og_ref, ol_ref, od_ref, lg0_ref, lg1_ref, lg2_ref, x_ref, mod_ref, wb_ref, wo_ref,
                  lng_ref, lnb_ref, rw_ref, rb_ref, x1_ref, h_ref, mrow_ref, mcol_ref, plen_ref, loff_ref):
    acc = None
    for s, (o_ref, lg_ref) in enumerate(((og_ref, lg0_ref), (ol_ref, lg1_ref), (od_ref, lg2_ref))):
        proj = _mm(o_ref[...], wb_ref[s])
        term = _sigmoid(lg_ref[...].astype(F32)) * proj
        acc = term if acc is None else acc + term
    m = _mm(acc, wo_ref[...])
    gate1 = mod_ref[:, 2 * D_MODEL:3 * D_MODEL]
    shift2 = mod_ref[:, 3 * D_MODEL:4 * D_MODEL]
    scale2 = mod_ref[:, 4 * D_MODEL:5 * D_MODEL]
    x1 = _layernorm(ALPHA * x_ref[...] + gate1 * m, lng_ref[...], lnb_ref[...])
    x1_ref[...] = x1
    h = x1 * (1.0 + scale2) + shift2
    h_ref[...] = h.astype(BF16)
    sel, gates_t = _route(_mm_hi_nt(rw_ref[...], h), rb_ref[...])
    rows, plen, loff = _route_meta(sel, gates_t)
    tm = rows.shape[1]
    mrow_ref[...] = rows
    mcol_ref[...] = jnp.concatenate([rows, jnp.zeros((LANE - rows.shape[0], tm), F32)], axis=0).T
    lane = lax.broadcasted_iota(jnp.int32, (N_EXPERTS, LANE), 1)
    cols = jnp.where(lane == 0, plen, jnp.where(lane == 1, loff, 0.0))
    cols_t = jnp.concatenate([cols, jnp.zeros((LANE - N_EXPERTS, LANE), F32)], axis=0).T
    plen_ref[...] = cols_t[0:1, :]
    loff_ref[...] = cols_t[1:2, :]


def _merge(o_gdn, o_gla, o_diff, pa, x, mod, wb, wo, ln_g, ln_b, rw_t, rb_col, layer, row0,
           tokens_per_row, tm):
    n = x.shape[0]
    mod_row = lambda i: (layer, row0 + (i * tm) // tokens_per_row, 0, 0)
    full = lambda shape: pl.BlockSpec(shape, lambda i: (0,) * len(shape))
    return pl.pallas_call(
        _merge_kernel,
        grid=(n // tm,),
        in_specs=[
            pl.BlockSpec((tm, MIX_W), lambda i: (i, 0)),
            pl.BlockSpec((tm, MIX_W), lambda i: (i, 0)),
            pl.BlockSpec((tm, MIX_W), lambda i: (i, 0)),
            pl.BlockSpec((tm, D_MODEL), lambda i: (i, A_MERGE // D_MODEL)),
            pl.BlockSpec((tm, D_MODEL), lambda i: (i, A_MERGE // D_MODEL + 1)),
            pl.BlockSpec((tm, D_MODEL), lambda i: (i, A_MERGE // D_MODEL + 2)),
            pl.BlockSpec((tm, D_MODEL), lambda i: (i, 0)),
            pl.BlockSpec((None, None, 1, 6 * D_MODEL), mod_row),
            full((N_BRANCH, MIX_W, D_MODEL)),
            full((D_MODEL, D_MODEL)),
            full((1, D_MODEL)),
            full((1, D_MODEL)),
            full((N_EXPERTS, D_MODEL)),
            full((N_EXPERTS, 1)),
        ],
        out_specs=[
            pl.BlockSpec((tm, D_MODEL), lambda i: (i, 0)),
            pl.BlockSpec((tm, D_MODEL), lambda i: (i, 0)),
            pl.BlockSpec((8, tm), lambda i: (0, i)),
            pl.BlockSpec((tm, LANE), lambda i: (i, 0)),
            pl.BlockSpec((None, 1, LANE), lambda i: (i, 0, 0)),
            pl.BlockSpec((None, 1, LANE), lambda i: (i, 0, 0)),
        ],
        out_shape=[
            jax.ShapeDtypeStruct((n, D_MODEL), F32),
            jax.ShapeDtypeStruct((n, D_MODEL), BF16),
            jax.ShapeDtypeStruct((8, n), F32),
            jax.ShapeDtypeStruct((n, LANE), F32),
            jax.ShapeDtypeStruct((n // tm, 1, LANE), F32),
            jax.ShapeDtypeStruct((n // tm, 1, LANE), F32),
        ],
        compiler_params=_cparams(("arbitrary",)),
    )(o_gdn, o_gla, o_diff, pa, pa, pa, x, mod, wb, wo, ln_g, ln_b, rw_t, rb_col)


def _plan_kernel(plen_ref, loff_ref, goff_ref, npc_ref, loffi_ref, tiles_ref, tail_ref, *, n_steps):
    plen = plen_ref[...]
    nb = plen.shape[0]
    earlier = jnp.where(_tri(nb, "gt"), 1.0, 0.0).astype(BF16)
    run = _dot(earlier, plen.astype(BF16))
    gtot = jnp.sum(plen, axis=0, keepdims=True)
    lane = lax.broadcasted_iota(jnp.int32, (1, LANE), 1).astype(F32)
    ntile = jnp.floor((gtot + (ROW_TILE - 1.0)) * (1.0 / ROW_TILE))
    before = jnp.where(_tri(LANE, "lt"), 1.0, 0.0).astype(BF16)
    first = _dot(jnp.broadcast_to(ntile, (8, LANE)).astype(BF16), before)[0:1, :]
    end = first + ntile
    n_used = jnp.sum(ntile, axis=1, keepdims=True)
    goff_ref[...] = (first * ROW_TILE + run).astype(jnp.int32)
    npc_ref[...] = (plen * (1.0 / PIECE)).astype(jnp.int32)
    loffi_ref[...] = loff_ref[...].astype(jnp.int32)
    end_col = jnp.concatenate([end, jnp.zeros((LANE - 1, LANE), F32)], axis=0).T[:, 0:1]
    e_col = lax.broadcasted_iota(jnp.int32, (LANE, 1), 0).astype(F32)
    tile = jnp.minimum(lane, n_used - 1.0)
    done = jnp.logical_and(end_col <= tile, e_col < N_EXPERTS)
    t_exp = jnp.minimum(jnp.sum(jnp.where(done, 1.0, 0.0), axis=0, keepdims=True), N_EXPERTS - 1.0)
    tiles_ref[...] = jnp.concatenate(
        [t_exp, jnp.broadcast_to(n_used, (1, LANE)), jnp.zeros((6, LANE), F32)], axis=0).astype(jnp.int32)
    tail_ref[...] = jnp.concatenate(
        [first * ROW_TILE + gtot, (ntile * ROW_TILE - gtot) * (1.0 / PIECE),
         jnp.broadcast_to(n_used, (1, LANE)), jnp.broadcast_to(n_steps - n_used, (1, LANE)),
         jnp.zeros((4, LANE), F32)], axis=0).astype(jnp.int32)


def _plan(plen, loff, n_steps):
    nb = plen.shape[0]
    i32 = lambda rows: jax.ShapeDtypeStruct((rows, LANE), jnp.int32)
    return pl.pallas_call(
        functools.partial(_plan_kernel, n_steps=n_steps),
        out_shape=[i32(nb), i32(nb), i32(nb), i32(8), i32(8)],
    )(plen, loff)


def _piece_copies(b, loff_s, npc_s, goff_s, local, remote, sem, to_remote):
    def copy(lo, go):
        src = local.at[pl.ds(pl.multiple_of(lo, PIECE), PIECE)]
        dst = remote.at[pl.ds(pl.multiple_of(go, PIECE), PIECE)]
        return pltpu.make_async_copy(src, dst, sem) if to_remote else pltpu.make_async_copy(dst, src, sem)

    total = 0
    for e in range(N_EXPERTS):
        n = npc_s[b, e]
        lo = loff_s[b, e]
        go = goff_s[b, e]

        def start(j, carry, lo=lo, go=go):
            copy(lo + j * PIECE, go + j * PIECE).start()
            return carry

        lax.fori_loop(0, n, start, 0)
        total = total + n

    def wait(j, carry):
        copy(0, 0).wait()
        return carry

    lax.fori_loop(0, total, wait, 0)


def _dispatch_kernel(loff_s, npc_s, goff_s, tail_s, h_ref, mrow_ref, mcol_ref, hs_hbm, local, zeros, sem,
                     tile_sem, *, slots):
    b = pl.program_id(0)
    tb = h_ref.shape[0]
    mrow = mrow_ref[...]
    mcol = mcol_ref[...]
    slot_id = lax.broadcasted_iota(jnp.int32, (slots, tb), 0).astype(F32)
    s_a = jnp.where(slot_id == mrow[0:1, :], 1.0, 0.0).astype(BF16)
    s_b = jnp.where(slot_id == mrow[1:2, :], 1.0, 0.0).astype(BF16)
    lane = lax.broadcasted_iota(jnp.int32, (tb, LANE), 1)

    def weight_lanes(w):
        hi = w.astype(BF16).astype(F32)
        return jnp.where(lane == 0, hi, jnp.where(lane == 1, w - hi, 0.0)).astype(BF16)

    local[:, :D_MODEL] = _dot(s_a + s_b, h_ref[...]).astype(BF16)
    local[:, D_MODEL:] = (_dot(s_a, weight_lanes(mcol[:, 2:3])) + _dot(s_b, weight_lanes(mcol[:, 3:4]))).astype(BF16)
    _piece_copies(b, loff_s, npc_s, goff_s, local, hs_hbm, sem, True)

    @pl.when(b == pl.num_programs(0) - 1)
    def _():
        zeros[...] = jnp.zeros_like(zeros)
        zero_piece = zeros.at[pl.ds(0, PIECE)]
        total = 0
        for e in range(N_EXPERTS):
            n = tail_s[1, e]
            start_row = tail_s[0, e]

            def start(j, carry, start_row=start_row):
                dst = hs_hbm.at[pl.ds(pl.multiple_of(start_row + j * PIECE, PIECE), PIECE)]
                pltpu.make_async_copy(zero_piece, dst, sem).start()
                return carry

            lax.fori_loop(0, n, start, 0)
            total = total + n

        def wait(j, carry):
            pltpu.make_async_copy(zero_piece, hs_hbm.at[pl.ds(0, PIECE)], sem).wait()
            return carry

        lax.fori_loop(0, total, wait, 0)

        def tile_copy(j):
            row = pl.multiple_of((tail_s[2, 0] + j) * ROW_TILE, ROW_TILE)
            return pltpu.make_async_copy(zeros, hs_hbm.at[pl.ds(row, ROW_TILE)], tile_sem)

        def start_tile(j, carry):
            tile_copy(j).start()
            return carry

        def wait_tile(j, carry):
            tile_copy(j).wait()
            return carry

        lax.fori_loop(0, tail_s[3, 0], start_tile, 0)
        lax.fori_loop(0, tail_s[3, 0], wait_tile, 0)


def _dispatch(loff_i, npc, goff, tail, h, mrow, mcol, tb, n_steps):
    n = h.shape[0]
    slots = 2 * tb + SLOT_PAD
    grid_spec = pltpu.PrefetchScalarGridSpec(
        num_scalar_prefetch=4,
        grid=(n // tb,),
        in_specs=[
            pl.BlockSpec((tb, D_MODEL), lambda i, *_: (i, 0)),
            pl.BlockSpec((8, tb), lambda i, *_: (0, i)),
            pl.BlockSpec((tb, LANE), lambda i, *_: (i, 0)),
        ],
        out_specs=pl.BlockSpec(memory_space=pl.ANY),
        scratch_shapes=[pltpu.VMEM((slots, HS_COLS), BF16), pltpu.VMEM((ROW_TILE, HS_COLS), BF16),
                        pltpu.SemaphoreType.DMA, pltpu.SemaphoreType.DMA],
    )
    return pl.pallas_call(
        functools.partial(_dispatch_kernel, slots=slots),
        grid_spec=grid_spec,
        out_shape=jax.ShapeDtypeStruct((n_steps * ROW_TILE, HS_COLS), BF16),
        compiler_params=_cparams(("arbitrary",)),
    )(loff_i, npc, goff, tail, h, mrow, mcol)


def _expert_kernel(tiles_s, hs_ref, wg_ref, wu_ref, wd_ref, y_ref, wg_b, wu_b, wd_b):
    i = pl.program_id(0)
    e = tiles_s[0, i]
    prev = tiles_s[0, jnp.maximum(i - 1, 0)]

    @pl.when(jnp.logical_or(i == 0, e != prev))
    def _():
        wg_b[...] = wg_ref[...].astype(BF16)
        wu_b[...] = wu_ref[...].astype(BF16)
        wd_b[...] = wd_ref[...].astype(BF16)

    @pl.when(i < tiles_s[1, 0])
    def _():
        x = hs_ref[:, :D_MODEL]
        w = hs_ref[:, D_MODEL:D_MODEL + 1].astype(F32) + hs_ref[:, D_MODEL + 1:D_MODEL + 2].astype(F32)
        act = _silu(_dot(x, wg_b[...])) * _dot(x, wu_b[...]) * w
        y_ref[...] = _dot(act.astype(BF16), wd_b[...]).astype(BF16)

    @pl.when(i >= tiles_s[1, 0])
    def _():
        y_ref[...] = jnp.zeros_like(y_ref)


def _experts(tiles, hs, wg, wu, wd, n_steps):
    grid_spec = pltpu.PrefetchScalarGridSpec(
        num_scalar_prefetch=1,
        grid=(n_steps,),
        in_specs=[
            pl.BlockSpec((ROW_TILE, HS_COLS), lambda i, t: (i, 0)),
            pl.BlockSpec((None, D_MODEL, D_FF), lambda i, t: (t[0, i], 0, 0)),
            pl.BlockSpec((None, D_MODEL, D_FF), lambda i, t: (t[0, i], 0, 0)),
            pl.BlockSpec((None, D_FF, D_MODEL), lambda i, t: (t[0, i], 0, 0)),
        ],
        out_specs=pl.BlockSpec((ROW_TILE, D_MODEL), lambda i, t: (i, 0)),
        scratch_shapes=[pltpu.VMEM((D_MODEL, D_FF), BF16), pltpu.VMEM((D_MODEL, D_FF), BF16),
                        pltpu.VMEM((D_FF, D_MODEL), BF16)],
    )
    return pl.pallas_call(
        _expert_kernel,
        grid_spec=grid_spec,
        out_shape=jax.ShapeDtypeStruct((hs.shape[0], D_MODEL), BF16),
        compiler_params=_cparams(("arbitrary",)),
    )(tiles, hs, wg, wu, wd)


def _combine_kernel(loff_s, npc_s, goff_s, y_hbm, mcol_ref, x1_ref, mod_ref, lng_ref, lnb_ref, o_ref,
                    local, sem, *, slots):
    b = pl.program_id(0)
    tb = x1_ref.shape[0]
    local[...] = jnp.zeros_like(local)
    _piece_copies(b, loff_s, npc_s, goff_s, local, y_hbm, sem, False)
    mcol = mcol_ref[...]
    slot_id = lax.broadcasted_iota(jnp.int32, (tb, slots), 1).astype(F32)
    pick = jnp.logical_or(slot_id == mcol[:, 0:1], slot_id == mcol[:, 1:2])
    f = _dot(jnp.where(pick, 1.0, 0.0).astype(BF16), local[...])
    gate2 = mod_ref[:, 5 * D_MODEL:6 * D_MODEL]
    o_ref[...] = _layernorm(ALPHA * x1_ref[...] + gate2 * f, lng_ref[...], lnb_ref[...])


def _combine(loff_i, npc, goff, y, mcol, x1, mod, ln_g, ln_b, layer, row0, tokens_per_row, tb):
    n = x1.shape[0]
    slots = 2 * tb + SLOT_PAD
    mod_row = lambda i, *_: (layer, row0 + (i * tb) // tokens_per_row, 0, 0)
    grid_spec = pltpu.PrefetchScalarGridSpec(
        num_scalar_prefetch=3,
        grid=(n // tb,),
        in_specs=[
            pl.BlockSpec(memory_space=pl.ANY),
            pl.BlockSpec((tb, LANE), lambda i, *_: (i, 0)),
            pl.BlockSpec((tb, D_MODEL), lambda i, *_: (i, 0)),
            pl.BlockSpec((None, None, 1, 6 * D_MODEL), mod_row),
            pl.BlockSpec((1, D_MODEL), lambda i, *_: (0, 0)),
            pl.BlockSpec((1, D_MODEL), lambda i, *_: (0, 0)),
        ],
        out_specs=pl.BlockSpec((tb, D_MODEL), lambda i, *_: (i, 0)),
        scratch_shapes=[pltpu.VMEM((slots, D_MODEL), BF16), pltpu.SemaphoreType.DMA],
    )
    return pl.pallas_call(
        functools.partial(_combine_kernel, slots=slots),
        grid_spec=grid_spec,
        out_shape=jax.ShapeDtypeStruct((n, D_MODEL), F32),
        compiler_params=_cparams(("arbitrary",)),
    )(loff_i, npc, goff, y, mcol, x1, mod, ln_g, ln_b)


def _moe(h, mrow, mcol, plen, loff, wg, wu, wd, x1, mod, ln_g, ln_b, layer, row0, tokens_per_row, tb):
    n = h.shape[0]
    nb = n // tb
    n_steps = (2 * n + nb * N_EXPERTS * (PIECE - 1)) // ROW_TILE + N_EXPERTS
    assert n_steps <= LANE
    goff, npc, loff_i, tiles, tail = _plan(plen.reshape(nb, LANE), loff.reshape(nb, LANE), n_steps)
    hs = _dispatch(loff_i, npc, goff, tail, h, mrow, mcol, tb, n_steps)
    y = _experts(tiles, hs, wg, wu, wd, n_steps)
    return _combine(loff_i, npc, goff, y, mcol, x1, mod, ln_g, ln_b, layer, row0, tokens_per_row, tb)


def _reorder_w_in(w):
    sizes = (3 * GDN_W, GDN_W, 2 * GDN_HEADS, 2 * GDN_HEADS, GLA_KW, GLA_KW, GLA_VW, GLA_VW,
             2 * GLA_RANK, DIFF_QW, DIFF_QW, DIFF_VW, N_BRANCH * D_MODEL)
    offs = np.concatenate([[0], np.cumsum(sizes)])
    seg = lambda i: w[:, offs[i]:offs[i + 1]]
    used = 2 * GDN_HEADS * 2 + 2 * GLA_RANK
    wa = jnp.concatenate([seg(0), seg(1), seg(4), seg(5), seg(6), seg(7), seg(9), seg(12)], axis=1)
    wf = jnp.concatenate([seg(10), seg(11), seg(2), seg(3), seg(8),
                          jnp.zeros((w.shape[0], LANE - used), w.dtype)], axis=1)
    return wa.astype(BF16), wf.astype(BF16)


def _lane_row(vals, offset):
    return jnp.zeros((1, LANE), F32).at[0, offset:offset + vals.shape[0]].set(vals)


def kernel(x_prompt, x_sample, c, state_gdn, state_gla, cache_k, cache_v, c_ctx, w_mod, b_mod, w_in,
           gdn_conv, gdn_a_log, gdn_dt_bias, gdn_norm, gla_w_gate, gla_b_gate, gla_norm, diff_lambda,
           diff_norm, w_branch, w_out, ln_g, ln_b, router_w, router_b, exp_w_gate, exp_w_up, exp_w_down):
    bp, tp, d = x_prompt.shape
    bs, ts, _ = x_sample.shape
    pad_rows = MOD_ROWS - 1 - bs
    cvecs = jnp.concatenate([c_ctx[None, :], c, jnp.zeros((pad_rows, d), F32)], axis=0)
    mod = _modulation(cvecs, w_mod, b_mod).reshape(DEPTH, MOD_ROWS, 1, 6 * d)

    rw_t = router_w.T
    rb_col = router_b.reshape(N_EXPERTS, 1)
    ck = cache_k.reshape(bs, DEPTH, cache_k.shape[2], DIFF_QW)
    cv = cache_v.reshape(bs, DEPTH, cache_v.shape[2], DIFF_VW)

    layer_w = []
    for l in range(DEPTH):
        wa, wf = _reorder_w_in(w_in[l])
        layer_w.append(dict(
            wa=wa, wf=wf,
            alog=_lane_row(gdn_a_log[l].reshape(-1), MISC_A),
            dtb=_lane_row(gdn_dt_bias[l].reshape(-1), MISC_A),
            wb=w_branch[l].astype(BF16),
            wo=w_out[l].astype(BF16),
        ))

    def layer(x, l, row0, tokens_per_row, s_gdn, s_gla, ctx_k, ctx_v, emit_state, tm):
        b, t, _ = x.shape
        n = b * t
        lw = layer_w[l]
        xf = x.reshape(n, d)
        pa, pf = _inproj(xf, mod, lw["wa"], lw["wf"], l, row0, tokens_per_row, tm)
        pa3 = pa.reshape(b, t, A_COLS)
        pf3 = pf.reshape(b, t, F_COLS)
        o_gdn, gdn_fin = _gdn(pa3, pf3, gdn_conv[l], lw["alog"], lw["dtb"], gdn_norm[l][None, :], s_gdn, l,
                              emit_state)
        o_gla, gla_fin = _gla(pa3, pf3, gla_w_gate[l], gla_b_gate[l], gla_norm[l][None, :], s_gla, l,
                              emit_state)
        lam_init = 0.8 - 0.6 * math.exp(-0.3 * l)
        o_diff = _diff(pa3, pf3, diff_lambda[l], diff_norm[l][None, :], ctx_k, ctx_v, l, lam_init)
        x1, h, mrow, mcol, plen, loff = _merge(
            o_gdn.reshape(n, MIX_W), o_gla.reshape(n, MIX_W), o_diff.reshape(n, MIX_W), pa, xf, mod,
            lw["wb"], lw["wo"], ln_g[l, 0][None, :], ln_b[l, 0][None, :], rw_t, rb_col, l, row0,
            tokens_per_row, tm)
        x2 = _moe(h, mrow, mcol, plen, loff, exp_w_gate[l], exp_w_up[l], exp_w_down[l], x1, mod,
                  ln_g[l, 1][None, :], ln_b[l, 1][None, :], l, row0, tokens_per_row, tm)
        k_own = pf3[:, :, F_DIFF_K:F_DIFF_K + DIFF_QW].reshape(b, t, DIFF_HEADS, 2 * DIFF_DH)
        v_own = pf3[:, :, F_DIFF_V:F_DIFF_V + DIFF_VW].reshape(b, t, DIFF_HEADS, DIFF_VD)
        return x2.reshape(b, t, d), gdn_fin, gla_fin, k_own, v_own

    hp = x_prompt
    gdn_states, gla_states, ks, vs = [], [], [], []
    for l in range(DEPTH):
        hp, s_gdn, s_gla, k_l, v_l = layer(hp, l, 0, bp * tp, None, None, None, None, True, 512)
        gdn_states.append(s_gdn)
        gla_states.append(s_gla)
        ks.append(k_l)
        vs.append(v_l)

    hs = x_sample
    for l in range(DEPTH):
        hs = layer(hs, l, 1, ts, state_gdn, state_gla, ck, cv, False, 512)[0]

    return (hp, hs, jnp.stack(gdn_states, axis=1), jnp.stack(gla_states, axis=1),
            jnp.stack(ks, axis=1), jnp.stack(vs, axis=1))
```

```python
import functools
import math

import numpy as np
import jax
import jax.numpy as jnp
from jax import lax
from jax.experimental import pallas as pl
from jax.experimental.pallas import tpu as pltpu

F32 = jnp.float32
BF16 = jnp.bfloat16

D_MODEL = 1024
DEPTH = 2
GRID_W = 64
GDN_HEADS = 4
GDN_DK = 128
GDN_DV = 128
GDN_W = GDN_HEADS * GDN_DV
SHORT_CONV = 5
GLA_HEADS = 4
GLA_DK = 64
GLA_DV = 128
GLA_KW = GLA_HEADS * GLA_DK
GLA_VW = GLA_HEADS * GLA_DV
GLA_RANK = 16
GLA_TAU = 16.0
CHUNK = 64
DIFF_HEADS = 4
DIFF_DH = 64
DIFF_VD = 2 * DIFF_DH
DIFF_QW = DIFF_HEADS * 2 * DIFF_DH
DIFF_VW = DIFF_HEADS * DIFF_VD
ROPE_THETA = 10000.0
MIX_W = 512
N_BRANCH = 3
N_EXPERTS = 16
N_GROUPS = 4
GROUP_SIZE = N_EXPERTS // N_GROUPS
D_FF = 512
ALPHA = (2 * DEPTH) ** 0.25
EPS = 1e-6

LANE = 128
MOD_ROWS = 8
VMEM_LIMIT = 56 * 1024 * 1024

A_GDN_QKV = 0
A_GDN_Z = 1536
A_GLA_Q = 2048
A_GLA_K = 2304
A_GLA_V = 2560
A_GLA_R = 3072
A_DIFF_Q = 3584
A_MERGE = 4096
A_COLS = 7168
F_DIFF_K = 0
F_DIFF_V = 512
F_MISC = 1024
F_COLS = 1152
MISC_B = 0
MISC_A = 8
MISC_LR = 16


def _cparams(sem):
    return pltpu.CompilerParams(dimension_semantics=sem, vmem_limit_bytes=VMEM_LIMIT)


def _dot(a, b):
    return jnp.dot(a, b, preferred_element_type=F32)


def _mm(a, b):
    return _dot(a.astype(BF16), b.astype(BF16))


def _mm_nt(a, b):
    return lax.dot_general(a.astype(BF16), b.astype(BF16), (((1,), (1,)), ((), ())),
                           preferred_element_type=F32)


def _split3(x):
    hi = x.astype(BF16)
    r = x - hi.astype(F32)
    mid = r.astype(BF16)
    lo = (r - mid.astype(F32)).astype(BF16)
    return hi, mid, lo


def _mm_mask_lhs(mask_bf16, x):
    hi, mid, lo = _split3(x)
    return _dot(mask_bf16, hi) + _dot(mask_bf16, mid) + _dot(mask_bf16, lo)


def _mm_mask_rhs(x, mask_bf16):
    hi, mid, lo = _split3(x)
    return _dot(hi, mask_bf16) + _dot(mid, mask_bf16) + _dot(lo, mask_bf16)


def _mm_hi(a, b):
    ah = a.astype(BF16)
    al = (a - ah.astype(F32)).astype(BF16)
    bh = b.astype(BF16)
    bl = (b - bh.astype(F32)).astype(BF16)
    return _dot(ah, bh) + _dot(ah, bl) + _dot(al, bh)


def _mm_hi_nt(a, b):
    ah = a.astype(BF16)
    al = (a - ah.astype(F32)).astype(BF16)
    bh = b.astype(BF16)
    bl = (b - bh.astype(F32)).astype(BF16)
    dot = lambda x, y: lax.dot_general(x, y, (((1,), (1,)), ((), ())), preferred_element_type=F32)
    return dot(ah, bh) + dot(ah, bl) + dot(al, bh)


def _sigmoid(x):
    return 1.0 / (1.0 + jnp.exp(-x))


def _silu(x):
    return x * _sigmoid(x)


def _softplus(x):
    return jnp.maximum(x, 0.0) + jnp.log1p(jnp.exp(-jnp.abs(x)))


def _rms(x, g):
    return x * lax.rsqrt(jnp.mean(x * x, axis=-1, keepdims=True) + EPS) * g


def _layernorm(y, g, b):
    mu = jnp.mean(y, axis=-1, keepdims=True)
    d = y - mu
    var = jnp.mean(d * d, axis=-1, keepdims=True)
    return d * lax.rsqrt(var + EPS) * g + b


def _pick_lane(x, idx):
    lane = lax.broadcasted_iota(jnp.int32, x.shape, 1)
    return jnp.sum(jnp.where(lane == idx, x, 0.0), axis=1, keepdims=True)


def _tri(n, kind):
    r = lax.broadcasted_iota(jnp.int32, (n, n), 0)
    c = lax.broadcasted_iota(jnp.int32, (n, n), 1)
    return {"ge": r >= c, "gt": r > c, "le": r <= c, "lt": r < c}[kind]


def _mod_kernel(c_ref, w_ref, b_ref, o_ref):
    s = _silu(c_ref[...])
    o_ref[...] = _mm_hi(s, w_ref[...]) + b_ref[...]


def _modulation(cvecs, w_mod, b_mod):
    tn = 1536
    return pl.pallas_call(
        _mod_kernel,
        grid=(DEPTH, 6 * D_MODEL // tn),
        in_specs=[
            pl.BlockSpec((MOD_ROWS, D_MODEL), lambda l, j: (0, 0)),
            pl.BlockSpec((None, D_MODEL, tn), lambda l, j: (l, 0, j)),
            pl.BlockSpec((None, 1, tn), lambda l, j: (l, 0, j)),
        ],
        out_specs=pl.BlockSpec((None, MOD_ROWS, tn), lambda l, j: (l, 0, j)),
        out_shape=jax.ShapeDtypeStruct((DEPTH, MOD_ROWS, 6 * D_MODEL), F32),
        compiler_params=_cparams(("arbitrary", "arbitrary")),
    )(cvecs, w_mod, b_mod.reshape(DEPTH, 1, 6 * D_MODEL))


def _inproj_kernel(*refs, emit_cache):
    x_ref, mod_ref, wa_ref, wf_ref, pa_ref, pf_ref = refs[:6]
    u_sc = refs[-1]

    @pl.when(pl.program_id(1) == 0)
    def _():
        shift = mod_ref[:, 0:D_MODEL]
        scale = mod_ref[:, D_MODEL:2 * D_MODEL]
        u = (x_ref[...] * (1.0 + scale) + shift).astype(BF16)
        u_sc[...] = u
        pf = _dot(u, wf_ref[...])
        pf_ref[...] = pf
        if emit_cache:
            ko_ref, vo_ref = refs[6:8]
            for h in range(DIFF_HEADS):
                ko_ref[:, h, :] = pf[:, F_DIFF_K + h * LANE:F_DIFF_K + (h + 1) * LANE]
                vo_ref[:, h, :] = pf[:, F_DIFF_V + h * LANE:F_DIFF_V + (h + 1) * LANE]

    pa_ref[...] = _dot(u_sc[...], wa_ref[...]).astype(BF16)


def _inproj(x, mod, wa, wf, layer, row0, tokens_per_row, tm, emit_cache):
    n = x.shape[0]
    tn = 1792
    mod_row = lambda i, j: (layer, row0 + (i * tm) // tokens_per_row, 0, 0)
    out_specs = [
        pl.BlockSpec((tm, tn), lambda i, j: (i, j)),
        pl.BlockSpec((tm, F_COLS), lambda i, j: (i, 0)),
    ]
    out_shape = [
        jax.ShapeDtypeStruct((n, A_COLS), BF16),
        jax.ShapeDtypeStruct((n, F_COLS), F32),
    ]
    if emit_cache:
        out_specs += [pl.BlockSpec((tm, DIFF_HEADS, LANE), lambda i, j: (i, 0, 0))] * 2
        out_shape += [jax.ShapeDtypeStruct((n, DIFF_HEADS, LANE), F32)] * 2
    return pl.pallas_call(
        functools.partial(_inproj_kernel, emit_cache=emit_cache),
        grid=(n // tm, A_COLS // tn),
        in_specs=[
            pl.BlockSpec((tm, D_MODEL), lambda i, j: (i, 0)),
            pl.BlockSpec((None, None, 1, 6 * D_MODEL), mod_row),
            pl.BlockSpec((D_MODEL, tn), lambda i, j: (0, j)),
            pl.BlockSpec((D_MODEL, F_COLS), lambda i, j: (0, 0)),
        ],
        out_specs=out_specs,
        out_shape=out_shape,
        scratch_shapes=[pltpu.VMEM((tm, D_MODEL), BF16)],
        compiler_params=_cparams(("arbitrary", "arbitrary")),
    )(x, mod, wa, wf)


def _short_conv(x, w, t_len):
    row = lax.broadcasted_iota(jnp.int32, x.shape, 0)
    half = SHORT_CONV // 2
    acc = x * w[half:half + 1, :]
    for d in range(-half, half + 1):
        if d == 0:
            continue
        shifted = pltpu.roll(x, (-d) % t_len, axis=0)
        valid = jnp.logical_and(row + d >= 0, row + d < t_len)
        acc = acc + jnp.where(valid, shifted, 0.0) * w[half + d:half + d + 1, :]
    return acc


def _solve_unit_tri(a_list, rhs_list):
    n = a_list[0].shape[0]
    eye = jnp.where(_tri(n, "ge") & _tri(n, "le"), 1.0, 0.0)
    a_b = [a.astype(BF16) for a in a_list]
    p_b = [-a for a in a_b]
    inv = [eye - a for a in a_list]
    for _ in range(int(math.log2(n)) - 1):
        p_b = [_dot(p, p).astype(BF16) for p in p_b]
        inv = [x + _dot(x.astype(BF16), p) for x, p in zip(inv, p_b)]
    inv_b = [x.astype(BF16) for x in inv]
    sol = [_dot(x, r.astype(BF16)) for x, r in zip(inv_b, rhs_list)]
    resid = [r - s - _dot(a, s.astype(BF16)) for r, s, a in zip(rhs_list, sol, a_b)]
    return [s + _dot(x, r.astype(BF16)) for s, x, r in zip(sol, inv_b, resid)]


def _gdn_kernel(*refs, t_len, has_state, emit_state):
    it = iter(refs)
    qkv_ref, z_ref, misc_ref, cw_ref, alog_ref, dtb_ref, g_ref = (next(it) for _ in range(7))
    s0_ref = next(it) if has_state else None
    o_ref = next(it)
    sfin_ref = next(it) if emit_state else None
    (q_s, k_s, v_s, kt_s, bet, gat, gat_t, u_s, w_s, qd_s, a_s, ktl_s, gts, o_f, o_b,
     s_s) = (next(it) for _ in range(16))

    c = CHUNK
    n_chunks = t_len // c
    nh = GDN_HEADS
    dk = GDN_DK

    for h in range(nh):
        hs = slice(h * dk, (h + 1) * dk)
        q, k, v = (
            _silu(_short_conv(qkv_ref[:, j * GDN_W + h * dk:j * GDN_W + (h + 1) * dk].astype(F32),
                              cw_ref[:, j * GDN_W + h * dk:j * GDN_W + (h + 1) * dk], t_len))
            for j in range(3))
        q = q * lax.rsqrt(jnp.sum(q * q, axis=-1, keepdims=True) + EPS) * (dk ** -0.5)
        k = k * lax.rsqrt(jnp.sum(k * k, axis=-1, keepdims=True) + EPS)
        q_s[:, hs] = q
        k_s[:, hs] = k
        v_s[:, hs] = v
        kt_s[hs, :] = k.T.astype(BF16)
    misc = misc_ref[...]
    bet[...] = _sigmoid(misc)
    g_all = -jnp.exp(alog_ref[...]) * _softplus(misc + dtb_ref[...])
    gat[...] = g_all
    gat_t[...] = g_all.T[MISC_A:MISC_A + 2 * nh, :]
    for d in range(2):
        for h in range(nh):
            s_s[d * nh + h] = s0_ref[d, h] if has_state else jnp.zeros((dk, GDN_DV), F32)

    two = 2 * c
    r2 = lax.broadcasted_iota(jnp.int32, (two, two), 0)
    c2 = lax.broadcasted_iota(jnp.int32, (two, two), 1)
    same = (r2 >= c) == (c2 >= c)
    lo2 = jnp.where(jnp.logical_and(same, r2 >= c2), 1.0, 0.0).astype(BF16)
    up2 = jnp.where(jnp.logical_and(same, r2 <= c2), 1.0, 0.0).astype(BF16)
    incl = (_tri(c, "ge"), _tri(c, "le"))
    strict = (_tri(c, "gt"), _tri(c, "lt"))
    last = (c - 1, 0)

    def phase1(cp, carry):
        r0 = pl.multiple_of(cp * two, two)
        g_blk = gat[pl.ds(r0, two), :]
        b_blk = bet[pl.ds(r0, two), :]
        gt_blk = gat_t[:, pl.ds(r0, two)]
        cum_col = (_mm_mask_lhs(lo2, g_blk), _mm_mask_lhs(up2, g_blk))
        cum_row = (_mm_mask_rhs(gt_blk, up2), _mm_mask_rhs(gt_blk, lo2))
        chains, a_list, rhs_list = [], [], []
        for h in range(nh):
            hs = slice(h * dk, (h + 1) * dk)
            kt_pair = kt_s[hs, pl.ds(r0, two)]
            for s in range(2):
                rows = pl.ds(pl.multiple_of(r0 + s * c, c), c)
                q_c = q_s[rows, hs]
                k_c = k_s[rows, hs]
                v_c = v_s[rows, hs]
                kt_c = kt_pair[:, s * c:(s + 1) * c]
                kk = _dot(k_c.astype(BF16), kt_c)
                qk = _dot(q_c.astype(BF16), kt_c)
                for d in range(2):
                    idx = d * nh + h
                    gc = cum_col[d][s * c:(s + 1) * c, MISC_A + idx:MISC_A + idx + 1]
                    gcr = cum_row[d][idx:idx + 1, s * c:(s + 1) * c]
                    beta = b_blk[s * c:(s + 1) * c, MISC_B + idx:MISC_B + idx + 1]
                    decay = jnp.where(incl[d], jnp.exp(jnp.where(incl[d], gc - gcr, 0.0)), 0.0)
                    e_gc = jnp.exp(gc)
                    a_list.append(jnp.where(strict[d], beta * kk * decay, 0.0))
                    rhs_list.append(jnp.concatenate([v_c * beta, k_c * (beta * e_gc)], axis=1))
                    a_s[idx, rows, :] = jnp.where(incl[d], qk * decay, 0.0).astype(BF16)
                    qd_s[idx, rows, :] = (q_c * e_gc).astype(BF16)
                    g_last = gcr[:, last[d]:last[d] + 1]
                    ktl_s[idx, cp * 2 + s] = (kt_c.astype(F32) * jnp.exp(g_last - gcr)).astype(BF16)
                    gts[cp * 2 + s, idx:idx + 1, :] = jnp.broadcast_to(jnp.exp(g_last), (1, LANE))
                    chains.append((idx, rows))
        for (idx, rows), sol in zip(chains, _solve_unit_tri(a_list, rhs_list)):
            u_s[idx, rows, :] = sol[:, :GDN_DV]
            w_s[idx, rows, :] = sol[:, GDN_DV:].astype(BF16)
        return carry

    lax.fori_loop(0, n_chunks // 2, phase1, 0)

    def phase2(i, carry):
        chains = []
        for d, out in ((0, o_f), (1, o_b)):
            ci = i if d == 0 else n_chunks - 1 - i
            rows = pl.ds(pl.multiple_of(ci * c, c), c)
            g_tail = gts[ci]
            for h in range(nh):
                chains.append((d * nh + h, h, ci, rows, out, g_tail))
        s_f32 = [s_s[idx] for idx, *_ in chains]
        s_b = [s.astype(BF16) for s in s_f32]
        v_b = [(u_s[idx, rows, :] - _dot(w_s[idx, rows, :], sb)).astype(BF16)
               for (idx, _, _, rows, _, _), sb in zip(chains, s_b)]
        for (idx, h, ci, rows, out, g_tail), s, sb, vb in zip(chains, s_f32, s_b, v_b):
            s_s[idx] = s * g_tail[idx:idx + 1, :] + _dot(ktl_s[idx, ci], vb)
            out[rows, h * GDN_DV:(h + 1) * GDN_DV] = _dot(qd_s[idx, rows, :], sb) + _dot(
                a_s[idx, rows, :], vb)
        return carry

    lax.fori_loop(0, n_chunks, phase2, 0)

    for h in range(nh):
        hs = slice(h * GDN_DV, (h + 1) * GDN_DV)
        o = o_f[:, hs] + o_b[:, hs]
        o_ref[:, hs] = _rms(o, g_ref[...]) * _silu(z_ref[:, hs].astype(F32))
    if emit_state:
        for d in range(2):
            for h in range(nh):
                sfin_ref[d, h] = s_s[d * nh + h]


def _gdn(pa3, pf3, conv_w, alog_row, dtb_row, norm_g, s0, layer, emit_state):
    b, t, _ = pa3.shape
    full = lambda shape: pl.BlockSpec(shape, lambda bi: (0,) * len(shape))
    in_specs = [
        pl.BlockSpec((None, t, 3 * GDN_W), lambda bi: (bi, 0, A_GDN_QKV // (3 * GDN_W))),
        pl.BlockSpec((None, t, GDN_W), lambda bi: (bi, 0, A_GDN_Z // GDN_W)),
        pl.BlockSpec((None, t, LANE), lambda bi: (bi, 0, F_MISC // LANE)),
        full((SHORT_CONV, 3 * GDN_W)),
        full((1, LANE)),
        full((1, LANE)),
        full((1, GDN_DV)),
    ]
    args = [pa3, pa3, pf3, conv_w, alog_row, dtb_row, norm_g]
    if s0 is not None:
        in_specs.append(pl.BlockSpec((None, None, 2, GDN_HEADS, GDN_DK, GDN_DV),
                                     lambda bi: (bi, layer, 0, 0, 0, 0)))
        args.append(s0)
    out_specs = [pl.BlockSpec((None, t, GDN_W), lambda bi: (bi, 0, 0))]
    out_shape = [jax.ShapeDtypeStruct((b, t, GDN_W), F32)]
    if emit_state:
        out_specs.append(pl.BlockSpec((None, 2, GDN_HEADS, GDN_DK, GDN_DV), lambda bi: (bi, 0, 0, 0, 0)))
        out_shape.append(jax.ShapeDtypeStruct((b, 2, GDN_HEADS, GDN_DK, GDN_DV), F32))
    nc = t // CHUNK
    nd = 2 * GDN_HEADS
    wide = pltpu.VMEM((t, GDN_W), F32)
    scratch = [
        wide, wide, wide,
        pltpu.VMEM((GDN_W, t), BF16),
        pltpu.VMEM((t, LANE), F32), pltpu.VMEM((t, LANE), F32),
        pltpu.VMEM((nd, t), F32),
        pltpu.VMEM((nd, t, GDN_DV), F32),
        pltpu.VMEM((nd, t, GDN_DK), BF16),
        pltpu.VMEM((nd, t, GDN_DK), BF16),
        pltpu.VMEM((nd, t, CHUNK), BF16),
        pltpu.VMEM((nd, nc, GDN_DK, CHUNK), BF16),
        pltpu.VMEM((nc, nd, LANE), F32),
        wide, wide,
        pltpu.VMEM((nd, GDN_DK, GDN_DV), F32),
    ]
    outs = pl.pallas_call(
        functools.partial(_gdn_kernel, t_len=t, has_state=s0 is not None, emit_state=emit_state),
        grid=(b,),
        in_specs=in_specs,
        out_specs=out_specs,
        out_shape=out_shape,
        scratch_shapes=scratch,
        compiler_params=_cparams(("arbitrary",)),
    )(*args)
    return outs if emit_state else (outs[0], None)


def _gla_kernel(*refs, t_len, has_state, emit_state):
    it = iter(refs)
    q_ref, k_ref, v_ref, r_ref, misc_ref, wg_ref, bg_ref, g_ref = (next(it) for _ in range(8))
    s0_ref = next(it) if has_state else None
    o_ref = next(it)
    sfin_ref = next(it) if emit_state else None
    glog, vt_s, o_f, o_b, st = (next(it) for _ in range(5))

    c = CHUNK
    two = 2 * c
    n_pairs = t_len // two
    nh = GLA_HEADS
    misc = misc_ref[...]
    for d in range(2):
        lr = misc[:, MISC_LR + d * GLA_RANK:MISC_LR + (d + 1) * GLA_RANK]
        logits = _mm_hi(lr, wg_ref[d]) + bg_ref[d:d + 1, :]
        glog[d] = -_softplus(-logits) / GLA_TAU
    for h in range(nh):
        hs = slice(h * GLA_DV, (h + 1) * GLA_DV)
        vt_s[hs, :] = v_ref[:, hs].astype(F32).T.astype(BF16)
    for d in range(2):
        for h in range(nh):
            if has_state:
                s0 = jnp.concatenate([s0_ref[d, h], jnp.zeros((GLA_DV - GLA_DK, GLA_DV), F32)], axis=0)
                st[d * nh + h] = s0.T[:, :GLA_DK]
            else:
                st[d * nh + h] = jnp.zeros((GLA_DV, GLA_DK), F32)

    r2 = lax.broadcasted_iota(jnp.int32, (two, two), 0)
    c2 = lax.broadcasted_iota(jnp.int32, (two, two), 1)
    same = (r2 >= c) == (c2 >= c)
    cum_mask = (jnp.where(jnp.logical_and(same, r2 >= c2), 1.0, 0.0).astype(BF16),
                jnp.where(jnp.logical_and(same, r2 <= c2), 1.0, 0.0).astype(BF16))
    incl = (_tri(c, "ge"), _tri(c, "le"))
    last = (c - 1, 0)
    mid = (c // 2, c - 1 - c // 2)
    order = ((0, 1), (1, 0))

    def body(i, carry):
        pieces = []
        for d in range(2):
            r0 = pl.multiple_of((i if d == 0 else n_pairs - 1 - i) * two, two)
            b_all = _mm_mask_lhs(cum_mask[d], glog[d, pl.ds(r0, two), :])
            q_all = q_ref[pl.ds(r0, two), :].astype(F32) * (GLA_DK ** -0.5)
            k_all = k_ref[pl.ds(r0, two), :].astype(F32)
            vt_pair = [vt_s[h * GLA_DV:(h + 1) * GLA_DV, pl.ds(r0, two)] for h in range(nh)]
            for s in range(2):
                rs = slice(s * c, (s + 1) * c)
                bq = b_all[rs]
                b_mid = bq[mid[d]:mid[d] + 1, :]
                b_last = bq[last[d]:last[d] + 1, :]
                qe = (q_all[rs] * jnp.exp(bq - b_mid)).astype(BF16)
                ke = (k_all[rs] * jnp.exp(b_mid - bq)).astype(BF16)
                qd = (q_all[rs] * jnp.exp(bq)).astype(BF16)
                kt = (k_all[rs] * jnp.exp(b_last - bq)).astype(BF16)
                g_last = jnp.exp(b_last)
                rows = pl.ds(pl.multiple_of(r0 + s * c, c), c)
                for h in range(nh):
                    ks = slice(h * GLA_DK, (h + 1) * GLA_DK)
                    pieces.append((d, s, h, rows, qe[:, ks], ke[:, ks], qd[:, ks], kt[:, ks], g_last[:, ks],
                                   vt_pair[h][:, rs], v_ref[rows, h * GLA_DV:(h + 1) * GLA_DV]))
        a_intra = [jnp.where(incl[p[0]], _mm_nt(p[4], p[5]), 0.0).astype(BF16) for p in pieces]
        upd = [_dot(p[9], p[7]) for p in pieces]
        intra = [_dot(a, p[10]) for a, p in zip(a_intra, pieces)]
        for step in range(2):
            for p, m, o_in in zip(pieces, upd, intra):
                d, s, h, rows = p[0], p[1], p[2], p[3]
                if s != order[d][step]:
                    continue
                s_t = st[d * nh + h]
                out = o_f if d == 0 else o_b
                out[rows, h * GLA_DV:(h + 1) * GLA_DV] = _mm_nt(p[6], s_t) + o_in
                st[d * nh + h] = s_t * p[8] + m
        return carry

    lax.fori_loop(0, n_pairs, body, 0)

    for h in range(nh):
        hs = slice(h * GLA_DV, (h + 1) * GLA_DV)
        o = o_f[:, hs] + o_b[:, hs]
        o_ref[:, hs] = _rms(o, g_ref[...]) * _silu(r_ref[:, hs].astype(F32))
    if emit_state:
        for d in range(2):
            for h in range(nh):
                s_pad = jnp.concatenate([st[d * nh + h], jnp.zeros((GLA_DV, GLA_DV - GLA_DK), F32)], axis=1)
                sfin_ref[d, h] = s_pad.T[:GLA_DK, :]


def _gla(pa3, pf3, wg, bg, norm_g, s0, layer, emit_state):
    b, t, _ = pa3.shape
    full = lambda shape: pl.BlockSpec(shape, lambda bi: (0,) * len(shape))
    in_specs = [
        pl.BlockSpec((None, t, GLA_KW), lambda bi: (bi, 0, A_GLA_Q // GLA_KW)),
        pl.BlockSpec((None, t, GLA_KW), lambda bi: (bi, 0, A_GLA_K // GLA_KW)),
        pl.BlockSpec((None, t, GLA_VW), lambda bi: (bi, 0, A_GLA_V // GLA_VW)),
        pl.BlockSpec((None, t, GLA_VW), lambda bi: (bi, 0, A_GLA_R // GLA_VW)),
        pl.BlockSpec((None, t, LANE), lambda bi: (bi, 0, F_MISC // LANE)),
        full((2, GLA_RANK, GLA_KW)),
        full((2, GLA_KW)),
        full((1, GLA_DV)),
    ]
    args = [pa3, pa3, pa3, pa3, pf3, wg, bg, norm_g]
    if s0 is not None:
        in_specs.append(pl.BlockSpec((None, None, 2, GLA_HEADS, GLA_DK, GLA_DV),
                                     lambda bi: (bi, layer, 0, 0, 0, 0)))
        args.append(s0)
    out_specs = [pl.BlockSpec((None, t, GLA_VW), lambda bi: (bi, 0, 0))]
    out_shape = [jax.ShapeDtypeStruct((b, t, GLA_VW), F32)]
    if emit_state:
        out_specs.append(pl.BlockSpec((None, 2, GLA_HEADS, GLA_DK, GLA_DV), lambda bi: (bi, 0, 0, 0, 0)))
        out_shape.append(jax.ShapeDtypeStruct((b, 2, GLA_HEADS, GLA_DK, GLA_DV), F32))
    scratch = [pltpu.VMEM((2, t, GLA_KW), F32),
               pltpu.VMEM((GLA_VW, t), BF16),
               pltpu.VMEM((t, GLA_VW), F32), pltpu.VMEM((t, GLA_VW), F32),
               pltpu.VMEM((2 * GLA_HEADS, GLA_DV, GLA_DK), F32)]
    outs = pl.pallas_call(
        functools.partial(_gla_kernel, t_len=t, has_state=s0 is not None, emit_state=emit_state),
        grid=(b,),
        in_specs=in_specs,
        out_specs=out_specs,
        out_shape=out_shape,
        scratch_shapes=scratch,
        compiler_params=_cparams(("arbitrary",)),
    )(*args)
    return outs if emit_state else (outs[0], None)


ATT_Q_BLOCK = 256


def _rope_tables(t_len):
    half = DIFF_DH // 2
    quarter = half // 2
    inv = ROPE_THETA ** (-np.arange(0, half, 2, dtype=np.float64) / half)
    tok = np.arange(t_len)
    pos = np.stack([tok // GRID_W, tok % GRID_W], axis=1).astype(np.float64)
    ang = pos[:, :, None] * inv[None, None, :]
    cos = np.concatenate([np.cos(ang), np.cos(ang)], axis=-1).reshape(t_len, DIFF_DH)
    sin = np.concatenate([-np.sin(ang), np.sin(ang)], axis=-1).reshape(t_len, DIFF_DH)
    cos = np.concatenate([cos, cos], axis=-1).astype(np.float32)
    sin = np.concatenate([sin, sin], axis=-1).astype(np.float32)
    first = ((np.arange(2 * DIFF_DH) % half) < quarter).astype(np.float32)[None, :]
    return jnp.asarray(cos), jnp.asarray(sin), jnp.asarray(first), quarter


def _rope(x, cos, sin, first, quarter):
    width = x.shape[-1]
    ahead = pltpu.roll(x, width - quarter, axis=1)
    behind = pltpu.roll(x, quarter, axis=1)
    partner = jnp.where(first > 0.5, ahead, behind)
    return x * cos + partner * sin


def _diff_kernel(*refs, t_len, ctx_len, lam_init):
    it = iter(refs)
    q_ref, k_ref, v_ref, lam_ref, g_ref = (next(it) for _ in range(5))
    if ctx_len:
        ck_ref, cv_ref, cos_ref, sin_ref, first_ref = (next(it) for _ in range(5))
    o_ref = next(it)
    q_sc, k_sc, v_sc = (next(it) for _ in range(3))

    scale = DIFF_DH ** -0.5
    for h in range(DIFF_HEADS):
        hs = slice(h * LANE, (h + 1) * LANE)
        q = q_ref[:, hs].astype(F32)
        k = k_ref[:, hs]
        if ctx_len:
            quarter = DIFF_DH // 4
            q = _rope(q, cos_ref[...], sin_ref[...], first_ref[...], quarter)
            k = _rope(k, cos_ref[...], sin_ref[...], first_ref[...], quarter)
        q_sc[:, hs] = (q * scale).astype(BF16)
        k_sc[0:t_len, hs] = k.astype(BF16)
    v_sc[0:t_len, :] = v_ref[...].astype(BF16)
    if ctx_len:
        k_sc[t_len:t_len + ctx_len, :] = ck_ref[...].astype(BF16)
        v_sc[t_len:t_len + ctx_len, :] = cv_ref[...].astype(BF16)

    lp = lam_ref[...]
    lam = (jnp.exp(jnp.sum(lp[0:1, :] * lp[1:2, :], axis=1, keepdims=True))
           - jnp.exp(jnp.sum(lp[2:3, :] * lp[3:4, :], axis=1, keepdims=True)) + lam_init)

    tq = min(ATT_Q_BLOCK, t_len)
    group = DIFF_HEADS if t_len + ctx_len <= 512 else 2

    def body(i, carry):
        r0 = pl.multiple_of(i * tq, tq)
        for h0 in range(0, DIFF_HEADS, group):
            parts = [(h, half) for h in range(h0, h0 + group) for half in range(2)]
            cols = [slice(h * LANE + half * DIFF_DH, h * LANE + (half + 1) * DIFF_DH) for h, half in parts]
            s = [_mm_nt(q_sc[pl.ds(r0, tq), c], k_sc[:, c]) for c in cols]
            e = [jnp.exp(x - jnp.max(x, axis=-1, keepdims=True)) for x in s]
            inv_l = [1.0 / jnp.sum(x, axis=-1, keepdims=True) for x in e]
            pv = [_dot(x.astype(BF16), v_sc[:, h * LANE:(h + 1) * LANE]) * r
                  for x, r, (h, _) in zip(e, inv_l, parts)]
            for j in range(group):
                h = h0 + j
                o = pv[2 * j] - lam * pv[2 * j + 1]
                o_ref[pl.ds(r0, tq), h * LANE:(h + 1) * LANE] = _rms(o, g_ref[...]) * (1.0 - lam_init)
        return carry

    lax.fori_loop(0, t_len // tq, body, 0)


def _diff(pa3, pf3, lam_p, norm_g, ctx_k, ctx_v, layer, lam_init):
    b, t, _ = pa3.shape
    ctx_len = 0 if ctx_k is None else ctx_k.shape[2]
    full = lambda shape: pl.BlockSpec(shape, lambda bi: (0,) * len(shape))
    in_specs = [
        pl.BlockSpec((None, t, DIFF_QW), lambda bi: (bi, 0, A_DIFF_Q // DIFF_QW)),
        pl.BlockSpec((None, t, DIFF_QW), lambda bi: (bi, 0, F_DIFF_K // DIFF_QW)),
        pl.BlockSpec((None, t, DIFF_VW), lambda bi: (bi, 0, F_DIFF_V // DIFF_VW)),
        full((4, DIFF_DH)),
        full((1, DIFF_VD)),
    ]
    args = [pa3, pf3, pf3, lam_p, norm_g]
    if ctx_len:
        cos, sin, first, _ = _rope_tables(t)
        in_specs += [
            pl.BlockSpec((None, None, ctx_len, DIFF_QW), lambda bi: (bi, layer, 0, 0)),
            pl.BlockSpec((None, None, ctx_len, DIFF_VW), lambda bi: (bi, layer, 0, 0)),
            full((t, LANE)),
            full((t, LANE)),
            full((1, LANE)),
        ]
        args += [ctx_k, ctx_v, cos, sin, first]
    tk = t + ctx_len
    return pl.pallas_call(
        functools.partial(_diff_kernel, t_len=t, ctx_len=ctx_len, lam_init=lam_init),
        grid=(b,),
        in_specs=in_specs,
        out_specs=pl.BlockSpec((None, t, DIFF_VW), lambda bi: (bi, 0, 0)),
        out_shape=jax.ShapeDtypeStruct((b, t, DIFF_VW), F32),
        scratch_shapes=[pltpu.VMEM((t, DIFF_QW), BF16), pltpu.VMEM((tk, DIFF_QW), BF16),
                        pltpu.VMEM((tk, DIFF_VW), BF16)],
        compiler_params=_cparams(("arbitrary",)),
    )(*args)


def _route(logits_t, bias_col):
    scores = _sigmoid(logits_t)
    biased = scores + bias_col
    rows = [biased[e:e + 1, :] for e in range(N_EXPERTS)]
    grp = []
    for g in range(N_GROUPS):
        a0, a1, a2, a3 = rows[g * GROUP_SIZE:(g + 1) * GROUP_SIZE]
        hi01, lo01 = jnp.maximum(a0, a1), jnp.minimum(a0, a1)
        hi23, lo23 = jnp.maximum(a2, a3), jnp.minimum(a2, a3)
        top1 = jnp.maximum(hi01, hi23)
        top2 = jnp.maximum(jnp.minimum(hi01, hi23), jnp.maximum(lo01, lo23))
        grp.append(top1 + top2)
    best = []
    for g in range(N_GROUPS):
        win = None
        for o in range(N_GROUPS):
            if o == g:
                continue
            cond = grp[g] > grp[o] if o < g else grp[g] >= grp[o]
            win = cond if win is None else jnp.logical_and(win, cond)
        best.append(win)
    sel_rows = []
    for e in range(N_EXPERTS):
        g = e // GROUP_SIZE
        beaten = jnp.zeros_like(rows[e])
        for o in range(g * GROUP_SIZE, (g + 1) * GROUP_SIZE):
            if o == e:
                continue
            ahead = rows[o] >= rows[e] if o < e else rows[o] > rows[e]
            beaten = beaten + jnp.where(ahead, 1.0, 0.0)
        sel_rows.append(jnp.where(jnp.logical_and(best[g], beaten < 1.5), 1.0, 0.0))
    sel = jnp.concatenate(sel_rows, axis=0)
    picked = sel * scores
    return sel, picked / jnp.sum(picked, axis=0, keepdims=True)


PIECE = 16
ROW_TILE = 256
SLOT_PAD = N_EXPERTS * PIECE
HS_COLS = D_MODEL + LANE


def _route_meta(sel, wts):
    n_e, t = sel.shape
    e_col = lax.broadcasted_iota(jnp.int32, (n_e, 1), 0).astype(F32)
    before = jnp.where(_tri(t, "lt"), 1.0, 0.0).astype(BF16)
    rank = _dot(sel.astype(BF16), before)
    cnt = jnp.sum(sel, axis=1, keepdims=True)
    plen = jnp.floor((cnt + (PIECE - 1.0)) * (1.0 / PIECE)) * PIECE
    lower = jnp.where(_tri(n_e, "gt"), 1.0, 0.0).astype(BF16)
    loff = _dot(lower, jnp.broadcast_to(plen, (n_e, LANE)).astype(BF16))[:, 0:1]
    slot = loff + rank
    chosen = sel > 0.5
    e_a = jnp.min(jnp.where(chosen, e_col, 2.0 * n_e), axis=0, keepdims=True)
    e_b = jnp.max(jnp.where(chosen, e_col, -1.0), axis=0, keepdims=True)
    take = lambda e_row, x: jnp.sum(jnp.where(e_col == e_row, x, 0.0), axis=0, keepdims=True)
    rows = jnp.concatenate([take(e_a, slot), take(e_b, slot), take(e_a, wts), take(e_b, wts), e_a, e_b,
                            jnp.zeros((2, t), F32)], axis=0)
    return rows, plen, loff


def _merge_kernel(og_ref, ol_ref, od_ref, lg0_ref, lg1_ref, lg2_ref, x_ref, mod_ref, wb_ref, wo_ref,
                  lng_ref, lnb_ref, rw_ref, rb_ref, x1_ref, h_ref, mrow_ref, mcol_ref, plen_ref, loff_ref):
    acc = None
    for s, (o_ref, lg_ref) in enumerate(((og_ref, lg0_ref), (ol_ref, lg1_ref), (od_ref, lg2_ref))):
        proj = _mm(o_ref[...], wb_ref[s])
        term = _sigmoid(lg_ref[...].astype(F32)) * proj
        acc = term if acc is None else acc + term
    m = _mm(acc, wo_ref[...])
    gate1 = mod_ref[:, 2 * D_MODEL:3 * D_MODEL]
    shift2 = mod_ref[:, 3 * D_MODEL:4 * D_MODEL]
    scale2 = mod_ref[:, 4 * D_MODEL:5 * D_MODEL]
    x1 = _layernorm(ALPHA * x_ref[...] + gate1 * m, lng_ref[...], lnb_ref[...])
    x1_ref[...] = x1
    h = x1 * (1.0 + scale2) + shift2
    h_ref[...] = h.astype(BF16)
    sel, gates_t = _route(_mm_hi_nt(rw_ref[...], h), rb_ref[...])
    rows, plen, loff = _route_meta(sel, gates_t)
    tm = rows.shape[1]
    mrow_ref[...] = rows
    mcol_ref[...] = jnp.concatenate([rows, jnp.zeros((LANE - rows.shape[0], tm), F32)], axis=0).T
    lane = lax.broadcasted_iota(jnp.int32, (N_EXPERTS, LANE), 1)
    cols = jnp.where(lane == 0, plen, jnp.where(lane == 1, loff, 0.0))
    cols_t = jnp.concatenate([cols, jnp.zeros((LANE - N_EXPERTS, LANE), F32)], axis=0).T
    plen_ref[...] = cols_t[0:1, :]
    loff_ref[...] = cols_t[1:2, :]


def _merge(o_gdn, o_gla, o_diff, pa, x, mod, wb, wo, ln_g, ln_b, rw_t, rb_col, layer, row0,
           tokens_per_row, tm):
    n = x.shape[0]
    mod_row = lambda i: (layer, row0 + (i * tm) // tokens_per_row, 0, 0)
    full = lambda shape: pl.BlockSpec(shape, lambda i: (0,) * len(shape))
    return pl.pallas_call(
        _merge_kernel,
        grid=(n // tm,),
        in_specs=[
            pl.BlockSpec((tm, MIX_W), lambda i: (i, 0)),
            pl.BlockSpec((tm, MIX_W), lambda i: (i, 0)),
            pl.BlockSpec((tm, MIX_W), lambda i: (i, 0)),
            pl.BlockSpec((tm, D_MODEL), lambda i: (i, A_MERGE // D_MODEL)),
            pl.BlockSpec((tm, D_MODEL), lambda i: (i, A_MERGE // D_MODEL + 1)),
            pl.BlockSpec((tm, D_MODEL), lambda i: (i, A_MERGE // D_MODEL + 2)),
            pl.BlockSpec((tm, D_MODEL), lambda i: (i, 0)),
            pl.BlockSpec((None, None, 1, 6 * D_MODEL), mod_row),
            full((N_BRANCH, MIX_W, D_MODEL)),
            full((D_MODEL, D_MODEL)),
            full((1, D_MODEL)),
            full((1, D_MODEL)),
            full((N_EXPERTS, D_MODEL)),
            full((N_EXPERTS, 1)),
        ],
        out_specs=[
            pl.BlockSpec((tm, D_MODEL), lambda i: (i, 0)),
            pl.BlockSpec((tm, D_MODEL), lambda i: (i, 0)),
            pl.BlockSpec((8, tm), lambda i: (0, i)),
            pl.BlockSpec((tm, LANE), lambda i: (i, 0)),
            pl.BlockSpec((None, 1, LANE), lambda i: (i, 0, 0)),
            pl.BlockSpec((None, 1, LANE), lambda i: (i, 0, 0)),
        ],
        out_shape=[
            jax.ShapeDtypeStruct((n, D_MODEL), F32),
            jax.ShapeDtypeStruct((n, D_MODEL), BF16),
            jax.ShapeDtypeStruct((8, n), F32),
            jax.ShapeDtypeStruct((n, LANE), F32),
            jax.ShapeDtypeStruct((n // tm, 1, LANE), F32),
            jax.ShapeDtypeStruct((n // tm, 1, LANE), F32),
        ],
        compiler_params=_cparams(("arbitrary",)),
    )(o_gdn, o_gla, o_diff, pa, pa, pa, x, mod, wb, wo, ln_g, ln_b, rw_t, rb_col)


def _plan_kernel(plen_ref, loff_ref, goff_ref, npc_ref, loffi_ref, tiles_ref, tail_ref, *, n_steps):
    plen = plen_ref[...]
    nb = plen.shape[0]
    earlier = jnp.where(_tri(nb, "gt"), 1.0, 0.0).astype(BF16)
    run = _dot(earlier, plen.astype(BF16))
    gtot = jnp.sum(plen, axis=0, keepdims=True)
    lane = lax.broadcasted_iota(jnp.int32, (1, LANE), 1).astype(F32)
    ntile = jnp.floor((gtot + (ROW_TILE - 1.0)) * (1.0 / ROW_TILE))
    before = jnp.where(_tri(LANE, "lt"), 1.0, 0.0).astype(BF16)
    first = _dot(jnp.broadcast_to(ntile, (8, LANE)).astype(BF16), before)[0:1, :]
    end = first + ntile
    n_used = jnp.sum(ntile, axis=1, keepdims=True)
    goff_ref[...] = (first * ROW_TILE + run).astype(jnp.int32)
    npc_ref[...] = (plen * (1.0 / PIECE)).astype(jnp.int32)
    loffi_ref[...] = loff_ref[...].astype(jnp.int32)
    end_col = jnp.concatenate([end, jnp.zeros((LANE - 1, LANE), F32)], axis=0).T[:, 0:1]
    e_col = lax.broadcasted_iota(jnp.int32, (LANE, 1), 0).astype(F32)
    tile = jnp.minimum(lane, n_used - 1.0)
    done = jnp.logical_and(end_col <= tile, e_col < N_EXPERTS)
    t_exp = jnp.minimum(jnp.sum(jnp.where(done, 1.0, 0.0), axis=0, keepdims=True), N_EXPERTS - 1.0)
    nonempty = jnp.where(ntile > 0.5, 1.0, 0.0)
    run_idx = _dot(jnp.broadcast_to(nonempty, (8, LANE)).astype(BF16), before)[0:1, :]
    owner_after = jnp.logical_and(end_col <= end, e_col < N_EXPERTS)
    nxt = jnp.minimum(jnp.sum(jnp.where(owner_after, 1.0, 0.0), axis=0, keepdims=True), N_EXPERTS - 1.0)
    has_next = jnp.where(end < n_used, 1.0, 0.0)
    by_expert = jnp.concatenate([run_idx - 2.0 * jnp.floor(run_idx * 0.5), nxt, has_next,
                                 jnp.zeros((LANE - 3, LANE), F32)], axis=0).T
    mine = e_col == t_exp
    per_tile = [jnp.sum(jnp.where(mine, by_expert[:, j:j + 1], 0.0), axis=0, keepdims=True) for j in range(3)]
    tiles_ref[...] = jnp.concatenate(
        [t_exp, jnp.broadcast_to(n_used, (1, LANE))] + per_tile + [jnp.zeros((3, LANE), F32)],
        axis=0).astype(jnp.int32)
    tail_ref[...] = jnp.concatenate(
        [first * ROW_TILE + gtot, (ntile * ROW_TILE - gtot) * (1.0 / PIECE),
         jnp.broadcast_to(n_used, (1, LANE)), jnp.broadcast_to(n_steps - n_used, (1, LANE)),
         jnp.zeros((4, LANE), F32)], axis=0).astype(jnp.int32)


def _plan(plen, loff, n_steps):
    nb = plen.shape[0]
    i32 = lambda rows: jax.ShapeDtypeStruct((rows, LANE), jnp.int32)
    return pl.pallas_call(
        functools.partial(_plan_kernel, n_steps=n_steps),
        out_shape=[i32(nb), i32(nb), i32(nb), i32(8), i32(8)],
    )(plen, loff)


def _piece_copies(b, loff_s, npc_s, goff_s, local, remote, sem, to_remote):
    def copy(lo, go):
        src = local.at[pl.ds(pl.multiple_of(lo, PIECE), PIECE)]
        dst = remote.at[pl.ds(pl.multiple_of(go, PIECE), PIECE)]
        return pltpu.make_async_copy(src, dst, sem) if to_remote else pltpu.make_async_copy(dst, src, sem)

    total = 0
    for e in range(N_EXPERTS):
        n = npc_s[b, e]
        lo = loff_s[b, e]
        go = goff_s[b, e]

        def start(j, carry, lo=lo, go=go):
            copy(lo + j * PIECE, go + j * PIECE).start()
            return carry

        lax.fori_loop(0, n, start, 0)
        total = total + n

    def wait(j, carry):
        copy(0, 0).wait()
        return carry

    lax.fori_loop(0, total, wait, 0)


def _dispatch_kernel(loff_s, npc_s, goff_s, tail_s, h_ref, mrow_ref, mcol_ref, hs_hbm, local, zeros, sem,
                     tile_sem, *, slots):
    b = pl.program_id(0)
    tb = h_ref.shape[0]
    mrow = mrow_ref[...]
    mcol = mcol_ref[...]
    slot_id = lax.broadcasted_iota(jnp.int32, (slots, tb), 0).astype(F32)
    s_a = jnp.where(slot_id == mrow[0:1, :], 1.0, 0.0).astype(BF16)
    s_b = jnp.where(slot_id == mrow[1:2, :], 1.0, 0.0).astype(BF16)
    lane = lax.broadcasted_iota(jnp.int32, (tb, LANE), 1)

    def weight_lanes(w):
        hi = w.astype(BF16).astype(F32)
        return jnp.where(lane == 0, hi, jnp.where(lane == 1, w - hi, 0.0)).astype(BF16)

    local[:, :D_MODEL] = _dot(s_a + s_b, h_ref[...]).astype(BF16)
    local[:, D_MODEL:] = (_dot(s_a, weight_lanes(mcol[:, 2:3])) + _dot(s_b, weight_lanes(mcol[:, 3:4]))).astype(BF16)
    _piece_copies(b, loff_s, npc_s, goff_s, local, hs_hbm, sem, True)

    @pl.when(b == pl.num_programs(0) - 1)
    def _():
        zeros[...] = jnp.zeros_like(zeros)
        zero_piece = zeros.at[pl.ds(0, PIECE)]
        total = 0
        for e in range(N_EXPERTS):
            n = tail_s[1, e]
            start_row = tail_s[0, e]

            def start(j, carry, start_row=start_row):
                dst = hs_hbm.at[pl.ds(pl.multiple_of(start_row + j * PIECE, PIECE), PIECE)]
                pltpu.make_async_copy(zero_piece, dst, sem).start()
                return carry

            lax.fori_loop(0, n, start, 0)
            total = total + n

        def wait(j, carry):
            pltpu.make_async_copy(zero_piece, hs_hbm.at[pl.ds(0, PIECE)], sem).wait()
            return carry

        lax.fori_loop(0, total, wait, 0)

        def tile_copy(j):
            row = pl.multiple_of((tail_s[2, 0] + j) * ROW_TILE, ROW_TILE)
            return pltpu.make_async_copy(zeros, hs_hbm.at[pl.ds(row, ROW_TILE)], tile_sem)

        def start_tile(j, carry):
            tile_copy(j).start()
            return carry

        def wait_tile(j, carry):
            tile_copy(j).wait()
            return carry

        lax.fori_loop(0, tail_s[3, 0], start_tile, 0)
        lax.fori_loop(0, tail_s[3, 0], wait_tile, 0)


def _dispatch(loff_i, npc, goff, tail, h, mrow, mcol, tb, n_steps):
    n = h.shape[0]
    slots = 2 * tb + SLOT_PAD
    grid_spec = pltpu.PrefetchScalarGridSpec(
        num_scalar_prefetch=4,
        grid=(n // tb,),
        in_specs=[
            pl.BlockSpec((tb, D_MODEL), lambda i, *_: (i, 0)),
            pl.BlockSpec((8, tb), lambda i, *_: (0, i)),
            pl.BlockSpec((tb, LANE), lambda i, *_: (i, 0)),
        ],
        out_specs=pl.BlockSpec(memory_space=pl.ANY),
        scratch_shapes=[pltpu.VMEM((slots, HS_COLS), BF16), pltpu.VMEM((ROW_TILE, HS_COLS), BF16),
                        pltpu.SemaphoreType.DMA, pltpu.SemaphoreType.DMA],
    )
    return pl.pallas_call(
        functools.partial(_dispatch_kernel, slots=slots),
        grid_spec=grid_spec,
        out_shape=jax.ShapeDtypeStruct((n_steps * ROW_TILE, HS_COLS), BF16),
        compiler_params=_cparams(("arbitrary",)),
    )(loff_i, npc, goff, tail, h, mrow, mcol)


def _expert_kernel(tiles_s, hs_ref, wg_hbm, wu_hbm, wd_hbm, y_ref, wg_f, wu_f, wd_f, wg_b, wu_b, wd_b, sems,
                   *, layer):
    i = pl.program_id(0)
    e = tiles_s[0, i]
    slot = tiles_s[2, i]
    first_of_run = jnp.logical_or(i == 0, e != tiles_s[0, jnp.maximum(i - 1, 0)])

    def weight_copies(expert, s):
        return [pltpu.make_async_copy(src.at[layer, expert], dst.at[s], sems.at[s, j])
                for j, (src, dst) in enumerate(((wg_hbm, wg_f), (wu_hbm, wu_f), (wd_hbm, wd_f)))]

    @pl.when(i == 0)
    def _():
        for cp in weight_copies(e, slot):
            cp.start()

    @pl.when(jnp.logical_and(first_of_run, i < tiles_s[1, 0]))
    def _():
        for cp in weight_copies(e, slot):
            cp.wait()

        @pl.when(tiles_s[4, i] > 0)
        def _():
            for cp in weight_copies(tiles_s[3, i], 1 - slot):
                cp.start()

        wg_b[...] = wg_f[slot].astype(BF16)
        wu_b[...] = wu_f[slot].astype(BF16)
        wd_b[...] = wd_f[slot].astype(BF16)

    @pl.when(i < tiles_s[1, 0])
    def _():
        halves = [pl.ds(r * (ROW_TILE // 2), ROW_TILE // 2) for r in range(2)]
        xs = [hs_ref[rows, :D_MODEL] for rows in halves]
        ws = [hs_ref[rows, D_MODEL:D_MODEL + 1].astype(F32) + hs_ref[rows, D_MODEL + 1:D_MODEL + 2].astype(F32)
              for rows in halves]
        gs = [_dot(x, wg_b[...]) for x in xs]
        us = [_dot(x, wu_b[...]) for x in xs]
        acts = [(_silu(g) * u * w).astype(BF16) for g, u, w in zip(gs, us, ws)]
        for rows, act in zip(halves, acts):
            y_ref[rows, :] = _dot(act, wd_b[...]).astype(BF16)

    @pl.when(i >= tiles_s[1, 0])
    def _():
        y_ref[...] = jnp.zeros_like(y_ref)


def _experts(tiles, hs, wg, wu, wd, layer, n_steps):
    grid_spec = pltpu.PrefetchScalarGridSpec(
        num_scalar_prefetch=1,
        grid=(n_steps,),
        in_specs=[
            pl.BlockSpec((ROW_TILE, HS_COLS), lambda i, t: (i, 0)),
            pl.BlockSpec(memory_space=pl.ANY),
            pl.BlockSpec(memory_space=pl.ANY),
            pl.BlockSpec(memory_space=pl.ANY),
        ],
        out_specs=pl.BlockSpec((ROW_TILE, D_MODEL), lambda i, t: (i, 0)),
        scratch_shapes=[pltpu.VMEM((2, D_MODEL, D_FF), F32), pltpu.VMEM((2, D_MODEL, D_FF), F32),
                        pltpu.VMEM((2, D_FF, D_MODEL), F32),
                        pltpu.VMEM((D_MODEL, D_FF), BF16), pltpu.VMEM((D_MODEL, D_FF), BF16),
                        pltpu.VMEM((D_FF, D_MODEL), BF16), pltpu.SemaphoreType.DMA((2, 3))],
    )
    return pl.pallas_call(
        functools.partial(_expert_kernel, layer=layer),
        grid_spec=grid_spec,
        out_shape=jax.ShapeDtypeStruct((hs.shape[0], D_MODEL), BF16),
        compiler_params=_cparams(("arbitrary",)),
    )(tiles, hs, wg, wu, wd)


def _combine_kernel(loff_s, npc_s, goff_s, y_hbm, mcol_ref, x1_ref, mod_ref, lng_ref, lnb_ref, o_ref,
                    local, sem, *, slots):
    b = pl.program_id(0)
    tb = x1_ref.shape[0]
    local[...] = jnp.zeros_like(local)
    _piece_copies(b, loff_s, npc_s, goff_s, local, y_hbm, sem, False)
    mcol = mcol_ref[...]
    slot_id = lax.broadcasted_iota(jnp.int32, (tb, slots), 1).astype(F32)
    pick = jnp.logical_or(slot_id == mcol[:, 0:1], slot_id == mcol[:, 1:2])
    f = _dot(jnp.where(pick, 1.0, 0.0).astype(BF16), local[...])
    gate2 = mod_ref[:, 5 * D_MODEL:6 * D_MODEL]
    o_ref[...] = _layernorm(ALPHA * x1_ref[...] + gate2 * f, lng_ref[...], lnb_ref[...])


def _combine(loff_i, npc, goff, y, mcol, x1, mod, ln_g, ln_b, layer, row0, tokens_per_row, tb):
    n = x1.shape[0]
    slots = 2 * tb + SLOT_PAD
    mod_row = lambda i, *_: (layer, row0 + (i * tb) // tokens_per_row, 0, 0)
    grid_spec = pltpu.PrefetchScalarGridSpec(
        num_scalar_prefetch=3,
        grid=(n // tb,),
        in_specs=[
            pl.BlockSpec(memory_space=pl.ANY),
            pl.BlockSpec((tb, LANE), lambda i, *_: (i, 0)),
            pl.BlockSpec((tb, D_MODEL), lambda i, *_: (i, 0)),
            pl.BlockSpec((None, None, 1, 6 * D_MODEL), mod_row),
            pl.BlockSpec((1, D_MODEL), lambda i, *_: (0, 0)),
            pl.BlockSpec((1, D_MODEL), lambda i, *_: (0, 0)),
        ],
        out_specs=pl.BlockSpec((tb, D_MODEL), lambda i, *_: (i, 0)),
        scratch_shapes=[pltpu.VMEM((slots, D_MODEL), BF16), pltpu.SemaphoreType.DMA],
    )
    return pl.pallas_call(
        functools.partial(_combine_kernel, slots=slots),
        grid_spec=grid_spec,
        out_shape=jax.ShapeDtypeStruct((n, D_MODEL), F32),
        compiler_params=_cparams(("arbitrary",)),
    )(loff_i, npc, goff, y, mcol, x1, mod, ln_g, ln_b)


def _moe(h, mrow, mcol, plen, loff, wg, wu, wd, x1, mod, ln_g, ln_b, layer, row0, tokens_per_row, tb):
    n = h.shape[0]
    nb = n // tb
    n_steps = (2 * n + nb * N_EXPERTS * (PIECE - 1)) // ROW_TILE + N_EXPERTS
    assert n_steps <= LANE
    goff, npc, loff_i, tiles, tail = _plan(plen.reshape(nb, LANE), loff.reshape(nb, LANE), n_steps)
    hs = _dispatch(loff_i, npc, goff, tail, h, mrow, mcol, tb, n_steps)
    y = _experts(tiles, hs, wg, wu, wd, layer, n_steps)
    return _combine(loff_i, npc, goff, y, mcol, x1, mod, ln_g, ln_b, layer, row0, tokens_per_row, tb)


def _reorder_w_in(w):
    sizes = (3 * GDN_W, GDN_W, 2 * GDN_HEADS, 2 * GDN_HEADS, GLA_KW, GLA_KW, GLA_VW, GLA_VW,
             2 * GLA_RANK, DIFF_QW, DIFF_QW, DIFF_VW, N_BRANCH * D_MODEL)
    offs = np.concatenate([[0], np.cumsum(sizes)])
    seg = lambda i: w[:, offs[i]:offs[i + 1]]
    used = 2 * GDN_HEADS * 2 + 2 * GLA_RANK
    wa = jnp.concatenate([seg(0), seg(1), seg(4), seg(5), seg(6), seg(7), seg(9), seg(12)], axis=1)
    wf = jnp.concatenate([seg(10), seg(11), seg(2), seg(3), seg(8),
                          jnp.zeros((w.shape[0], LANE - used), w.dtype)], axis=1)
    return wa.astype(BF16), wf.astype(BF16)


def _lane_row(vals, offset):
    return jnp.zeros((1, LANE), F32).at[0, offset:offset + vals.shape[0]].set(vals)


def kernel(x_prompt, x_sample, c, state_gdn, state_gla, cache_k, cache_v, c_ctx, w_mod, b_mod, w_in,
           gdn_conv, gdn_a_log, gdn_dt_bias, gdn_norm, gla_w_gate, gla_b_gate, gla_norm, diff_lambda,
           diff_norm, w_branch, w_out, ln_g, ln_b, router_w, router_b, exp_w_gate, exp_w_up, exp_w_down):
    bp, tp, d = x_prompt.shape
    bs, ts, _ = x_sample.shape
    pad_rows = MOD_ROWS - 1 - bs
    cvecs = jnp.concatenate([c_ctx[None, :], c, jnp.zeros((pad_rows, d), F32)], axis=0)
    mod = _modulation(cvecs, w_mod, b_mod).reshape(DEPTH, MOD_ROWS, 1, 6 * d)

    rw_t = router_w.T
    rb_col = router_b.reshape(N_EXPERTS, 1)
    ck = cache_k.reshape(bs, DEPTH, cache_k.shape[2], DIFF_QW)
    cv = cache_v.reshape(bs, DEPTH, cache_v.shape[2], DIFF_VW)

    layer_w = []
    for l in range(DEPTH):
        wa, wf = _reorder_w_in(w_in[l])
        layer_w.append(dict(
            wa=wa, wf=wf,
            alog=_lane_row(gdn_a_log[l].reshape(-1), MISC_A),
            dtb=_lane_row(gdn_dt_bias[l].reshape(-1), MISC_A),
            wb=w_branch[l].astype(BF16),
            wo=w_out[l].astype(BF16),
        ))

    def layer(x, l, row0, tokens_per_row, s_gdn, s_gla, ctx_k, ctx_v, emit_state, tm):
        b, t, _ = x.shape
        n = b * t
        lw = layer_w[l]
        xf = x.reshape(n, d)
        pa, pf, *cache = _inproj(xf, mod, lw["wa"], lw["wf"], l, row0, tokens_per_row, tm, emit_state)
        pa3 = pa.reshape(b, t, A_COLS)
        pf3 = pf.reshape(b, t, F_COLS)
        o_gdn, gdn_fin = _gdn(pa3, pf3, gdn_conv[l], lw["alog"], lw["dtb"], gdn_norm[l][None, :], s_gdn, l,
                              emit_state)
        o_gla, gla_fin = _gla(pa3, pf3, gla_w_gate[l], gla_b_gate[l], gla_norm[l][None, :], s_gla, l,
                              emit_state)
        lam_init = 0.8 - 0.6 * math.exp(-0.3 * l)
        o_diff = _diff(pa3, pf3, diff_lambda[l], diff_norm[l][None, :], ctx_k, ctx_v, l, lam_init)
        x1, h, mrow, mcol, plen, loff = _merge(
            o_gdn.reshape(n, MIX_W), o_gla.reshape(n, MIX_W), o_diff.reshape(n, MIX_W), pa, xf, mod,
            lw["wb"], lw["wo"], ln_g[l, 0][None, :], ln_b[l, 0][None, :], rw_t, rb_col, l, row0,
            tokens_per_row, tm)
        x2 = _moe(h, mrow, mcol, plen, loff, exp_w_gate, exp_w_up, exp_w_down, x1, mod,
                  ln_g[l, 1][None, :], ln_b[l, 1][None, :], l, row0, tokens_per_row, tm)
        k_own, v_own = [a.reshape(b, t, DIFF_HEADS, LANE) for a in cache] if emit_state else (None, None)
        return x2.reshape(b, t, d), gdn_fin, gla_fin, k_own, v_own

    hp = x_prompt
    gdn_states, gla_states, ks, vs = [], [], [], []
    for l in range(DEPTH):
        hp, s_gdn, s_gla, k_l, v_l = layer(hp, l, 0, bp * tp, None, None, None, None, True, 512)
        gdn_states.append(s_gdn)
        gla_states.append(s_gla)
        ks.append(k_l)
        vs.append(v_l)

    hs = x_sample
    for l in range(DEPTH):
        hs = layer(hs, l, 1, ts, state_gdn, state_gla, ck, cv, False, 512)[0]

    return (hp, hs, jnp.stack(gdn_states, axis=1), jnp.stack(gla_states, axis=1),
            jnp.stack(ks, axis=1), jnp.stack(vs, axis=1))
```

```python
import functools
import math

import numpy as np
import jax
import jax.numpy as jnp
from jax import lax
from jax.experimental import pallas as pl
from jax.experimental.pallas import tpu as pltpu

F32 = jnp.float32
BF16 = jnp.bfloat16

D_MODEL = 1024
DEPTH = 2
GRID_W = 64
GDN_HEADS = 4
GDN_DK = 128
GDN_DV = 128
GDN_W = GDN_HEADS * GDN_DV
SHORT_CONV = 5
GLA_HEADS = 4
GLA_DK = 64
GLA_DV = 128
GLA_KW = GLA_HEADS * GLA_DK
GLA_VW = GLA_HEADS * GLA_DV
GLA_RANK = 16
GLA_TAU = 16.0
CHUNK = 64
DIFF_HEADS = 4
DIFF_DH = 64
DIFF_VD = 2 * DIFF_DH
DIFF_QW = DIFF_HEADS * 2 * DIFF_DH
DIFF_VW = DIFF_HEADS * DIFF_VD
ROPE_THETA = 10000.0
MIX_W = 512
N_BRANCH = 3
N_EXPERTS = 16
N_GROUPS = 4
GROUP_SIZE = N_EXPERTS // N_GROUPS
D_FF = 512
ALPHA = (2 * DEPTH) ** 0.25
EPS = 1e-6

LANE = 128
MOD_ROWS = 8
VMEM_LIMIT = 56 * 1024 * 1024

A_GDN_QKV = 0
A_GDN_Z = 1536
A_GLA_Q = 2048
A_GLA_K = 2304
A_GLA_V = 2560
A_GLA_R = 3072
A_DIFF_Q = 3584
A_MERGE = 4096
A_COLS = 7168
F_DIFF_K = 0
F_DIFF_V = 512
F_MISC = 1024
F_COLS = 1152
MISC_B = 0
MISC_A = 8
MISC_LR = 16


def _cparams(sem):
    return pltpu.CompilerParams(dimension_semantics=sem, vmem_limit_bytes=VMEM_LIMIT)


def _dot(a, b):
    return jnp.dot(a, b, preferred_element_type=F32)


def _mm(a, b):
    return _dot(a.astype(BF16), b.astype(BF16))


def _mm_nt(a, b):
    return lax.dot_general(a.astype(BF16), b.astype(BF16), (((1,), (1,)), ((), ())),
                           preferred_element_type=F32)


def _split3(x):
    hi = x.astype(BF16)
    r = x - hi.astype(F32)
    mid = r.astype(BF16)
    lo = (r - mid.astype(F32)).astype(BF16)
    return hi, mid, lo


def _mm_mask_lhs(mask_bf16, x):
    hi, mid, lo = _split3(x)
    return _dot(mask_bf16, hi) + _dot(mask_bf16, mid) + _dot(mask_bf16, lo)


def _mm_mask_rhs(x, mask_bf16):
    hi, mid, lo = _split3(x)
    return _dot(hi, mask_bf16) + _dot(mid, mask_bf16) + _dot(lo, mask_bf16)


def _mm_hi(a, b):
    ah = a.astype(BF16)
    al = (a - ah.astype(F32)).astype(BF16)
    bh = b.astype(BF16)
    bl = (b - bh.astype(F32)).astype(BF16)
    return _dot(ah, bh) + _dot(ah, bl) + _dot(al, bh)


def _mm_hi_nt(a, b):
    ah = a.astype(BF16)
    al = (a - ah.astype(F32)).astype(BF16)
    bh = b.astype(BF16)
    bl = (b - bh.astype(F32)).astype(BF16)
    dot = lambda x, y: lax.dot_general(x, y, (((1,), (1,)), ((), ())), preferred_element_type=F32)
    return dot(ah, bh) + dot(ah, bl) + dot(al, bh)


def _sigmoid(x):
    return 1.0 / (1.0 + jnp.exp(-x))


def _silu(x):
    return x * _sigmoid(x)


def _softplus(x):
    return jnp.maximum(x, 0.0) + jnp.log1p(jnp.exp(-jnp.abs(x)))


def _rms(x, g):
    return x * lax.rsqrt(jnp.mean(x * x, axis=-1, keepdims=True) + EPS) * g


def _layernorm(y, g, b):
    mu = jnp.mean(y, axis=-1, keepdims=True)
    d = y - mu
    var = jnp.mean(d * d, axis=-1, keepdims=True)
    return d * lax.rsqrt(var + EPS) * g + b


def _pick_lane(x, idx):
    lane = lax.broadcasted_iota(jnp.int32, x.shape, 1)
    return jnp.sum(jnp.where(lane == idx, x, 0.0), axis=1, keepdims=True)


def _tri(n, kind):
    r = lax.broadcasted_iota(jnp.int32, (n, n), 0)
    c = lax.broadcasted_iota(jnp.int32, (n, n), 1)
    return {"ge": r >= c, "gt": r > c, "le": r <= c, "lt": r < c}[kind]


def _mod_kernel(c_ref, w_ref, b_ref, o_ref):
    s = _silu(c_ref[...])
    o_ref[...] = _mm_hi(s, w_ref[...]) + b_ref[...]


def _modulation(cvecs, w_mod, b_mod):
    tn = 1536
    return pl.pallas_call(
        _mod_kernel,
        grid=(DEPTH, 6 * D_MODEL // tn),
        in_specs=[
            pl.BlockSpec((MOD_ROWS, D_MODEL), lambda l, j: (0, 0)),
            pl.BlockSpec((None, D_MODEL, tn), lambda l, j: (l, 0, j)),
            pl.BlockSpec((None, 1, tn), lambda l, j: (l, 0, j)),
        ],
        out_specs=pl.BlockSpec((None, MOD_ROWS, tn), lambda l, j: (l, 0, j)),
        out_shape=jax.ShapeDtypeStruct((DEPTH, MOD_ROWS, 6 * D_MODEL), F32),
        compiler_params=_cparams(("arbitrary", "arbitrary")),
    )(cvecs, w_mod, b_mod.reshape(DEPTH, 1, 6 * D_MODEL))


INPROJ_COL_STEP = 1792


def _inproj_kernel(*refs, emit_cache):
    x_ref, mod_ref, wa_ref, wf_ref, pa_ref, pf_ref = refs[:6]
    shift = mod_ref[:, 0:D_MODEL]
    scale = mod_ref[:, D_MODEL:2 * D_MODEL]
    u = (x_ref[...] * (1.0 + scale) + shift).astype(BF16)
    pf = _dot(u, wf_ref[...])
    pf_ref[...] = pf
    if emit_cache:
        ko_ref, vo_ref = refs[6:8]
        for h in range(DIFF_HEADS):
            ko_ref[:, h, :] = pf[:, F_DIFF_K + h * LANE:F_DIFF_K + (h + 1) * LANE]
            vo_ref[:, h, :] = pf[:, F_DIFF_V + h * LANE:F_DIFF_V + (h + 1) * LANE]
    for c0 in range(0, A_COLS, INPROJ_COL_STEP):
        cols = slice(c0, c0 + INPROJ_COL_STEP)
        pa_ref[:, cols] = _dot(u, wa_ref[:, cols]).astype(BF16)


def _inproj(x, mod, wa, wf, layer, row0, tokens_per_row, tm, emit_cache):
    n = x.shape[0]
    mod_row = lambda i: (layer, row0 + (i * tm) // tokens_per_row, 0, 0)
    out_specs = [
        pl.BlockSpec((tm, A_COLS), lambda i: (i, 0)),
        pl.BlockSpec((tm, F_COLS), lambda i: (i, 0)),
    ]
    out_shape = [
        jax.ShapeDtypeStruct((n, A_COLS), BF16),
        jax.ShapeDtypeStruct((n, F_COLS), F32),
    ]
    if emit_cache:
        out_specs += [pl.BlockSpec((tm, DIFF_HEADS, LANE), lambda i: (i, 0, 0))] * 2
        out_shape += [jax.ShapeDtypeStruct((n, DIFF_HEADS, LANE), F32)] * 2
    resident = lambda shape: pl.BlockSpec(shape, lambda i: (0, 0), pipeline_mode=pl.Buffered(1))
    return pl.pallas_call(
        functools.partial(_inproj_kernel, emit_cache=emit_cache),
        grid=(n // tm,),
        in_specs=[
            pl.BlockSpec((tm, D_MODEL), lambda i: (i, 0)),
            pl.BlockSpec((None, None, 1, 6 * D_MODEL), mod_row),
            resident((D_MODEL, A_COLS)),
            resident((D_MODEL, F_COLS)),
        ],
        out_specs=out_specs,
        out_shape=out_shape,
        compiler_params=_cparams(("arbitrary",)),
    )(x, mod, wa, wf)


def _conv_masks(shape, t_len):
    row = lax.broadcasted_iota(jnp.int32, shape, 0)
    half = SHORT_CONV // 2
    return {d: jnp.logical_and(row + d >= 0, row + d < t_len) for d in range(-half, half + 1) if d}


def _short_conv(x, w, t_len, masks):
    half = SHORT_CONV // 2
    acc = x * w[half:half + 1, :]
    for d, valid in masks.items():
        shifted = pltpu.roll(x, (-d) % t_len, axis=0)
        acc = acc + jnp.where(valid, shifted, 0.0) * w[half + d:half + d + 1, :]
    return acc


def _solve_unit_tri(a_list, rhs_list):
    n = a_list[0].shape[0]
    eye = jnp.where(_tri(n, "ge") & _tri(n, "le"), 1.0, 0.0)
    a_b = [a.astype(BF16) for a in a_list]
    p_b = [-a for a in a_b]
    inv = [eye - a for a in a_list]
    for _ in range(int(math.log2(n)) - 1):
        p_b = [_dot(p, p).astype(BF16) for p in p_b]
        inv = [x + _dot(x.astype(BF16), p) for x, p in zip(inv, p_b)]
    inv_b = [x.astype(BF16) for x in inv]
    sol = [_dot(x, r.astype(BF16)) for x, r in zip(inv_b, rhs_list)]
    resid = [r - s - _dot(a, s.astype(BF16)) for r, s, a in zip(rhs_list, sol, a_b)]
    return [s + _dot(x, r.astype(BF16)) for s, x, r in zip(sol, inv_b, resid)]


def _gdn_kernel(*refs, t_len, has_state, emit_state):
    it = iter(refs)
    qkv_ref, z_ref, misc_ref, cw_ref, alog_ref, dtb_ref, g_ref = (next(it) for _ in range(7))
    s0_ref = next(it) if has_state else None
    if emit_state == "later":
        next(it)
    o_ref = next(it)
    sfin_ref = next(it) if emit_state else None
    (q_s, k_s, v_s, kt_s, bet, gat, gat_t, u_s, w_s, qd_s, a_s, ktl_s, gts, o_f, o_b,
     s_s) = (next(it) for _ in range(16))

    c = CHUNK
    n_chunks = t_len // c
    nh = GDN_HEADS
    dk = GDN_DK

    masks = _conv_masks((t_len, dk), t_len)
    for h in range(nh):
        hs = slice(h * dk, (h + 1) * dk)
        q, k, v = (
            _silu(_short_conv(qkv_ref[:, j * GDN_W + h * dk:j * GDN_W + (h + 1) * dk].astype(F32),
                              cw_ref[:, j * GDN_W + h * dk:j * GDN_W + (h + 1) * dk], t_len, masks))
            for j in range(3))
        q = q * lax.rsqrt(jnp.sum(q * q, axis=-1, keepdims=True) + EPS) * (dk ** -0.5)
        k = k * lax.rsqrt(jnp.sum(k * k, axis=-1, keepdims=True) + EPS)
        q_s[:, hs] = q
        k_s[:, hs] = k
        v_s[:, hs] = v
        kt_s[hs, :] = k.T.astype(BF16)
    misc = misc_ref[...]
    bet[...] = _sigmoid(misc)
    g_all = -jnp.exp(alog_ref[...]) * _softplus(misc + dtb_ref[...])
    gat[...] = g_all
    gat_t[...] = g_all.T[MISC_A:MISC_A + 2 * nh, :]
    for d in range(2):
        for h in range(nh):
            s_s[d * nh + h] = s0_ref[d, h] if has_state else jnp.zeros((dk, GDN_DV), F32)

    two = 2 * c
    r2 = lax.broadcasted_iota(jnp.int32, (two, two), 0)
    c2 = lax.broadcasted_iota(jnp.int32, (two, two), 1)
    same = (r2 >= c) == (c2 >= c)
    lo2 = jnp.where(jnp.logical_and(same, r2 >= c2), 1.0, 0.0).astype(BF16)
    up2 = jnp.where(jnp.logical_and(same, r2 <= c2), 1.0, 0.0).astype(BF16)
    incl = (_tri(c, "ge"), _tri(c, "le"))
    strict = (_tri(c, "gt"), _tri(c, "lt"))
    last = (c - 1, 0)

    def phase1(cp, carry):
        r0 = pl.multiple_of(cp * two, two)
        g_blk = gat[pl.ds(r0, two), :]
        b_blk = bet[pl.ds(r0, two), :]
        gt_blk = gat_t[:, pl.ds(r0, two)]
        cum_col = (_mm_mask_lhs(lo2, g_blk), _mm_mask_lhs(up2, g_blk))
        cum_row = (_mm_mask_rhs(gt_blk, up2), _mm_mask_rhs(gt_blk, lo2))
        chains, a_list, rhs_list = [], [], []
        for h in range(nh):
            hs = slice(h * dk, (h + 1) * dk)
            kt_pair = kt_s[hs, pl.ds(r0, two)]
            for s in range(2):
                rows = pl.ds(pl.multiple_of(r0 + s * c, c), c)
                q_c = q_s[rows, hs]
                k_c = k_s[rows, hs]
                v_c = v_s[rows, hs]
                kt_c = kt_pair[:, s * c:(s + 1) * c]
                kk = _dot(k_c.astype(BF16), kt_c)
                qk = _dot(q_c.astype(BF16), kt_c)
                for d in range(2):
                    idx = d * nh + h
                    gc = cum_col[d][s * c:(s + 1) * c, MISC_A + idx:MISC_A + idx + 1]
                    gcr = cum_row[d][idx:idx + 1, s * c:(s + 1) * c]
                    beta = b_blk[s * c:(s + 1) * c, MISC_B + idx:MISC_B + idx + 1]
                    decay = jnp.where(incl[d], jnp.exp(jnp.where(incl[d], gc - gcr, 0.0)), 0.0)
                    e_gc = jnp.exp(gc)
                    a_list.append(jnp.where(strict[d], beta * kk * decay, 0.0))
                    rhs_list.append(jnp.concatenate([v_c * beta, k_c * (beta * e_gc)], axis=1))
                    a_s[idx, rows, :] = jnp.where(incl[d], qk * decay, 0.0).astype(BF16)
                    qd_s[idx, rows, :] = (q_c * e_gc).astype(BF16)
                    g_last = gcr[:, last[d]:last[d] + 1]
                    ktl_s[idx, cp * 2 + s] = (kt_c.astype(F32) * jnp.exp(g_last - gcr)).astype(BF16)
                    gts[cp * 2 + s, idx:idx + 1, :] = jnp.broadcast_to(jnp.exp(g_last), (1, LANE))
                    chains.append((idx, rows))
        for (idx, rows), sol in zip(chains, _solve_unit_tri(a_list, rhs_list)):
            u_s[idx, rows, :] = sol[:, :GDN_DV]
            w_s[idx, rows, :] = sol[:, GDN_DV:].astype(BF16)
        return carry

    lax.fori_loop(0, n_chunks // 2, phase1, 0)

    def phase2(i, carry):
        chains = []
        for d, out in ((0, o_f), (1, o_b)):
            ci = i if d == 0 else n_chunks - 1 - i
            rows = pl.ds(pl.multiple_of(ci * c, c), c)
            g_tail = gts[ci]
            for h in range(nh):
                chains.append((d * nh + h, h, ci, rows, out, g_tail))
        s_f32 = [s_s[idx] for idx, *_ in chains]
        s_b = [s.astype(BF16) for s in s_f32]
        v_b = [(u_s[idx, rows, :] - _dot(w_s[idx, rows, :], sb)).astype(BF16)
               for (idx, _, _, rows, _, _), sb in zip(chains, s_b)]
        for (idx, h, ci, rows, out, g_tail), s, sb, vb in zip(chains, s_f32, s_b, v_b):
            s_s[idx] = s * g_tail[idx:idx + 1, :] + _dot(ktl_s[idx, ci], vb)
            out[rows, h * GDN_DV:(h + 1) * GDN_DV] = _dot(qd_s[idx, rows, :], sb) + _dot(
                a_s[idx, rows, :], vb)
        return carry

    lax.fori_loop(0, n_chunks, phase2, 0)

    for h in range(nh):
        hs = slice(h * GDN_DV, (h + 1) * GDN_DV)
        o = o_f[:, hs] + o_b[:, hs]
        o_ref[:, hs] = (_rms(o, g_ref[...]) * _silu(z_ref[:, hs].astype(F32))).astype(BF16)
    if emit_state:
        for d in range(2):
            for h in range(nh):
                _store_state(sfin_ref, emit_state, d, h, s_s[d * nh + h])


def _store_state(sfin_ref, mode, d, h, s):
    if mode == "first":
        sfin_ref[0, d, h] = s
        for l in range(1, DEPTH):
            sfin_ref[l, d, h] = jnp.zeros_like(s)
    else:
        sfin_ref[d, h] = s


def _state_output(b, layer, heads, dk, dv, prev, in_specs, args):
    shape = jax.ShapeDtypeStruct((b, DEPTH, 2, heads, dk, dv), F32)
    if prev is None:
        return "first", pl.BlockSpec((None, DEPTH, 2, heads, dk, dv), lambda bi: (bi, 0, 0, 0, 0, 0)), shape, {}
    in_specs.append(pl.BlockSpec(memory_space=pl.ANY))
    args.append(prev)
    spec = pl.BlockSpec((None, None, 2, heads, dk, dv), lambda bi: (bi, layer, 0, 0, 0, 0))
    return "later", spec, shape, {len(args) - 1: 1}


def _gdn(pa3, pf3, conv_w, alog_row, dtb_row, norm_g, s0, layer, emit_state, prev_states):
    b, t, _ = pa3.shape
    full = lambda shape: pl.BlockSpec(shape, lambda bi: (0,) * len(shape))
    in_specs = [
        pl.BlockSpec((None, t, 3 * GDN_W), lambda bi: (bi, 0, A_GDN_QKV // (3 * GDN_W))),
        pl.BlockSpec((None, t, GDN_W), lambda bi: (bi, 0, A_GDN_Z // GDN_W)),
        pl.BlockSpec((None, t, LANE), lambda bi: (bi, 0, F_MISC // LANE)),
        full((SHORT_CONV, 3 * GDN_W)),
        full((1, LANE)),
        full((1, LANE)),
        full((1, GDN_DV)),
    ]
    args = [pa3, pa3, pf3, conv_w, alog_row, dtb_row, norm_g]
    if s0 is not None:
        in_specs.append(pl.BlockSpec((None, None, 2, GDN_HEADS, GDN_DK, GDN_DV),
                                     lambda bi: (bi, layer, 0, 0, 0, 0)))
        args.append(s0)
    out_specs = [pl.BlockSpec((None, t, GDN_W), lambda bi: (bi, 0, 0))]
    out_shape = [jax.ShapeDtypeStruct((b, t, GDN_W), BF16)]
    mode, aliases = None, {}
    if emit_state:
        mode, spec, shape, aliases = _state_output(b, layer, GDN_HEADS, GDN_DK, GDN_DV, prev_states, in_specs, args)
        out_specs.append(spec)
        out_shape.append(shape)
    nc = t // CHUNK
    nd = 2 * GDN_HEADS
    wide = pltpu.VMEM((t, GDN_W), F32)
    scratch = [
        wide, wide, wide,
        pltpu.VMEM((GDN_W, t), BF16),
        pltpu.VMEM((t, LANE), F32), pltpu.VMEM((t, LANE), F32),
        pltpu.VMEM((nd, t), F32),
        pltpu.VMEM((nd, t, GDN_DV), F32),
        pltpu.VMEM((nd, t, GDN_DK), BF16),
        pltpu.VMEM((nd, t, GDN_DK), BF16),
        pltpu.VMEM((nd, t, CHUNK), BF16),
        pltpu.VMEM((nd, nc, GDN_DK, CHUNK), BF16),
        pltpu.VMEM((nc, nd, LANE), F32),
        wide, wide,
        pltpu.VMEM((nd, GDN_DK, GDN_DV), F32),
    ]
    outs = pl.pallas_call(
        functools.partial(_gdn_kernel, t_len=t, has_state=s0 is not None, emit_state=mode),
        grid=(b,),
        in_specs=in_specs,
        out_specs=out_specs,
        out_shape=out_shape,
        scratch_shapes=scratch,
        input_output_aliases=aliases,
        compiler_params=_cparams(("arbitrary",)),
    )(*args)
    return outs if emit_state else (outs[0], None)


def _gla_kernel(*refs, t_len, has_state, emit_state):
    it = iter(refs)
    q_ref, k_ref, v_ref, r_ref, misc_ref, wg_ref, bg_ref, g_ref = (next(it) for _ in range(8))
    s0_ref = next(it) if has_state else None
    if emit_state == "later":
        next(it)
    o_ref = next(it)
    sfin_ref = next(it) if emit_state else None
    glog, vt_s, o_f, o_b, st = (next(it) for _ in range(5))

    c = CHUNK
    two = 2 * c
    n_pairs = t_len // two
    nh = GLA_HEADS
    misc = misc_ref[...]
    for d in range(2):
        lr = misc[:, MISC_LR + d * GLA_RANK:MISC_LR + (d + 1) * GLA_RANK]
        logits = _mm_hi(lr, wg_ref[d]) + bg_ref[d:d + 1, :]
        glog[d] = -_softplus(-logits) / GLA_TAU
    for h in range(nh):
        hs = slice(h * GLA_DV, (h + 1) * GLA_DV)
        vt_s[hs, :] = v_ref[:, hs].astype(F32).T.astype(BF16)
    for d in range(2):
        for h in range(nh):
            if has_state:
                s0 = jnp.concatenate([s0_ref[d, h], jnp.zeros((GLA_DV - GLA_DK, GLA_DV), F32)], axis=0)
                st[d * nh + h] = s0.T[:, :GLA_DK]
            else:
                st[d * nh + h] = jnp.zeros((GLA_DV, GLA_DK), F32)

    r2 = lax.broadcasted_iota(jnp.int32, (two, two), 0)
    c2 = lax.broadcasted_iota(jnp.int32, (two, two), 1)
    same = (r2 >= c) == (c2 >= c)
    cum_mask = (jnp.where(jnp.logical_and(same, r2 >= c2), 1.0, 0.0).astype(BF16),
                jnp.where(jnp.logical_and(same, r2 <= c2), 1.0, 0.0).astype(BF16))
    incl = (_tri(c, "ge"), _tri(c, "le"))
    last = (c - 1, 0)
    mid = (c // 2, c - 1 - c // 2)
    order = ((0, 1), (1, 0))

    def body(i, carry):
        pieces = []
        for d in range(2):
            r0 = pl.multiple_of((i if d == 0 else n_pairs - 1 - i) * two, two)
            b_all = _mm_mask_lhs(cum_mask[d], glog[d, pl.ds(r0, two), :])
            q_all = q_ref[pl.ds(r0, two), :].astype(F32) * (GLA_DK ** -0.5)
            k_all = k_ref[pl.ds(r0, two), :].astype(F32)
            vt_pair = [vt_s[h * GLA_DV:(h + 1) * GLA_DV, pl.ds(r0, two)] for h in range(nh)]
            for s in range(2):
                rs = slice(s * c, (s + 1) * c)
                bq = b_all[rs]
                b_mid = bq[mid[d]:mid[d] + 1, :]
                b_last = bq[last[d]:last[d] + 1, :]
                qe = (q_all[rs] * jnp.exp(bq - b_mid)).astype(BF16)
                ke = (k_all[rs] * jnp.exp(b_mid - bq)).astype(BF16)
                qd = (q_all[rs] * jnp.exp(bq)).astype(BF16)
                kt = (k_all[rs] * jnp.exp(b_last - bq)).astype(BF16)
                g_last = jnp.exp(b_last)
                rows = pl.ds(pl.multiple_of(r0 + s * c, c), c)
                for h in range(nh):
                    ks = slice(h * GLA_DK, (h + 1) * GLA_DK)
                    pieces.append((d, s, h, rows, qe[:, ks], ke[:, ks], qd[:, ks], kt[:, ks], g_last[:, ks],
                                   vt_pair[h][:, rs], v_ref[rows, h * GLA_DV:(h + 1) * GLA_DV]))
        a_intra = [jnp.where(incl[p[0]], _mm_nt(p[4], p[5]), 0.0).astype(BF16) for p in pieces]
        upd = [_dot(p[9], p[7]) for p in pieces]
        intra = [_dot(a, p[10]) for a, p in zip(a_intra, pieces)]
        for step in range(2):
            for p, m, o_in in zip(pieces, upd, intra):
                d, s, h, rows = p[0], p[1], p[2], p[3]
                if s != order[d][step]:
                    continue
                s_t = st[d * nh + h]
                out = o_f if d == 0 else o_b
                out[rows, h * GLA_DV:(h + 1) * GLA_DV] = _mm_nt(p[6], s_t) + o_in
                st[d * nh + h] = s_t * p[8] + m
        return carry

    lax.fori_loop(0, n_pairs, body, 0)

    for h in range(nh):
        hs = slice(h * GLA_DV, (h + 1) * GLA_DV)
        o = o_f[:, hs] + o_b[:, hs]
        o_ref[:, hs] = (_rms(o, g_ref[...]) * _silu(r_ref[:, hs].astype(F32))).astype(BF16)
    if emit_state:
        for d in range(2):
            for h in range(nh):
                s_pad = jnp.concatenate([st[d * nh + h], jnp.zeros((GLA_DV, GLA_DV - GLA_DK), F32)], axis=1)
                _store_state(sfin_ref, emit_state, d, h, s_pad.T[:GLA_DK, :])


def _gla(pa3, pf3, wg, bg, norm_g, s0, layer, emit_state, prev_states):
    b, t, _ = pa3.shape
    full = lambda shape: pl.BlockSpec(shape, lambda bi: (0,) * len(shape))
    in_specs = [
        pl.BlockSpec((None, t, GLA_KW), lambda bi: (bi, 0, A_GLA_Q // GLA_KW)),
        pl.BlockSpec((None, t, GLA_KW), lambda bi: (bi, 0, A_GLA_K // GLA_KW)),
        pl.BlockSpec((None, t, GLA_VW), lambda bi: (bi, 0, A_GLA_V // GLA_VW)),
        pl.BlockSpec((None, t, GLA_VW), lambda bi: (bi, 0, A_GLA_R // GLA_VW)),
        pl.BlockSpec((None, t, LANE), lambda bi: (bi, 0, F_MISC // LANE)),
        full((2, GLA_RANK, GLA_KW)),
        full((2, GLA_KW)),
        full((1, GLA_DV)),
    ]
    args = [pa3, pa3, pa3, pa3, pf3, wg, bg, norm_g]
    if s0 is not None:
        in_specs.append(pl.BlockSpec((None, None, 2, GLA_HEADS, GLA_DK, GLA_DV),
                                     lambda bi: (bi, layer, 0, 0, 0, 0)))
        args.append(s0)
    out_specs = [pl.BlockSpec((None, t, GLA_VW), lambda bi: (bi, 0, 0))]
    out_shape = [jax.ShapeDtypeStruct((b, t, GLA_VW), BF16)]
    mode, aliases = None, {}
    if emit_state:
        mode, spec, shape, aliases = _state_output(b, layer, GLA_HEADS, GLA_DK, GLA_DV, prev_states, in_specs, args)
        out_specs.append(spec)
        out_shape.append(shape)
    scratch = [pltpu.VMEM((2, t, GLA_KW), F32),
               pltpu.VMEM((GLA_VW, t), BF16),
               pltpu.VMEM((t, GLA_VW), F32), pltpu.VMEM((t, GLA_VW), F32),
               pltpu.VMEM((2 * GLA_HEADS, GLA_DV, GLA_DK), F32)]
    outs = pl.pallas_call(
        functools.partial(_gla_kernel, t_len=t, has_state=s0 is not None, emit_state=mode),
        grid=(b,),
        in_specs=in_specs,
        out_specs=out_specs,
        out_shape=out_shape,
        scratch_shapes=scratch,
        input_output_aliases=aliases,
        compiler_params=_cparams(("arbitrary",)),
    )(*args)
    return outs if emit_state else (outs[0], None)


ATT_Q_BLOCK = 256


def _rope_tables(t_len):
    half = DIFF_DH // 2
    quarter = half // 2
    inv = ROPE_THETA ** (-np.arange(0, half, 2, dtype=np.float64) / half)
    tok = np.arange(t_len)
    pos = np.stack([tok // GRID_W, tok % GRID_W], axis=1).astype(np.float64)
    ang = pos[:, :, None] * inv[None, None, :]
    cos = np.concatenate([np.cos(ang), np.cos(ang)], axis=-1).reshape(t_len, DIFF_DH)
    sin = np.concatenate([-np.sin(ang), np.sin(ang)], axis=-1).reshape(t_len, DIFF_DH)
    cos = np.concatenate([cos, cos], axis=-1).astype(np.float32)
    sin = np.concatenate([sin, sin], axis=-1).astype(np.float32)
    first = ((np.arange(2 * DIFF_DH) % half) < quarter).astype(np.float32)[None, :]
    return jnp.asarray(cos), jnp.asarray(sin), jnp.asarray(first), quarter


def _rope(x, cos, sin, first, quarter):
    width = x.shape[-1]
    ahead = pltpu.roll(x, width - quarter, axis=1)
    behind = pltpu.roll(x, quarter, axis=1)
    partner = jnp.where(first > 0.5, ahead, behind)
    return x * cos + partner * sin


def _diff_kernel(*refs, t_len, ctx_len, lam_init):
    it = iter(refs)
    q_ref, k_ref, v_ref, lam_ref, g_ref = (next(it) for _ in range(5))
    if ctx_len:
        ck_ref, cv_ref, cos_ref, sin_ref, first_ref = (next(it) for _ in range(5))
    o_ref = next(it)
    q_sc, k_sc, v_sc = (next(it) for _ in range(3))

    scale = DIFF_DH ** -0.5
    for h in range(DIFF_HEADS):
        hs = slice(h * LANE, (h + 1) * LANE)
        q = q_ref[:, hs].astype(F32)
        k = k_ref[:, hs]
        if ctx_len:
            quarter = DIFF_DH // 4
            q = _rope(q, cos_ref[...], sin_ref[...], first_ref[...], quarter)
            k = _rope(k, cos_ref[...], sin_ref[...], first_ref[...], quarter)
        q_sc[:, hs] = (q * scale).astype(BF16)
        k_sc[0:t_len, hs] = k.astype(BF16)
    v_sc[0:t_len, :] = v_ref[...].astype(BF16)
    if ctx_len:
        k_sc[t_len:t_len + ctx_len, :] = ck_ref[...].astype(BF16)
        v_sc[t_len:t_len + ctx_len, :] = cv_ref[...].astype(BF16)

    lp = lam_ref[...]
    lam = (jnp.exp(jnp.sum(lp[0:1, :] * lp[1:2, :], axis=1, keepdims=True))
           - jnp.exp(jnp.sum(lp[2:3, :] * lp[3:4, :], axis=1, keepdims=True)) + lam_init)

    tq = min(ATT_Q_BLOCK, t_len)
    group = DIFF_HEADS if t_len + ctx_len <= 512 else 2

    def body(i, carry):
        r0 = pl.multiple_of(i * tq, tq)
        for h0 in range(0, DIFF_HEADS, group):
            parts = [(h, half) for h in range(h0, h0 + group) for half in range(2)]
            cols = [slice(h * LANE + half * DIFF_DH, h * LANE + (half + 1) * DIFF_DH) for h, half in parts]
            s = [_mm_nt(q_sc[pl.ds(r0, tq), c], k_sc[:, c]) for c in cols]
            e = [jnp.exp(x - jnp.max(x, axis=-1, keepdims=True)) for x in s]
            inv_l = [1.0 / jnp.sum(x, axis=-1, keepdims=True) for x in e]
            pv = [_dot(x.astype(BF16), v_sc[:, h * LANE:(h + 1) * LANE]) * r
                  for x, r, (h, _) in zip(e, inv_l, parts)]
            for j in range(group):
                h = h0 + j
                o = pv[2 * j] - lam * pv[2 * j + 1]
                o_ref[pl.ds(r0, tq), h * LANE:(h + 1) * LANE] = (
                    _rms(o, g_ref[...]) * (1.0 - lam_init)).astype(BF16)
        return carry

    lax.fori_loop(0, t_len // tq, body, 0)


def _diff(pa3, pf3, lam_p, norm_g, ctx_k, ctx_v, layer, lam_init):
    b, t, _ = pa3.shape
    ctx_len = 0 if ctx_k is None else ctx_k.shape[2]
    full = lambda shape: pl.BlockSpec(shape, lambda bi: (0,) * len(shape))
    in_specs = [
        pl.BlockSpec((None, t, DIFF_QW), lambda bi: (bi, 0, A_DIFF_Q // DIFF_QW)),
        pl.BlockSpec((None, t, DIFF_QW), lambda bi: (bi, 0, F_DIFF_K // DIFF_QW)),
        pl.BlockSpec((None, t, DIFF_VW), lambda bi: (bi, 0, F_DIFF_V // DIFF_VW)),
        full((4, DIFF_DH)),
        full((1, DIFF_VD)),
    ]
    args = [pa3, pf3, pf3, lam_p, norm_g]
    if ctx_len:
        cos, sin, first, _ = _rope_tables(t)
        in_specs += [
            pl.BlockSpec((None, None, ctx_len, DIFF_QW), lambda bi: (bi, layer, 0, 0)),
            pl.BlockSpec((None, None, ctx_len, DIFF_VW), lambda bi: (bi, layer, 0, 0)),
            full((t, LANE)),
            full((t, LANE)),
            full((1, LANE)),
        ]
        args += [ctx_k, ctx_v, cos, sin, first]
    tk = t + ctx_len
    return pl.pallas_call(
        functools.partial(_diff_kernel, t_len=t, ctx_len=ctx_len, lam_init=lam_init),
        grid=(b,),
        in_specs=in_specs,
        out_specs=pl.BlockSpec((None, t, DIFF_VW), lambda bi: (bi, 0, 0)),
        out_shape=jax.ShapeDtypeStruct((b, t, DIFF_VW), BF16),
        scratch_shapes=[pltpu.VMEM((t, DIFF_QW), BF16), pltpu.VMEM((tk, DIFF_QW), BF16),
                        pltpu.VMEM((tk, DIFF_VW), BF16)],
        compiler_params=_cparams(("arbitrary",)),
    )(*args)


def _route(logits_t, bias_col):
    scores = _sigmoid(logits_t)
    biased = scores + bias_col
    rows = [biased[e:e + 1, :] for e in range(N_EXPERTS)]
    grp = []
    for g in range(N_GROUPS):
        a0, a1, a2, a3 = rows[g * GROUP_SIZE:(g + 1) * GROUP_SIZE]
        hi01, lo01 = jnp.maximum(a0, a1), jnp.minimum(a0, a1)
        hi23, lo23 = jnp.maximum(a2, a3), jnp.minimum(a2, a3)
        top1 = jnp.maximum(hi01, hi23)
        top2 = jnp.maximum(jnp.minimum(hi01, hi23), jnp.maximum(lo01, lo23))
        grp.append(top1 + top2)
    best = []
    for g in range(N_GROUPS):
        win = None
        for o in range(N_GROUPS):
            if o == g:
                continue
            cond = grp[g] > grp[o] if o < g else grp[g] >= grp[o]
            win = cond if win is None else jnp.logical_and(win, cond)
        best.append(win)
    sel_rows = []
    for e in range(N_EXPERTS):
        g = e // GROUP_SIZE
        beaten = jnp.zeros_like(rows[e])
        for o in range(g * GROUP_SIZE, (g + 1) * GROUP_SIZE):
            if o == e:
                continue
            ahead = rows[o] >= rows[e] if o < e else rows[o] > rows[e]
            beaten = beaten + jnp.where(ahead, 1.0, 0.0)
        sel_rows.append(jnp.where(jnp.logical_and(best[g], beaten < 1.5), 1.0, 0.0))
    sel = jnp.concatenate(sel_rows, axis=0)
    picked = sel * scores
    return sel, picked / jnp.sum(picked, axis=0, keepdims=True)


PIECE = 16
ROW_TILE = 256
SLOT_PAD = N_EXPERTS * PIECE
HS_COLS = D_MODEL + LANE


def _route_meta(sel, wts):
    n_e, t = sel.shape
    e_col = lax.broadcasted_iota(jnp.int32, (n_e, 1), 0).astype(F32)
    before = jnp.where(_tri(t, "lt"), 1.0, 0.0).astype(BF16)
    rank = _dot(sel.astype(BF16), before)
    cnt = jnp.sum(sel, axis=1, keepdims=True)
    plen = jnp.floor((cnt + (PIECE - 1.0)) * (1.0 / PIECE)) * PIECE
    lower = jnp.where(_tri(n_e, "gt"), 1.0, 0.0).astype(BF16)
    loff = _dot(lower, jnp.broadcast_to(plen, (n_e, LANE)).astype(BF16))[:, 0:1]
    slot = loff + rank
    chosen = sel > 0.5
    e_a = jnp.min(jnp.where(chosen, e_col, 2.0 * n_e), axis=0, keepdims=True)
    e_b = jnp.max(jnp.where(chosen, e_col, -1.0), axis=0, keepdims=True)
    take = lambda e_row, x: jnp.sum(jnp.where(e_col == e_row, x, 0.0), axis=0, keepdims=True)
    rows = jnp.concatenate([take(e_a, slot), take(e_b, slot), take(e_a, wts), take(e_b, wts), e_a, e_b,
                            jnp.zeros((2, t), F32)], axis=0)
    return rows, plen, loff


def _merge_kernel(og_ref, ol_ref, od_ref, lg0_ref, lg1_ref, lg2_ref, x_ref, mod_ref, wb_ref, wo_ref,
                  lng_ref, lnb_ref, rw_ref, rb_ref, x1_ref, h_ref, mrow_ref, mcol_ref, plen_ref, loff_ref):
    acc = None
    for s, (o_ref, lg_ref) in enumerate(((og_ref, lg0_ref), (ol_ref, lg1_ref), (od_ref, lg2_ref))):
        proj = _mm(o_ref[...], wb_ref[s])
        term = _sigmoid(lg_ref[...].astype(F32)) * proj
        acc = term if acc is None else acc + term
    m = _mm(acc, wo_ref[...])
    gate1 = mod_ref[:, 2 * D_MODEL:3 * D_MODEL]
    shift2 = mod_ref[:, 3 * D_MODEL:4 * D_MODEL]
    scale2 = mod_ref[:, 4 * D_MODEL:5 * D_MODEL]
    x1 = _layernorm(ALPHA * x_ref[...] + gate1 * m, lng_ref[...], lnb_ref[...])
    x1_ref[...] = x1
    h = x1 * (1.0 + scale2) + shift2
    h_ref[...] = h.astype(BF16)
    sel, gates_t = _route(_mm_hi_nt(rw_ref[...], h), rb_ref[...])
    rows, plen, loff = _route_meta(sel, gates_t)
    tm = rows.shape[1]
    mrow_ref[...] = rows
    mcol_ref[...] = jnp.concatenate([rows, jnp.zeros((LANE - rows.shape[0], tm), F32)], axis=0).T
    lane = lax.broadcasted_iota(jnp.int32, (N_EXPERTS, LANE), 1)
    cols = jnp.where(lane == 0, plen, jnp.where(lane == 1, loff, 0.0))
    cols_t = jnp.concatenate([cols, jnp.zeros((LANE - N_EXPERTS, LANE), F32)], axis=0).T
    plen_ref[...] = cols_t[0:1, :]
    loff_ref[...] = cols_t[1:2, :]


def _merge(o_gdn, o_gla, o_diff, pa, x, mod, wb, wo, ln_g, ln_b, rw_t, rb_col, layer, row0,
           tokens_per_row, tm):
    n = x.shape[0]
    mod_row = lambda i: (layer, row0 + (i * tm) // tokens_per_row, 0, 0)
    full = lambda shape: pl.BlockSpec(shape, lambda i: (0,) * len(shape))
    return pl.pallas_call(
        _merge_kernel,
        grid=(n // tm,),
        in_specs=[
            pl.BlockSpec((tm, MIX_W), lambda i: (i, 0)),
            pl.BlockSpec((tm, MIX_W), lambda i: (i, 0)),
            pl.BlockSpec((tm, MIX_W), lambda i: (i, 0)),
            pl.BlockSpec((tm, D_MODEL), lambda i: (i, A_MERGE // D_MODEL)),
            pl.BlockSpec((tm, D_MODEL), lambda i: (i, A_MERGE // D_MODEL + 1)),
            pl.BlockSpec((tm, D_MODEL), lambda i: (i, A_MERGE // D_MODEL + 2)),
            pl.BlockSpec((tm, D_MODEL), lambda i: (i, 0)),
            pl.BlockSpec((None, None, 1, 6 * D_MODEL), mod_row),
            full((N_BRANCH, MIX_W, D_MODEL)),
            full((D_MODEL, D_MODEL)),
            full((1, D_MODEL)),
            full((1, D_MODEL)),
            full((N_EXPERTS, D_MODEL)),
            full((N_EXPERTS, 1)),
        ],
        out_specs=[
            pl.BlockSpec((tm, D_MODEL), lambda i: (i, 0)),
            pl.BlockSpec((tm, D_MODEL), lambda i: (i, 0)),
            pl.BlockSpec((8, tm), lambda i: (0, i)),
            pl.BlockSpec((tm, LANE), lambda i: (i, 0)),
            pl.BlockSpec((None, 1, LANE), lambda i: (i, 0, 0)),
            pl.BlockSpec((None, 1, LANE), lambda i: (i, 0, 0)),
        ],
        out_shape=[
            jax.ShapeDtypeStruct((n, D_MODEL), F32),
            jax.ShapeDtypeStruct((n, D_MODEL), BF16),
            jax.ShapeDtypeStruct((8, n), F32),
            jax.ShapeDtypeStruct((n, LANE), F32),
            jax.ShapeDtypeStruct((n // tm, 1, LANE), F32),
            jax.ShapeDtypeStruct((n // tm, 1, LANE), F32),
        ],
        compiler_params=_cparams(("arbitrary",)),
    )(o_gdn, o_gla, o_diff, pa, pa, pa, x, mod, wb, wo, ln_g, ln_b, rw_t, rb_col)


def _plan_kernel(plen_ref, loff_ref, goff_ref, npc_ref, loffi_ref, tiles_ref, tail_ref, *, n_steps):
    plen = plen_ref[...]
    nb = plen.shape[0]
    earlier = jnp.where(_tri(nb, "gt"), 1.0, 0.0).astype(BF16)
    run = _dot(earlier, plen.astype(BF16))
    gtot = jnp.sum(plen, axis=0, keepdims=True)
    lane = lax.broadcasted_iota(jnp.int32, (1, LANE), 1).astype(F32)
    ntile = jnp.floor((gtot + (ROW_TILE - 1.0)) * (1.0 / ROW_TILE))
    before = jnp.where(_tri(LANE, "lt"), 1.0, 0.0).astype(BF16)
    first = _dot(jnp.broadcast_to(ntile, (8, LANE)).astype(BF16), before)[0:1, :]
    end = first + ntile
    n_used = jnp.sum(ntile, axis=1, keepdims=True)
    goff_ref[...] = (first * ROW_TILE + run).astype(jnp.int32)
    npc_ref[...] = (plen * (1.0 / PIECE)).astype(jnp.int32)
    loffi_ref[...] = loff_ref[...].astype(jnp.int32)
    end_col = jnp.concatenate([end, jnp.zeros((LANE - 1, LANE), F32)], axis=0).T[:, 0:1]
    e_col = lax.broadcasted_iota(jnp.int32, (LANE, 1), 0).astype(F32)
    tile = jnp.minimum(lane, n_used - 1.0)
    done = jnp.logical_and(end_col <= tile, e_col < N_EXPERTS)
    t_exp = jnp.minimum(jnp.sum(jnp.where(done, 1.0, 0.0), axis=0, keepdims=True), N_EXPERTS - 1.0)
    nonempty = jnp.where(ntile > 0.5, 1.0, 0.0)
    run_idx = _dot(jnp.broadcast_to(nonempty, (8, LANE)).astype(BF16), before)[0:1, :]
    owner_after = jnp.logical_and(end_col <= end, e_col < N_EXPERTS)
    nxt = jnp.minimum(jnp.sum(jnp.where(owner_after, 1.0, 0.0), axis=0, keepdims=True), N_EXPERTS - 1.0)
    has_next = jnp.where(end < n_used, 1.0, 0.0)
    by_expert = jnp.concatenate([run_idx - 2.0 * jnp.floor(run_idx * 0.5), nxt, has_next,
                                 jnp.zeros((LANE - 3, LANE), F32)], axis=0).T
    mine = e_col == t_exp
    per_tile = [jnp.sum(jnp.where(mine, by_expert[:, j:j + 1], 0.0), axis=0, keepdims=True) for j in range(3)]
    tiles_ref[...] = jnp.concatenate(
        [t_exp, jnp.broadcast_to(n_used, (1, LANE))] + per_tile + [jnp.zeros((3, LANE), F32)],
        axis=0).astype(jnp.int32)
    tail_ref[...] = jnp.concatenate(
        [first * ROW_TILE + gtot, (ntile * ROW_TILE - gtot) * (1.0 / PIECE),
         jnp.broadcast_to(n_used, (1, LANE)), jnp.broadcast_to(n_steps - n_used, (1, LANE)),
         jnp.zeros((4, LANE), F32)], axis=0).astype(jnp.int32)


def _plan(plen, loff, n_steps):
    nb = plen.shape[0]
    i32 = lambda rows: jax.ShapeDtypeStruct((rows, LANE), jnp.int32)
    return pl.pallas_call(
        functools.partial(_plan_kernel, n_steps=n_steps),
        out_shape=[i32(nb), i32(nb), i32(nb), i32(8), i32(8)],
    )(plen, loff)


def _piece_copies(b, loff_s, npc_s, goff_s, local, remote, sem, to_remote):
    def copy(lo, go):
        src = local.at[pl.ds(pl.multiple_of(lo, PIECE), PIECE)]
        dst = remote.at[pl.ds(pl.multiple_of(go, PIECE), PIECE)]
        return pltpu.make_async_copy(src, dst, sem) if to_remote else pltpu.make_async_copy(dst, src, sem)

    total = 0
    for e in range(N_EXPERTS):
        n = npc_s[b, e]
        lo = loff_s[b, e]
        go = goff_s[b, e]

        def start(j, carry, lo=lo, go=go):
            copy(lo + j * PIECE, go + j * PIECE).start()
            return carry

        lax.fori_loop(0, n, start, 0)
        total = total + n

    def wait(j, carry):
        copy(0, 0).wait()
        return carry

    lax.fori_loop(0, total, wait, 0)


def _dispatch_kernel(loff_s, npc_s, goff_s, tail_s, h_ref, mrow_ref, mcol_ref, hs_hbm, local, zeros, sem,
                     tile_sem, *, slots):
    b = pl.program_id(0)
    tb = h_ref.shape[0]
    mrow = mrow_ref[...]
    mcol = mcol_ref[...]
    slot_id = lax.broadcasted_iota(jnp.int32, (slots, tb), 0).astype(F32)
    s_a = jnp.where(slot_id == mrow[0:1, :], 1.0, 0.0).astype(BF16)
    s_b = jnp.where(slot_id == mrow[1:2, :], 1.0, 0.0).astype(BF16)
    lane = lax.broadcasted_iota(jnp.int32, (tb, LANE), 1)

    def weight_lanes(w):
        hi = w.astype(BF16).astype(F32)
        return jnp.where(lane == 0, hi, jnp.where(lane == 1, w - hi, 0.0)).astype(BF16)

    local[:, :D_MODEL] = _dot(s_a + s_b, h_ref[...]).astype(BF16)
    local[:, D_MODEL:] = (_dot(s_a, weight_lanes(mcol[:, 2:3])) + _dot(s_b, weight_lanes(mcol[:, 3:4]))).astype(BF16)
    _piece_copies(b, loff_s, npc_s, goff_s, local, hs_hbm, sem, True)

    @pl.when(b == pl.num_programs(0) - 1)
    def _():
        zeros[...] = jnp.zeros_like(zeros)
        zero_piece = zeros.at[pl.ds(0, PIECE)]
        total = 0
        for e in range(N_EXPERTS):
            n = tail_s[1, e]
            start_row = tail_s[0, e]

            def start(j, carry, start_row=start_row):
                dst = hs_hbm.at[pl.ds(pl.multiple_of(start_row + j * PIECE, PIECE), PIECE)]
                pltpu.make_async_copy(zero_piece, dst, sem).start()
                return carry

            lax.fori_loop(0, n, start, 0)
            total = total + n

        def wait(j, carry):
            pltpu.make_async_copy(zero_piece, hs_hbm.at[pl.ds(0, PIECE)], sem).wait()
            return carry

        lax.fori_loop(0, total, wait, 0)

        def tile_copy(j):
            row = pl.multiple_of((tail_s[2, 0] + j) * ROW_TILE, ROW_TILE)
            return pltpu.make_async_copy(zeros, hs_hbm.at[pl.ds(row, ROW_TILE)], tile_sem)

        def start_tile(j, carry):
            tile_copy(j).start()
            return carry

        def wait_tile(j, carry):
            tile_copy(j).wait()
            return carry

        lax.fori_loop(0, tail_s[3, 0], start_tile, 0)
        lax.fori_loop(0, tail_s[3, 0], wait_tile, 0)


def _dispatch(loff_i, npc, goff, tail, h, mrow, mcol, tb, n_steps):
    n = h.shape[0]
    slots = 2 * tb + SLOT_PAD
    grid_spec = pltpu.PrefetchScalarGridSpec(
        num_scalar_prefetch=4,
        grid=(n // tb,),
        in_specs=[
            pl.BlockSpec((tb, D_MODEL), lambda i, *_: (i, 0)),
            pl.BlockSpec((8, tb), lambda i, *_: (0, i)),
            pl.BlockSpec((tb, LANE), lambda i, *_: (i, 0)),
        ],
        out_specs=pl.BlockSpec(memory_space=pl.ANY),
        scratch_shapes=[pltpu.VMEM((slots, HS_COLS), BF16), pltpu.VMEM((ROW_TILE, HS_COLS), BF16),
                        pltpu.SemaphoreType.DMA, pltpu.SemaphoreType.DMA],
    )
    return pl.pallas_call(
        functools.partial(_dispatch_kernel, slots=slots),
        grid_spec=grid_spec,
        out_shape=jax.ShapeDtypeStruct((n_steps * ROW_TILE, HS_COLS), BF16),
        compiler_params=_cparams(("arbitrary",)),
    )(loff_i, npc, goff, tail, h, mrow, mcol)


def _expert_kernel(tiles_s, hs_ref, wg_hbm, wu_hbm, wd_hbm, y_ref, wg_f, wu_f, wd_f, wg_b, wu_b, wd_b, sems,
                   *, layer):
    i = pl.program_id(0)
    e = tiles_s[0, i]
    slot = tiles_s[2, i]
    first_of_run = jnp.logical_or(i == 0, e != tiles_s[0, jnp.maximum(i - 1, 0)])

    def weight_copies(expert, s):
        return [pltpu.make_async_copy(src.at[layer, expert], dst.at[s], sems.at[s, j])
                for j, (src, dst) in enumerate(((wg_hbm, wg_f), (wu_hbm, wu_f), (wd_hbm, wd_f)))]

    @pl.when(i == 0)
    def _():
        for cp in weight_copies(e, slot):
            cp.start()

    @pl.when(jnp.logical_and(first_of_run, i < tiles_s[1, 0]))
    def _():
        for cp in weight_copies(e, slot):
            cp.wait()

        @pl.when(tiles_s[4, i] > 0)
        def _():
            for cp in weight_copies(tiles_s[3, i], 1 - slot):
                cp.start()

        wg_b[...] = wg_f[slot].astype(BF16)
        wu_b[...] = wu_f[slot].astype(BF16)
        wd_b[...] = wd_f[slot].astype(BF16)

    @pl.when(i < tiles_s[1, 0])
    def _():
        halves = [pl.ds(r * (ROW_TILE // 2), ROW_TILE // 2) for r in range(2)]
        xs = [hs_ref[rows, :D_MODEL] for rows in halves]
        ws = [hs_ref[rows, D_MODEL:D_MODEL + 1].astype(F32) + hs_ref[rows, D_MODEL + 1:D_MODEL + 2].astype(F32)
              for rows in halves]
        gs = [_dot(x, wg_b[...]) for x in xs]
        us = [_dot(x, wu_b[...]) for x in xs]
        acts = [(_silu(g) * u * w).astype(BF16) for g, u, w in zip(gs, us, ws)]
        for rows, act in zip(halves, acts):
            y_ref[rows, :] = _dot(act, wd_b[...]).astype(BF16)

    @pl.when(i >= tiles_s[1, 0])
    def _():
        y_ref[...] = jnp.zeros_like(y_ref)


def _experts(tiles, hs, wg, wu, wd, layer, n_steps):
    grid_spec = pltpu.PrefetchScalarGridSpec(
        num_scalar_prefetch=1,
        grid=(n_steps,),
        in_specs=[
            pl.BlockSpec((ROW_TILE, HS_COLS), lambda i, t: (i, 0)),
            pl.BlockSpec(memory_space=pl.ANY),
            pl.BlockSpec(memory_space=pl.ANY),
            pl.BlockSpec(memory_space=pl.ANY),
        ],
        out_specs=pl.BlockSpec((ROW_TILE, D_MODEL), lambda i, t: (i, 0)),
        scratch_shapes=[pltpu.VMEM((2, D_MODEL, D_FF), F32), pltpu.VMEM((2, D_MODEL, D_FF), F32),
                        pltpu.VMEM((2, D_FF, D_MODEL), F32),
                        pltpu.VMEM((D_MODEL, D_FF), BF16), pltpu.VMEM((D_MODEL, D_FF), BF16),
                        pltpu.VMEM((D_FF, D_MODEL), BF16), pltpu.SemaphoreType.DMA((2, 3))],
    )
    return pl.pallas_call(
        functools.partial(_expert_kernel, layer=layer),
        grid_spec=grid_spec,
        out_shape=jax.ShapeDtypeStruct((hs.shape[0], D_MODEL), BF16),
        compiler_params=_cparams(("arbitrary",)),
    )(tiles, hs, wg, wu, wd)


def _combine_kernel(loff_s, npc_s, goff_s, y_hbm, mcol_ref, x1_ref, mod_ref, lng_ref, lnb_ref, o_ref,
                    local, sem, *, slots):
    b = pl.program_id(0)
    tb = x1_ref.shape[0]
    local[...] = jnp.zeros_like(local)
    _piece_copies(b, loff_s, npc_s, goff_s, local, y_hbm, sem, False)
    mcol = mcol_ref[...]
    slot_id = lax.broadcasted_iota(jnp.int32, (tb, slots), 1).astype(F32)
    pick = jnp.logical_or(slot_id == mcol[:, 0:1], slot_id == mcol[:, 1:2])
    f = _dot(jnp.where(pick, 1.0, 0.0).astype(BF16), local[...])
    gate2 = mod_ref[:, 5 * D_MODEL:6 * D_MODEL]
    o_ref[...] = _layernorm(ALPHA * x1_ref[...] + gate2 * f, lng_ref[...], lnb_ref[...])


def _combine(loff_i, npc, goff, y, mcol, x1, mod, ln_g, ln_b, layer, row0, tokens_per_row, tb):
    n = x1.shape[0]
    slots = 2 * tb + SLOT_PAD
    mod_row = lambda i, *_: (layer, row0 + (i * tb) // tokens_per_row, 0, 0)
    grid_spec = pltpu.PrefetchScalarGridSpec(
        num_scalar_prefetch=3,
        grid=(n // tb,),
        in_specs=[
            pl.BlockSpec(memory_space=pl.ANY),
            pl.BlockSpec((tb, LANE), lambda i, *_: (i, 0)),
            pl.BlockSpec((tb, D_MODEL), lambda i, *_: (i, 0)),
            pl.BlockSpec((None, None, 1, 6 * D_MODEL), mod_row),
            pl.BlockSpec((1, D_MODEL), lambda i, *_: (0, 0)),
            pl.BlockSpec((1, D_MODEL), lambda i, *_: (0, 0)),
        ],
        out_specs=pl.BlockSpec((tb, D_MODEL), lambda i, *_: (i, 0)),
        scratch_shapes=[pltpu.VMEM((slots, D_MODEL), BF16), pltpu.SemaphoreType.DMA],
    )
    return pl.pallas_call(
        functools.partial(_combine_kernel, slots=slots),
        grid_spec=grid_spec,
        out_shape=jax.ShapeDtypeStruct((n, D_MODEL), F32),
        compiler_params=_cparams(("arbitrary",)),
    )(loff_i, npc, goff, y, mcol, x1, mod, ln_g, ln_b)


def _moe(h, mrow, mcol, plen, loff, wg, wu, wd, x1, mod, ln_g, ln_b, layer, row0, tokens_per_row, tb):
    n = h.shape[0]
    nb = n // tb
    n_steps = (2 * n + nb * N_EXPERTS * (PIECE - 1)) // ROW_TILE + N_EXPERTS
    assert n_steps <= LANE
    goff, npc, loff_i, tiles, tail = _plan(plen.reshape(nb, LANE), loff.reshape(nb, LANE), n_steps)
    hs = _dispatch(loff_i, npc, goff, tail, h, mrow, mcol, tb, n_steps)
    y = _experts(tiles, hs, wg, wu, wd, layer, n_steps)
    return _combine(loff_i, npc, goff, y, mcol, x1, mod, ln_g, ln_b, layer, row0, tokens_per_row, tb)


def _reorder_w_in(w):
    sizes = (3 * GDN_W, GDN_W, 2 * GDN_HEADS, 2 * GDN_HEADS, GLA_KW, GLA_KW, GLA_VW, GLA_VW,
             2 * GLA_RANK, DIFF_QW, DIFF_QW, DIFF_VW, N_BRANCH * D_MODEL)
    offs = np.concatenate([[0], np.cumsum(sizes)])
    seg = lambda i: w[:, offs[i]:offs[i + 1]]
    used = 2 * GDN_HEADS * 2 + 2 * GLA_RANK
    wa = jnp.concatenate([seg(0), seg(1), seg(4), seg(5), seg(6), seg(7), seg(9), seg(12)], axis=1)
    wf = jnp.concatenate([seg(10), seg(11), seg(2), seg(3), seg(8),
                          jnp.zeros((w.shape[0], LANE - used), w.dtype)], axis=1)
    return wa.astype(BF16), wf.astype(BF16)


def _lane_row(vals, offset):
    return jnp.zeros((1, LANE), F32).at[0, offset:offset + vals.shape[0]].set(vals)


def kernel(x_prompt, x_sample, c, state_gdn, state_gla, cache_k, cache_v, c_ctx, w_mod, b_mod, w_in,
           gdn_conv, gdn_a_log, gdn_dt_bias, gdn_norm, gla_w_gate, gla_b_gate, gla_norm, diff_lambda,
           diff_norm, w_branch, w_out, ln_g, ln_b, router_w, router_b, exp_w_gate, exp_w_up, exp_w_down):
    bp, tp, d = x_prompt.shape
    bs, ts, _ = x_sample.shape
    pad_rows = MOD_ROWS - 1 - bs
    cvecs = jnp.concatenate([c_ctx[None, :], c, jnp.zeros((pad_rows, d), F32)], axis=0)
    mod = _modulation(cvecs, w_mod, b_mod).reshape(DEPTH, MOD_ROWS, 1, 6 * d)

    rw_t = router_w.T
    rb_col = router_b.reshape(N_EXPERTS, 1)
    ck = cache_k.reshape(bs, DEPTH, cache_k.shape[2], DIFF_QW)
    cv = cache_v.reshape(bs, DEPTH, cache_v.shape[2], DIFF_VW)

    layer_w = []
    for l in range(DEPTH):
        wa, wf = _reorder_w_in(w_in[l])
        layer_w.append(dict(
            wa=wa, wf=wf,
            alog=_lane_row(gdn_a_log[l].reshape(-1), MISC_A),
            dtb=_lane_row(gdn_dt_bias[l].reshape(-1), MISC_A),
            wb=w_branch[l].astype(BF16),
            wo=w_out[l].astype(BF16),
        ))

    def layer(x, l, row0, tokens_per_row, s_gdn, s_gla, ctx_k, ctx_v, emit_state, tm, prev_gdn=None,
              prev_gla=None):
        b, t, _ = x.shape
        n = b * t
        lw = layer_w[l]
        xf = x.reshape(n, d)
        pa, pf, *cache = _inproj(xf, mod, lw["wa"], lw["wf"], l, row0, tokens_per_row, tm, emit_state)
        pa3 = pa.reshape(b, t, A_COLS)
        pf3 = pf.reshape(b, t, F_COLS)
        o_gdn, gdn_fin = _gdn(pa3, pf3, gdn_conv[l], lw["alog"], lw["dtb"], gdn_norm[l][None, :], s_gdn, l,
                              emit_state, prev_gdn)
        o_gla, gla_fin = _gla(pa3, pf3, gla_w_gate[l], gla_b_gate[l], gla_norm[l][None, :], s_gla, l,
                              emit_state, prev_gla)
        lam_init = 0.8 - 0.6 * math.exp(-0.3 * l)
        o_diff = _diff(pa3, pf3, diff_lambda[l], diff_norm[l][None, :], ctx_k, ctx_v, l, lam_init)
        x1, h, mrow, mcol, plen, loff = _merge(
            o_gdn.reshape(n, MIX_W), o_gla.reshape(n, MIX_W), o_diff.reshape(n, MIX_W), pa, xf, mod,
            lw["wb"], lw["wo"], ln_g[l, 0][None, :], ln_b[l, 0][None, :], rw_t, rb_col, l, row0,
            tokens_per_row, tm)
        x2 = _moe(h, mrow, mcol, plen, loff, exp_w_gate, exp_w_up, exp_w_down, x1, mod,
                  ln_g[l, 1][None, :], ln_b[l, 1][None, :], l, row0, tokens_per_row, tm)
        k_own, v_own = [a.reshape(b, t, DIFF_HEADS, LANE) for a in cache] if emit_state else (None, None)
        return x2.reshape(b, t, d), gdn_fin, gla_fin, k_own, v_own

    hp = x_prompt
    gdn_states, gla_states, ks, vs = None, None, [], []
    for l in range(DEPTH):
        hp, gdn_states, gla_states, k_l, v_l = layer(hp, l, 0, bp * tp, None, None, None, None, True, 512,
                                                     gdn_states, gla_states)
        ks.append(k_l)
        vs.append(v_l)

    hs = x_sample
    for l in range(DEPTH):
        hs = layer(hs, l, 1, ts, state_gdn, state_gla, ck, cv, False, 512)[0]

    return hp, hs, gdn_states, gla_states, jnp.stack(ks, axis=1), jnp.stack(vs, axis=1)
```

```python
import functools
import math

import numpy as np
import jax
import jax.numpy as jnp
from jax import lax
from jax.experimental import pallas as pl
from jax.experimental.pallas import tpu as pltpu

F32 = jnp.float32
BF16 = jnp.bfloat16

D_MODEL = 1024
DEPTH = 2
GRID_W = 64
GDN_HEADS = 4
GDN_DK = 128
GDN_DV = 128
GDN_W = GDN_HEADS * GDN_DV
SHORT_CONV = 5
GLA_HEADS = 4
GLA_DK = 64
GLA_DV = 128
GLA_KW = GLA_HEADS * GLA_DK
GLA_VW = GLA_HEADS * GLA_DV
GLA_RANK = 16
GLA_TAU = 16.0
CHUNK = 64
DIFF_HEADS = 4
DIFF_DH = 64
DIFF_VD = 2 * DIFF_DH
DIFF_QW = DIFF_HEADS * 2 * DIFF_DH
DIFF_VW = DIFF_HEADS * DIFF_VD
ROPE_THETA = 10000.0
MIX_W = 512
N_BRANCH = 3
N_EXPERTS = 16
N_GROUPS = 4
GROUP_SIZE = N_EXPERTS // N_GROUPS
D_FF = 512
ALPHA = (2 * DEPTH) ** 0.25
EPS = 1e-6

LANE = 128
MOD_ROWS = 8
VMEM_LIMIT = 56 * 1024 * 1024

A_GDN_QKV = 0
A_GDN_Z = 1536
A_GLA_Q = 2048
A_GLA_K = 2304
A_GLA_V = 2560
A_GLA_R = 3072
A_DIFF_Q = 3584
A_MERGE = 4096
A_COLS = 7168
F_DIFF_K = 0
F_DIFF_V = 512
F_MISC = 1024
F_COLS = 1152
MISC_B = 0
MISC_A = 8
MISC_LR = 16


def _cparams(sem):
    return pltpu.CompilerParams(dimension_semantics=sem, vmem_limit_bytes=VMEM_LIMIT)


def _dot(a, b):
    return jnp.dot(a, b, preferred_element_type=F32)


def _mm(a, b):
    return _dot(a.astype(BF16), b.astype(BF16))


def _mm_nt(a, b):
    return lax.dot_general(a.astype(BF16), b.astype(BF16), (((1,), (1,)), ((), ())),
                           preferred_element_type=F32)


def _split3(x):
    hi = x.astype(BF16)
    r = x - hi.astype(F32)
    mid = r.astype(BF16)
    lo = (r - mid.astype(F32)).astype(BF16)
    return hi, mid, lo


def _mm_mask_lhs(mask_bf16, x):
    hi, mid, lo = _split3(x)
    return _dot(mask_bf16, hi) + _dot(mask_bf16, mid) + _dot(mask_bf16, lo)


def _mm_mask_rhs(x, mask_bf16):
    hi, mid, lo = _split3(x)
    return _dot(hi, mask_bf16) + _dot(mid, mask_bf16) + _dot(lo, mask_bf16)


def _mm_hi(a, b):
    ah = a.astype(BF16)
    al = (a - ah.astype(F32)).astype(BF16)
    bh = b.astype(BF16)
    bl = (b - bh.astype(F32)).astype(BF16)
    return _dot(ah, bh) + _dot(ah, bl) + _dot(al, bh)


def _mm_hi_nt(a, b):
    ah = a.astype(BF16)
    al = (a - ah.astype(F32)).astype(BF16)
    bh = b.astype(BF16)
    bl = (b - bh.astype(F32)).astype(BF16)
    dot = lambda x, y: lax.dot_general(x, y, (((1,), (1,)), ((), ())), preferred_element_type=F32)
    return dot(ah, bh) + dot(ah, bl) + dot(al, bh)


def _sigmoid(x):
    return 1.0 / (1.0 + jnp.exp(-x))


def _silu(x):
    return x * _sigmoid(x)


def _softplus(x):
    return jnp.maximum(x, 0.0) + jnp.log1p(jnp.exp(-jnp.abs(x)))


def _rms(x, g):
    return x * lax.rsqrt(jnp.mean(x * x, axis=-1, keepdims=True) + EPS) * g


def _layernorm(y, g, b):
    mu = jnp.mean(y, axis=-1, keepdims=True)
    d = y - mu
    var = jnp.mean(d * d, axis=-1, keepdims=True)
    return d * lax.rsqrt(var + EPS) * g + b


def _pick_lane(x, idx):
    lane = lax.broadcasted_iota(jnp.int32, x.shape, 1)
    return jnp.sum(jnp.where(lane == idx, x, 0.0), axis=1, keepdims=True)


def _tri(n, kind):
    r = lax.broadcasted_iota(jnp.int32, (n, n), 0)
    c = lax.broadcasted_iota(jnp.int32, (n, n), 1)
    return {"ge": r >= c, "gt": r > c, "le": r <= c, "lt": r < c}[kind]


def _mod_kernel(c_ref, w_ref, b_ref, o_ref):
    s = _silu(c_ref[...])
    o_ref[...] = _mm_hi(s, w_ref[...]) + b_ref[...]


def _modulation(cvecs, w_mod, b_mod):
    tn = 1536
    return pl.pallas_call(
        _mod_kernel,
        grid=(DEPTH, 6 * D_MODEL // tn),
        in_specs=[
            pl.BlockSpec((MOD_ROWS, D_MODEL), lambda l, j: (0, 0)),
            pl.BlockSpec((None, D_MODEL, tn), lambda l, j: (l, 0, j)),
            pl.BlockSpec((None, 1, tn), lambda l, j: (l, 0, j)),
        ],
        out_specs=pl.BlockSpec((None, MOD_ROWS, tn), lambda l, j: (l, 0, j)),
        out_shape=jax.ShapeDtypeStruct((DEPTH, MOD_ROWS, 6 * D_MODEL), F32),
        compiler_params=_cparams(("arbitrary", "arbitrary")),
    )(cvecs, w_mod, b_mod.reshape(DEPTH, 1, 6 * D_MODEL))


INPROJ_COL_STEP = 1792


def _inproj_kernel(*refs, emit_cache):
    x_ref, mod_ref, wa_ref, wf_ref, pa_ref, pf_ref = refs[:6]
    shift = mod_ref[:, 0:D_MODEL]
    scale = mod_ref[:, D_MODEL:2 * D_MODEL]
    u = (x_ref[...] * (1.0 + scale) + shift).astype(BF16)
    pf = _dot(u, wf_ref[...])
    pf_ref[...] = pf
    if emit_cache:
        ko_ref, vo_ref = refs[6:8]
        for h in range(DIFF_HEADS):
            ko_ref[:, h, :] = pf[:, F_DIFF_K + h * LANE:F_DIFF_K + (h + 1) * LANE]
            vo_ref[:, h, :] = pf[:, F_DIFF_V + h * LANE:F_DIFF_V + (h + 1) * LANE]
    for c0 in range(0, A_COLS, INPROJ_COL_STEP):
        cols = slice(c0, c0 + INPROJ_COL_STEP)
        pa_ref[:, cols] = _dot(u, wa_ref[:, cols]).astype(BF16)


def _inproj(x, mod, wa, wf, layer, row0, tokens_per_row, tm, emit_cache):
    n = x.shape[0]
    mod_row = lambda i: (layer, row0 + (i * tm) // tokens_per_row, 0, 0)
    out_specs = [
        pl.BlockSpec((tm, A_COLS), lambda i: (i, 0)),
        pl.BlockSpec((tm, F_COLS), lambda i: (i, 0)),
    ]
    out_shape = [
        jax.ShapeDtypeStruct((n, A_COLS), BF16),
        jax.ShapeDtypeStruct((n, F_COLS), F32),
    ]
    if emit_cache:
        out_specs += [pl.BlockSpec((tm, DIFF_HEADS, LANE), lambda i: (i, 0, 0))] * 2
        out_shape += [jax.ShapeDtypeStruct((n, DIFF_HEADS, LANE), F32)] * 2
    resident = lambda shape: pl.BlockSpec(shape, lambda i: (0, 0), pipeline_mode=pl.Buffered(1))
    return pl.pallas_call(
        functools.partial(_inproj_kernel, emit_cache=emit_cache),
        grid=(n // tm,),
        in_specs=[
            pl.BlockSpec((tm, D_MODEL), lambda i: (i, 0)),
            pl.BlockSpec((None, None, 1, 6 * D_MODEL), mod_row),
            resident((D_MODEL, A_COLS)),
            resident((D_MODEL, F_COLS)),
        ],
        out_specs=out_specs,
        out_shape=out_shape,
        compiler_params=_cparams(("arbitrary",)),
    )(x, mod, wa, wf)


def _conv_masks(shape, t_len):
    row = lax.broadcasted_iota(jnp.int32, shape, 0)
    half = SHORT_CONV // 2
    return {d: jnp.logical_and(row + d >= 0, row + d < t_len) for d in range(-half, half + 1) if d}


def _short_conv(x, w, t_len, masks):
    half = SHORT_CONV // 2
    acc = x * w[half:half + 1, :]
    for d, valid in masks.items():
        shifted = pltpu.roll(x, (-d) % t_len, axis=0)
        acc = acc + jnp.where(valid, shifted, 0.0) * w[half + d:half + d + 1, :]
    return acc


def _solve_unit_tri(a_list, rhs_list):
    n = a_list[0].shape[0]
    eye = jnp.where(_tri(n, "ge") & _tri(n, "le"), 1.0, 0.0)
    a_b = [a.astype(BF16) for a in a_list]
    p_b = [-a for a in a_b]
    inv = [eye - a for a in a_list]
    for _ in range(int(math.log2(n)) - 1):
        p_b = [_dot(p, p).astype(BF16) for p in p_b]
        inv = [x + _dot(x.astype(BF16), p) for x, p in zip(inv, p_b)]
    inv_b = [x.astype(BF16) for x in inv]
    sol = [_dot(x, r.astype(BF16)) for x, r in zip(inv_b, rhs_list)]
    resid = [r - s - _dot(a, s.astype(BF16)) for r, s, a in zip(rhs_list, sol, a_b)]
    return [s + _dot(x, r.astype(BF16)) for s, x, r in zip(sol, inv_b, resid)]


def _gdn_kernel(*refs, t_len, has_state, emit_state):
    it = iter(refs)
    qkv_ref, z_ref, misc_ref, cw_ref, alog_ref, dtb_ref, g_ref = (next(it) for _ in range(7))
    s0_ref = next(it) if has_state else None
    if emit_state == "later":
        next(it)
    o_ref = next(it)
    sfin_ref = next(it) if emit_state else None
    (q_s, k_s, v_s, kt_s, bet, gat, gat_t, u_s, w_s, qd_s, a_s, ktl_s, gts, o_f, o_b,
     s_s) = (next(it) for _ in range(16))

    c = CHUNK
    n_chunks = t_len // c
    nh = GDN_HEADS
    dk = GDN_DK

    masks = _conv_masks((t_len, dk), t_len)
    for h in range(nh):
        hs = slice(h * dk, (h + 1) * dk)
        q, k, v = (
            _silu(_short_conv(qkv_ref[:, j * GDN_W + h * dk:j * GDN_W + (h + 1) * dk].astype(F32),
                              cw_ref[:, j * GDN_W + h * dk:j * GDN_W + (h + 1) * dk], t_len, masks))
            for j in range(3))
        q = q * lax.rsqrt(jnp.sum(q * q, axis=-1, keepdims=True) + EPS) * (dk ** -0.5)
        k = k * lax.rsqrt(jnp.sum(k * k, axis=-1, keepdims=True) + EPS)
        q_s[:, hs] = q
        k_s[:, hs] = k
        v_s[:, hs] = v
        kt_s[hs, :] = k.T.astype(BF16)
    misc = misc_ref[...]
    bet[...] = _sigmoid(misc)
    g_all = -jnp.exp(alog_ref[...]) * _softplus(misc + dtb_ref[...])
    gat[...] = g_all
    gat_t[...] = g_all.T[MISC_A:MISC_A + 2 * nh, :]
    for d in range(2):
        for h in range(nh):
            s_s[d * nh + h] = s0_ref[d, h] if has_state else jnp.zeros((dk, GDN_DV), F32)

    two = 2 * c
    r2 = lax.broadcasted_iota(jnp.int32, (two, two), 0)
    c2 = lax.broadcasted_iota(jnp.int32, (two, two), 1)
    same = (r2 >= c) == (c2 >= c)
    lo2 = jnp.where(jnp.logical_and(same, r2 >= c2), 1.0, 0.0).astype(BF16)
    up2 = jnp.where(jnp.logical_and(same, r2 <= c2), 1.0, 0.0).astype(BF16)
    incl = (_tri(c, "ge"), _tri(c, "le"))
    strict = (_tri(c, "gt"), _tri(c, "lt"))
    last = (c - 1, 0)

    def phase1(cp, carry):
        r0 = pl.multiple_of(cp * two, two)
        g_blk = gat[pl.ds(r0, two), :]
        b_blk = bet[pl.ds(r0, two), :]
        gt_blk = gat_t[:, pl.ds(r0, two)]
        cum_col = (_mm_mask_lhs(lo2, g_blk), _mm_mask_lhs(up2, g_blk))
        cum_row = (_mm_mask_rhs(gt_blk, up2), _mm_mask_rhs(gt_blk, lo2))
        chains, a_list, rhs_list = [], [], []
        for h in range(nh):
            hs = slice(h * dk, (h + 1) * dk)
            kt_pair = kt_s[hs, pl.ds(r0, two)]
            for s in range(2):
                rows = pl.ds(pl.multiple_of(r0 + s * c, c), c)
                q_c = q_s[rows, hs]
                k_c = k_s[rows, hs]
                v_c = v_s[rows, hs]
                kt_c = kt_pair[:, s * c:(s + 1) * c]
                kk = _dot(k_c.astype(BF16), kt_c)
                qk = _dot(q_c.astype(BF16), kt_c)
                for d in range(2):
                    idx = d * nh + h
                    gc = cum_col[d][s * c:(s + 1) * c, MISC_A + idx:MISC_A + idx + 1]
                    gcr = cum_row[d][idx:idx + 1, s * c:(s + 1) * c]
                    beta = b_blk[s * c:(s + 1) * c, MISC_B + idx:MISC_B + idx + 1]
                    decay = jnp.where(incl[d], jnp.exp(jnp.where(incl[d], gc - gcr, 0.0)), 0.0)
                    e_gc = jnp.exp(gc)
                    a_list.append(jnp.where(strict[d], beta * kk * decay, 0.0))
                    rhs_list.append(jnp.concatenate([v_c * beta, k_c * (beta * e_gc)], axis=1))
                    a_s[idx, rows, :] = jnp.where(incl[d], qk * decay, 0.0).astype(BF16)
                    qd_s[idx, rows, :] = (q_c * e_gc).astype(BF16)
                    g_last = gcr[:, last[d]:last[d] + 1]
                    ktl_s[idx, cp * 2 + s] = (kt_c.astype(F32) * jnp.exp(g_last - gcr)).astype(BF16)
                    gts[cp * 2 + s, idx:idx + 1, :] = jnp.broadcast_to(jnp.exp(g_last), (1, LANE))
                    chains.append((idx, rows))
        for (idx, rows), sol in zip(chains, _solve_unit_tri(a_list, rhs_list)):
            u_s[idx, rows, :] = sol[:, :GDN_DV]
            w_s[idx, rows, :] = sol[:, GDN_DV:].astype(BF16)
        return carry

    lax.fori_loop(0, n_chunks // 2, phase1, 0)

    def phase2(i, carry):
        chains = []
        for d, out in ((0, o_f), (1, o_b)):
            ci = i if d == 0 else n_chunks - 1 - i
            rows = pl.ds(pl.multiple_of(ci * c, c), c)
            g_tail = gts[ci]
            for h in range(nh):
                chains.append((d * nh + h, h, ci, rows, out, g_tail))
        s_f32 = [s_s[idx] for idx, *_ in chains]
        s_b = [s.astype(BF16) for s in s_f32]
        v_b = [(u_s[idx, rows, :] - _dot(w_s[idx, rows, :], sb)).astype(BF16)
               for (idx, _, _, rows, _, _), sb in zip(chains, s_b)]
        for (idx, h, ci, rows, out, g_tail), s, sb, vb in zip(chains, s_f32, s_b, v_b):
            s_s[idx] = s * g_tail[idx:idx + 1, :] + _dot(ktl_s[idx, ci], vb)
            out[rows, h * GDN_DV:(h + 1) * GDN_DV] = _dot(qd_s[idx, rows, :], sb) + _dot(
                a_s[idx, rows, :], vb)
        return carry

    lax.fori_loop(0, n_chunks, phase2, 0)

    for h in range(nh):
        hs = slice(h * GDN_DV, (h + 1) * GDN_DV)
        o = o_f[:, hs] + o_b[:, hs]
        o_ref[:, hs] = (_rms(o, g_ref[...]) * _silu(z_ref[:, hs].astype(F32))).astype(BF16)
    if emit_state:
        for d in range(2):
            for h in range(nh):
                _store_state(sfin_ref, emit_state, d, h, s_s[d * nh + h])


def _store_state(sfin_ref, mode, d, h, s):
    if mode == "first":
        sfin_ref[0, d, h] = s
        for l in range(1, DEPTH):
            sfin_ref[l, d, h] = jnp.zeros_like(s)
    else:
        sfin_ref[d, h] = s


def _state_output(b, layer, heads, dk, dv, prev, in_specs, args):
    shape = jax.ShapeDtypeStruct((b, DEPTH, 2, heads, dk, dv), F32)
    if prev is None:
        return "first", pl.BlockSpec((None, DEPTH, 2, heads, dk, dv), lambda bi: (bi, 0, 0, 0, 0, 0)), shape, {}
    in_specs.append(pl.BlockSpec(memory_space=pl.ANY))
    args.append(prev)
    spec = pl.BlockSpec((None, None, 2, heads, dk, dv), lambda bi: (bi, layer, 0, 0, 0, 0))
    return "later", spec, shape, {len(args) - 1: 1}


def _gdn(pa3, pf3, conv_w, alog_row, dtb_row, norm_g, s0, layer, emit_state, prev_states):
    b, t, _ = pa3.shape
    full = lambda shape: pl.BlockSpec(shape, lambda bi: (0,) * len(shape))
    in_specs = [
        pl.BlockSpec((None, t, 3 * GDN_W), lambda bi: (bi, 0, A_GDN_QKV // (3 * GDN_W))),
        pl.BlockSpec((None, t, GDN_W), lambda bi: (bi, 0, A_GDN_Z // GDN_W)),
        pl.BlockSpec((None, t, LANE), lambda bi: (bi, 0, F_MISC // LANE)),
        full((SHORT_CONV, 3 * GDN_W)),
        full((1, LANE)),
        full((1, LANE)),
        full((1, GDN_DV)),
    ]
    args = [pa3, pa3, pf3, conv_w, alog_row, dtb_row, norm_g]
    if s0 is not None:
        in_specs.append(pl.BlockSpec((None, None, 2, GDN_HEADS, GDN_DK, GDN_DV),
                                     lambda bi: (bi, layer, 0, 0, 0, 0)))
        args.append(s0)
    out_specs = [pl.BlockSpec((None, t, GDN_W), lambda bi: (bi, 0, 0))]
    out_shape = [jax.ShapeDtypeStruct((b, t, GDN_W), BF16)]
    mode, aliases = None, {}
    if emit_state:
        mode, spec, shape, aliases = _state_output(b, layer, GDN_HEADS, GDN_DK, GDN_DV, prev_states, in_specs, args)
        out_specs.append(spec)
        out_shape.append(shape)
    nc = t // CHUNK
    nd = 2 * GDN_HEADS
    wide = pltpu.VMEM((t, GDN_W), F32)
    scratch = [
        wide, wide, wide,
        pltpu.VMEM((GDN_W, t), BF16),
        pltpu.VMEM((t, LANE), F32), pltpu.VMEM((t, LANE), F32),
        pltpu.VMEM((nd, t), F32),
        pltpu.VMEM((nd, t, GDN_DV), F32),
        pltpu.VMEM((nd, t, GDN_DK), BF16),
        pltpu.VMEM((nd, t, GDN_DK), BF16),
        pltpu.VMEM((nd, t, CHUNK), BF16),
        pltpu.VMEM((nd, nc, GDN_DK, CHUNK), BF16),
        pltpu.VMEM((nc, nd, LANE), F32),
        wide, wide,
        pltpu.VMEM((nd, GDN_DK, GDN_DV), F32),
    ]
    outs = pl.pallas_call(
        functools.partial(_gdn_kernel, t_len=t, has_state=s0 is not None, emit_state=mode),
        grid=(b,),
        in_specs=in_specs,
        out_specs=out_specs,
        out_shape=out_shape,
        scratch_shapes=scratch,
        input_output_aliases=aliases,
        compiler_params=_cparams(("arbitrary",)),
    )(*args)
    return outs if emit_state else (outs[0], None)


def _gla_kernel(*refs, t_len, has_state, emit_state):
    it = iter(refs)
    q_ref, k_ref, v_ref, r_ref, misc_ref, wg_ref, bg_ref, g_ref = (next(it) for _ in range(8))
    s0_ref = next(it) if has_state else None
    if emit_state == "later":
        next(it)
    o_ref = next(it)
    sfin_ref = next(it) if emit_state else None
    glog, vt_s, o_f, o_b, st = (next(it) for _ in range(5))

    c = CHUNK
    two = 2 * c
    n_pairs = t_len // two
    nh = GLA_HEADS
    misc = misc_ref[...]
    for d in range(2):
        lr = misc[:, MISC_LR + d * GLA_RANK:MISC_LR + (d + 1) * GLA_RANK]
        logits = _mm_hi(lr, wg_ref[d]) + bg_ref[d:d + 1, :]
        glog[d] = -_softplus(-logits) / GLA_TAU
    for h in range(nh):
        hs = slice(h * GLA_DV, (h + 1) * GLA_DV)
        vt_s[hs, :] = v_ref[:, hs].astype(F32).T.astype(BF16)
    for d in range(2):
        for h in range(nh):
            if has_state:
                s0 = jnp.concatenate([s0_ref[d, h], jnp.zeros((GLA_DV - GLA_DK, GLA_DV), F32)], axis=0)
                st[d * nh + h] = s0.T[:, :GLA_DK]
            else:
                st[d * nh + h] = jnp.zeros((GLA_DV, GLA_DK), F32)

    r2 = lax.broadcasted_iota(jnp.int32, (two, two), 0)
    c2 = lax.broadcasted_iota(jnp.int32, (two, two), 1)
    same = (r2 >= c) == (c2 >= c)
    cum_mask = (jnp.where(jnp.logical_and(same, r2 >= c2), 1.0, 0.0).astype(BF16),
                jnp.where(jnp.logical_and(same, r2 <= c2), 1.0, 0.0).astype(BF16))
    incl = (_tri(c, "ge"), _tri(c, "le"))
    last = (c - 1, 0)
    mid = (c // 2, c - 1 - c // 2)
    order = ((0, 1), (1, 0))

    def body(i, carry):
        pieces = []
        for d in range(2):
            r0 = pl.multiple_of((i if d == 0 else n_pairs - 1 - i) * two, two)
            b_all = _mm_mask_lhs(cum_mask[d], glog[d, pl.ds(r0, two), :])
            q_all = q_ref[pl.ds(r0, two), :].astype(F32) * (GLA_DK ** -0.5)
            k_all = k_ref[pl.ds(r0, two), :].astype(F32)
            vt_pair = [vt_s[h * GLA_DV:(h + 1) * GLA_DV, pl.ds(r0, two)] for h in range(nh)]
            for s in range(2):
                rs = slice(s * c, (s + 1) * c)
                bq = b_all[rs]
                b_mid = bq[mid[d]:mid[d] + 1, :]
                b_last = bq[last[d]:last[d] + 1, :]
                qe = (q_all[rs] * jnp.exp(bq - b_mid)).astype(BF16)
                ke = (k_all[rs] * jnp.exp(b_mid - bq)).astype(BF16)
                qd = (q_all[rs] * jnp.exp(bq)).astype(BF16)
                kt = (k_all[rs] * jnp.exp(b_last - bq)).astype(BF16)
                g_last = jnp.exp(b_last)
                rows = pl.ds(pl.multiple_of(r0 + s * c, c), c)
                for h in range(nh):
                    ks = slice(h * GLA_DK, (h + 1) * GLA_DK)
                    pieces.append((d, s, h, rows, qe[:, ks], ke[:, ks], qd[:, ks], kt[:, ks], g_last[:, ks],
                                   vt_pair[h][:, rs], v_ref[rows, h * GLA_DV:(h + 1) * GLA_DV]))
        a_intra = [jnp.where(incl[p[0]], _mm_nt(p[4], p[5]), 0.0).astype(BF16) for p in pieces]
        upd = [_dot(p[9], p[7]) for p in pieces]
        intra = [_dot(a, p[10]) for a, p in zip(a_intra, pieces)]
        for step in range(2):
            for p, m, o_in in zip(pieces, upd, intra):
                d, s, h, rows = p[0], p[1], p[2], p[3]
                if s != order[d][step]:
                    continue
                s_t = st[d * nh + h]
                out = o_f if d == 0 else o_b
                out[rows, h * GLA_DV:(h + 1) * GLA_DV] = _mm_nt(p[6], s_t) + o_in
                st[d * nh + h] = s_t * p[8] + m
        return carry

    lax.fori_loop(0, n_pairs, body, 0)

    for h in range(nh):
        hs = slice(h * GLA_DV, (h + 1) * GLA_DV)
        o = o_f[:, hs] + o_b[:, hs]
        o_ref[:, hs] = (_rms(o, g_ref[...]) * _silu(r_ref[:, hs].astype(F32))).astype(BF16)
    if emit_state:
        for d in range(2):
            for h in range(nh):
                s_pad = jnp.concatenate([st[d * nh + h], jnp.zeros((GLA_DV, GLA_DV - GLA_DK), F32)], axis=1)
                _store_state(sfin_ref, emit_state, d, h, s_pad.T[:GLA_DK, :])


def _gla(pa3, pf3, wg, bg, norm_g, s0, layer, emit_state, prev_states):
    b, t, _ = pa3.shape
    full = lambda shape: pl.BlockSpec(shape, lambda bi: (0,) * len(shape))
    in_specs = [
        pl.BlockSpec((None, t, GLA_KW), lambda bi: (bi, 0, A_GLA_Q // GLA_KW)),
        pl.BlockSpec((None, t, GLA_KW), lambda bi: (bi, 0, A_GLA_K // GLA_KW)),
        pl.BlockSpec((None, t, GLA_VW), lambda bi: (bi, 0, A_GLA_V // GLA_VW)),
        pl.BlockSpec((None, t, GLA_VW), lambda bi: (bi, 0, A_GLA_R // GLA_VW)),
        pl.BlockSpec((None, t, LANE), lambda bi: (bi, 0, F_MISC // LANE)),
        full((2, GLA_RANK, GLA_KW)),
        full((2, GLA_KW)),
        full((1, GLA_DV)),
    ]
    args = [pa3, pa3, pa3, pa3, pf3, wg, bg, norm_g]
    if s0 is not None:
        in_specs.append(pl.BlockSpec((None, None, 2, GLA_HEADS, GLA_DK, GLA_DV),
                                     lambda bi: (bi, layer, 0, 0, 0, 0)))
        args.append(s0)
    out_specs = [pl.BlockSpec((None, t, GLA_VW), lambda bi: (bi, 0, 0))]
    out_shape = [jax.ShapeDtypeStruct((b, t, GLA_VW), BF16)]
    mode, aliases = None, {}
    if emit_state:
        mode, spec, shape, aliases = _state_output(b, layer, GLA_HEADS, GLA_DK, GLA_DV, prev_states, in_specs, args)
        out_specs.append(spec)
        out_shape.append(shape)
    scratch = [pltpu.VMEM((2, t, GLA_KW), F32),
               pltpu.VMEM((GLA_VW, t), BF16),
               pltpu.VMEM((t, GLA_VW), F32), pltpu.VMEM((t, GLA_VW), F32),
               pltpu.VMEM((2 * GLA_HEADS, GLA_DV, GLA_DK), F32)]
    outs = pl.pallas_call(
        functools.partial(_gla_kernel, t_len=t, has_state=s0 is not None, emit_state=mode),
        grid=(b,),
        in_specs=in_specs,
        out_specs=out_specs,
        out_shape=out_shape,
        scratch_shapes=scratch,
        input_output_aliases=aliases,
        compiler_params=_cparams(("arbitrary",)),
    )(*args)
    return outs if emit_state else (outs[0], None)


ATT_Q_BLOCK = 256


def _rope_tables(t_len):
    half = DIFF_DH // 2
    quarter = half // 2
    inv = ROPE_THETA ** (-np.arange(0, half, 2, dtype=np.float64) / half)
    tok = np.arange(t_len)
    pos = np.stack([tok // GRID_W, tok % GRID_W], axis=1).astype(np.float64)
    ang = pos[:, :, None] * inv[None, None, :]
    cos = np.concatenate([np.cos(ang), np.cos(ang)], axis=-1).reshape(t_len, DIFF_DH)
    sin = np.concatenate([-np.sin(ang), np.sin(ang)], axis=-1).reshape(t_len, DIFF_DH)
    cos = np.concatenate([cos, cos], axis=-1).astype(np.float32)
    sin = np.concatenate([sin, sin], axis=-1).astype(np.float32)
    first = ((np.arange(2 * DIFF_DH) % half) < quarter).astype(np.float32)[None, :]
    return jnp.asarray(cos), jnp.asarray(sin), jnp.asarray(first), quarter


def _rope(x, cos, sin, first, quarter):
    width = x.shape[-1]
    ahead = pltpu.roll(x, width - quarter, axis=1)
    behind = pltpu.roll(x, quarter, axis=1)
    partner = jnp.where(first > 0.5, ahead, behind)
    return x * cos + partner * sin


def _diff_kernel(*refs, t_len, ctx_len, lam_init):
    it = iter(refs)
    q_ref, k_ref, v_ref, lam_ref, g_ref = (next(it) for _ in range(5))
    if ctx_len:
        ck_ref, cv_ref, cos_ref, sin_ref, first_ref = (next(it) for _ in range(5))
    o_ref = next(it)
    q_sc, k_sc, v_sc = (next(it) for _ in range(3))

    scale = DIFF_DH ** -0.5
    for h in range(DIFF_HEADS):
        hs = slice(h * LANE, (h + 1) * LANE)
        q = q_ref[:, hs].astype(F32)
        k = k_ref[:, hs]
        if ctx_len:
            quarter = DIFF_DH // 4
            q = _rope(q, cos_ref[...], sin_ref[...], first_ref[...], quarter)
            k = _rope(k, cos_ref[...], sin_ref[...], first_ref[...], quarter)
        q_sc[:, hs] = (q * scale).astype(BF16)
        k_sc[0:t_len, hs] = k.astype(BF16)
    v_sc[0:t_len, :] = v_ref[...].astype(BF16)
    if ctx_len:
        k_sc[t_len:t_len + ctx_len, :] = ck_ref[...].astype(BF16)
        v_sc[t_len:t_len + ctx_len, :] = cv_ref[...].astype(BF16)

    lp = lam_ref[...]
    lam = (jnp.exp(jnp.sum(lp[0:1, :] * lp[1:2, :], axis=1, keepdims=True))
           - jnp.exp(jnp.sum(lp[2:3, :] * lp[3:4, :], axis=1, keepdims=True)) + lam_init)

    tq = min(ATT_Q_BLOCK, t_len)
    group = DIFF_HEADS if t_len + ctx_len <= 512 else 2

    def body(i, carry):
        r0 = pl.multiple_of(i * tq, tq)
        for h0 in range(0, DIFF_HEADS, group):
            parts = [(h, half) for h in range(h0, h0 + group) for half in range(2)]
            cols = [slice(h * LANE + half * DIFF_DH, h * LANE + (half + 1) * DIFF_DH) for h, half in parts]
            s = [_mm_nt(q_sc[pl.ds(r0, tq), c], k_sc[:, c]) for c in cols]
            e = [jnp.exp(x - jnp.max(x, axis=-1, keepdims=True)) for x in s]
            inv_l = [1.0 / jnp.sum(x, axis=-1, keepdims=True) for x in e]
            pv = [_dot(x.astype(BF16), v_sc[:, h * LANE:(h + 1) * LANE]) * r
                  for x, r, (h, _) in zip(e, inv_l, parts)]
            for j in range(group):
                h = h0 + j
                o = pv[2 * j] - lam * pv[2 * j + 1]
                o_ref[pl.ds(r0, tq), h * LANE:(h + 1) * LANE] = (
                    _rms(o, g_ref[...]) * (1.0 - lam_init)).astype(BF16)
        return carry

    lax.fori_loop(0, t_len // tq, body, 0)


def _diff(pa3, pf3, lam_p, norm_g, ctx_k, ctx_v, layer, lam_init):
    b, t, _ = pa3.shape
    ctx_len = 0 if ctx_k is None else ctx_k.shape[2]
    full = lambda shape: pl.BlockSpec(shape, lambda bi: (0,) * len(shape))
    in_specs = [
        pl.BlockSpec((None, t, DIFF_QW), lambda bi: (bi, 0, A_DIFF_Q // DIFF_QW)),
        pl.BlockSpec((None, t, DIFF_QW), lambda bi: (bi, 0, F_DIFF_K // DIFF_QW)),
        pl.BlockSpec((None, t, DIFF_VW), lambda bi: (bi, 0, F_DIFF_V // DIFF_VW)),
        full((4, DIFF_DH)),
        full((1, DIFF_VD)),
    ]
    args = [pa3, pf3, pf3, lam_p, norm_g]
    if ctx_len:
        cos, sin, first, _ = _rope_tables(t)
        in_specs += [
            pl.BlockSpec((None, None, ctx_len, DIFF_QW), lambda bi: (bi, layer, 0, 0)),
            pl.BlockSpec((None, None, ctx_len, DIFF_VW), lambda bi: (bi, layer, 0, 0)),
            full((t, LANE)),
            full((t, LANE)),
            full((1, LANE)),
        ]
        args += [ctx_k, ctx_v, cos, sin, first]
    tk = t + ctx_len
    return pl.pallas_call(
        functools.partial(_diff_kernel, t_len=t, ctx_len=ctx_len, lam_init=lam_init),
        grid=(b,),
        in_specs=in_specs,
        out_specs=pl.BlockSpec((None, t, DIFF_VW), lambda bi: (bi, 0, 0)),
        out_shape=jax.ShapeDtypeStruct((b, t, DIFF_VW), BF16),
        scratch_shapes=[pltpu.VMEM((t, DIFF_QW), BF16), pltpu.VMEM((tk, DIFF_QW), BF16),
                        pltpu.VMEM((tk, DIFF_VW), BF16)],
        compiler_params=_cparams(("arbitrary",)),
    )(*args)


def _route(logits_t, bias_col):
    scores = _sigmoid(logits_t)
    biased = scores + bias_col
    rows = [biased[e:e + 1, :] for e in range(N_EXPERTS)]
    grp = []
    for g in range(N_GROUPS):
        a0, a1, a2, a3 = rows[g * GROUP_SIZE:(g + 1) * GROUP_SIZE]
        hi01, lo01 = jnp.maximum(a0, a1), jnp.minimum(a0, a1)
        hi23, lo23 = jnp.maximum(a2, a3), jnp.minimum(a2, a3)
        top1 = jnp.maximum(hi01, hi23)
        top2 = jnp.maximum(jnp.minimum(hi01, hi23), jnp.maximum(lo01, lo23))
        grp.append(top1 + top2)
    best = []
    for g in range(N_GROUPS):
        win = None
        for o in range(N_GROUPS):
            if o == g:
                continue
            cond = grp[g] > grp[o] if o < g else grp[g] >= grp[o]
            win = cond if win is None else jnp.logical_and(win, cond)
        best.append(win)
    sel_rows = []
    for e in range(N_EXPERTS):
        g = e // GROUP_SIZE
        beaten = jnp.zeros_like(rows[e])
        for o in range(g * GROUP_SIZE, (g + 1) * GROUP_SIZE):
            if o == e:
                continue
            ahead = rows[o] >= rows[e] if o < e else rows[o] > rows[e]
            beaten = beaten + jnp.where(ahead, 1.0, 0.0)
        sel_rows.append(jnp.where(jnp.logical_and(best[g], beaten < 1.5), 1.0, 0.0))
    sel = jnp.concatenate(sel_rows, axis=0)
    picked = sel * scores
    return sel, picked / jnp.sum(picked, axis=0, keepdims=True)


PIECE = 16
ROW_TILE = 256
SLOT_PAD = N_EXPERTS * PIECE
HS_COLS = D_MODEL + LANE


def _route_meta(sel, wts):
    n_e, t = sel.shape
    e_col = lax.broadcasted_iota(jnp.int32, (n_e, 1), 0).astype(F32)
    before = jnp.where(_tri(t, "lt"), 1.0, 0.0).astype(BF16)
    rank = _dot(sel.astype(BF16), before)
    cnt = jnp.sum(sel, axis=1, keepdims=True)
    plen = jnp.floor((cnt + (PIECE - 1.0)) * (1.0 / PIECE)) * PIECE
    lower = jnp.where(_tri(n_e, "gt"), 1.0, 0.0).astype(BF16)
    loff = _dot(lower, jnp.broadcast_to(plen, (n_e, LANE)).astype(BF16))[:, 0:1]
    slot = loff + rank
    chosen = sel > 0.5
    e_a = jnp.min(jnp.where(chosen, e_col, 2.0 * n_e), axis=0, keepdims=True)
    e_b = jnp.max(jnp.where(chosen, e_col, -1.0), axis=0, keepdims=True)
    take = lambda e_row, x: jnp.sum(jnp.where(e_col == e_row, x, 0.0), axis=0, keepdims=True)
    rows = jnp.concatenate([take(e_a, slot), take(e_b, slot), take(e_a, wts), take(e_b, wts), e_a, e_b,
                            jnp.zeros((2, t), F32)], axis=0)
    return rows, plen, loff


def _merge_kernel(og_ref, ol_ref, od_ref, lg0_ref, lg1_ref, lg2_ref, x_ref, mod_ref, wb_ref, wo_ref,
                  lng_ref, lnb_ref, rw_ref, rb_ref, x1_ref, h_ref, mrow_ref, mcol_ref, plen_ref, loff_ref):
    gate1 = mod_ref[:, 2 * D_MODEL:3 * D_MODEL]
    shift2 = mod_ref[:, 3 * D_MODEL:4 * D_MODEL]
    scale2 = mod_ref[:, 4 * D_MODEL:5 * D_MODEL]
    tm = x_ref.shape[0]
    halves = [pl.ds(r * (tm // 2), tm // 2) for r in range(2)]
    accs = [None, None]
    for s, (o_ref, lg_ref) in enumerate(((og_ref, lg0_ref), (ol_ref, lg1_ref), (od_ref, lg2_ref))):
        projs = [_dot(o_ref[rows, :], wb_ref[s]) for rows in halves]
        terms = [_sigmoid(lg_ref[rows, :].astype(F32)) * p for rows, p in zip(halves, projs)]
        accs = [t if a is None else a + t for a, t in zip(accs, terms)]
    ms = [_mm(a, wo_ref[...]) for a in accs]
    logit_parts = []
    for rows, m in zip(halves, ms):
        x1 = _layernorm(ALPHA * x_ref[rows, :] + gate1 * m, lng_ref[...], lnb_ref[...])
        x1_ref[rows, :] = x1
        h = x1 * (1.0 + scale2) + shift2
        h_ref[rows, :] = h.astype(BF16)
        logit_parts.append(_mm_hi_nt(rw_ref[...], h))
    sel, gates_t = _route(jnp.concatenate(logit_parts, axis=1), rb_ref[...])
    rows, plen, loff = _route_meta(sel, gates_t)
    mrow_ref[...] = rows
    mcol_ref[...] = jnp.concatenate([rows, jnp.zeros((LANE - rows.shape[0], tm), F32)], axis=0).T
    lane = lax.broadcasted_iota(jnp.int32, (N_EXPERTS, LANE), 1)
    cols = jnp.where(lane == 0, plen, jnp.where(lane == 1, loff, 0.0))
    cols_t = jnp.concatenate([cols, jnp.zeros((LANE - N_EXPERTS, LANE), F32)], axis=0).T
    plen_ref[...] = cols_t[0:1, :]
    loff_ref[...] = cols_t[1:2, :]


def _merge(o_gdn, o_gla, o_diff, pa, x, mod, wb, wo, ln_g, ln_b, rw_t, rb_col, layer, row0,
           tokens_per_row, tm):
    n = x.shape[0]
    mod_row = lambda i: (layer, row0 + (i * tm) // tokens_per_row, 0, 0)
    full = lambda shape: pl.BlockSpec(shape, lambda i: (0,) * len(shape))
    return pl.pallas_call(
        _merge_kernel,
        grid=(n // tm,),
        in_specs=[
            pl.BlockSpec((tm, MIX_W), lambda i: (i, 0)),
            pl.BlockSpec((tm, MIX_W), lambda i: (i, 0)),
            pl.BlockSpec((tm, MIX_W), lambda i: (i, 0)),
            pl.BlockSpec((tm, D_MODEL), lambda i: (i, A_MERGE // D_MODEL)),
            pl.BlockSpec((tm, D_MODEL), lambda i: (i, A_MERGE // D_MODEL + 1)),
            pl.BlockSpec((tm, D_MODEL), lambda i: (i, A_MERGE // D_MODEL + 2)),
            pl.BlockSpec((tm, D_MODEL), lambda i: (i, 0)),
            pl.BlockSpec((None, None, 1, 6 * D_MODEL), mod_row),
            full((N_BRANCH, MIX_W, D_MODEL)),
            full((D_MODEL, D_MODEL)),
            full((1, D_MODEL)),
            full((1, D_MODEL)),
            full((N_EXPERTS, D_MODEL)),
            full((N_EXPERTS, 1)),
        ],
        out_specs=[
            pl.BlockSpec((tm, D_MODEL), lambda i: (i, 0)),
            pl.BlockSpec((tm, D_MODEL), lambda i: (i, 0)),
            pl.BlockSpec((8, tm), lambda i: (0, i)),
            pl.BlockSpec((tm, LANE), lambda i: (i, 0)),
            pl.BlockSpec((None, 1, LANE), lambda i: (i, 0, 0)),
            pl.BlockSpec((None, 1, LANE), lambda i: (i, 0, 0)),
        ],
        out_shape=[
            jax.ShapeDtypeStruct((n, D_MODEL), F32),
            jax.ShapeDtypeStruct((n, D_MODEL), BF16),
            jax.ShapeDtypeStruct((8, n), F32),
            jax.ShapeDtypeStruct((n, LANE), F32),
            jax.ShapeDtypeStruct((n // tm, 1, LANE), F32),
            jax.ShapeDtypeStruct((n // tm, 1, LANE), F32),
        ],
        compiler_params=_cparams(("arbitrary",)),
    )(o_gdn, o_gla, o_diff, pa, pa, pa, x, mod, wb, wo, ln_g, ln_b, rw_t, rb_col)


def _plan_kernel(plen_ref, loff_ref, goff_ref, npc_ref, loffi_ref, tiles_ref, tail_ref, *, n_steps):
    plen = plen_ref[...]
    nb = plen.shape[0]
    earlier = jnp.where(_tri(nb, "gt"), 1.0, 0.0).astype(BF16)
    run = _dot(earlier, plen.astype(BF16))
    gtot = jnp.sum(plen, axis=0, keepdims=True)
    lane = lax.broadcasted_iota(jnp.int32, (1, LANE), 1).astype(F32)
    ntile = jnp.floor((gtot + (ROW_TILE - 1.0)) * (1.0 / ROW_TILE))
    before = jnp.where(_tri(LANE, "lt"), 1.0, 0.0).astype(BF16)
    first = _dot(jnp.broadcast_to(ntile, (8, LANE)).astype(BF16), before)[0:1, :]
    end = first + ntile
    n_used = jnp.sum(ntile, axis=1, keepdims=True)
    goff_ref[...] = (first * ROW_TILE + run).astype(jnp.int32)
    npc_ref[...] = (plen * (1.0 / PIECE)).astype(jnp.int32)
    loffi_ref[...] = loff_ref[...].astype(jnp.int32)
    end_col = jnp.concatenate([end, jnp.zeros((LANE - 1, LANE), F32)], axis=0).T[:, 0:1]
    e_col = lax.broadcasted_iota(jnp.int32, (LANE, 1), 0).astype(F32)
    tile = jnp.minimum(lane, n_used - 1.0)
    done = jnp.logical_and(end_col <= tile, e_col < N_EXPERTS)
    t_exp = jnp.minimum(jnp.sum(jnp.where(done, 1.0, 0.0), axis=0, keepdims=True), N_EXPERTS - 1.0)
    nonempty = jnp.where(ntile > 0.5, 1.0, 0.0)
    run_idx = _dot(jnp.broadcast_to(nonempty, (8, LANE)).astype(BF16), before)[0:1, :]
    owner_after = jnp.logical_and(end_col <= end, e_col < N_EXPERTS)
    nxt = jnp.minimum(jnp.sum(jnp.where(owner_after, 1.0, 0.0), axis=0, keepdims=True), N_EXPERTS - 1.0)
    has_next = jnp.where(end < n_used, 1.0, 0.0)
    by_expert = jnp.concatenate([run_idx - 2.0 * jnp.floor(run_idx * 0.5), nxt, has_next,
                                 jnp.zeros((LANE - 3, LANE), F32)], axis=0).T
    mine = e_col == t_exp
    per_tile = [jnp.sum(jnp.where(mine, by_expert[:, j:j + 1], 0.0), axis=0, keepdims=True) for j in range(3)]
    tiles_ref[...] = jnp.concatenate(
        [t_exp, jnp.broadcast_to(n_used, (1, LANE))] + per_tile + [jnp.zeros((3, LANE), F32)],
        axis=0).astype(jnp.int32)
    tail_ref[...] = jnp.concatenate(
        [first * ROW_TILE + gtot, (ntile * ROW_TILE - gtot) * (1.0 / PIECE),
         jnp.broadcast_to(n_used, (1, LANE)), jnp.broadcast_to(n_steps - n_used, (1, LANE)),
         jnp.zeros((4, LANE), F32)], axis=0).astype(jnp.int32)


def _plan(plen, loff, n_steps):
    nb = plen.shape[0]
    i32 = lambda rows: jax.ShapeDtypeStruct((rows, LANE), jnp.int32)
    return pl.pallas_call(
        functools.partial(_plan_kernel, n_steps=n_steps),
        out_shape=[i32(nb), i32(nb), i32(nb), i32(8), i32(8)],
    )(plen, loff)


def _piece_copy(local, remote, sem, to_remote, lo, go):
    src = local.at[pl.ds(pl.multiple_of(lo, PIECE), PIECE)]
    dst = remote.at[pl.ds(pl.multiple_of(go, PIECE), PIECE)]
    return pltpu.make_async_copy(src, dst, sem) if to_remote else pltpu.make_async_copy(dst, src, sem)


def _start_pieces(b, loff_s, npc_s, goff_s, local, remote, sem, to_remote):
    for e in range(N_EXPERTS):
        lo = loff_s[b, e]
        go = goff_s[b, e]

        def start(j, carry, lo=lo, go=go):
            _piece_copy(local, remote, sem, to_remote, lo + j * PIECE, go + j * PIECE).start()
            return carry

        lax.fori_loop(0, npc_s[b, e], start, 0)


def _wait_pieces(b, npc_s, local, remote, sem, to_remote):
    total = 0
    for e in range(N_EXPERTS):
        total = total + npc_s[b, e]

    def wait(j, carry):
        _piece_copy(local, remote, sem, to_remote, 0, 0).wait()
        return carry

    lax.fori_loop(0, total, wait, 0)


def _dispatch_kernel(loff_s, npc_s, goff_s, tail_s, h_ref, mrow_ref, mcol_ref, hs_hbm, local, zeros, sems,
                     tail_sem, tile_sem, *, slots):
    b = pl.program_id(0)
    last = pl.num_programs(0) - 1
    slot = b % 2
    buf = local.at[slot]
    sem = sems.at[slot]

    @pl.when(b >= 2)
    def _():
        _wait_pieces(b - 2, npc_s, buf, hs_hbm, sem, True)

    tb = h_ref.shape[0]
    mrow = mrow_ref[...]
    mcol = mcol_ref[...]
    slot_id = lax.broadcasted_iota(jnp.int32, (slots, tb), 0).astype(F32)
    s_a = jnp.where(slot_id == mrow[0:1, :], 1.0, 0.0).astype(BF16)
    s_b = jnp.where(slot_id == mrow[1:2, :], 1.0, 0.0).astype(BF16)
    lane = lax.broadcasted_iota(jnp.int32, (tb, LANE), 1)

    def weight_lanes(w):
        hi = w.astype(BF16).astype(F32)
        return jnp.where(lane == 0, hi, jnp.where(lane == 1, w - hi, 0.0)).astype(BF16)

    local[slot, :, :D_MODEL] = _dot(s_a + s_b, h_ref[...]).astype(BF16)
    local[slot, :, D_MODEL:] = (_dot(s_a, weight_lanes(mcol[:, 2:3]))
                                + _dot(s_b, weight_lanes(mcol[:, 3:4]))).astype(BF16)
    _start_pieces(b, loff_s, npc_s, goff_s, buf, hs_hbm, sem, True)

    @pl.when(b == last)
    def _():
        _wait_pieces(b, npc_s, buf, hs_hbm, sem, True)

        @pl.when(b >= 1)
        def _():
            _wait_pieces(b - 1, npc_s, local.at[1 - slot], hs_hbm, sems.at[1 - slot], True)

        zeros[...] = jnp.zeros_like(zeros)
        zero_piece = zeros.at[pl.ds(0, PIECE)]
        total = 0
        for e in range(N_EXPERTS):
            n = tail_s[1, e]
            start_row = tail_s[0, e]

            def start(j, carry, start_row=start_row):
                dst = hs_hbm.at[pl.ds(pl.multiple_of(start_row + j * PIECE, PIECE), PIECE)]
                pltpu.make_async_copy(zero_piece, dst, tail_sem).start()
                return carry

            lax.fori_loop(0, n, start, 0)
            total = total + n

        def wait(j, carry):
            pltpu.make_async_copy(zero_piece, hs_hbm.at[pl.ds(0, PIECE)], tail_sem).wait()
            return carry

        lax.fori_loop(0, total, wait, 0)

        def tile_copy(j):
            row = pl.multiple_of((tail_s[2, 0] + j) * ROW_TILE, ROW_TILE)
            return pltpu.make_async_copy(zeros, hs_hbm.at[pl.ds(row, ROW_TILE)], tile_sem)

        def start_tile(j, carry):
            tile_copy(j).start()
            return carry

        def wait_tile(j, carry):
            tile_copy(j).wait()
            return carry

        lax.fori_loop(0, tail_s[3, 0], start_tile, 0)
        lax.fori_loop(0, tail_s[3, 0], wait_tile, 0)


def _dispatch(loff_i, npc, goff, tail, h, mrow, mcol, tb, n_steps):
    n = h.shape[0]
    slots = 2 * tb + SLOT_PAD
    grid_spec = pltpu.PrefetchScalarGridSpec(
        num_scalar_prefetch=4,
        grid=(n // tb,),
        in_specs=[
            pl.BlockSpec((tb, D_MODEL), lambda i, *_: (i, 0)),
            pl.BlockSpec((8, tb), lambda i, *_: (0, i)),
            pl.BlockSpec((tb, LANE), lambda i, *_: (i, 0)),
        ],
        out_specs=pl.BlockSpec(memory_space=pl.ANY),
        scratch_shapes=[pltpu.VMEM((2, slots, HS_COLS), BF16), pltpu.VMEM((ROW_TILE, HS_COLS), BF16),
                        pltpu.SemaphoreType.DMA((2,)), pltpu.SemaphoreType.DMA, pltpu.SemaphoreType.DMA],
    )
    return pl.pallas_call(
        functools.partial(_dispatch_kernel, slots=slots),
        grid_spec=grid_spec,
        out_shape=jax.ShapeDtypeStruct((n_steps * ROW_TILE, HS_COLS), BF16),
        compiler_params=_cparams(("arbitrary",)),
    )(loff_i, npc, goff, tail, h, mrow, mcol)


def _expert_kernel(tiles_s, hs_ref, wg_hbm, wu_hbm, wd_hbm, y_ref, wg_f, wu_f, wd_f, wg_b, wu_b, wd_b, sems,
                   *, layer):
    i = pl.program_id(0)
    e = tiles_s[0, i]
    slot = tiles_s[2, i]
    first_of_run = jnp.logical_or(i == 0, e != tiles_s[0, jnp.maximum(i - 1, 0)])

    def weight_copies(expert, s):
        return [pltpu.make_async_copy(src.at[layer, expert], dst.at[s], sems.at[s, j])
                for j, (src, dst) in enumerate(((wg_hbm, wg_f), (wu_hbm, wu_f), (wd_hbm, wd_f)))]

    @pl.when(i == 0)
    def _():
        for cp in weight_copies(e, slot):
            cp.start()

    @pl.when(jnp.logical_and(first_of_run, i < tiles_s[1, 0]))
    def _():
        for cp in weight_copies(e, slot):
            cp.wait()

        @pl.when(tiles_s[4, i] > 0)
        def _():
            for cp in weight_copies(tiles_s[3, i], 1 - slot):
                cp.start()

        wg_b[...] = wg_f[slot].astype(BF16)
        wu_b[...] = wu_f[slot].astype(BF16)
        wd_b[...] = wd_f[slot].astype(BF16)

    @pl.when(i < tiles_s[1, 0])
    def _():
        halves = [pl.ds(r * (ROW_TILE // 2), ROW_TILE // 2) for r in range(2)]
        xs = [hs_ref[rows, :D_MODEL] for rows in halves]
        ws = [hs_ref[rows, D_MODEL:D_MODEL + 1].astype(F32) + hs_ref[rows, D_MODEL + 1:D_MODEL + 2].astype(F32)
              for rows in halves]
        gs = [_dot(x, wg_b[...]) for x in xs]
        us = [_dot(x, wu_b[...]) for x in xs]
        acts = [(_silu(g) * u * w).astype(BF16) for g, u, w in zip(gs, us, ws)]
        for rows, act in zip(halves, acts):
            y_ref[rows, :] = _dot(act, wd_b[...]).astype(BF16)

    @pl.when(i >= tiles_s[1, 0])
    def _():
        y_ref[...] = jnp.zeros_like(y_ref)


def _experts(tiles, hs, wg, wu, wd, layer, n_steps):
    grid_spec = pltpu.PrefetchScalarGridSpec(
        num_scalar_prefetch=1,
        grid=(n_steps,),
        in_specs=[
            pl.BlockSpec((ROW_TILE, HS_COLS), lambda i, t: (i, 0)),
            pl.BlockSpec(memory_space=pl.ANY),
            pl.BlockSpec(memory_space=pl.ANY),
            pl.BlockSpec(memory_space=pl.ANY),
        ],
        out_specs=pl.BlockSpec((ROW_TILE, D_MODEL), lambda i, t: (i, 0)),
        scratch_shapes=[pltpu.VMEM((2, D_MODEL, D_FF), F32), pltpu.VMEM((2, D_MODEL, D_FF), F32),
                        pltpu.VMEM((2, D_FF, D_MODEL), F32),
                        pltpu.VMEM((D_MODEL, D_FF), BF16), pltpu.VMEM((D_MODEL, D_FF), BF16),
                        pltpu.VMEM((D_FF, D_MODEL), BF16), pltpu.SemaphoreType.DMA((2, 3))],
    )
    return pl.pallas_call(
        functools.partial(_expert_kernel, layer=layer),
        grid_spec=grid_spec,
        out_shape=jax.ShapeDtypeStruct((hs.shape[0], D_MODEL), BF16),
        compiler_params=_cparams(("arbitrary",)),
    )(tiles, hs, wg, wu, wd)


def _combine_kernel(loff_s, npc_s, goff_s, y_hbm, mcol_ref, x1_ref, mod_ref, lng_ref, lnb_ref, o_ref,
                    local, sems, *, slots):
    b = pl.program_id(0)
    tb = x1_ref.shape[0]
    slot = b % 2

    def fetch(blk, s):
        local[s] = jnp.zeros(local.shape[1:], local.dtype)
        _start_pieces(blk, loff_s, npc_s, goff_s, local.at[s], y_hbm, sems.at[s], False)

    @pl.when(b == 0)
    def _():
        fetch(0, 0)

    _wait_pieces(b, npc_s, local.at[slot], y_hbm, sems.at[slot], False)

    @pl.when(b + 1 < pl.num_programs(0))
    def _():
        fetch(b + 1, 1 - slot)

    mcol = mcol_ref[...]
    slot_id = lax.broadcasted_iota(jnp.int32, (tb, slots), 1).astype(F32)
    pick = jnp.logical_or(slot_id == mcol[:, 0:1], slot_id == mcol[:, 1:2])
    f = _dot(jnp.where(pick, 1.0, 0.0).astype(BF16), local[slot])
    gate2 = mod_ref[:, 5 * D_MODEL:6 * D_MODEL]
    o_ref[...] = _layernorm(ALPHA * x1_ref[...] + gate2 * f, lng_ref[...], lnb_ref[...])


def _combine(loff_i, npc, goff, y, mcol, x1, mod, ln_g, ln_b, layer, row0, tokens_per_row, tb):
    n = x1.shape[0]
    slots = 2 * tb + SLOT_PAD
    mod_row = lambda i, *_: (layer, row0 + (i * tb) // tokens_per_row, 0, 0)
    grid_spec = pltpu.PrefetchScalarGridSpec(
        num_scalar_prefetch=3,
        grid=(n // tb,),
        in_specs=[
            pl.BlockSpec(memory_space=pl.ANY),
            pl.BlockSpec((tb, LANE), lambda i, *_: (i, 0)),
            pl.BlockSpec((tb, D_MODEL), lambda i, *_: (i, 0)),
            pl.BlockSpec((None, None, 1, 6 * D_MODEL), mod_row),
            pl.BlockSpec((1, D_MODEL), lambda i, *_: (0, 0)),
            pl.BlockSpec((1, D_MODEL), lambda i, *_: (0, 0)),
        ],
        out_specs=pl.BlockSpec((tb, D_MODEL), lambda i, *_: (i, 0)),
        scratch_shapes=[pltpu.VMEM((2, slots, D_MODEL), BF16), pltpu.SemaphoreType.DMA((2,))],
    )
    return pl.pallas_call(
        functools.partial(_combine_kernel, slots=slots),
        grid_spec=grid_spec,
        out_shape=jax.ShapeDtypeStruct((n, D_MODEL), F32),
        compiler_params=_cparams(("arbitrary",)),
    )(loff_i, npc, goff, y, mcol, x1, mod, ln_g, ln_b)


def _moe(h, mrow, mcol, plen, loff, wg, wu, wd, x1, mod, ln_g, ln_b, layer, row0, tokens_per_row, tb):
    n = h.shape[0]
    nb = n // tb
    n_steps = (2 * n + nb * N_EXPERTS * (PIECE - 1)) // ROW_TILE + N_EXPERTS
    assert n_steps <= LANE
    goff, npc, loff_i, tiles, tail = _plan(plen.reshape(nb, LANE), loff.reshape(nb, LANE), n_steps)
    hs = _dispatch(loff_i, npc, goff, tail, h, mrow, mcol, tb, n_steps)
    y = _experts(tiles, hs, wg, wu, wd, layer, n_steps)
    return _combine(loff_i, npc, goff, y, mcol, x1, mod, ln_g, ln_b, layer, row0, tokens_per_row, tb)


def _reorder_w_in(w):
    sizes = (3 * GDN_W, GDN_W, 2 * GDN_HEADS, 2 * GDN_HEADS, GLA_KW, GLA_KW, GLA_VW, GLA_VW,
             2 * GLA_RANK, DIFF_QW, DIFF_QW, DIFF_VW, N_BRANCH * D_MODEL)
    offs = np.concatenate([[0], np.cumsum(sizes)])
    seg = lambda i: w[:, offs[i]:offs[i + 1]]
    used = 2 * GDN_HEADS * 2 + 2 * GLA_RANK
    wa = jnp.concatenate([seg(0), seg(1), seg(4), seg(5), seg(6), seg(7), seg(9), seg(12)], axis=1)
    wf = jnp.concatenate([seg(10), seg(11), seg(2), seg(3), seg(8),
                          jnp.zeros((w.shape[0], LANE - used), w.dtype)], axis=1)
    return wa.astype(BF16), wf.astype(BF16)


def _lane_row(vals, offset):
    return jnp.zeros((1, LANE), F32).at[0, offset:offset + vals.shape[0]].set(vals)


def kernel(x_prompt, x_sample, c, state_gdn, state_gla, cache_k, cache_v, c_ctx, w_mod, b_mod, w_in,
           gdn_conv, gdn_a_log, gdn_dt_bias, gdn_norm, gla_w_gate, gla_b_gate, gla_norm, diff_lambda,
           diff_norm, w_branch, w_out, ln_g, ln_b, router_w, router_b, exp_w_gate, exp_w_up, exp_w_down):
    bp, tp, d = x_prompt.shape
    bs, ts, _ = x_sample.shape
    pad_rows = MOD_ROWS - 1 - bs
    cvecs = jnp.concatenate([c_ctx[None, :], c, jnp.zeros((pad_rows, d), F32)], axis=0)
    mod = _modulation(cvecs, w_mod, b_mod).reshape(DEPTH, MOD_ROWS, 1, 6 * d)

    rw_t = router_w.T
    rb_col = router_b.reshape(N_EXPERTS, 1)
    ck = cache_k.reshape(bs, DEPTH, cache_k.shape[2], DIFF_QW)
    cv = cache_v.reshape(bs, DEPTH, cache_v.shape[2], DIFF_VW)

    layer_w = []
    for l in range(DEPTH):
        wa, wf = _reorder_w_in(w_in[l])
        layer_w.append(dict(
            wa=wa, wf=wf,
            alog=_lane_row(gdn_a_log[l].reshape(-1), MISC_A),
            dtb=_lane_row(gdn_dt_bias[l].reshape(-1), MISC_A),
            wb=w_branch[l].astype(BF16),
            wo=w_out[l].astype(BF16),
        ))

    def layer(x, l, row0, tokens_per_row, s_gdn, s_gla, ctx_k, ctx_v, emit_state, tm, prev_gdn=None,
              prev_gla=None):
        b, t, _ = x.shape
        n = b * t
        lw = layer_w[l]
        xf = x.reshape(n, d)
        pa, pf, *cache = _inproj(xf, mod, lw["wa"], lw["wf"], l, row0, tokens_per_row, tm, emit_state)
        pa3 = pa.reshape(b, t, A_COLS)
        pf3 = pf.reshape(b, t, F_COLS)
        o_gdn, gdn_fin = _gdn(pa3, pf3, gdn_conv[l], lw["alog"], lw["dtb"], gdn_norm[l][None, :], s_gdn, l,
                              emit_state, prev_gdn)
        o_gla, gla_fin = _gla(pa3, pf3, gla_w_gate[l], gla_b_gate[l], gla_norm[l][None, :], s_gla, l,
                              emit_state, prev_gla)
        lam_init = 0.8 - 0.6 * math.exp(-0.3 * l)
        o_diff = _diff(pa3, pf3, diff_lambda[l], diff_norm[l][None, :], ctx_k, ctx_v, l, lam_init)
        x1, h, mrow, mcol, plen, loff = _merge(
            o_gdn.reshape(n, MIX_W), o_gla.reshape(n, MIX_W), o_diff.reshape(n, MIX_W), pa, xf, mod,
            lw["wb"], lw["wo"], ln_g[l, 0][None, :], ln_b[l, 0][None, :], rw_t, rb_col, l, row0,
            tokens_per_row, tm)
        x2 = _moe(h, mrow, mcol, plen, loff, exp_w_gate, exp_w_up, exp_w_down, x1, mod,
                  ln_g[l, 1][None, :], ln_b[l, 1][None, :], l, row0, tokens_per_row, tm)
        k_own, v_own = [a.reshape(b, t, DIFF_HEADS, LANE) for a in cache] if emit_state else (None, None)
        return x2.reshape(b, t, d), gdn_fin, gla_fin, k_own, v_own

    hp = x_prompt
    gdn_states, gla_states, ks, vs = None, None, [], []
    for l in range(DEPTH):
        hp, gdn_states, gla_states, k_l, v_l = layer(hp, l, 0, bp * tp, None, None, None, None, True, 512,
                                                     gdn_states, gla_states)
        ks.append(k_l)
        vs.append(v_l)

    hs = x_sample
    for l in range(DEPTH):
        hs = layer(hs, l, 1, ts, state_gdn, state_gla, ck, cv, False, 512)[0]

    return hp, hs, gdn_states, gla_states, jnp.stack(ks, axis=1), jnp.stack(vs, axis=1)
```

```python
import functools
import math

import numpy as np
import jax
import jax.numpy as jnp
from jax import lax
from jax.experimental import pallas as pl
from jax.experimental.pallas import tpu as pltpu

F32 = jnp.float32
BF16 = jnp.bfloat16

D_MODEL = 1024
DEPTH = 2
GRID_W = 64
GDN_HEADS = 4
GDN_DK = 128
GDN_DV = 128
GDN_W = GDN_HEADS * GDN_DV
SHORT_CONV = 5
GLA_HEADS = 4
GLA_DK = 64
GLA_DV = 128
GLA_KW = GLA_HEADS * GLA_DK
GLA_VW = GLA_HEADS * GLA_DV
GLA_RANK = 16
GLA_TAU = 16.0
CHUNK = 64
DIFF_HEADS = 4
DIFF_DH = 64
DIFF_VD = 2 * DIFF_DH
DIFF_QW = DIFF_HEADS * 2 * DIFF_DH
DIFF_VW = DIFF_HEADS * DIFF_VD
ROPE_THETA = 10000.0
MIX_W = 512
N_BRANCH = 3
N_EXPERTS = 16
N_GROUPS = 4
GROUP_SIZE = N_EXPERTS // N_GROUPS
D_FF = 512
ALPHA = (2 * DEPTH) ** 0.25
EPS = 1e-6

LANE = 128
MOD_ROWS = 8
VMEM_LIMIT = 56 * 1024 * 1024

A_GDN_QKV = 0
A_GDN_Z = 1536
A_GLA_Q = 2048
A_GLA_K = 2304
A_GLA_V = 2560
A_GLA_R = 3072
A_DIFF_Q = 3584
A_MERGE = 4096
A_COLS = 7168
F_DIFF_K = 0
F_DIFF_V = 512
F_MISC = 1024
F_COLS = 1152
MISC_B = 0
MISC_A = 8
MISC_LR = 16


def _cparams(sem):
    return pltpu.CompilerParams(dimension_semantics=sem, vmem_limit_bytes=VMEM_LIMIT)


def _dot(a, b):
    return jnp.dot(a, b, preferred_element_type=F32)


def _mm(a, b):
    return _dot(a.astype(BF16), b.astype(BF16))


def _mm_nt(a, b):
    return lax.dot_general(a.astype(BF16), b.astype(BF16), (((1,), (1,)), ((), ())),
                           preferred_element_type=F32)


def _split3(x):
    hi = x.astype(BF16)
    r = x - hi.astype(F32)
    mid = r.astype(BF16)
    lo = (r - mid.astype(F32)).astype(BF16)
    return hi, mid, lo


def _mm_mask_lhs(mask_bf16, x):
    hi, mid, lo = _split3(x)
    return _dot(mask_bf16, hi) + _dot(mask_bf16, mid) + _dot(mask_bf16, lo)


def _mm_mask_rhs(x, mask_bf16):
    hi, mid, lo = _split3(x)
    return _dot(hi, mask_bf16) + _dot(mid, mask_bf16) + _dot(lo, mask_bf16)


def _mm_hi(a, b):
    ah = a.astype(BF16)
    al = (a - ah.astype(F32)).astype(BF16)
    bh = b.astype(BF16)
    bl = (b - bh.astype(F32)).astype(BF16)
    return _dot(ah, bh) + _dot(ah, bl) + _dot(al, bh)


def _mm_hi_nt(a, b):
    ah = a.astype(BF16)
    al = (a - ah.astype(F32)).astype(BF16)
    bh = b.astype(BF16)
    bl = (b - bh.astype(F32)).astype(BF16)
    dot = lambda x, y: lax.dot_general(x, y, (((1,), (1,)), ((), ())), preferred_element_type=F32)
    return dot(ah, bh) + dot(ah, bl) + dot(al, bh)


def _sigmoid(x):
    return 1.0 / (1.0 + jnp.exp(-x))


def _silu(x):
    return x * _sigmoid(x)


def _softplus(x):
    return jnp.maximum(x, 0.0) + jnp.log1p(jnp.exp(-jnp.abs(x)))


def _rms(x, g):
    return x * lax.rsqrt(jnp.mean(x * x, axis=-1, keepdims=True) + EPS) * g


def _layernorm(y, g, b):
    mu = jnp.mean(y, axis=-1, keepdims=True)
    d = y - mu
    var = jnp.mean(d * d, axis=-1, keepdims=True)
    return d * lax.rsqrt(var + EPS) * g + b


def _pick_lane(x, idx):
    lane = lax.broadcasted_iota(jnp.int32, x.shape, 1)
    return jnp.sum(jnp.where(lane == idx, x, 0.0), axis=1, keepdims=True)


def _tri(n, kind):
    r = lax.broadcasted_iota(jnp.int32, (n, n), 0)
    c = lax.broadcasted_iota(jnp.int32, (n, n), 1)
    return {"ge": r >= c, "gt": r > c, "le": r <= c, "lt": r < c}[kind]


def _mod_kernel(c_ref, w_ref, b_ref, o_ref):
    s = _silu(c_ref[...])
    o_ref[...] = _mm_hi(s, w_ref[...]) + b_ref[...]


def _modulation(cvecs, w_mod, b_mod):
    tn = 1536
    return pl.pallas_call(
        _mod_kernel,
        grid=(DEPTH, 6 * D_MODEL // tn),
        in_specs=[
            pl.BlockSpec((MOD_ROWS, D_MODEL), lambda l, j: (0, 0)),
            pl.BlockSpec((None, D_MODEL, tn), lambda l, j: (l, 0, j)),
            pl.BlockSpec((None, 1, tn), lambda l, j: (l, 0, j)),
        ],
        out_specs=pl.BlockSpec((None, MOD_ROWS, tn), lambda l, j: (l, 0, j)),
        out_shape=jax.ShapeDtypeStruct((DEPTH, MOD_ROWS, 6 * D_MODEL), F32),
        compiler_params=_cparams(("arbitrary", "arbitrary")),
    )(cvecs, w_mod, b_mod.reshape(DEPTH, 1, 6 * D_MODEL))


INPROJ_COL_STEP = 1792


def _inproj_kernel(*refs, cache_mode):
    x_ref, mod_ref, wa_ref, wf_ref = refs[:4]
    outs = refs[6:] if cache_mode == "later" else refs[4:]
    pa_ref, pf_ref = outs[:2]
    shift = mod_ref[:, 0:D_MODEL]
    scale = mod_ref[:, D_MODEL:2 * D_MODEL]
    u = (x_ref[...] * (1.0 + scale) + shift).astype(BF16)
    pf = _dot(u, wf_ref[...])
    pf_ref[...] = pf
    if cache_mode:
        for o_ref, base in zip(outs[2:4], (F_DIFF_K, F_DIFF_V)):
            if cache_mode == "first":
                for l in range(1, DEPTH):
                    o_ref[l] = jnp.zeros(o_ref.shape[1:], F32)
                o_ref = o_ref.at[0]
            for h in range(DIFF_HEADS):
                o_ref[:, h, :] = pf[:, base + h * LANE:base + (h + 1) * LANE]
    for c0 in range(0, A_COLS, INPROJ_COL_STEP):
        cols = slice(c0, c0 + INPROJ_COL_STEP)
        pa_ref[:, cols] = _dot(u, wa_ref[:, cols]).astype(BF16)


def _inproj(x, mod, wa, wf, layer, row0, tokens_per_row, tm, cache_shape, prev_cache):
    n = x.shape[0]
    mod_row = lambda i: (layer, row0 + (i * tm) // tokens_per_row, 0, 0)
    out_specs = [
        pl.BlockSpec((tm, A_COLS), lambda i: (i, 0)),
        pl.BlockSpec((tm, F_COLS), lambda i: (i, 0)),
    ]
    out_shape = [
        jax.ShapeDtypeStruct((n, A_COLS), BF16),
        jax.ShapeDtypeStruct((n, F_COLS), F32),
    ]
    resident = lambda shape: pl.BlockSpec(shape, lambda i: (0, 0), pipeline_mode=pl.Buffered(1))
    in_specs = [
        pl.BlockSpec((tm, D_MODEL), lambda i: (i, 0)),
        pl.BlockSpec((None, None, 1, 6 * D_MODEL), mod_row),
        resident((D_MODEL, A_COLS)),
        resident((D_MODEL, F_COLS)),
    ]
    args = [x, mod, wa, wf]
    cache_mode, aliases = None, {}
    if cache_shape is not None:
        b, t = cache_shape
        assert tm == t
        out_shape += [jax.ShapeDtypeStruct((b, DEPTH, t, DIFF_HEADS, LANE), F32)] * 2
        if prev_cache is None:
            cache_mode = "first"
            out_specs += [pl.BlockSpec((None, DEPTH, t, DIFF_HEADS, LANE), lambda i: (i, 0, 0, 0, 0))] * 2
        else:
            cache_mode = "later"
            out_specs += [pl.BlockSpec((None, None, t, DIFF_HEADS, LANE), lambda i: (i, layer, 0, 0, 0))] * 2
            in_specs += [pl.BlockSpec(memory_space=pl.ANY)] * 2
            args += list(prev_cache)
            aliases = {4: 2, 5: 3}
    return pl.pallas_call(
        functools.partial(_inproj_kernel, cache_mode=cache_mode),
        grid=(n // tm,),
        in_specs=in_specs,
        out_specs=out_specs,
        out_shape=out_shape,
        input_output_aliases=aliases,
        compiler_params=_cparams(("arbitrary",)),
    )(*args)


def _conv_masks(shape, t_len):
    row = lax.broadcasted_iota(jnp.int32, shape, 0)
    half = SHORT_CONV // 2
    return {d: jnp.logical_and(row + d >= 0, row + d < t_len) for d in range(-half, half + 1) if d}


def _short_conv(x, w, t_len, masks):
    half = SHORT_CONV // 2
    acc = x * w[half:half + 1, :]
    for d, valid in masks.items():
        shifted = pltpu.roll(x, (-d) % t_len, axis=0)
        acc = acc + jnp.where(valid, shifted, 0.0) * w[half + d:half + d + 1, :]
    return acc


GDN_PAIRS = 2


def _solve_unit_tri(a_list, rhs_list):
    n = a_list[0].shape[0]
    assert n == 64
    eye = jnp.where(_tri(n, "ge") & _tri(n, "le"), 1.0, 0.0)
    a_b = [a.astype(BF16) for a in a_list]
    bf = lambda xs: [x.astype(BF16) for x in xs]
    mul = lambda xs, ys: [_dot(x, y) for x, y in zip(xs, ys)]
    pair = lambda xs, ys, xb, yb: [x + y + xy for x, y, xy in zip(xs, ys, mul(xb, yb))]
    p0 = [-a for a in a_list]
    p0b = [-a for a in a_b]
    p1 = mul(p0b, p0b)
    p1b = bf(p1)
    p2 = mul(p1b, p1b)
    n01 = pair(p0, p1, p0b, p1b)
    p2b = bf(p2)
    p3 = mul(p2b, p2b)
    p3b = bf(p3)
    p4 = mul(p3b, p3b)
    n23 = pair(p2, p3, p2b, p3b)
    p4b = bf(p4)
    p5 = mul(p4b, p4b)
    n03 = pair(n01, n23, bf(n01), bf(n23))
    p5b = bf(p5)
    n45 = pair(p4, p5, p4b, p5b)
    inv_b = [(eye + x).astype(BF16) for x in pair(n03, n45, bf(n03), bf(n45))]
    sol = [_dot(x, r.astype(BF16)) for x, r in zip(inv_b, rhs_list)]
    resid = [r - s - _dot(a, s.astype(BF16)) for r, s, a in zip(rhs_list, sol, a_b)]
    return [s + _dot(x, r.astype(BF16)) for s, x, r in zip(sol, inv_b, resid)]


def _gdn_kernel(*refs, t_len, has_state, emit_state):
    it = iter(refs)
    qkv_ref, z_ref, misc_ref, cw_ref, alog_ref, dtb_ref, g_ref = (next(it) for _ in range(7))
    s0_ref = next(it) if has_state else None
    if emit_state == "later":
        next(it)
    o_ref = next(it)
    sfin_ref = next(it) if emit_state else None
    (q_s, k_s, v_s, kt_s, bet, gat, gat_t, cumc, cumr, u_s, w_s, qd_s, a_s, ktl_s, gts, o_f, o_b,
     s_s) = (next(it) for _ in range(18))

    c = CHUNK
    n_chunks = t_len // c
    nh = GDN_HEADS
    dk = GDN_DK

    masks = _conv_masks((t_len, dk), t_len)
    for h in range(nh):
        hs = slice(h * dk, (h + 1) * dk)
        q, k, v = (
            _silu(_short_conv(qkv_ref[:, j * GDN_W + h * dk:j * GDN_W + (h + 1) * dk].astype(F32),
                              cw_ref[:, j * GDN_W + h * dk:j * GDN_W + (h + 1) * dk], t_len, masks))
            for j in range(3))
        q = q * lax.rsqrt(jnp.sum(q * q, axis=-1, keepdims=True) + EPS) * (dk ** -0.5)
        k = k * lax.rsqrt(jnp.sum(k * k, axis=-1, keepdims=True) + EPS)
        q_s[:, hs] = q
        k_s[:, hs] = k
        v_s[:, hs] = v
        kt_s[hs, :] = k.T.astype(BF16)
    misc = misc_ref[...]
    bet[...] = _sigmoid(misc)
    g_all = -jnp.exp(alog_ref[...]) * _softplus(misc + dtb_ref[...])
    gat[...] = g_all
    gat_t[...] = g_all.T[MISC_A:MISC_A + 2 * nh, :]
    for d in range(2):
        for h in range(nh):
            s_s[d * nh + h] = s0_ref[d, h] if has_state else jnp.zeros((dk, GDN_DV), F32)

    two = 2 * c
    r2 = lax.broadcasted_iota(jnp.int32, (two, two), 0)
    c2 = lax.broadcasted_iota(jnp.int32, (two, two), 1)
    same = (r2 >= c) == (c2 >= c)
    lo2 = jnp.where(jnp.logical_and(same, r2 >= c2), 1.0, 0.0).astype(BF16)
    up2 = jnp.where(jnp.logical_and(same, r2 <= c2), 1.0, 0.0).astype(BF16)
    incl = (_tri(c, "ge"), _tri(c, "le"))
    strict = (_tri(c, "gt"), _tri(c, "lt"))
    last = (c - 1, 0)

    def cumulate(cp, carry):
        r0 = pl.multiple_of(cp * two, two)
        g_blk = gat[pl.ds(r0, two), :]
        gt_blk = gat_t[:, pl.ds(r0, two)]
        cumc[0, pl.ds(r0, two), :] = _mm_mask_lhs(lo2, g_blk)
        cumc[1, pl.ds(r0, two), :] = _mm_mask_lhs(up2, g_blk)
        cumr[0, :, pl.ds(r0, two)] = _mm_mask_rhs(gt_blk, up2)
        cumr[1, :, pl.ds(r0, two)] = _mm_mask_rhs(gt_blk, lo2)
        return carry

    lax.fori_loop(0, n_chunks // 2, cumulate, 0)

    def phase1(step, carry):
        chains, a_list, rhs_list = [], [], []
        for pp in range(GDN_PAIRS):
            gather_chains(step * GDN_PAIRS + pp, chains, a_list, rhs_list)
        for (idx, rows), sol in zip(chains, _solve_unit_tri(a_list, rhs_list)):
            u_s[idx, rows, :] = sol[:, :GDN_DV]
            w_s[idx, rows, :] = sol[:, GDN_DV:].astype(BF16)
        return carry

    def gather_chains(cp, chains, a_list, rhs_list):
        r0 = pl.multiple_of(cp * two, two)
        b_blk = bet[pl.ds(r0, two), :]
        cum_col = (cumc[0, pl.ds(r0, two), :], cumc[1, pl.ds(r0, two), :])
        cum_row = (cumr[0, :, pl.ds(r0, two)], cumr[1, :, pl.ds(r0, two)])
        for h in range(nh):
            hs = slice(h * dk, (h + 1) * dk)
            kt_pair = kt_s[hs, pl.ds(r0, two)]
            for s in range(2):
                rows = pl.ds(pl.multiple_of(r0 + s * c, c), c)
                q_c = q_s[rows, hs]
                k_c = k_s[rows, hs]
                v_c = v_s[rows, hs]
                kt_c = kt_pair[:, s * c:(s + 1) * c]
                kk = _dot(k_c.astype(BF16), kt_c)
                qk = _dot(q_c.astype(BF16), kt_c)
                for d in range(2):
                    idx = d * nh + h
                    gc = cum_col[d][s * c:(s + 1) * c, MISC_A + idx:MISC_A + idx + 1]
                    gcr = cum_row[d][idx:idx + 1, s * c:(s + 1) * c]
                    beta = b_blk[s * c:(s + 1) * c, MISC_B + idx:MISC_B + idx + 1]
                    decay = jnp.where(incl[d], jnp.exp(jnp.where(incl[d], gc - gcr, 0.0)), 0.0)
                    e_gc = jnp.exp(gc)
                    a_list.append(jnp.where(strict[d], beta * kk * decay, 0.0))
                    rhs_list.append(jnp.concatenate([v_c * beta, k_c * (beta * e_gc)], axis=1))
                    a_s[idx, rows, :] = jnp.where(incl[d], qk * decay, 0.0).astype(BF16)
                    qd_s[idx, rows, :] = (q_c * e_gc).astype(BF16)
                    g_last = gcr[:, last[d]:last[d] + 1]
                    ktl_s[idx, cp * 2 + s] = (kt_c.astype(F32) * jnp.exp(g_last - gcr)).astype(BF16)
                    gts[cp * 2 + s, idx:idx + 1, :] = jnp.broadcast_to(jnp.exp(g_last), (1, LANE))
                    chains.append((idx, rows))

    lax.fori_loop(0, n_chunks // (2 * GDN_PAIRS), phase1, 0)

    def phase2(i, carry):
        chains = []
        for d, out in ((0, o_f), (1, o_b)):
            ci = i if d == 0 else n_chunks - 1 - i
            rows = pl.ds(pl.multiple_of(ci * c, c), c)
            g_tail = gts[ci]
            for h in range(nh):
                chains.append((d * nh + h, h, ci, rows, out, g_tail))
        s_f32 = [s_s[idx] for idx, *_ in chains]
        s_b = [s.astype(BF16) for s in s_f32]
        v_b = [(u_s[idx, rows, :] - _dot(w_s[idx, rows, :], sb)).astype(BF16)
               for (idx, _, _, rows, _, _), sb in zip(chains, s_b)]
        for (idx, h, ci, rows, out, g_tail), s, sb, vb in zip(chains, s_f32, s_b, v_b):
            s_s[idx] = s * g_tail[idx:idx + 1, :] + _dot(ktl_s[idx, ci], vb)
            out[rows, h * GDN_DV:(h + 1) * GDN_DV] = _dot(qd_s[idx, rows, :], sb) + _dot(
                a_s[idx, rows, :], vb)
        return carry

    lax.fori_loop(0, n_chunks, phase2, 0)

    for h in range(nh):
        hs = slice(h * GDN_DV, (h + 1) * GDN_DV)
        o = o_f[:, hs] + o_b[:, hs]
        o_ref[:, hs] = (_rms(o, g_ref[...]) * _silu(z_ref[:, hs].astype(F32))).astype(BF16)
    if emit_state:
        for d in range(2):
            for h in range(nh):
                _store_state(sfin_ref, emit_state, d, h, s_s[d * nh + h])


def _store_state(sfin_ref, mode, d, h, s):
    if mode == "first":
        sfin_ref[0, d, h] = s
        for l in range(1, DEPTH):
            sfin_ref[l, d, h] = jnp.zeros_like(s)
    else:
        sfin_ref[d, h] = s


def _state_output(b, layer, heads, dk, dv, prev, in_specs, args):
    shape = jax.ShapeDtypeStruct((b, DEPTH, 2, heads, dk, dv), F32)
    if prev is None:
        return "first", pl.BlockSpec((None, DEPTH, 2, heads, dk, dv), lambda bi: (bi, 0, 0, 0, 0, 0)), shape, {}
    in_specs.append(pl.BlockSpec(memory_space=pl.ANY))
    args.append(prev)
    spec = pl.BlockSpec((None, None, 2, heads, dk, dv), lambda bi: (bi, layer, 0, 0, 0, 0))
    return "later", spec, shape, {len(args) - 1: 1}


def _gdn(pa3, pf3, conv_w, alog_row, dtb_row, norm_g, s0, layer, emit_state, prev_states):
    b, t, _ = pa3.shape
    full = lambda shape: pl.BlockSpec(shape, lambda bi: (0,) * len(shape))
    in_specs = [
        pl.BlockSpec((None, t, 3 * GDN_W), lambda bi: (bi, 0, A_GDN_QKV // (3 * GDN_W))),
        pl.BlockSpec((None, t, GDN_W), lambda bi: (bi, 0, A_GDN_Z // GDN_W)),
        pl.BlockSpec((None, t, LANE), lambda bi: (bi, 0, F_MISC // LANE)),
        full((SHORT_CONV, 3 * GDN_W)),
        full((1, LANE)),
        full((1, LANE)),
        full((1, GDN_DV)),
    ]
    args = [pa3, pa3, pf3, conv_w, alog_row, dtb_row, norm_g]
    if s0 is not None:
        in_specs.append(pl.BlockSpec((None, None, 2, GDN_HEADS, GDN_DK, GDN_DV),
                                     lambda bi: (bi, layer, 0, 0, 0, 0)))
        args.append(s0)
    out_specs = [pl.BlockSpec((None, t, GDN_W), lambda bi: (bi, 0, 0))]
    out_shape = [jax.ShapeDtypeStruct((b, t, GDN_W), BF16)]
    mode, aliases = None, {}
    if emit_state:
        mode, spec, shape, aliases = _state_output(b, layer, GDN_HEADS, GDN_DK, GDN_DV, prev_states, in_specs, args)
        out_specs.append(spec)
        out_shape.append(shape)
    nc = t // CHUNK
    nd = 2 * GDN_HEADS
    wide = pltpu.VMEM((t, GDN_W), F32)
    scratch = [
        wide, wide, wide,
        pltpu.VMEM((GDN_W, t), BF16),
        pltpu.VMEM((t, LANE), F32), pltpu.VMEM((t, LANE), F32),
        pltpu.VMEM((nd, t), F32),
        pltpu.VMEM((2, t, LANE), F32), pltpu.VMEM((2, nd, t), F32),
        pltpu.VMEM((nd, t, GDN_DV), F32),
        pltpu.VMEM((nd, t, GDN_DK), BF16),
        pltpu.VMEM((nd, t, GDN_DK), BF16),
        pltpu.VMEM((nd, t, CHUNK), BF16),
        pltpu.VMEM((nd, nc, GDN_DK, CHUNK), BF16),
        pltpu.VMEM((nc, nd, LANE), F32),
        wide, wide,
        pltpu.VMEM((nd, GDN_DK, GDN_DV), F32),
    ]
    outs = pl.pallas_call(
        functools.partial(_gdn_kernel, t_len=t, has_state=s0 is not None, emit_state=mode),
        grid=(b,),
        in_specs=in_specs,
        out_specs=out_specs,
        out_shape=out_shape,
        scratch_shapes=scratch,
        input_output_aliases=aliases,
        compiler_params=_cparams(("arbitrary",)),
    )(*args)
    return outs if emit_state else (outs[0], None)


def _gla_kernel(*refs, t_len, has_state, emit_state):
    it = iter(refs)
    q_ref, k_ref, v_ref, r_ref, misc_ref, wg_ref, bg_ref, g_ref = (next(it) for _ in range(8))
    s0_ref = next(it) if has_state else None
    if emit_state == "later":
        next(it)
    o_ref = next(it)
    sfin_ref = next(it) if emit_state else None
    glog, vt_s, o_f, o_b, st = (next(it) for _ in range(5))

    c = CHUNK
    two = 2 * c
    n_pairs = t_len // two
    nh = GLA_HEADS
    misc = misc_ref[...]
    for d in range(2):
        lr = misc[:, MISC_LR + d * GLA_RANK:MISC_LR + (d + 1) * GLA_RANK]
        logits = _mm_hi(lr, wg_ref[d]) + bg_ref[d:d + 1, :]
        glog[d] = -_softplus(-logits) / GLA_TAU
    for h in range(nh):
        hs = slice(h * GLA_DV, (h + 1) * GLA_DV)
        vt_s[hs, :] = v_ref[:, hs].astype(F32).T.astype(BF16)
    for d in range(2):
        for h in range(nh):
            if has_state:
                s0 = jnp.concatenate([s0_ref[d, h], jnp.zeros((GLA_DV - GLA_DK, GLA_DV), F32)], axis=0)
                st[d * nh + h] = s0.T[:, :GLA_DK]
            else:
                st[d * nh + h] = jnp.zeros((GLA_DV, GLA_DK), F32)

    r2 = lax.broadcasted_iota(jnp.int32, (two, two), 0)
    c2 = lax.broadcasted_iota(jnp.int32, (two, two), 1)
    same = (r2 >= c) == (c2 >= c)
    cum_mask = (jnp.where(jnp.logical_and(same, r2 >= c2), 1.0, 0.0).astype(BF16),
                jnp.where(jnp.logical_and(same, r2 <= c2), 1.0, 0.0).astype(BF16))
    incl = (_tri(c, "ge"), _tri(c, "le"))
    last = (c - 1, 0)
    mid = (c // 2, c - 1 - c // 2)
    order = ((0, 1), (1, 0))

    def cumulate(cp, carry):
        r0 = pl.multiple_of(cp * two, two)
        for d in range(2):
            glog[d, pl.ds(r0, two), :] = _mm_mask_lhs(cum_mask[d], glog[d, pl.ds(r0, two), :])
        return carry

    lax.fori_loop(0, n_pairs, cumulate, 0)

    def body(i, carry):
        pieces = []
        for d in range(2):
            r0 = pl.multiple_of((i if d == 0 else n_pairs - 1 - i) * two, two)
            b_all = glog[d, pl.ds(r0, two), :]
            q_all = q_ref[pl.ds(r0, two), :].astype(F32) * (GLA_DK ** -0.5)
            k_all = k_ref[pl.ds(r0, two), :].astype(F32)
            vt_pair = [vt_s[h * GLA_DV:(h + 1) * GLA_DV, pl.ds(r0, two)] for h in range(nh)]
            for s in range(2):
                rs = slice(s * c, (s + 1) * c)
                bq = b_all[rs]
                b_mid = bq[mid[d]:mid[d] + 1, :]
                b_last = bq[last[d]:last[d] + 1, :]
                qe = (q_all[rs] * jnp.exp(bq - b_mid)).astype(BF16)
                ke = (k_all[rs] * jnp.exp(b_mid - bq)).astype(BF16)
                qd = (q_all[rs] * jnp.exp(bq)).astype(BF16)
                kt = (k_all[rs] * jnp.exp(b_last - bq)).astype(BF16)
                g_last = jnp.exp(b_last)
                rows = pl.ds(pl.multiple_of(r0 + s * c, c), c)
                for h in range(nh):
                    ks = slice(h * GLA_DK, (h + 1) * GLA_DK)
                    pieces.append((d, s, h, rows, qe[:, ks], ke[:, ks], qd[:, ks], kt[:, ks], g_last[:, ks],
                                   vt_pair[h][:, rs], v_ref[rows, h * GLA_DV:(h + 1) * GLA_DV]))
        a_intra = [jnp.where(incl[p[0]], _mm_nt(p[4], p[5]), 0.0).astype(BF16) for p in pieces]
        upd = [_dot(p[9], p[7]) for p in pieces]
        intra = [_dot(a, p[10]) for a, p in zip(a_intra, pieces)]
        for step in range(2):
            for p, m, o_in in zip(pieces, upd, intra):
                d, s, h, rows = p[0], p[1], p[2], p[3]
                if s != order[d][step]:
                    continue
                s_t = st[d * nh + h]
                out = o_f if d == 0 else o_b
                out[rows, h * GLA_DV:(h + 1) * GLA_DV] = _mm_nt(p[6], s_t) + o_in
                st[d * nh + h] = s_t * p[8] + m
        return carry

    lax.fori_loop(0, n_pairs, body, 0)

    for h in range(nh):
        hs = slice(h * GLA_DV, (h + 1) * GLA_DV)
        o = o_f[:, hs] + o_b[:, hs]
        o_ref[:, hs] = (_rms(o, g_ref[...]) * _silu(r_ref[:, hs].astype(F32))).astype(BF16)
    if emit_state:
        for d in range(2):
            for h in range(nh):
                s_pad = jnp.concatenate([st[d * nh + h], jnp.zeros((GLA_DV, GLA_DV - GLA_DK), F32)], axis=1)
                _store_state(sfin_ref, emit_state, d, h, s_pad.T[:GLA_DK, :])


def _gla(pa3, pf3, wg, bg, norm_g, s0, layer, emit_state, prev_states):
    b, t, _ = pa3.shape
    full = lambda shape: pl.BlockSpec(shape, lambda bi: (0,) * len(shape))
    in_specs = [
        pl.BlockSpec((None, t, GLA_KW), lambda bi: (bi, 0, A_GLA_Q // GLA_KW)),
        pl.BlockSpec((None, t, GLA_KW), lambda bi: (bi, 0, A_GLA_K // GLA_KW)),
        pl.BlockSpec((None, t, GLA_VW), lambda bi: (bi, 0, A_GLA_V // GLA_VW)),
        pl.BlockSpec((None, t, GLA_VW), lambda bi: (bi, 0, A_GLA_R // GLA_VW)),
        pl.BlockSpec((None, t, LANE), lambda bi: (bi, 0, F_MISC // LANE)),
        full((2, GLA_RANK, GLA_KW)),
        full((2, GLA_KW)),
        full((1, GLA_DV)),
    ]
    args = [pa3, pa3, pa3, pa3, pf3, wg, bg, norm_g]
    if s0 is not None:
        in_specs.append(pl.BlockSpec((None, None, 2, GLA_HEADS, GLA_DK, GLA_DV),
                                     lambda bi: (bi, layer, 0, 0, 0, 0)))
        args.append(s0)
    out_specs = [pl.BlockSpec((None, t, GLA_VW), lambda bi: (bi, 0, 0))]
    out_shape = [jax.ShapeDtypeStruct((b, t, GLA_VW), BF16)]
    mode, aliases = None, {}
    if emit_state:
        mode, spec, shape, aliases = _state_output(b, layer, GLA_HEADS, GLA_DK, GLA_DV, prev_states, in_specs, args)
        out_specs.append(spec)
        out_shape.append(shape)
    scratch = [pltpu.VMEM((2, t, GLA_KW), F32),
               pltpu.VMEM((GLA_VW, t), BF16),
               pltpu.VMEM((t, GLA_VW), F32), pltpu.VMEM((t, GLA_VW), F32),
               pltpu.VMEM((2 * GLA_HEADS, GLA_DV, GLA_DK), F32)]
    outs = pl.pallas_call(
        functools.partial(_gla_kernel, t_len=t, has_state=s0 is not None, emit_state=mode),
        grid=(b,),
        in_specs=in_specs,
        out_specs=out_specs,
        out_shape=out_shape,
        scratch_shapes=scratch,
        input_output_aliases=aliases,
        compiler_params=_cparams(("arbitrary",)),
    )(*args)
    return outs if emit_state else (outs[0], None)


ATT_Q_BLOCK = 256


def _rope_tables(t_len):
    half = DIFF_DH // 2
    quarter = half // 2
    inv = ROPE_THETA ** (-np.arange(0, half, 2, dtype=np.float64) / half)
    tok = np.arange(t_len)
    pos = np.stack([tok // GRID_W, tok % GRID_W], axis=1).astype(np.float64)
    ang = pos[:, :, None] * inv[None, None, :]
    cos = np.concatenate([np.cos(ang), np.cos(ang)], axis=-1).reshape(t_len, DIFF_DH)
    sin = np.concatenate([-np.sin(ang), np.sin(ang)], axis=-1).reshape(t_len, DIFF_DH)
    cos = np.concatenate([cos, cos], axis=-1).astype(np.float32)
    sin = np.concatenate([sin, sin], axis=-1).astype(np.float32)
    first = ((np.arange(2 * DIFF_DH) % half) < quarter).astype(np.float32)[None, :]
    return jnp.asarray(cos), jnp.asarray(sin), jnp.asarray(first), quarter


def _rope(x, cos, sin, first, quarter):
    width = x.shape[-1]
    ahead = pltpu.roll(x, width - quarter, axis=1)
    behind = pltpu.roll(x, quarter, axis=1)
    partner = jnp.where(first > 0.5, ahead, behind)
    return x * cos + partner * sin


def _diff_kernel(*refs, t_len, ctx_len, lam_init):
    it = iter(refs)
    q_ref, k_ref, v_ref, lam_ref, g_ref = (next(it) for _ in range(5))
    if ctx_len:
        ck_ref, cv_ref, cos_ref, sin_ref, first_ref = (next(it) for _ in range(5))
    o_ref = next(it)
    q_sc, k_sc, v_sc = (next(it) for _ in range(3))

    scale = DIFF_DH ** -0.5
    for h in range(DIFF_HEADS):
        hs = slice(h * LANE, (h + 1) * LANE)
        q = q_ref[:, hs].astype(F32)
        k = k_ref[:, hs]
        if ctx_len:
            quarter = DIFF_DH // 4
            q = _rope(q, cos_ref[...], sin_ref[...], first_ref[...], quarter)
            k = _rope(k, cos_ref[...], sin_ref[...], first_ref[...], quarter)
        q_sc[:, hs] = (q * scale).astype(BF16)
        k_sc[0:t_len, hs] = k.astype(BF16)
    v_sc[0:t_len, :] = v_ref[...].astype(BF16)
    if ctx_len:
        k_sc[t_len:t_len + ctx_len, :] = ck_ref[...].astype(BF16)
        v_sc[t_len:t_len + ctx_len, :] = cv_ref[...].astype(BF16)

    lp = lam_ref[...]
    lam = (jnp.exp(jnp.sum(lp[0:1, :] * lp[1:2, :], axis=1, keepdims=True))
           - jnp.exp(jnp.sum(lp[2:3, :] * lp[3:4, :], axis=1, keepdims=True)) + lam_init)

    tq = min(ATT_Q_BLOCK, t_len)
    group = DIFF_HEADS if t_len + ctx_len <= 512 else 2

    def body(i, carry):
        r0 = pl.multiple_of(i * tq, tq)
        for h0 in range(0, DIFF_HEADS, group):
            parts = [(h, half) for h in range(h0, h0 + group) for half in range(2)]
            cols = [slice(h * LANE + half * DIFF_DH, h * LANE + (half + 1) * DIFF_DH) for h, half in parts]
            s = [_mm_nt(q_sc[pl.ds(r0, tq), c], k_sc[:, c]) for c in cols]
            e = [jnp.exp(x - jnp.max(x, axis=-1, keepdims=True)) for x in s]
            inv_l = [1.0 / jnp.sum(x, axis=-1, keepdims=True) for x in e]
            pv = [_dot(x.astype(BF16), v_sc[:, h * LANE:(h + 1) * LANE]) * r
                  for x, r, (h, _) in zip(e, inv_l, parts)]
            for j in range(group):
                h = h0 + j
                o = pv[2 * j] - lam * pv[2 * j + 1]
                o_ref[pl.ds(r0, tq), h * LANE:(h + 1) * LANE] = (
                    _rms(o, g_ref[...]) * (1.0 - lam_init)).astype(BF16)
        return carry

    lax.fori_loop(0, t_len // tq, body, 0)


def _diff(pa3, pf3, lam_p, norm_g, ctx_k, ctx_v, layer, lam_init):
    b, t, _ = pa3.shape
    ctx_len = 0 if ctx_k is None else ctx_k.shape[2]
    full = lambda shape: pl.BlockSpec(shape, lambda bi: (0,) * len(shape))
    in_specs = [
        pl.BlockSpec((None, t, DIFF_QW), lambda bi: (bi, 0, A_DIFF_Q // DIFF_QW)),
        pl.BlockSpec((None, t, DIFF_QW), lambda bi: (bi, 0, F_DIFF_K // DIFF_QW)),
        pl.BlockSpec((None, t, DIFF_VW), lambda bi: (bi, 0, F_DIFF_V // DIFF_VW)),
        full((4, DIFF_DH)),
        full((1, DIFF_VD)),
    ]
    args = [pa3, pf3, pf3, lam_p, norm_g]
    if ctx_len:
        cos, sin, first, _ = _rope_tables(t)
        in_specs += [
            pl.BlockSpec((None, None, ctx_len, DIFF_QW), lambda bi: (bi, layer, 0, 0)),
            pl.BlockSpec((None, None, ctx_len, DIFF_VW), lambda bi: (bi, layer, 0, 0)),
            full((t, LANE)),
            full((t, LANE)),
            full((1, LANE)),
        ]
        args += [ctx_k, ctx_v, cos, sin, first]
    tk = t + ctx_len
    return pl.pallas_call(
        functools.partial(_diff_kernel, t_len=t, ctx_len=ctx_len, lam_init=lam_init),
        grid=(b,),
        in_specs=in_specs,
        out_specs=pl.BlockSpec((None, t, DIFF_VW), lambda bi: (bi, 0, 0)),
        out_shape=jax.ShapeDtypeStruct((b, t, DIFF_VW), BF16),
        scratch_shapes=[pltpu.VMEM((t, DIFF_QW), BF16), pltpu.VMEM((tk, DIFF_QW), BF16),
                        pltpu.VMEM((tk, DIFF_VW), BF16)],
        compiler_params=_cparams(("arbitrary",)),
    )(*args)


def _route(logits_t, bias_col):
    scores = _sigmoid(logits_t)
    biased = scores + bias_col
    rows = [biased[e:e + 1, :] for e in range(N_EXPERTS)]
    grp = []
    for g in range(N_GROUPS):
        a0, a1, a2, a3 = rows[g * GROUP_SIZE:(g + 1) * GROUP_SIZE]
        hi01, lo01 = jnp.maximum(a0, a1), jnp.minimum(a0, a1)
        hi23, lo23 = jnp.maximum(a2, a3), jnp.minimum(a2, a3)
        top1 = jnp.maximum(hi01, hi23)
        top2 = jnp.maximum(jnp.minimum(hi01, hi23), jnp.maximum(lo01, lo23))
        grp.append(top1 + top2)
    best = []
    for g in range(N_GROUPS):
        win = None
        for o in range(N_GROUPS):
            if o == g:
                continue
            cond = grp[g] > grp[o] if o < g else grp[g] >= grp[o]
            win = cond if win is None else jnp.logical_and(win, cond)
        best.append(win)
    sel_rows = []
    for e in range(N_EXPERTS):
        g = e // GROUP_SIZE
        beaten = jnp.zeros_like(rows[e])
        for o in range(g * GROUP_SIZE, (g + 1) * GROUP_SIZE):
            if o == e:
                continue
            ahead = rows[o] >= rows[e] if o < e else rows[o] > rows[e]
            beaten = beaten + jnp.where(ahead, 1.0, 0.0)
        sel_rows.append(jnp.where(jnp.logical_and(best[g], beaten < 1.5), 1.0, 0.0))
    sel = jnp.concatenate(sel_rows, axis=0)
    picked = sel * scores
    return sel, picked / jnp.sum(picked, axis=0, keepdims=True)


PIECE = 16
ROW_TILE = 256
SLOT_PAD = N_EXPERTS * PIECE
HS_COLS = D_MODEL + LANE


def _route_meta(sel, wts):
    n_e, t = sel.shape
    e_col = lax.broadcasted_iota(jnp.int32, (n_e, 1), 0).astype(F32)
    before = jnp.where(_tri(t, "lt"), 1.0, 0.0).astype(BF16)
    rank = _dot(sel.astype(BF16), before)
    cnt = jnp.sum(sel, axis=1, keepdims=True)
    plen = jnp.floor((cnt + (PIECE - 1.0)) * (1.0 / PIECE)) * PIECE
    lower = jnp.where(_tri(n_e, "gt"), 1.0, 0.0).astype(BF16)
    loff = _dot(lower, jnp.broadcast_to(plen, (n_e, LANE)).astype(BF16))[:, 0:1]
    slot = loff + rank
    chosen = sel > 0.5
    e_a = jnp.min(jnp.where(chosen, e_col, 2.0 * n_e), axis=0, keepdims=True)
    e_b = jnp.max(jnp.where(chosen, e_col, -1.0), axis=0, keepdims=True)
    take = lambda e_row, x: jnp.sum(jnp.where(e_col == e_row, x, 0.0), axis=0, keepdims=True)
    rows = jnp.concatenate([take(e_a, slot), take(e_b, slot), take(e_a, wts), take(e_b, wts), e_a, e_b,
                            jnp.zeros((2, t), F32)], axis=0)
    return rows, plen, loff


def _merge_kernel(og_ref, ol_ref, od_ref, lg0_ref, lg1_ref, lg2_ref, x_ref, mod_ref, wb_ref, wo_ref,
                  lng_ref, lnb_ref, rw_ref, rb_ref, x1_ref, h_ref, mrow_ref, mcol_ref, plen_ref, loff_ref):
    gate1 = mod_ref[:, 2 * D_MODEL:3 * D_MODEL]
    shift2 = mod_ref[:, 3 * D_MODEL:4 * D_MODEL]
    scale2 = mod_ref[:, 4 * D_MODEL:5 * D_MODEL]
    tm = x_ref.shape[0]
    halves = [pl.ds(r * (tm // 2), tm // 2) for r in range(2)]
    accs = [None, None]
    for s, (o_ref, lg_ref) in enumerate(((og_ref, lg0_ref), (ol_ref, lg1_ref), (od_ref, lg2_ref))):
        projs = [_dot(o_ref[rows, :], wb_ref[s]) for rows in halves]
        terms = [_sigmoid(lg_ref[rows, :].astype(F32)) * p for rows, p in zip(halves, projs)]
        accs = [t if a is None else a + t for a, t in zip(accs, terms)]
    ms = [_mm(a, wo_ref[...]) for a in accs]
    logit_parts = []
    for rows, m in zip(halves, ms):
        x1 = _layernorm(ALPHA * x_ref[rows, :] + gate1 * m, lng_ref[...], lnb_ref[...])
        x1_ref[rows, :] = x1
        h = x1 * (1.0 + scale2) + shift2
        h_ref[rows, :] = h.astype(BF16)
        logit_parts.append(_mm_hi_nt(rw_ref[...], h))
    sel, gates_t = _route(jnp.concatenate(logit_parts, axis=1), rb_ref[...])
    rows, plen, loff = _route_meta(sel, gates_t)
    mrow_ref[...] = rows
    mcol_ref[...] = jnp.concatenate([rows, jnp.zeros((LANE - rows.shape[0], tm), F32)], axis=0).T
    lane = lax.broadcasted_iota(jnp.int32, (N_EXPERTS, LANE), 1)
    cols = jnp.where(lane == 0, plen, jnp.where(lane == 1, loff, 0.0))
    cols_t = jnp.concatenate([cols, jnp.zeros((LANE - N_EXPERTS, LANE), F32)], axis=0).T
    plen_ref[...] = cols_t[0:1, :]
    loff_ref[...] = cols_t[1:2, :]


def _merge(o_gdn, o_gla, o_diff, pa, x, mod, wb, wo, ln_g, ln_b, rw_t, rb_col, layer, row0,
           tokens_per_row, tm):
    n = x.shape[0]
    mod_row = lambda i: (layer, row0 + (i * tm) // tokens_per_row, 0, 0)
    full = lambda shape: pl.BlockSpec(shape, lambda i: (0,) * len(shape))
    return pl.pallas_call(
        _merge_kernel,
        grid=(n // tm,),
        in_specs=[
            pl.BlockSpec((tm, MIX_W), lambda i: (i, 0)),
            pl.BlockSpec((tm, MIX_W), lambda i: (i, 0)),
            pl.BlockSpec((tm, MIX_W), lambda i: (i, 0)),
            pl.BlockSpec((tm, D_MODEL), lambda i: (i, A_MERGE // D_MODEL)),
            pl.BlockSpec((tm, D_MODEL), lambda i: (i, A_MERGE // D_MODEL + 1)),
            pl.BlockSpec((tm, D_MODEL), lambda i: (i, A_MERGE // D_MODEL + 2)),
            pl.BlockSpec((tm, D_MODEL), lambda i: (i, 0)),
            pl.BlockSpec((None, None, 1, 6 * D_MODEL), mod_row),
            full((N_BRANCH, MIX_W, D_MODEL)),
            full((D_MODEL, D_MODEL)),
            full((1, D_MODEL)),
            full((1, D_MODEL)),
            full((N_EXPERTS, D_MODEL)),
            full((N_EXPERTS, 1)),
        ],
        out_specs=[
            pl.BlockSpec((tm, D_MODEL), lambda i: (i, 0)),
            pl.BlockSpec((tm, D_MODEL), lambda i: (i, 0)),
            pl.BlockSpec((8, tm), lambda i: (0, i)),
            pl.BlockSpec((tm, LANE), lambda i: (i, 0)),
            pl.BlockSpec((None, 1, LANE), lambda i: (i, 0, 0)),
            pl.BlockSpec((None, 1, LANE), lambda i: (i, 0, 0)),
        ],
        out_shape=[
            jax.ShapeDtypeStruct((n, D_MODEL), F32),
            jax.ShapeDtypeStruct((n, D_MODEL), BF16),
            jax.ShapeDtypeStruct((8, n), F32),
            jax.ShapeDtypeStruct((n, LANE), F32),
            jax.ShapeDtypeStruct((n // tm, 1, LANE), F32),
            jax.ShapeDtypeStruct((n // tm, 1, LANE), F32),
        ],
        compiler_params=_cparams(("arbitrary",)),
    )(o_gdn, o_gla, o_diff, pa, pa, pa, x, mod, wb, wo, ln_g, ln_b, rw_t, rb_col)


def _plan_kernel(plen_ref, loff_ref, goff_ref, npc_ref, loffi_ref, tiles_ref, tail_ref, *, n_steps):
    plen = plen_ref[...]
    nb = plen.shape[0]
    earlier = jnp.where(_tri(nb, "gt"), 1.0, 0.0).astype(BF16)
    run = _dot(earlier, plen.astype(BF16))
    gtot = jnp.sum(plen, axis=0, keepdims=True)
    lane = lax.broadcasted_iota(jnp.int32, (1, LANE), 1).astype(F32)
    ntile = jnp.floor((gtot + (ROW_TILE - 1.0)) * (1.0 / ROW_TILE))
    before = jnp.where(_tri(LANE, "lt"), 1.0, 0.0).astype(BF16)
    first = _dot(jnp.broadcast_to(ntile, (8, LANE)).astype(BF16), before)[0:1, :]
    end = first + ntile
    n_used = jnp.sum(ntile, axis=1, keepdims=True)
    goff_ref[...] = (first * ROW_TILE + run).astype(jnp.int32)
    npc_ref[...] = (plen * (1.0 / PIECE)).astype(jnp.int32)
    loffi_ref[...] = loff_ref[...].astype(jnp.int32)
    end_col = jnp.concatenate([end, jnp.zeros((LANE - 1, LANE), F32)], axis=0).T[:, 0:1]
    e_col = lax.broadcasted_iota(jnp.int32, (LANE, 1), 0).astype(F32)
    tile = jnp.minimum(lane, n_used - 1.0)
    done = jnp.logical_and(end_col <= tile, e_col < N_EXPERTS)
    t_exp = jnp.minimum(jnp.sum(jnp.where(done, 1.0, 0.0), axis=0, keepdims=True), N_EXPERTS - 1.0)
    nonempty = jnp.where(ntile > 0.5, 1.0, 0.0)
    run_idx = _dot(jnp.broadcast_to(nonempty, (8, LANE)).astype(BF16), before)[0:1, :]
    owner_after = jnp.logical_and(end_col <= end, e_col < N_EXPERTS)
    nxt = jnp.minimum(jnp.sum(jnp.where(owner_after, 1.0, 0.0), axis=0, keepdims=True), N_EXPERTS - 1.0)
    has_next = jnp.where(end < n_used, 1.0, 0.0)
    by_expert = jnp.concatenate([run_idx - 2.0 * jnp.floor(run_idx * 0.5), nxt, has_next,
                                 jnp.zeros((LANE - 3, LANE), F32)], axis=0).T
    mine = e_col == t_exp
    per_tile = [jnp.sum(jnp.where(mine, by_expert[:, j:j + 1], 0.0), axis=0, keepdims=True) for j in range(3)]
    tiles_ref[...] = jnp.concatenate(
        [t_exp, jnp.broadcast_to(n_used, (1, LANE))] + per_tile + [jnp.zeros((3, LANE), F32)],
        axis=0).astype(jnp.int32)
    tail_ref[...] = jnp.concatenate(
        [first * ROW_TILE + gtot, (ntile * ROW_TILE - gtot) * (1.0 / PIECE),
         jnp.broadcast_to(n_used, (1, LANE)), jnp.broadcast_to(n_steps - n_used, (1, LANE)),
         jnp.zeros((4, LANE), F32)], axis=0).astype(jnp.int32)


def _plan(plen, loff, n_steps):
    nb = plen.shape[0]
    i32 = lambda rows: jax.ShapeDtypeStruct((rows, LANE), jnp.int32)
    return pl.pallas_call(
        functools.partial(_plan_kernel, n_steps=n_steps),
        out_shape=[i32(nb), i32(nb), i32(nb), i32(8), i32(8)],
    )(plen, loff)


def _piece_copy(local, remote, sem, to_remote, lo, go):
    src = local.at[pl.ds(pl.multiple_of(lo, PIECE), PIECE)]
    dst = remote.at[pl.ds(pl.multiple_of(go, PIECE), PIECE)]
    return pltpu.make_async_copy(src, dst, sem) if to_remote else pltpu.make_async_copy(dst, src, sem)


def _start_pieces(b, loff_s, npc_s, goff_s, local, remote, sem, to_remote):
    for e in range(N_EXPERTS):
        lo = loff_s[b, e]
        go = goff_s[b, e]

        def start(j, carry, lo=lo, go=go):
            _piece_copy(local, remote, sem, to_remote, lo + j * PIECE, go + j * PIECE).start()
            return carry

        lax.fori_loop(0, npc_s[b, e], start, 0)


def _wait_pieces(b, npc_s, local, remote, sem, to_remote):
    total = 0
    for e in range(N_EXPERTS):
        total = total + npc_s[b, e]

    def wait(j, carry):
        _piece_copy(local, remote, sem, to_remote, 0, 0).wait()
        return carry

    lax.fori_loop(0, total, wait, 0)


def _dispatch_kernel(loff_s, npc_s, goff_s, tail_s, h_ref, mrow_ref, mcol_ref, hs_hbm, local, zeros, sems,
                     tail_sem, tile_sem, *, slots):
    b = pl.program_id(0)
    last = pl.num_programs(0) - 1
    slot = b % 2
    buf = local.at[slot]
    sem = sems.at[slot]

    @pl.when(b >= 2)
    def _():
        _wait_pieces(b - 2, npc_s, buf, hs_hbm, sem, True)

    tb = h_ref.shape[0]
    mrow = mrow_ref[...]
    mcol = mcol_ref[...]
    slot_id = lax.broadcasted_iota(jnp.int32, (slots, tb), 0).astype(F32)
    s_a = jnp.where(slot_id == mrow[0:1, :], 1.0, 0.0).astype(BF16)
    s_b = jnp.where(slot_id == mrow[1:2, :], 1.0, 0.0).astype(BF16)
    lane = lax.broadcasted_iota(jnp.int32, (tb, LANE), 1)

    def weight_lanes(w):
        hi = w.astype(BF16).astype(F32)
        return jnp.where(lane == 0, hi, jnp.where(lane == 1, w - hi, 0.0)).astype(BF16)

    local[slot, :, :D_MODEL] = _dot(s_a + s_b, h_ref[...]).astype(BF16)
    local[slot, :, D_MODEL:] = (_dot(s_a, weight_lanes(mcol[:, 2:3]))
                                + _dot(s_b, weight_lanes(mcol[:, 3:4]))).astype(BF16)
    _start_pieces(b, loff_s, npc_s, goff_s, buf, hs_hbm, sem, True)

    @pl.when(b == last)
    def _():
        _wait_pieces(b, npc_s, buf, hs_hbm, sem, True)

        @pl.when(b >= 1)
        def _():
            _wait_pieces(b - 1, npc_s, local.at[1 - slot], hs_hbm, sems.at[1 - slot], True)

        zeros[...] = jnp.zeros_like(zeros)
        zero_piece = zeros.at[pl.ds(0, PIECE)]
        total = 0
        for e in range(N_EXPERTS):
            n = tail_s[1, e]
            start_row = tail_s[0, e]

            def start(j, carry, start_row=start_row):
                dst = hs_hbm.at[pl.ds(pl.multiple_of(start_row + j * PIECE, PIECE), PIECE)]
                pltpu.make_async_copy(zero_piece, dst, tail_sem).start()
                return carry

            lax.fori_loop(0, n, start, 0)
            total = total + n

        def wait(j, carry):
            pltpu.make_async_copy(zero_piece, hs_hbm.at[pl.ds(0, PIECE)], tail_sem).wait()
            return carry

        lax.fori_loop(0, total, wait, 0)

        def tile_copy(j):
            row = pl.multiple_of((tail_s[2, 0] + j) * ROW_TILE, ROW_TILE)
            return pltpu.make_async_copy(zeros, hs_hbm.at[pl.ds(row, ROW_TILE)], tile_sem)

        def start_tile(j, carry):
            tile_copy(j).start()
            return carry

        def wait_tile(j, carry):
            tile_copy(j).wait()
            return carry

        lax.fori_loop(0, tail_s[3, 0], start_tile, 0)
        lax.fori_loop(0, tail_s[3, 0], wait_tile, 0)


def _dispatch(loff_i, npc, goff, tail, h, mrow, mcol, tb, n_steps):
    n = h.shape[0]
    slots = 2 * tb + SLOT_PAD
    grid_spec = pltpu.PrefetchScalarGridSpec(
        num_scalar_prefetch=4,
        grid=(n // tb,),
        in_specs=[
            pl.BlockSpec((tb, D_MODEL), lambda i, *_: (i, 0)),
            pl.BlockSpec((8, tb), lambda i, *_: (0, i)),
            pl.BlockSpec((tb, LANE), lambda i, *_: (i, 0)),
        ],
        out_specs=pl.BlockSpec(memory_space=pl.ANY),
        scratch_shapes=[pltpu.VMEM((2, slots, HS_COLS), BF16), pltpu.VMEM((ROW_TILE, HS_COLS), BF16),
                        pltpu.SemaphoreType.DMA((2,)), pltpu.SemaphoreType.DMA, pltpu.SemaphoreType.DMA],
    )
    return pl.pallas_call(
        functools.partial(_dispatch_kernel, slots=slots),
        grid_spec=grid_spec,
        out_shape=jax.ShapeDtypeStruct((n_steps * ROW_TILE, HS_COLS), BF16),
        compiler_params=_cparams(("arbitrary",)),
    )(loff_i, npc, goff, tail, h, mrow, mcol)


def _expert_kernel(tiles_s, hs_ref, wg_hbm, wu_hbm, wd_hbm, y_ref, wg_f, wu_f, wd_f, wg_b, wu_b, wd_b, sems,
                   *, layer):
    i = pl.program_id(0)
    e = tiles_s[0, i]
    slot = tiles_s[2, i]
    first_of_run = jnp.logical_or(i == 0, e != tiles_s[0, jnp.maximum(i - 1, 0)])

    def weight_copies(expert, s):
        return [pltpu.make_async_copy(src.at[layer, expert], dst.at[s], sems.at[s, j])
                for j, (src, dst) in enumerate(((wg_hbm, wg_f), (wu_hbm, wu_f), (wd_hbm, wd_f)))]

    @pl.when(i == 0)
    def _():
        for cp in weight_copies(e, slot):
            cp.start()

    @pl.when(jnp.logical_and(first_of_run, i < tiles_s[1, 0]))
    def _():
        for cp in weight_copies(e, slot):
            cp.wait()

        @pl.when(tiles_s[4, i] > 0)
        def _():
            for cp in weight_copies(tiles_s[3, i], 1 - slot):
                cp.start()

        wg_b[...] = wg_f[slot].astype(BF16)
        wu_b[...] = wu_f[slot].astype(BF16)
        wd_b[...] = wd_f[slot].astype(BF16)

    @pl.when(i < tiles_s[1, 0])
    def _():
        halves = [pl.ds(r * (ROW_TILE // 2), ROW_TILE // 2) for r in range(2)]
        xs = [hs_ref[rows, :D_MODEL] for rows in halves]
        ws = [hs_ref[rows, D_MODEL:D_MODEL + 1].astype(F32) + hs_ref[rows, D_MODEL + 1:D_MODEL + 2].astype(F32)
              for rows in halves]
        gs = [_dot(x, wg_b[...]) for x in xs]
        us = [_dot(x, wu_b[...]) for x in xs]
        acts = [(_silu(g) * u * w).astype(BF16) for g, u, w in zip(gs, us, ws)]
        for rows, act in zip(halves, acts):
            y_ref[rows, :] = _dot(act, wd_b[...]).astype(BF16)

    @pl.when(i >= tiles_s[1, 0])
    def _():
        y_ref[...] = jnp.zeros_like(y_ref)


def _experts(tiles, hs, wg, wu, wd, layer, n_steps):
    grid_spec = pltpu.PrefetchScalarGridSpec(
        num_scalar_prefetch=1,
        grid=(n_steps,),
        in_specs=[
            pl.BlockSpec((ROW_TILE, HS_COLS), lambda i, t: (i, 0)),
            pl.BlockSpec(memory_space=pl.ANY),
            pl.BlockSpec(memory_space=pl.ANY),
            pl.BlockSpec(memory_space=pl.ANY),
        ],
        out_specs=pl.BlockSpec((ROW_TILE, D_MODEL), lambda i, t: (i, 0)),
        scratch_shapes=[pltpu.VMEM((2, D_MODEL, D_FF), F32), pltpu.VMEM((2, D_MODEL, D_FF), F32),
                        pltpu.VMEM((2, D_FF, D_MODEL), F32),
                        pltpu.VMEM((D_MODEL, D_FF), BF16), pltpu.VMEM((D_MODEL, D_FF), BF16),
                        pltpu.VMEM((D_FF, D_MODEL), BF16), pltpu.SemaphoreType.DMA((2, 3))],
    )
    return pl.pallas_call(
        functools.partial(_expert_kernel, layer=layer),
        grid_spec=grid_spec,
        out_shape=jax.ShapeDtypeStruct((hs.shape[0], D_MODEL), BF16),
        compiler_params=_cparams(("arbitrary",)),
    )(tiles, hs, wg, wu, wd)


def _combine_kernel(loff_s, npc_s, goff_s, y_hbm, mcol_ref, x1_ref, mod_ref, lng_ref, lnb_ref, o_ref,
                    local, sems, *, slots):
    b = pl.program_id(0)
    tb = x1_ref.shape[0]
    slot = b % 2

    def fetch(blk, s):
        local[s] = jnp.zeros(local.shape[1:], local.dtype)
        _start_pieces(blk, loff_s, npc_s, goff_s, local.at[s], y_hbm, sems.at[s], False)

    @pl.when(b == 0)
    def _():
        fetch(0, 0)

    _wait_pieces(b, npc_s, local.at[slot], y_hbm, sems.at[slot], False)

    @pl.when(b + 1 < pl.num_programs(0))
    def _():
        fetch(b + 1, 1 - slot)

    mcol = mcol_ref[...]
    slot_id = lax.broadcasted_iota(jnp.int32, (tb, slots), 1).astype(F32)
    pick = jnp.logical_or(slot_id == mcol[:, 0:1], slot_id == mcol[:, 1:2])
    f = _dot(jnp.where(pick, 1.0, 0.0).astype(BF16), local[slot])
    gate2 = mod_ref[:, 5 * D_MODEL:6 * D_MODEL]
    o_ref[...] = _layernorm(ALPHA * x1_ref[...] + gate2 * f, lng_ref[...], lnb_ref[...])


def _combine(loff_i, npc, goff, y, mcol, x1, mod, ln_g, ln_b, layer, row0, tokens_per_row, tb):
    n = x1.shape[0]
    slots = 2 * tb + SLOT_PAD
    mod_row = lambda i, *_: (layer, row0 + (i * tb) // tokens_per_row, 0, 0)
    grid_spec = pltpu.PrefetchScalarGridSpec(
        num_scalar_prefetch=3,
        grid=(n // tb,),
        in_specs=[
            pl.BlockSpec(memory_space=pl.ANY),
            pl.BlockSpec((tb, LANE), lambda i, *_: (i, 0)),
            pl.BlockSpec((tb, D_MODEL), lambda i, *_: (i, 0)),
            pl.BlockSpec((None, None, 1, 6 * D_MODEL), mod_row),
            pl.BlockSpec((1, D_MODEL), lambda i, *_: (0, 0)),
            pl.BlockSpec((1, D_MODEL), lambda i, *_: (0, 0)),
        ],
        out_specs=pl.BlockSpec((tb, D_MODEL), lambda i, *_: (i, 0)),
        scratch_shapes=[pltpu.VMEM((2, slots, D_MODEL), BF16), pltpu.SemaphoreType.DMA((2,))],
    )
    return pl.pallas_call(
        functools.partial(_combine_kernel, slots=slots),
        grid_spec=grid_spec,
        out_shape=jax.ShapeDtypeStruct((n, D_MODEL), F32),
        compiler_params=_cparams(("arbitrary",)),
    )(loff_i, npc, goff, y, mcol, x1, mod, ln_g, ln_b)


def _moe(h, mrow, mcol, plen, loff, wg, wu, wd, x1, mod, ln_g, ln_b, layer, row0, tokens_per_row, tb):
    n = h.shape[0]
    nb = n // tb
    n_steps = (2 * n + nb * N_EXPERTS * (PIECE - 1)) // ROW_TILE + N_EXPERTS
    assert n_steps <= LANE
    goff, npc, loff_i, tiles, tail = _plan(plen.reshape(nb, LANE), loff.reshape(nb, LANE), n_steps)
    hs = _dispatch(loff_i, npc, goff, tail, h, mrow, mcol, tb, n_steps)
    y = _experts(tiles, hs, wg, wu, wd, layer, n_steps)
    return _combine(loff_i, npc, goff, y, mcol, x1, mod, ln_g, ln_b, layer, row0, tokens_per_row, tb)


def _reorder_w_in(w):
    sizes = (3 * GDN_W, GDN_W, 2 * GDN_HEADS, 2 * GDN_HEADS, GLA_KW, GLA_KW, GLA_VW, GLA_VW,
             2 * GLA_RANK, DIFF_QW, DIFF_QW, DIFF_VW, N_BRANCH * D_MODEL)
    offs = np.concatenate([[0], np.cumsum(sizes)])
    seg = lambda i: w[:, offs[i]:offs[i + 1]]
    used = 2 * GDN_HEADS * 2 + 2 * GLA_RANK
    wa = jnp.concatenate([seg(0), seg(1), seg(4), seg(5), seg(6), seg(7), seg(9), seg(12)], axis=1)
    wf = jnp.concatenate([seg(10), seg(11), seg(2), seg(3), seg(8),
                          jnp.zeros((w.shape[0], LANE - used), w.dtype)], axis=1)
    return wa.astype(BF16), wf.astype(BF16)


def _lane_row(vals, offset):
    return jnp.zeros((1, LANE), F32).at[0, offset:offset + vals.shape[0]].set(vals)


def kernel(x_prompt, x_sample, c, state_gdn, state_gla, cache_k, cache_v, c_ctx, w_mod, b_mod, w_in,
           gdn_conv, gdn_a_log, gdn_dt_bias, gdn_norm, gla_w_gate, gla_b_gate, gla_norm, diff_lambda,
           diff_norm, w_branch, w_out, ln_g, ln_b, router_w, router_b, exp_w_gate, exp_w_up, exp_w_down):
    bp, tp, d = x_prompt.shape
    bs, ts, _ = x_sample.shape
    pad_rows = MOD_ROWS - 1 - bs
    cvecs = jnp.concatenate([c_ctx[None, :], c, jnp.zeros((pad_rows, d), F32)], axis=0)
    mod = _modulation(cvecs, w_mod, b_mod).reshape(DEPTH, MOD_ROWS, 1, 6 * d)

    rw_t = router_w.T
    rb_col = router_b.reshape(N_EXPERTS, 1)
    ck = cache_k.reshape(bs, DEPTH, cache_k.shape[2], DIFF_QW)
    cv = cache_v.reshape(bs, DEPTH, cache_v.shape[2], DIFF_VW)

    layer_w = []
    for l in range(DEPTH):
        wa, wf = _reorder_w_in(w_in[l])
        layer_w.append(dict(
            wa=wa, wf=wf,
            alog=_lane_row(gdn_a_log[l].reshape(-1), MISC_A),
            dtb=_lane_row(gdn_dt_bias[l].reshape(-1), MISC_A),
            wb=w_branch[l].astype(BF16),
            wo=w_out[l].astype(BF16),
        ))

    def layer(x, l, row0, tokens_per_row, s_gdn, s_gla, ctx_k, ctx_v, emit_state, tm, prev_gdn=None,
              prev_gla=None, prev_cache=None):
        b, t, _ = x.shape
        n = b * t
        lw = layer_w[l]
        xf = x.reshape(n, d)
        if emit_state:
            pa, pf, *cache = _inproj(xf, mod, lw["wa"], lw["wf"], l, row0, tokens_per_row, t, (b, t), prev_cache)
        else:
            pa, pf = _inproj(xf, mod, lw["wa"], lw["wf"], l, row0, tokens_per_row, tm, None, None)
            cache = None
        pa3 = pa.reshape(b, t, A_COLS)
        pf3 = pf.reshape(b, t, F_COLS)
        o_gdn, gdn_fin = _gdn(pa3, pf3, gdn_conv[l], lw["alog"], lw["dtb"], gdn_norm[l][None, :], s_gdn, l,
                              emit_state, prev_gdn)
        o_gla, gla_fin = _gla(pa3, pf3, gla_w_gate[l], gla_b_gate[l], gla_norm[l][None, :], s_gla, l,
                              emit_state, prev_gla)
        lam_init = 0.8 - 0.6 * math.exp(-0.3 * l)
        o_diff = _diff(pa3, pf3, diff_lambda[l], diff_norm[l][None, :], ctx_k, ctx_v, l, lam_init)
        x1, h, mrow, mcol, plen, loff = _merge(
            o_gdn.reshape(n, MIX_W), o_gla.reshape(n, MIX_W), o_diff.reshape(n, MIX_W), pa, xf, mod,
            lw["wb"], lw["wo"], ln_g[l, 0][None, :], ln_b[l, 0][None, :], rw_t, rb_col, l, row0,
            tokens_per_row, tm)
        x2 = _moe(h, mrow, mcol, plen, loff, exp_w_gate, exp_w_up, exp_w_down, x1, mod,
                  ln_g[l, 1][None, :], ln_b[l, 1][None, :], l, row0, tokens_per_row, tm)
        return x2.reshape(b, t, d), gdn_fin, gla_fin, cache

    hp = x_prompt
    gdn_states, gla_states, cache = None, None, None
    for l in range(DEPTH):
        hp, gdn_states, gla_states, cache = layer(hp, l, 0, bp * tp, None, None, None, None, True, 512,
                                                  gdn_states, gla_states, cache)

    hs = x_sample
    for l in range(DEPTH):
        hs = layer(hs, l, 1, ts, state_gdn, state_gla, ck, cv, False, 512)[0]

    return hp, hs, gdn_states, gla_states, cache[0], cache[1]
```

```python
import functools
import math

import numpy as np
import jax
import jax.numpy as jnp
from jax import lax
from jax.experimental import pallas as pl
from jax.experimental.pallas import tpu as pltpu

F32 = jnp.float32
BF16 = jnp.bfloat16

D_MODEL = 1024
DEPTH = 2
GRID_W = 64
GDN_HEADS = 4
GDN_DK = 128
GDN_DV = 128
GDN_W = GDN_HEADS * GDN_DV
SHORT_CONV = 5
GLA_HEADS = 4
GLA_DK = 64
GLA_DV = 128
GLA_KW = GLA_HEADS * GLA_DK
GLA_VW = GLA_HEADS * GLA_DV
GLA_RANK = 16
GLA_TAU = 16.0
CHUNK = 64
DIFF_HEADS = 4
DIFF_DH = 64
DIFF_VD = 2 * DIFF_DH
DIFF_QW = DIFF_HEADS * 2 * DIFF_DH
DIFF_VW = DIFF_HEADS * DIFF_VD
ROPE_THETA = 10000.0
MIX_W = 512
N_BRANCH = 3
N_EXPERTS = 16
N_GROUPS = 4
GROUP_SIZE = N_EXPERTS // N_GROUPS
D_FF = 512
ALPHA = (2 * DEPTH) ** 0.25
EPS = 1e-6

LANE = 128
MOD_ROWS = 8
VMEM_LIMIT = 56 * 1024 * 1024

A_GDN_QKV = 0
A_GDN_Z = 1536
A_GLA_Q = 2048
A_GLA_K = 2304
A_GLA_V = 2560
A_GLA_R = 3072
A_DIFF_Q = 3584
A_MERGE = 4096
A_COLS = 7168
F_DIFF_K = 0
F_DIFF_V = 512
F_MISC = 1024
F_COLS = 1152
MISC_B = 0
MISC_A = 8
MISC_LR = 16


def _cparams(sem):
    return pltpu.CompilerParams(dimension_semantics=sem, vmem_limit_bytes=VMEM_LIMIT)


def _dot(a, b):
    return jnp.dot(a, b, preferred_element_type=F32)


def _mm(a, b):
    return _dot(a.astype(BF16), b.astype(BF16))


def _mm_nt(a, b):
    return lax.dot_general(a.astype(BF16), b.astype(BF16), (((1,), (1,)), ((), ())),
                           preferred_element_type=F32)


def _split3(x):
    hi = x.astype(BF16)
    r = x - hi.astype(F32)
    mid = r.astype(BF16)
    lo = (r - mid.astype(F32)).astype(BF16)
    return hi, mid, lo


def _mm_mask_lhs(mask_bf16, x):
    hi, mid, lo = _split3(x)
    return _dot(mask_bf16, hi) + _dot(mask_bf16, mid) + _dot(mask_bf16, lo)


def _mm_mask_rhs(x, mask_bf16):
    hi, mid, lo = _split3(x)
    return _dot(hi, mask_bf16) + _dot(mid, mask_bf16) + _dot(lo, mask_bf16)


def _mm_hi(a, b):
    ah = a.astype(BF16)
    al = (a - ah.astype(F32)).astype(BF16)
    bh = b.astype(BF16)
    bl = (b - bh.astype(F32)).astype(BF16)
    return _dot(ah, bh) + _dot(ah, bl) + _dot(al, bh)


def _mm_hi_nt(a, b):
    ah = a.astype(BF16)
    al = (a - ah.astype(F32)).astype(BF16)
    bh = b.astype(BF16)
    bl = (b - bh.astype(F32)).astype(BF16)
    dot = lambda x, y: lax.dot_general(x, y, (((1,), (1,)), ((), ())), preferred_element_type=F32)
    return dot(ah, bh) + dot(ah, bl) + dot(al, bh)


def _sigmoid(x):
    return 1.0 / (1.0 + jnp.exp(-x))


def _silu(x):
    return x * _sigmoid(x)


def _softplus(x):
    return jnp.maximum(x, 0.0) + jnp.log1p(jnp.exp(-jnp.abs(x)))


def _rms(x, g):
    return x * lax.rsqrt(jnp.mean(x * x, axis=-1, keepdims=True) + EPS) * g


def _layernorm(y, g, b):
    mu = jnp.mean(y, axis=-1, keepdims=True)
    d = y - mu
    var = jnp.mean(d * d, axis=-1, keepdims=True)
    return d * lax.rsqrt(var + EPS) * g + b


def _pick_lane(x, idx):
    lane = lax.broadcasted_iota(jnp.int32, x.shape, 1)
    return jnp.sum(jnp.where(lane == idx, x, 0.0), axis=1, keepdims=True)


def _tri(n, kind):
    r = lax.broadcasted_iota(jnp.int32, (n, n), 0)
    c = lax.broadcasted_iota(jnp.int32, (n, n), 1)
    return {"ge": r >= c, "gt": r > c, "le": r <= c, "lt": r < c}[kind]


def _mod_kernel(c_ref, w_ref, b_ref, o_ref):
    s = _silu(c_ref[...])
    o_ref[...] = _mm_hi(s, w_ref[...]) + b_ref[...]


def _modulation(cvecs, w_mod, b_mod):
    tn = 1536
    return pl.pallas_call(
        _mod_kernel,
        grid=(DEPTH, 6 * D_MODEL // tn),
        in_specs=[
            pl.BlockSpec((MOD_ROWS, D_MODEL), lambda l, j: (0, 0)),
            pl.BlockSpec((None, D_MODEL, tn), lambda l, j: (l, 0, j)),
            pl.BlockSpec((None, 1, tn), lambda l, j: (l, 0, j)),
        ],
        out_specs=pl.BlockSpec((None, MOD_ROWS, tn), lambda l, j: (l, 0, j)),
        out_shape=jax.ShapeDtypeStruct((DEPTH, MOD_ROWS, 6 * D_MODEL), F32),
        compiler_params=_cparams(("arbitrary", "arbitrary")),
    )(cvecs, w_mod, b_mod.reshape(DEPTH, 1, 6 * D_MODEL))


INPROJ_COL_STEP = 1792


def _inproj_kernel(*refs, cache_mode):
    x_ref, mod_ref, wa_ref, wf_ref = refs[:4]
    outs = refs[6:] if cache_mode == "later" else refs[4:]
    pa_ref, pf_ref = outs[:2]
    shift = mod_ref[:, 0:D_MODEL]
    scale = mod_ref[:, D_MODEL:2 * D_MODEL]
    u = (x_ref[...] * (1.0 + scale) + shift).astype(BF16)
    pf = _dot(u, wf_ref[...])
    pf_ref[...] = pf
    if cache_mode:
        for o_ref, base in zip(outs[2:4], (F_DIFF_K, F_DIFF_V)):
            if cache_mode == "first":
                for l in range(1, DEPTH):
                    o_ref[l] = jnp.zeros(o_ref.shape[1:], F32)
                o_ref = o_ref.at[0]
            for h in range(DIFF_HEADS):
                o_ref[:, h, :] = pf[:, base + h * LANE:base + (h + 1) * LANE]
    for c0 in range(0, A_COLS, INPROJ_COL_STEP):
        cols = slice(c0, c0 + INPROJ_COL_STEP)
        pa_ref[:, cols] = _dot(u, wa_ref[:, cols]).astype(BF16)


def _inproj(x, mod, wa, wf, layer, row0, tokens_per_row, tm, cache_shape, prev_cache):
    n = x.shape[0]
    mod_row = lambda i: (layer, row0 + (i * tm) // tokens_per_row, 0, 0)
    out_specs = [
        pl.BlockSpec((tm, A_COLS), lambda i: (i, 0)),
        pl.BlockSpec((tm, F_COLS), lambda i: (i, 0)),
    ]
    out_shape = [
        jax.ShapeDtypeStruct((n, A_COLS), BF16),
        jax.ShapeDtypeStruct((n, F_COLS), F32),
    ]
    resident = lambda shape: pl.BlockSpec((None,) + shape, lambda i: (layer, 0, 0), pipeline_mode=pl.Buffered(1))
    in_specs = [
        pl.BlockSpec((tm, D_MODEL), lambda i: (i, 0)),
        pl.BlockSpec((None, None, 1, 6 * D_MODEL), mod_row),
        resident((D_MODEL, A_COLS)),
        resident((D_MODEL, F_COLS)),
    ]
    args = [x, mod, wa, wf]
    cache_mode, aliases = None, {}
    if cache_shape is not None:
        b, t = cache_shape
        assert tm == t
        out_shape += [jax.ShapeDtypeStruct((b, DEPTH, t, DIFF_HEADS, LANE), F32)] * 2
        if prev_cache is None:
            cache_mode = "first"
            out_specs += [pl.BlockSpec((None, DEPTH, t, DIFF_HEADS, LANE), lambda i: (i, 0, 0, 0, 0))] * 2
        else:
            cache_mode = "later"
            out_specs += [pl.BlockSpec((None, None, t, DIFF_HEADS, LANE), lambda i: (i, layer, 0, 0, 0))] * 2
            in_specs += [pl.BlockSpec(memory_space=pl.ANY)] * 2
            args += list(prev_cache)
            aliases = {4: 2, 5: 3}
    return pl.pallas_call(
        functools.partial(_inproj_kernel, cache_mode=cache_mode),
        grid=(n // tm,),
        in_specs=in_specs,
        out_specs=out_specs,
        out_shape=out_shape,
        input_output_aliases=aliases,
        compiler_params=_cparams(("arbitrary",)),
    )(*args)


def _conv_masks(shape, t_len):
    row = lax.broadcasted_iota(jnp.int32, shape, 0)
    half = SHORT_CONV // 2
    return {d: jnp.logical_and(row + d >= 0, row + d < t_len) for d in range(-half, half + 1) if d}


def _short_conv(x, w, t_len, masks):
    half = SHORT_CONV // 2
    acc = x * w[half:half + 1, :]
    for d, valid in masks.items():
        shifted = pltpu.roll(x, (-d) % t_len, axis=0)
        acc = acc + jnp.where(valid, shifted, 0.0) * w[half + d:half + d + 1, :]
    return acc


GDN_PAIRS = 2


def _solve_unit_tri(a_list, rhs_list):
    n = a_list[0].shape[0]
    assert n == 64
    eye = jnp.where(_tri(n, "ge") & _tri(n, "le"), 1.0, 0.0)
    a_b = [a.astype(BF16) for a in a_list]
    bf = lambda xs: [x.astype(BF16) for x in xs]
    mul = lambda xs, ys: [_dot(x, y) for x, y in zip(xs, ys)]
    pair = lambda xs, ys, xb, yb: [x + y + xy for x, y, xy in zip(xs, ys, mul(xb, yb))]
    p0 = [-a for a in a_list]
    p0b = [-a for a in a_b]
    p1 = mul(p0b, p0b)
    p1b = bf(p1)
    p2 = mul(p1b, p1b)
    n01 = pair(p0, p1, p0b, p1b)
    p2b = bf(p2)
    p3 = mul(p2b, p2b)
    p3b = bf(p3)
    p4 = mul(p3b, p3b)
    n23 = pair(p2, p3, p2b, p3b)
    p4b = bf(p4)
    p5 = mul(p4b, p4b)
    n03 = pair(n01, n23, bf(n01), bf(n23))
    p5b = bf(p5)
    n45 = pair(p4, p5, p4b, p5b)
    inv_b = [(eye + x).astype(BF16) for x in pair(n03, n45, bf(n03), bf(n45))]
    sol = [_dot(x, r.astype(BF16)) for x, r in zip(inv_b, rhs_list)]
    resid = [r - s - _dot(a, s.astype(BF16)) for r, s, a in zip(rhs_list, sol, a_b)]
    return [s + _dot(x, r.astype(BF16)) for s, x, r in zip(sol, inv_b, resid)]


def _gdn_kernel(*refs, t_len, has_state, emit_state):
    it = iter(refs)
    qkv_ref, z_ref, misc_ref, cw_ref, alog_ref, dtb_ref, g_ref = (next(it) for _ in range(7))
    s0_ref = next(it) if has_state else None
    if emit_state == "later":
        next(it)
    o_ref = next(it)
    sfin_ref = next(it) if emit_state else None
    (q_s, k_s, v_s, kt_s, bet, gat, gat_t, cumc, cumr, u_s, w_s, qd_s, a_s, ktl_s, gts, o_f, o_b,
     s_s) = (next(it) for _ in range(18))

    c = CHUNK
    n_chunks = t_len // c
    nh = GDN_HEADS
    dk = GDN_DK

    masks = _conv_masks((t_len, dk), t_len)
    for h in range(nh):
        hs = slice(h * dk, (h + 1) * dk)
        q, k, v = (
            _silu(_short_conv(qkv_ref[:, j * GDN_W + h * dk:j * GDN_W + (h + 1) * dk].astype(F32),
                              cw_ref[:, j * GDN_W + h * dk:j * GDN_W + (h + 1) * dk], t_len, masks))
            for j in range(3))
        q = q * lax.rsqrt(jnp.sum(q * q, axis=-1, keepdims=True) + EPS) * (dk ** -0.5)
        k = k * lax.rsqrt(jnp.sum(k * k, axis=-1, keepdims=True) + EPS)
        q_s[:, hs] = q
        k_s[:, hs] = k
        v_s[:, hs] = v
        kt_s[hs, :] = k.T.astype(BF16)
    misc = misc_ref[...]
    bet[...] = _sigmoid(misc)
    g_all = -jnp.exp(alog_ref[...]) * _softplus(misc + dtb_ref[...])
    gat[...] = g_all
    gat_t[...] = g_all.T[MISC_A:MISC_A + 2 * nh, :]
    for d in range(2):
        for h in range(nh):
            s_s[d * nh + h] = s0_ref[d, h] if has_state else jnp.zeros((dk, GDN_DV), F32)

    two = 2 * c
    r2 = lax.broadcasted_iota(jnp.int32, (two, two), 0)
    c2 = lax.broadcasted_iota(jnp.int32, (two, two), 1)
    same = (r2 >= c) == (c2 >= c)
    lo2 = jnp.where(jnp.logical_and(same, r2 >= c2), 1.0, 0.0).astype(BF16)
    up2 = jnp.where(jnp.logical_and(same, r2 <= c2), 1.0, 0.0).astype(BF16)
    incl = (_tri(c, "ge"), _tri(c, "le"))
    strict = (_tri(c, "gt"), _tri(c, "lt"))
    last = (c - 1, 0)

    def cumulate(cp, carry):
        r0 = pl.multiple_of(cp * two, two)
        g_blk = gat[pl.ds(r0, two), :]
        gt_blk = gat_t[:, pl.ds(r0, two)]
        cumc[0, pl.ds(r0, two), :] = _mm_mask_lhs(lo2, g_blk)
        cumc[1, pl.ds(r0, two), :] = _mm_mask_lhs(up2, g_blk)
        cumr[0, :, pl.ds(r0, two)] = _mm_mask_rhs(gt_blk, up2)
        cumr[1, :, pl.ds(r0, two)] = _mm_mask_rhs(gt_blk, lo2)
        return carry

    lax.fori_loop(0, n_chunks // 2, cumulate, 0)

    def phase1(step, carry):
        chains, a_list, rhs_list = [], [], []
        for pp in range(GDN_PAIRS):
            gather_chains(step * GDN_PAIRS + pp, chains, a_list, rhs_list)
        for (idx, rows), sol in zip(chains, _solve_unit_tri(a_list, rhs_list)):
            u_s[idx, rows, :] = sol[:, :GDN_DV]
            w_s[idx, rows, :] = sol[:, GDN_DV:].astype(BF16)
        return carry

    def gather_chains(cp, chains, a_list, rhs_list):
        r0 = pl.multiple_of(cp * two, two)
        b_blk = bet[pl.ds(r0, two), :]
        cum_col = (cumc[0, pl.ds(r0, two), :], cumc[1, pl.ds(r0, two), :])
        cum_row = (cumr[0, :, pl.ds(r0, two)], cumr[1, :, pl.ds(r0, two)])
        for h in range(nh):
            hs = slice(h * dk, (h + 1) * dk)
            kt_pair = kt_s[hs, pl.ds(r0, two)]
            for s in range(2):
                rows = pl.ds(pl.multiple_of(r0 + s * c, c), c)
                q_c = q_s[rows, hs]
                k_c = k_s[rows, hs]
                v_c = v_s[rows, hs]
                kt_c = kt_pair[:, s * c:(s + 1) * c]
                kk = _dot(k_c.astype(BF16), kt_c)
                qk = _dot(q_c.astype(BF16), kt_c)
                for d in range(2):
                    idx = d * nh + h
                    gc = cum_col[d][s * c:(s + 1) * c, MISC_A + idx:MISC_A + idx + 1]
                    gcr = cum_row[d][idx:idx + 1, s * c:(s + 1) * c]
                    beta = b_blk[s * c:(s + 1) * c, MISC_B + idx:MISC_B + idx + 1]
                    decay = jnp.where(incl[d], jnp.exp(jnp.where(incl[d], gc - gcr, 0.0)), 0.0)
                    e_gc = jnp.exp(gc)
                    a_list.append(jnp.where(strict[d], beta * kk * decay, 0.0))
                    rhs_list.append(jnp.concatenate([v_c * beta, k_c * (beta * e_gc)], axis=1))
                    a_s[idx, rows, :] = jnp.where(incl[d], qk * decay, 0.0).astype(BF16)
                    qd_s[idx, rows, :] = (q_c * e_gc).astype(BF16)
                    g_last = gcr[:, last[d]:last[d] + 1]
                    ktl_s[idx, cp * 2 + s] = (kt_c.astype(F32) * jnp.exp(g_last - gcr)).astype(BF16)
                    gts[cp * 2 + s, idx:idx + 1, :] = jnp.broadcast_to(jnp.exp(g_last), (1, LANE))
                    chains.append((idx, rows))

    lax.fori_loop(0, n_chunks // (2 * GDN_PAIRS), phase1, 0)

    def phase2(i, carry):
        chains = []
        for d, out in ((0, o_f), (1, o_b)):
            ci = i if d == 0 else n_chunks - 1 - i
            rows = pl.ds(pl.multiple_of(ci * c, c), c)
            g_tail = gts[ci]
            for h in range(nh):
                chains.append((d * nh + h, h, ci, rows, out, g_tail))
        s_f32 = [s_s[idx] for idx, *_ in chains]
        s_b = [s.astype(BF16) for s in s_f32]
        v_b = [(u_s[idx, rows, :] - _dot(w_s[idx, rows, :], sb)).astype(BF16)
               for (idx, _, _, rows, _, _), sb in zip(chains, s_b)]
        for (idx, h, ci, rows, out, g_tail), s, sb, vb in zip(chains, s_f32, s_b, v_b):
            s_s[idx] = s * g_tail[idx:idx + 1, :] + _dot(ktl_s[idx, ci], vb)
            out[rows, h * GDN_DV:(h + 1) * GDN_DV] = _dot(qd_s[idx, rows, :], sb) + _dot(
                a_s[idx, rows, :], vb)
        return carry

    lax.fori_loop(0, n_chunks, phase2, 0)

    for h in range(nh):
        hs = slice(h * GDN_DV, (h + 1) * GDN_DV)
        o = o_f[:, hs] + o_b[:, hs]
        o_ref[:, hs] = (_rms(o, g_ref[...]) * _silu(z_ref[:, hs].astype(F32))).astype(BF16)
    if emit_state:
        for d in range(2):
            for h in range(nh):
                _store_state(sfin_ref, emit_state, d, h, s_s[d * nh + h])


def _store_state(sfin_ref, mode, d, h, s):
    if mode == "first":
        sfin_ref[0, d, h] = s
        for l in range(1, DEPTH):
            sfin_ref[l, d, h] = jnp.zeros_like(s)
    else:
        sfin_ref[d, h] = s


def _state_output(b, layer, heads, dk, dv, prev, in_specs, args):
    shape = jax.ShapeDtypeStruct((b, DEPTH, 2, heads, dk, dv), F32)
    if prev is None:
        return "first", pl.BlockSpec((None, DEPTH, 2, heads, dk, dv), lambda bi: (bi, 0, 0, 0, 0, 0)), shape, {}
    in_specs.append(pl.BlockSpec(memory_space=pl.ANY))
    args.append(prev)
    spec = pl.BlockSpec((None, None, 2, heads, dk, dv), lambda bi: (bi, layer, 0, 0, 0, 0))
    return "later", spec, shape, {len(args) - 1: 1}


def _gdn(pa3, pf3, conv_w, alog_row, dtb_row, norm_g, s0, layer, emit_state, prev_states):
    b, t, _ = pa3.shape
    full = lambda shape: pl.BlockSpec(shape, lambda bi: (0,) * len(shape))
    in_specs = [
        pl.BlockSpec((None, t, 3 * GDN_W), lambda bi: (bi, 0, A_GDN_QKV // (3 * GDN_W))),
        pl.BlockSpec((None, t, GDN_W), lambda bi: (bi, 0, A_GDN_Z // GDN_W)),
        pl.BlockSpec((None, t, LANE), lambda bi: (bi, 0, F_MISC // LANE)),
        full((SHORT_CONV, 3 * GDN_W)),
        full((1, LANE)),
        full((1, LANE)),
        full((1, GDN_DV)),
    ]
    args = [pa3, pa3, pf3, conv_w, alog_row, dtb_row, norm_g]
    if s0 is not None:
        in_specs.append(pl.BlockSpec((None, None, 2, GDN_HEADS, GDN_DK, GDN_DV),
                                     lambda bi: (bi, layer, 0, 0, 0, 0)))
        args.append(s0)
    out_specs = [pl.BlockSpec((None, t, GDN_W), lambda bi: (bi, 0, 0))]
    out_shape = [jax.ShapeDtypeStruct((b, t, GDN_W), BF16)]
    mode, aliases = None, {}
    if emit_state:
        mode, spec, shape, aliases = _state_output(b, layer, GDN_HEADS, GDN_DK, GDN_DV, prev_states, in_specs, args)
        out_specs.append(spec)
        out_shape.append(shape)
    nc = t // CHUNK
    nd = 2 * GDN_HEADS
    wide = pltpu.VMEM((t, GDN_W), F32)
    scratch = [
        wide, wide, wide,
        pltpu.VMEM((GDN_W, t), BF16),
        pltpu.VMEM((t, LANE), F32), pltpu.VMEM((t, LANE), F32),
        pltpu.VMEM((nd, t), F32),
        pltpu.VMEM((2, t, LANE), F32), pltpu.VMEM((2, nd, t), F32),
        pltpu.VMEM((nd, t, GDN_DV), F32),
        pltpu.VMEM((nd, t, GDN_DK), BF16),
        pltpu.VMEM((nd, t, GDN_DK), BF16),
        pltpu.VMEM((nd, t, CHUNK), BF16),
        pltpu.VMEM((nd, nc, GDN_DK, CHUNK), BF16),
        pltpu.VMEM((nc, nd, LANE), F32),
        wide, wide,
        pltpu.VMEM((nd, GDN_DK, GDN_DV), F32),
    ]
    outs = pl.pallas_call(
        functools.partial(_gdn_kernel, t_len=t, has_state=s0 is not None, emit_state=mode),
        grid=(b,),
        in_specs=in_specs,
        out_specs=out_specs,
        out_shape=out_shape,
        scratch_shapes=scratch,
        input_output_aliases=aliases,
        compiler_params=_cparams(("arbitrary",)),
    )(*args)
    return outs if emit_state else (outs[0], None)


def _gla_kernel(*refs, t_len, has_state, emit_state):
    it = iter(refs)
    q_ref, k_ref, v_ref, r_ref, misc_ref, wg_ref, bg_ref, g_ref = (next(it) for _ in range(8))
    s0_ref = next(it) if has_state else None
    if emit_state == "later":
        next(it)
    o_ref = next(it)
    sfin_ref = next(it) if emit_state else None
    glog, vt_s, o_f, o_b, st = (next(it) for _ in range(5))

    c = CHUNK
    two = 2 * c
    n_pairs = t_len // two
    nh = GLA_HEADS
    misc = misc_ref[...]
    for d in range(2):
        lr = misc[:, MISC_LR + d * GLA_RANK:MISC_LR + (d + 1) * GLA_RANK]
        logits = _mm_hi(lr, wg_ref[d]) + bg_ref[d:d + 1, :]
        glog[d] = -_softplus(-logits) / GLA_TAU
    for h in range(nh):
        hs = slice(h * GLA_DV, (h + 1) * GLA_DV)
        vt_s[hs, :] = v_ref[:, hs].astype(F32).T.astype(BF16)
    for d in range(2):
        for h in range(nh):
            if has_state:
                s0 = jnp.concatenate([s0_ref[d, h], jnp.zeros((GLA_DV - GLA_DK, GLA_DV), F32)], axis=0)
                st[d * nh + h] = s0.T[:, :GLA_DK]
            else:
                st[d * nh + h] = jnp.zeros((GLA_DV, GLA_DK), F32)

    r2 = lax.broadcasted_iota(jnp.int32, (two, two), 0)
    c2 = lax.broadcasted_iota(jnp.int32, (two, two), 1)
    same = (r2 >= c) == (c2 >= c)
    cum_mask = (jnp.where(jnp.logical_and(same, r2 >= c2), 1.0, 0.0).astype(BF16),
                jnp.where(jnp.logical_and(same, r2 <= c2), 1.0, 0.0).astype(BF16))
    incl = (_tri(c, "ge"), _tri(c, "le"))
    last = (c - 1, 0)
    mid = (c // 2, c - 1 - c // 2)
    order = ((0, 1), (1, 0))

    def cumulate(cp, carry):
        r0 = pl.multiple_of(cp * two, two)
        for d in range(2):
            glog[d, pl.ds(r0, two), :] = _mm_mask_lhs(cum_mask[d], glog[d, pl.ds(r0, two), :])
        return carry

    lax.fori_loop(0, n_pairs, cumulate, 0)

    def body(i, carry):
        pieces = []
        for d in range(2):
            r0 = pl.multiple_of((i if d == 0 else n_pairs - 1 - i) * two, two)
            b_all = glog[d, pl.ds(r0, two), :]
            q_all = q_ref[pl.ds(r0, two), :].astype(F32) * (GLA_DK ** -0.5)
            k_all = k_ref[pl.ds(r0, two), :].astype(F32)
            vt_pair = [vt_s[h * GLA_DV:(h + 1) * GLA_DV, pl.ds(r0, two)] for h in range(nh)]
            for s in range(2):
                rs = slice(s * c, (s + 1) * c)
                bq = b_all[rs]
                b_mid = bq[mid[d]:mid[d] + 1, :]
                b_last = bq[last[d]:last[d] + 1, :]
                qe = (q_all[rs] * jnp.exp(bq - b_mid)).astype(BF16)
                ke = (k_all[rs] * jnp.exp(b_mid - bq)).astype(BF16)
                qd = (q_all[rs] * jnp.exp(bq)).astype(BF16)
                kt = (k_all[rs] * jnp.exp(b_last - bq)).astype(BF16)
                g_last = jnp.exp(b_last)
                rows = pl.ds(pl.multiple_of(r0 + s * c, c), c)
                for h in range(nh):
                    ks = slice(h * GLA_DK, (h + 1) * GLA_DK)
                    pieces.append((d, s, h, rows, qe[:, ks], ke[:, ks], qd[:, ks], kt[:, ks], g_last[:, ks],
                                   vt_pair[h][:, rs], v_ref[rows, h * GLA_DV:(h + 1) * GLA_DV]))
        a_intra = [jnp.where(incl[p[0]], _mm_nt(p[4], p[5]), 0.0).astype(BF16) for p in pieces]
        upd = [_dot(p[9], p[7]) for p in pieces]
        intra = [_dot(a, p[10]) for a, p in zip(a_intra, pieces)]
        for step in range(2):
            for p, m, o_in in zip(pieces, upd, intra):
                d, s, h, rows = p[0], p[1], p[2], p[3]
                if s != order[d][step]:
                    continue
                s_t = st[d * nh + h]
                out = o_f if d == 0 else o_b
                out[rows, h * GLA_DV:(h + 1) * GLA_DV] = _mm_nt(p[6], s_t) + o_in
                st[d * nh + h] = s_t * p[8] + m
        return carry

    lax.fori_loop(0, n_pairs, body, 0)

    for h in range(nh):
        hs = slice(h * GLA_DV, (h + 1) * GLA_DV)
        o = o_f[:, hs] + o_b[:, hs]
        o_ref[:, hs] = (_rms(o, g_ref[...]) * _silu(r_ref[:, hs].astype(F32))).astype(BF16)
    if emit_state:
        for d in range(2):
            for h in range(nh):
                s_pad = jnp.concatenate([st[d * nh + h], jnp.zeros((GLA_DV, GLA_DV - GLA_DK), F32)], axis=1)
                _store_state(sfin_ref, emit_state, d, h, s_pad.T[:GLA_DK, :])


def _gla(pa3, pf3, wg, bg, norm_g, s0, layer, emit_state, prev_states):
    b, t, _ = pa3.shape
    full = lambda shape: pl.BlockSpec(shape, lambda bi: (0,) * len(shape))
    in_specs = [
        pl.BlockSpec((None, t, GLA_KW), lambda bi: (bi, 0, A_GLA_Q // GLA_KW)),
        pl.BlockSpec((None, t, GLA_KW), lambda bi: (bi, 0, A_GLA_K // GLA_KW)),
        pl.BlockSpec((None, t, GLA_VW), lambda bi: (bi, 0, A_GLA_V // GLA_VW)),
        pl.BlockSpec((None, t, GLA_VW), lambda bi: (bi, 0, A_GLA_R // GLA_VW)),
        pl.BlockSpec((None, t, LANE), lambda bi: (bi, 0, F_MISC // LANE)),
        full((2, GLA_RANK, GLA_KW)),
        full((2, GLA_KW)),
        full((1, GLA_DV)),
    ]
    args = [pa3, pa3, pa3, pa3, pf3, wg, bg, norm_g]
    if s0 is not None:
        in_specs.append(pl.BlockSpec((None, None, 2, GLA_HEADS, GLA_DK, GLA_DV),
                                     lambda bi: (bi, layer, 0, 0, 0, 0)))
        args.append(s0)
    out_specs = [pl.BlockSpec((None, t, GLA_VW), lambda bi: (bi, 0, 0))]
    out_shape = [jax.ShapeDtypeStruct((b, t, GLA_VW), BF16)]
    mode, aliases = None, {}
    if emit_state:
        mode, spec, shape, aliases = _state_output(b, layer, GLA_HEADS, GLA_DK, GLA_DV, prev_states, in_specs, args)
        out_specs.append(spec)
        out_shape.append(shape)
    scratch = [pltpu.VMEM((2, t, GLA_KW), F32),
               pltpu.VMEM((GLA_VW, t), BF16),
               pltpu.VMEM((t, GLA_VW), F32), pltpu.VMEM((t, GLA_VW), F32),
               pltpu.VMEM((2 * GLA_HEADS, GLA_DV, GLA_DK), F32)]
    outs = pl.pallas_call(
        functools.partial(_gla_kernel, t_len=t, has_state=s0 is not None, emit_state=mode),
        grid=(b,),
        in_specs=in_specs,
        out_specs=out_specs,
        out_shape=out_shape,
        scratch_shapes=scratch,
        input_output_aliases=aliases,
        compiler_params=_cparams(("arbitrary",)),
    )(*args)
    return outs if emit_state else (outs[0], None)


ATT_Q_BLOCK = 256


def _rope_tables(t_len):
    half = DIFF_DH // 2
    quarter = half // 2
    inv = ROPE_THETA ** (-np.arange(0, half, 2, dtype=np.float64) / half)
    tok = np.arange(t_len)
    pos = np.stack([tok // GRID_W, tok % GRID_W], axis=1).astype(np.float64)
    ang = pos[:, :, None] * inv[None, None, :]
    cos = np.concatenate([np.cos(ang), np.cos(ang)], axis=-1).reshape(t_len, DIFF_DH)
    sin = np.concatenate([-np.sin(ang), np.sin(ang)], axis=-1).reshape(t_len, DIFF_DH)
    cos = np.concatenate([cos, cos], axis=-1).astype(np.float32)
    sin = np.concatenate([sin, sin], axis=-1).astype(np.float32)
    first = ((np.arange(2 * DIFF_DH) % half) < quarter).astype(np.float32)[None, :]
    return jnp.asarray(cos), jnp.asarray(sin), jnp.asarray(first), quarter


def _rope(x, cos, sin, first, quarter):
    width = x.shape[-1]
    ahead = pltpu.roll(x, width - quarter, axis=1)
    behind = pltpu.roll(x, quarter, axis=1)
    partner = jnp.where(first > 0.5, ahead, behind)
    return x * cos + partner * sin


def _diff_kernel(*refs, t_len, ctx_len, lam_init):
    it = iter(refs)
    q_ref, k_ref, v_ref, lam_ref, g_ref = (next(it) for _ in range(5))
    if ctx_len:
        ck_ref, cv_ref, cos_ref, sin_ref, first_ref = (next(it) for _ in range(5))
    o_ref = next(it)
    q_sc, k_sc, v_sc = (next(it) for _ in range(3))

    scale = DIFF_DH ** -0.5
    for h in range(DIFF_HEADS):
        hs = slice(h * LANE, (h + 1) * LANE)
        q = q_ref[:, hs].astype(F32)
        k = k_ref[:, hs]
        if ctx_len:
            quarter = DIFF_DH // 4
            q = _rope(q, cos_ref[...], sin_ref[...], first_ref[...], quarter)
            k = _rope(k, cos_ref[...], sin_ref[...], first_ref[...], quarter)
        q_sc[:, hs] = (q * scale).astype(BF16)
        k_sc[0:t_len, hs] = k.astype(BF16)
    v_sc[0:t_len, :] = v_ref[...].astype(BF16)
    if ctx_len:
        k_sc[t_len:t_len + ctx_len, :] = ck_ref[...].astype(BF16)
        v_sc[t_len:t_len + ctx_len, :] = cv_ref[...].astype(BF16)

    lp = lam_ref[...]
    lam = (jnp.exp(jnp.sum(lp[0:1, :] * lp[1:2, :], axis=1, keepdims=True))
           - jnp.exp(jnp.sum(lp[2:3, :] * lp[3:4, :], axis=1, keepdims=True)) + lam_init)

    tq = min(ATT_Q_BLOCK, t_len)
    group = DIFF_HEADS if t_len + ctx_len <= 512 else 2

    def body(i, carry):
        r0 = pl.multiple_of(i * tq, tq)
        for h0 in range(0, DIFF_HEADS, group):
            parts = [(h, half) for h in range(h0, h0 + group) for half in range(2)]
            cols = [slice(h * LANE + half * DIFF_DH, h * LANE + (half + 1) * DIFF_DH) for h, half in parts]
            s = [_mm_nt(q_sc[pl.ds(r0, tq), c], k_sc[:, c]) for c in cols]
            e = [jnp.exp(x - jnp.max(x, axis=-1, keepdims=True)) for x in s]
            inv_l = [1.0 / jnp.sum(x, axis=-1, keepdims=True) for x in e]
            pv = [_dot(x.astype(BF16), v_sc[:, h * LANE:(h + 1) * LANE]) * r
                  for x, r, (h, _) in zip(e, inv_l, parts)]
            for j in range(group):
                h = h0 + j
                o = pv[2 * j] - lam * pv[2 * j + 1]
                o_ref[pl.ds(r0, tq), h * LANE:(h + 1) * LANE] = (
                    _rms(o, g_ref[...]) * (1.0 - lam_init)).astype(BF16)
        return carry

    lax.fori_loop(0, t_len // tq, body, 0)


def _diff(pa3, pf3, lam_p, norm_g, ctx_k, ctx_v, layer, lam_init):
    b, t, _ = pa3.shape
    ctx_len = 0 if ctx_k is None else ctx_k.shape[2]
    full = lambda shape: pl.BlockSpec(shape, lambda bi: (0,) * len(shape))
    in_specs = [
        pl.BlockSpec((None, t, DIFF_QW), lambda bi: (bi, 0, A_DIFF_Q // DIFF_QW)),
        pl.BlockSpec((None, t, DIFF_QW), lambda bi: (bi, 0, F_DIFF_K // DIFF_QW)),
        pl.BlockSpec((None, t, DIFF_VW), lambda bi: (bi, 0, F_DIFF_V // DIFF_VW)),
        full((4, DIFF_DH)),
        full((1, DIFF_VD)),
    ]
    args = [pa3, pf3, pf3, lam_p, norm_g]
    if ctx_len:
        cos, sin, first, _ = _rope_tables(t)
        in_specs += [
            pl.BlockSpec((None, None, ctx_len, DIFF_QW), lambda bi: (bi, layer, 0, 0)),
            pl.BlockSpec((None, None, ctx_len, DIFF_VW), lambda bi: (bi, layer, 0, 0)),
            full((t, LANE)),
            full((t, LANE)),
            full((1, LANE)),
        ]
        args += [ctx_k, ctx_v, cos, sin, first]
    tk = t + ctx_len
    return pl.pallas_call(
        functools.partial(_diff_kernel, t_len=t, ctx_len=ctx_len, lam_init=lam_init),
        grid=(b,),
        in_specs=in_specs,
        out_specs=pl.BlockSpec((None, t, DIFF_VW), lambda bi: (bi, 0, 0)),
        out_shape=jax.ShapeDtypeStruct((b, t, DIFF_VW), BF16),
        scratch_shapes=[pltpu.VMEM((t, DIFF_QW), BF16), pltpu.VMEM((tk, DIFF_QW), BF16),
                        pltpu.VMEM((tk, DIFF_VW), BF16)],
        compiler_params=_cparams(("arbitrary",)),
    )(*args)


def _route(logits_t, bias_col):
    scores = _sigmoid(logits_t)
    biased = scores + bias_col
    rows = [biased[e:e + 1, :] for e in range(N_EXPERTS)]
    grp = []
    for g in range(N_GROUPS):
        a0, a1, a2, a3 = rows[g * GROUP_SIZE:(g + 1) * GROUP_SIZE]
        hi01, lo01 = jnp.maximum(a0, a1), jnp.minimum(a0, a1)
        hi23, lo23 = jnp.maximum(a2, a3), jnp.minimum(a2, a3)
        top1 = jnp.maximum(hi01, hi23)
        top2 = jnp.maximum(jnp.minimum(hi01, hi23), jnp.maximum(lo01, lo23))
        grp.append(top1 + top2)
    best = []
    for g in range(N_GROUPS):
        win = None
        for o in range(N_GROUPS):
            if o == g:
                continue
            cond = grp[g] > grp[o] if o < g else grp[g] >= grp[o]
            win = cond if win is None else jnp.logical_and(win, cond)
        best.append(win)
    sel_rows = []
    for e in range(N_EXPERTS):
        g = e // GROUP_SIZE
        beaten = jnp.zeros_like(rows[e])
        for o in range(g * GROUP_SIZE, (g + 1) * GROUP_SIZE):
            if o == e:
                continue
            ahead = rows[o] >= rows[e] if o < e else rows[o] > rows[e]
            beaten = beaten + jnp.where(ahead, 1.0, 0.0)
        sel_rows.append(jnp.where(jnp.logical_and(best[g], beaten < 1.5), 1.0, 0.0))
    sel = jnp.concatenate(sel_rows, axis=0)
    picked = sel * scores
    return sel, picked / jnp.sum(picked, axis=0, keepdims=True)


PIECE = 16
ROW_TILE = 256
SLOT_PAD = N_EXPERTS * PIECE
HS_COLS = D_MODEL + LANE


def _route_meta(sel, wts):
    n_e, t = sel.shape
    e_col = lax.broadcasted_iota(jnp.int32, (n_e, 1), 0).astype(F32)
    before = jnp.where(_tri(t, "lt"), 1.0, 0.0).astype(BF16)
    rank = _dot(sel.astype(BF16), before)
    cnt = jnp.sum(sel, axis=1, keepdims=True)
    plen = jnp.floor((cnt + (PIECE - 1.0)) * (1.0 / PIECE)) * PIECE
    lower = jnp.where(_tri(n_e, "gt"), 1.0, 0.0).astype(BF16)
    loff = _dot(lower, jnp.broadcast_to(plen, (n_e, LANE)).astype(BF16))[:, 0:1]
    slot = loff + rank
    chosen = sel > 0.5
    e_a = jnp.min(jnp.where(chosen, e_col, 2.0 * n_e), axis=0, keepdims=True)
    e_b = jnp.max(jnp.where(chosen, e_col, -1.0), axis=0, keepdims=True)
    take = lambda e_row, x: jnp.sum(jnp.where(e_col == e_row, x, 0.0), axis=0, keepdims=True)
    rows = jnp.concatenate([take(e_a, slot), take(e_b, slot), take(e_a, wts), take(e_b, wts), e_a, e_b,
                            jnp.zeros((2, t), F32)], axis=0)
    return rows, plen, loff


def _merge_kernel(og_ref, ol_ref, od_ref, lg0_ref, lg1_ref, lg2_ref, x_ref, mod_ref, wb_ref, wo_ref,
                  lng_ref, lnb_ref, rw_ref, rb_ref, x1_ref, h_ref, mrow_ref, mcol_ref, plen_ref, loff_ref):
    gate1 = mod_ref[:, 2 * D_MODEL:3 * D_MODEL]
    shift2 = mod_ref[:, 3 * D_MODEL:4 * D_MODEL]
    scale2 = mod_ref[:, 4 * D_MODEL:5 * D_MODEL]
    tm = x_ref.shape[0]
    halves = [pl.ds(r * (tm // 2), tm // 2) for r in range(2)]
    accs = [None, None]
    for s, (o_ref, lg_ref) in enumerate(((og_ref, lg0_ref), (ol_ref, lg1_ref), (od_ref, lg2_ref))):
        projs = [_dot(o_ref[rows, :], wb_ref[s]) for rows in halves]
        terms = [_sigmoid(lg_ref[rows, :].astype(F32)) * p for rows, p in zip(halves, projs)]
        accs = [t if a is None else a + t for a, t in zip(accs, terms)]
    ms = [_mm(a, wo_ref[...]) for a in accs]
    logit_parts = []
    for rows, m in zip(halves, ms):
        x1 = _layernorm(ALPHA * x_ref[rows, :] + gate1 * m, lng_ref[...], lnb_ref[...])
        x1_ref[rows, :] = x1
        h = x1 * (1.0 + scale2) + shift2
        h_ref[rows, :] = h.astype(BF16)
        logit_parts.append(_mm_hi_nt(rw_ref[...], h))
    sel, gates_t = _route(jnp.concatenate(logit_parts, axis=1), rb_ref[...])
    rows, plen, loff = _route_meta(sel, gates_t)
    mrow_ref[...] = rows
    mcol_ref[...] = jnp.concatenate([rows, jnp.zeros((LANE - rows.shape[0], tm), F32)], axis=0).T
    lane = lax.broadcasted_iota(jnp.int32, (N_EXPERTS, LANE), 1)
    cols = jnp.where(lane == 0, plen, jnp.where(lane == 1, loff, 0.0))
    cols_t = jnp.concatenate([cols, jnp.zeros((LANE - N_EXPERTS, LANE), F32)], axis=0).T
    plen_ref[...] = cols_t[0:1, :]
    loff_ref[...] = cols_t[1:2, :]


def _merge(o_gdn, o_gla, o_diff, pa, x, mod, wb, wo, ln_g, ln_b, rw_t, rb_col, layer, row0,
           tokens_per_row, tm):
    n = x.shape[0]
    mod_row = lambda i: (layer, row0 + (i * tm) // tokens_per_row, 0, 0)
    full = lambda shape: pl.BlockSpec(shape, lambda i: (0,) * len(shape))
    return pl.pallas_call(
        _merge_kernel,
        grid=(n // tm,),
        in_specs=[
            pl.BlockSpec((tm, MIX_W), lambda i: (i, 0)),
            pl.BlockSpec((tm, MIX_W), lambda i: (i, 0)),
            pl.BlockSpec((tm, MIX_W), lambda i: (i, 0)),
            pl.BlockSpec((tm, D_MODEL), lambda i: (i, A_MERGE // D_MODEL)),
            pl.BlockSpec((tm, D_MODEL), lambda i: (i, A_MERGE // D_MODEL + 1)),
            pl.BlockSpec((tm, D_MODEL), lambda i: (i, A_MERGE // D_MODEL + 2)),
            pl.BlockSpec((tm, D_MODEL), lambda i: (i, 0)),
            pl.BlockSpec((None, None, 1, 6 * D_MODEL), mod_row),
            full((N_BRANCH, MIX_W, D_MODEL)),
            full((D_MODEL, D_MODEL)),
            full((1, D_MODEL)),
            full((1, D_MODEL)),
            full((N_EXPERTS, D_MODEL)),
            full((N_EXPERTS, 1)),
        ],
        out_specs=[
            pl.BlockSpec((tm, D_MODEL), lambda i: (i, 0)),
            pl.BlockSpec((tm, D_MODEL), lambda i: (i, 0)),
            pl.BlockSpec((8, tm), lambda i: (0, i)),
            pl.BlockSpec((tm, LANE), lambda i: (i, 0)),
            pl.BlockSpec((None, 1, LANE), lambda i: (i, 0, 0)),
            pl.BlockSpec((None, 1, LANE), lambda i: (i, 0, 0)),
        ],
        out_shape=[
            jax.ShapeDtypeStruct((n, D_MODEL), F32),
            jax.ShapeDtypeStruct((n, D_MODEL), BF16),
            jax.ShapeDtypeStruct((8, n), F32),
            jax.ShapeDtypeStruct((n, LANE), F32),
            jax.ShapeDtypeStruct((n // tm, 1, LANE), F32),
            jax.ShapeDtypeStruct((n // tm, 1, LANE), F32),
        ],
        compiler_params=_cparams(("arbitrary",)),
    )(o_gdn, o_gla, o_diff, pa, pa, pa, x, mod, wb, wo, ln_g, ln_b, rw_t, rb_col)


def _plan_kernel(plen_ref, loff_ref, goff_ref, npc_ref, loffi_ref, tiles_ref, tail_ref, *, n_steps):
    plen = plen_ref[...]
    nb = plen.shape[0]
    earlier = jnp.where(_tri(nb, "gt"), 1.0, 0.0).astype(BF16)
    run = _dot(earlier, plen.astype(BF16))
    gtot = jnp.sum(plen, axis=0, keepdims=True)
    lane = lax.broadcasted_iota(jnp.int32, (1, LANE), 1).astype(F32)
    ntile = jnp.floor((gtot + (ROW_TILE - 1.0)) * (1.0 / ROW_TILE))
    before = jnp.where(_tri(LANE, "lt"), 1.0, 0.0).astype(BF16)
    first = _dot(jnp.broadcast_to(ntile, (8, LANE)).astype(BF16), before)[0:1, :]
    end = first + ntile
    n_used = jnp.sum(ntile, axis=1, keepdims=True)
    goff_ref[...] = (first * ROW_TILE + run).astype(jnp.int32)
    npc_ref[...] = (plen * (1.0 / PIECE)).astype(jnp.int32)
    loffi_ref[...] = loff_ref[...].astype(jnp.int32)
    end_col = jnp.concatenate([end, jnp.zeros((LANE - 1, LANE), F32)], axis=0).T[:, 0:1]
    e_col = lax.broadcasted_iota(jnp.int32, (LANE, 1), 0).astype(F32)
    tile = jnp.minimum(lane, n_used - 1.0)
    done = jnp.logical_and(end_col <= tile, e_col < N_EXPERTS)
    t_exp = jnp.minimum(jnp.sum(jnp.where(done, 1.0, 0.0), axis=0, keepdims=True), N_EXPERTS - 1.0)
    nonempty = jnp.where(ntile > 0.5, 1.0, 0.0)
    run_idx = _dot(jnp.broadcast_to(nonempty, (8, LANE)).astype(BF16), before)[0:1, :]
    owner_after = jnp.logical_and(end_col <= end, e_col < N_EXPERTS)
    nxt = jnp.minimum(jnp.sum(jnp.where(owner_after, 1.0, 0.0), axis=0, keepdims=True), N_EXPERTS - 1.0)
    has_next = jnp.where(end < n_used, 1.0, 0.0)
    by_expert = jnp.concatenate([run_idx - 2.0 * jnp.floor(run_idx * 0.5), nxt, has_next,
                                 jnp.zeros((LANE - 3, LANE), F32)], axis=0).T
    mine = e_col == t_exp
    per_tile = [jnp.sum(jnp.where(mine, by_expert[:, j:j + 1], 0.0), axis=0, keepdims=True) for j in range(3)]
    tiles_ref[...] = jnp.concatenate(
        [t_exp, jnp.broadcast_to(n_used, (1, LANE))] + per_tile + [jnp.zeros((3, LANE), F32)],
        axis=0).astype(jnp.int32)
    tail_ref[...] = jnp.concatenate(
        [first * ROW_TILE + gtot, (ntile * ROW_TILE - gtot) * (1.0 / PIECE),
         jnp.broadcast_to(n_used, (1, LANE)), jnp.broadcast_to(n_steps - n_used, (1, LANE)),
         jnp.zeros((4, LANE), F32)], axis=0).astype(jnp.int32)


def _plan(plen, loff, n_steps):
    nb = plen.shape[0]
    i32 = lambda rows: jax.ShapeDtypeStruct((rows, LANE), jnp.int32)
    return pl.pallas_call(
        functools.partial(_plan_kernel, n_steps=n_steps),
        out_shape=[i32(nb), i32(nb), i32(nb), i32(8), i32(8)],
    )(plen, loff)


def _piece_copy(local, remote, sem, to_remote, lo, go):
    src = local.at[pl.ds(pl.multiple_of(lo, PIECE), PIECE)]
    dst = remote.at[pl.ds(pl.multiple_of(go, PIECE), PIECE)]
    return pltpu.make_async_copy(src, dst, sem) if to_remote else pltpu.make_async_copy(dst, src, sem)


def _start_pieces(b, loff_s, npc_s, goff_s, local, remote, sem, to_remote):
    for e in range(N_EXPERTS):
        lo = loff_s[b, e]
        go = goff_s[b, e]

        def start(j, carry, lo=lo, go=go):
            _piece_copy(local, remote, sem, to_remote, lo + j * PIECE, go + j * PIECE).start()
            return carry

        lax.fori_loop(0, npc_s[b, e], start, 0)


def _wait_pieces(b, npc_s, local, remote, sem, to_remote):
    total = 0
    for e in range(N_EXPERTS):
        total = total + npc_s[b, e]

    def wait(j, carry):
        _piece_copy(local, remote, sem, to_remote, 0, 0).wait()
        return carry

    lax.fori_loop(0, total, wait, 0)


def _dispatch_kernel(loff_s, npc_s, goff_s, tail_s, h_ref, mrow_ref, mcol_ref, hs_hbm, local, zeros, sems,
                     tail_sem, tile_sem, *, slots):
    b = pl.program_id(0)
    last = pl.num_programs(0) - 1
    slot = b % 2
    buf = local.at[slot]
    sem = sems.at[slot]

    @pl.when(b >= 2)
    def _():
        _wait_pieces(b - 2, npc_s, buf, hs_hbm, sem, True)

    tb = h_ref.shape[0]
    mrow = mrow_ref[...]
    mcol = mcol_ref[...]
    slot_id = lax.broadcasted_iota(jnp.int32, (slots, tb), 0).astype(F32)
    s_a = jnp.where(slot_id == mrow[0:1, :], 1.0, 0.0).astype(BF16)
    s_b = jnp.where(slot_id == mrow[1:2, :], 1.0, 0.0).astype(BF16)
    lane = lax.broadcasted_iota(jnp.int32, (tb, LANE), 1)

    def weight_lanes(w):
        hi = w.astype(BF16).astype(F32)
        return jnp.where(lane == 0, hi, jnp.where(lane == 1, w - hi, 0.0)).astype(BF16)

    local[slot, :, :D_MODEL] = _dot(s_a + s_b, h_ref[...]).astype(BF16)
    local[slot, :, D_MODEL:] = (_dot(s_a, weight_lanes(mcol[:, 2:3]))
                                + _dot(s_b, weight_lanes(mcol[:, 3:4]))).astype(BF16)
    _start_pieces(b, loff_s, npc_s, goff_s, buf, hs_hbm, sem, True)

    @pl.when(b == last)
    def _():
        _wait_pieces(b, npc_s, buf, hs_hbm, sem, True)

        @pl.when(b >= 1)
        def _():
            _wait_pieces(b - 1, npc_s, local.at[1 - slot], hs_hbm, sems.at[1 - slot], True)

        zeros[...] = jnp.zeros_like(zeros)
        zero_piece = zeros.at[pl.ds(0, PIECE)]
        total = 0
        for e in range(N_EXPERTS):
            n = tail_s[1, e]
            start_row = tail_s[0, e]

            def start(j, carry, start_row=start_row):
                dst = hs_hbm.at[pl.ds(pl.multiple_of(start_row + j * PIECE, PIECE), PIECE)]
                pltpu.make_async_copy(zero_piece, dst, tail_sem).start()
                return carry

            lax.fori_loop(0, n, start, 0)
            total = total + n

        def wait(j, carry):
            pltpu.make_async_copy(zero_piece, hs_hbm.at[pl.ds(0, PIECE)], tail_sem).wait()
            return carry

        lax.fori_loop(0, total, wait, 0)

        def tile_copy(j):
            row = pl.multiple_of((tail_s[2, 0] + j) * ROW_TILE, ROW_TILE)
            return pltpu.make_async_copy(zeros, hs_hbm.at[pl.ds(row, ROW_TILE)], tile_sem)

        def start_tile(j, carry):
            tile_copy(j).start()
            return carry

        def wait_tile(j, carry):
            tile_copy(j).wait()
            return carry

        lax.fori_loop(0, tail_s[3, 0], start_tile, 0)
        lax.fori_loop(0, tail_s[3, 0], wait_tile, 0)


def _dispatch(loff_i, npc, goff, tail, h, mrow, mcol, tb, n_steps):
    n = h.shape[0]
    slots = 2 * tb + SLOT_PAD
    grid_spec = pltpu.PrefetchScalarGridSpec(
        num_scalar_prefetch=4,
        grid=(n // tb,),
        in_specs=[
            pl.BlockSpec((tb, D_MODEL), lambda i, *_: (i, 0)),
            pl.BlockSpec((8, tb), lambda i, *_: (0, i)),
            pl.BlockSpec((tb, LANE), lambda i, *_: (i, 0)),
        ],
        out_specs=pl.BlockSpec(memory_space=pl.ANY),
        scratch_shapes=[pltpu.VMEM((2, slots, HS_COLS), BF16), pltpu.VMEM((ROW_TILE, HS_COLS), BF16),
                        pltpu.SemaphoreType.DMA((2,)), pltpu.SemaphoreType.DMA, pltpu.SemaphoreType.DMA],
    )
    return pl.pallas_call(
        functools.partial(_dispatch_kernel, slots=slots),
        grid_spec=grid_spec,
        out_shape=jax.ShapeDtypeStruct((n_steps * ROW_TILE, HS_COLS), BF16),
        compiler_params=_cparams(("arbitrary",)),
    )(loff_i, npc, goff, tail, h, mrow, mcol)


def _expert_kernel(tiles_s, hs_ref, wg_hbm, wu_hbm, wd_hbm, y_ref, wg_f, wu_f, wd_f, wg_b, wu_b, wd_b, sems,
                   *, layer):
    i = pl.program_id(0)
    e = tiles_s[0, i]
    slot = tiles_s[2, i]
    first_of_run = jnp.logical_or(i == 0, e != tiles_s[0, jnp.maximum(i - 1, 0)])

    def weight_copies(expert, s):
        return [pltpu.make_async_copy(src.at[layer, expert], dst.at[s], sems.at[s, j])
                for j, (src, dst) in enumerate(((wg_hbm, wg_f), (wu_hbm, wu_f), (wd_hbm, wd_f)))]

    @pl.when(i == 0)
    def _():
        for cp in weight_copies(e, slot):
            cp.start()

    @pl.when(jnp.logical_and(first_of_run, i < tiles_s[1, 0]))
    def _():
        for cp in weight_copies(e, slot):
            cp.wait()

        @pl.when(tiles_s[4, i] > 0)
        def _():
            for cp in weight_copies(tiles_s[3, i], 1 - slot):
                cp.start()

        wg_b[...] = wg_f[slot].astype(BF16)
        wu_b[...] = wu_f[slot].astype(BF16)
        wd_b[...] = wd_f[slot].astype(BF16)

    @pl.when(i < tiles_s[1, 0])
    def _():
        halves = [pl.ds(r * (ROW_TILE // 2), ROW_TILE // 2) for r in range(2)]
        xs = [hs_ref[rows, :D_MODEL] for rows in halves]
        ws = [hs_ref[rows, D_MODEL:D_MODEL + 1].astype(F32) + hs_ref[rows, D_MODEL + 1:D_MODEL + 2].astype(F32)
              for rows in halves]
        gs = [_dot(x, wg_b[...]) for x in xs]
        us = [_dot(x, wu_b[...]) for x in xs]
        acts = [(_silu(g) * u * w).astype(BF16) for g, u, w in zip(gs, us, ws)]
        for rows, act in zip(halves, acts):
            y_ref[rows, :] = _dot(act, wd_b[...]).astype(BF16)

    @pl.when(i >= tiles_s[1, 0])
    def _():
        y_ref[...] = jnp.zeros_like(y_ref)


def _experts(tiles, hs, wg, wu, wd, layer, n_steps):
    grid_spec = pltpu.PrefetchScalarGridSpec(
        num_scalar_prefetch=1,
        grid=(n_steps,),
        in_specs=[
            pl.BlockSpec((ROW_TILE, HS_COLS), lambda i, t: (i, 0)),
            pl.BlockSpec(memory_space=pl.ANY),
            pl.BlockSpec(memory_space=pl.ANY),
            pl.BlockSpec(memory_space=pl.ANY),
        ],
        out_specs=pl.BlockSpec((ROW_TILE, D_MODEL), lambda i, t: (i, 0)),
        scratch_shapes=[pltpu.VMEM((2, D_MODEL, D_FF), F32), pltpu.VMEM((2, D_MODEL, D_FF), F32),
                        pltpu.VMEM((2, D_FF, D_MODEL), F32),
                        pltpu.VMEM((D_MODEL, D_FF), BF16), pltpu.VMEM((D_MODEL, D_FF), BF16),
                        pltpu.VMEM((D_FF, D_MODEL), BF16), pltpu.SemaphoreType.DMA((2, 3))],
    )
    return pl.pallas_call(
        functools.partial(_expert_kernel, layer=layer),
        grid_spec=grid_spec,
        out_shape=jax.ShapeDtypeStruct((hs.shape[0], D_MODEL), BF16),
        compiler_params=_cparams(("arbitrary",)),
    )(tiles, hs, wg, wu, wd)


def _combine_kernel(loff_s, npc_s, goff_s, y_hbm, mcol_ref, x1_ref, mod_ref, lng_ref, lnb_ref, o_ref,
                    local, sems, *, slots):
    b = pl.program_id(0)
    tb = x1_ref.shape[0]
    slot = b % 2

    def fetch(blk, s):
        local[s] = jnp.zeros(local.shape[1:], local.dtype)
        _start_pieces(blk, loff_s, npc_s, goff_s, local.at[s], y_hbm, sems.at[s], False)

    @pl.when(b == 0)
    def _():
        fetch(0, 0)

    _wait_pieces(b, npc_s, local.at[slot], y_hbm, sems.at[slot], False)

    @pl.when(b + 1 < pl.num_programs(0))
    def _():
        fetch(b + 1, 1 - slot)

    mcol = mcol_ref[...]
    slot_id = lax.broadcasted_iota(jnp.int32, (tb, slots), 1).astype(F32)
    pick = jnp.logical_or(slot_id == mcol[:, 0:1], slot_id == mcol[:, 1:2])
    f = _dot(jnp.where(pick, 1.0, 0.0).astype(BF16), local[slot])
    gate2 = mod_ref[:, 5 * D_MODEL:6 * D_MODEL]
    o_ref[...] = _layernorm(ALPHA * x1_ref[...] + gate2 * f, lng_ref[...], lnb_ref[...])


def _combine(loff_i, npc, goff, y, mcol, x1, mod, ln_g, ln_b, layer, row0, tokens_per_row, tb):
    n = x1.shape[0]
    slots = 2 * tb + SLOT_PAD
    mod_row = lambda i, *_: (layer, row0 + (i * tb) // tokens_per_row, 0, 0)
    grid_spec = pltpu.PrefetchScalarGridSpec(
        num_scalar_prefetch=3,
        grid=(n // tb,),
        in_specs=[
            pl.BlockSpec(memory_space=pl.ANY),
            pl.BlockSpec((tb, LANE), lambda i, *_: (i, 0)),
            pl.BlockSpec((tb, D_MODEL), lambda i, *_: (i, 0)),
            pl.BlockSpec((None, None, 1, 6 * D_MODEL), mod_row),
            pl.BlockSpec((1, D_MODEL), lambda i, *_: (0, 0)),
            pl.BlockSpec((1, D_MODEL), lambda i, *_: (0, 0)),
        ],
        out_specs=pl.BlockSpec((tb, D_MODEL), lambda i, *_: (i, 0)),
        scratch_shapes=[pltpu.VMEM((2, slots, D_MODEL), BF16), pltpu.SemaphoreType.DMA((2,))],
    )
    return pl.pallas_call(
        functools.partial(_combine_kernel, slots=slots),
        grid_spec=grid_spec,
        out_shape=jax.ShapeDtypeStruct((n, D_MODEL), F32),
        compiler_params=_cparams(("arbitrary",)),
    )(loff_i, npc, goff, y, mcol, x1, mod, ln_g, ln_b)


def _moe(h, mrow, mcol, plen, loff, wg, wu, wd, x1, mod, ln_g, ln_b, layer, row0, tokens_per_row, tb):
    n = h.shape[0]
    nb = n // tb
    n_steps = (2 * n + nb * N_EXPERTS * (PIECE - 1)) // ROW_TILE + N_EXPERTS
    assert n_steps <= LANE
    goff, npc, loff_i, tiles, tail = _plan(plen.reshape(nb, LANE), loff.reshape(nb, LANE), n_steps)
    hs = _dispatch(loff_i, npc, goff, tail, h, mrow, mcol, tb, n_steps)
    y = _experts(tiles, hs, wg, wu, wd, layer, n_steps)
    return _combine(loff_i, npc, goff, y, mcol, x1, mod, ln_g, ln_b, layer, row0, tokens_per_row, tb)


IN_SIZES = (3 * GDN_W, GDN_W, 2 * GDN_HEADS, 2 * GDN_HEADS, GLA_KW, GLA_KW, GLA_VW, GLA_VW,
            2 * GLA_RANK, DIFF_QW, DIFF_QW, DIFF_VW, N_BRANCH * D_MODEL)
IN_OFFS = tuple(int(v) for v in np.concatenate([[0], np.cumsum(IN_SIZES)]))
D_IN = IN_OFFS[-1]


def _reorder_kernel(w_ref, wa_ref, wf_ref):
    o = IN_OFFS
    for lo, hi, dst in ((o[0], o[2], A_GDN_QKV), (o[4], o[8], A_GLA_Q), (o[9], o[10], A_DIFF_Q),
                        (o[12], o[13], A_MERGE)):
        wa_ref[:, dst:dst + hi - lo] = w_ref[:, lo:hi].astype(BF16)
    wf_ref[:, F_DIFF_K:F_DIFF_K + o[12] - o[10]] = w_ref[:, o[10]:o[12]].astype(BF16)
    assert o[2] % LANE == 0 and o[8] % LANE == MISC_LR and o[4] - o[2] == MISC_LR
    gates = w_ref[:, o[2]:o[2] + LANE]
    low_rank = w_ref[:, o[8] - MISC_LR:o[8] - MISC_LR + LANE]
    lane = lax.broadcasted_iota(jnp.int32, gates.shape, 1)
    misc = jnp.where(lane < MISC_LR, gates, jnp.where(lane < MISC_LR + 2 * GLA_RANK, low_rank, 0.0))
    wf_ref[:, F_MISC:F_MISC + LANE] = misc.astype(BF16)


def _reorder_w_in(w_in):
    tr = 256
    return pl.pallas_call(
        _reorder_kernel,
        grid=(DEPTH, D_MODEL // tr),
        in_specs=[pl.BlockSpec((None, tr, D_IN), lambda l, i: (l, i, 0))],
        out_specs=[pl.BlockSpec((None, tr, A_COLS), lambda l, i: (l, i, 0)),
                   pl.BlockSpec((None, tr, F_COLS), lambda l, i: (l, i, 0))],
        out_shape=[jax.ShapeDtypeStruct((DEPTH, D_MODEL, A_COLS), BF16),
                   jax.ShapeDtypeStruct((DEPTH, D_MODEL, F_COLS), BF16)],
        compiler_params=_cparams(("arbitrary", "arbitrary")),
    )(w_in)


def _lane_row(vals, offset):
    return jnp.zeros((1, LANE), F32).at[0, offset:offset + vals.shape[0]].set(vals)


def kernel(x_prompt, x_sample, c, state_gdn, state_gla, cache_k, cache_v, c_ctx, w_mod, b_mod, w_in,
           gdn_conv, gdn_a_log, gdn_dt_bias, gdn_norm, gla_w_gate, gla_b_gate, gla_norm, diff_lambda,
           diff_norm, w_branch, w_out, ln_g, ln_b, router_w, router_b, exp_w_gate, exp_w_up, exp_w_down):
    bp, tp, d = x_prompt.shape
    bs, ts, _ = x_sample.shape
    pad_rows = MOD_ROWS - 1 - bs
    cvecs = jnp.concatenate([c_ctx[None, :], c, jnp.zeros((pad_rows, d), F32)], axis=0)
    mod = _modulation(cvecs, w_mod, b_mod).reshape(DEPTH, MOD_ROWS, 1, 6 * d)

    rw_t = router_w.T
    rb_col = router_b.reshape(N_EXPERTS, 1)
    ck = cache_k.reshape(bs, DEPTH, cache_k.shape[2], DIFF_QW)
    cv = cache_v.reshape(bs, DEPTH, cache_v.shape[2], DIFF_VW)

    wa, wf = _reorder_w_in(w_in)
    layer_w = []
    for l in range(DEPTH):
        layer_w.append(dict(
            wa=wa, wf=wf,
            alog=_lane_row(gdn_a_log[l].reshape(-1), MISC_A),
            dtb=_lane_row(gdn_dt_bias[l].reshape(-1), MISC_A),
            wb=w_branch[l].astype(BF16),
            wo=w_out[l].astype(BF16),
        ))

    def layer(x, l, row0, tokens_per_row, s_gdn, s_gla, ctx_k, ctx_v, emit_state, tm, prev_gdn=None,
              prev_gla=None, prev_cache=None):
        b, t, _ = x.shape
        n = b * t
        lw = layer_w[l]
        xf = x.reshape(n, d)
        if emit_state:
            pa, pf, *cache = _inproj(xf, mod, lw["wa"], lw["wf"], l, row0, tokens_per_row, t, (b, t), prev_cache)
        else:
            pa, pf = _inproj(xf, mod, lw["wa"], lw["wf"], l, row0, tokens_per_row, tm, None, None)
            cache = None
        pa3 = pa.reshape(b, t, A_COLS)
        pf3 = pf.reshape(b, t, F_COLS)
        o_gdn, gdn_fin = _gdn(pa3, pf3, gdn_conv[l], lw["alog"], lw["dtb"], gdn_norm[l][None, :], s_gdn, l,
                              emit_state, prev_gdn)
        o_gla, gla_fin = _gla(pa3, pf3, gla_w_gate[l], gla_b_gate[l], gla_norm[l][None, :], s_gla, l,
                              emit_state, prev_gla)
        lam_init = 0.8 - 0.6 * math.exp(-0.3 * l)
        o_diff = _diff(pa3, pf3, diff_lambda[l], diff_norm[l][None, :], ctx_k, ctx_v, l, lam_init)
        x1, h, mrow, mcol, plen, loff = _merge(
            o_gdn.reshape(n, MIX_W), o_gla.reshape(n, MIX_W), o_diff.reshape(n, MIX_W), pa, xf, mod,
            lw["wb"], lw["wo"], ln_g[l, 0][None, :], ln_b[l, 0][None, :], rw_t, rb_col, l, row0,
            tokens_per_row, tm)
        x2 = _moe(h, mrow, mcol, plen, loff, exp_w_gate, exp_w_up, exp_w_down, x1, mod,
                  ln_g[l, 1][None, :], ln_b[l, 1][None, :], l, row0, tokens_per_row, tm)
        return x2.reshape(b, t, d), gdn_fin, gla_fin, cache

    hp = x_prompt
    gdn_states, gla_states, cache = None, None, None
    for l in range(DEPTH):
        hp, gdn_states, gla_states, cache = layer(hp, l, 0, bp * tp, None, None, None, None, True, 512,
                                                  gdn_states, gla_states, cache)

    hs = x_sample
    for l in range(DEPTH):
        hs = layer(hs, l, 1, ts, state_gdn, state_gla, ck, cv, False, 512)[0]

    return hp, hs, gdn_states, gla_states, cache[0], cache[1]
```

```python
import functools
import math

import numpy as np
import jax
import jax.numpy as jnp
from jax import lax
from jax.experimental import pallas as pl
from jax.experimental.pallas import tpu as pltpu

F32 = jnp.float32
BF16 = jnp.bfloat16

D_MODEL = 1024
DEPTH = 2
GRID_W = 64
GDN_HEADS = 4
GDN_DK = 128
GDN_DV = 128
GDN_W = GDN_HEADS * GDN_DV
SHORT_CONV = 5
GLA_HEADS = 4
GLA_DK = 64
GLA_DV = 128
GLA_KW = GLA_HEADS * GLA_DK
GLA_VW = GLA_HEADS * GLA_DV
GLA_RANK = 16
GLA_TAU = 16.0
CHUNK = 64
DIFF_HEADS = 4
DIFF_DH = 64
DIFF_VD = 2 * DIFF_DH
DIFF_QW = DIFF_HEADS * 2 * DIFF_DH
DIFF_VW = DIFF_HEADS * DIFF_VD
ROPE_THETA = 10000.0
MIX_W = 512
N_BRANCH = 3
N_EXPERTS = 16
N_GROUPS = 4
GROUP_SIZE = N_EXPERTS // N_GROUPS
D_FF = 512
ALPHA = (2 * DEPTH) ** 0.25
EPS = 1e-6

LANE = 128
MOD_ROWS = 8
VMEM_LIMIT = 56 * 1024 * 1024

A_GDN_QKV = 0
A_GDN_Z = 1536
A_GLA_Q = 2048
A_GLA_K = 2304
A_GLA_V = 2560
A_GLA_R = 3072
A_DIFF_Q = 3584
A_MERGE = 4096
A_COLS = 7168
F_DIFF_K = 0
F_DIFF_V = 512
F_MISC = 1024
F_COLS = 1152
MISC_B = 0
MISC_A = 8
MISC_LR = 16


def _cparams(sem):
    return pltpu.CompilerParams(dimension_semantics=sem, vmem_limit_bytes=VMEM_LIMIT)


def _dot(a, b):
    return jnp.dot(a, b, preferred_element_type=F32)


def _mm(a, b):
    return _dot(a.astype(BF16), b.astype(BF16))


def _mm_nt(a, b):
    return lax.dot_general(a.astype(BF16), b.astype(BF16), (((1,), (1,)), ((), ())),
                           preferred_element_type=F32)


def _split3(x):
    hi = x.astype(BF16)
    r = x - hi.astype(F32)
    mid = r.astype(BF16)
    lo = (r - mid.astype(F32)).astype(BF16)
    return hi, mid, lo


def _mm_mask_lhs(mask_bf16, x):
    hi, mid, lo = _split3(x)
    return _dot(mask_bf16, hi) + _dot(mask_bf16, mid) + _dot(mask_bf16, lo)


def _mm_mask_rhs(x, mask_bf16):
    hi, mid, lo = _split3(x)
    return _dot(hi, mask_bf16) + _dot(mid, mask_bf16) + _dot(lo, mask_bf16)


def _mm_hi(a, b):
    ah = a.astype(BF16)
    al = (a - ah.astype(F32)).astype(BF16)
    bh = b.astype(BF16)
    bl = (b - bh.astype(F32)).astype(BF16)
    return _dot(ah, bh) + _dot(ah, bl) + _dot(al, bh)


def _mm_hi_nt(a, b):
    ah = a.astype(BF16)
    al = (a - ah.astype(F32)).astype(BF16)
    bh = b.astype(BF16)
    bl = (b - bh.astype(F32)).astype(BF16)
    dot = lambda x, y: lax.dot_general(x, y, (((1,), (1,)), ((), ())), preferred_element_type=F32)
    return dot(ah, bh) + dot(ah, bl) + dot(al, bh)


def _sigmoid(x):
    return 1.0 / (1.0 + jnp.exp(-x))


def _silu(x):
    return x * _sigmoid(x)


def _softplus(x):
    return jnp.maximum(x, 0.0) + jnp.log1p(jnp.exp(-jnp.abs(x)))


def _rms(x, g):
    return x * lax.rsqrt(jnp.mean(x * x, axis=-1, keepdims=True) + EPS) * g


def _layernorm(y, g, b):
    mu = jnp.mean(y, axis=-1, keepdims=True)
    d = y - mu
    var = jnp.mean(d * d, axis=-1, keepdims=True)
    return d * lax.rsqrt(var + EPS) * g + b


def _pick_lane(x, idx):
    lane = lax.broadcasted_iota(jnp.int32, x.shape, 1)
    return jnp.sum(jnp.where(lane == idx, x, 0.0), axis=1, keepdims=True)


def _tri(n, kind):
    r = lax.broadcasted_iota(jnp.int32, (n, n), 0)
    c = lax.broadcasted_iota(jnp.int32, (n, n), 1)
    return {"ge": r >= c, "gt": r > c, "le": r <= c, "lt": r < c}[kind]


def _mod_kernel(c_ref, w_ref, b_ref, o_ref):
    s = _silu(c_ref[...])
    s_hi = s.astype(BF16)
    s_lo = (s - s_hi.astype(F32)).astype(BF16)
    w = w_ref[...].astype(BF16)
    o_ref[...] = _dot(s_hi, w) + _dot(s_lo, w) + b_ref[...]


def _modulation(cvecs, w_mod, b_mod):
    tn = 1536
    return pl.pallas_call(
        _mod_kernel,
        grid=(DEPTH, 6 * D_MODEL // tn),
        in_specs=[
            pl.BlockSpec((MOD_ROWS, D_MODEL), lambda l, j: (0, 0)),
            pl.BlockSpec((None, D_MODEL, tn), lambda l, j: (l, 0, j)),
            pl.BlockSpec((None, 1, tn), lambda l, j: (l, 0, j)),
        ],
        out_specs=pl.BlockSpec((None, MOD_ROWS, tn), lambda l, j: (l, 0, j)),
        out_shape=jax.ShapeDtypeStruct((DEPTH, MOD_ROWS, 6 * D_MODEL), F32),
        compiler_params=_cparams(("arbitrary", "arbitrary")),
    )(cvecs, w_mod, b_mod.reshape(DEPTH, 1, 6 * D_MODEL))


INPROJ_COL_STEP = 1792


def _inproj_kernel(*refs, cache_mode):
    x_ref, mod_ref, wa_ref, wf_ref = refs[:4]
    outs = refs[6:] if cache_mode == "later" else refs[4:]
    pa_ref, pf_ref = outs[:2]
    shift = mod_ref[:, 0:D_MODEL]
    scale = mod_ref[:, D_MODEL:2 * D_MODEL]
    u = (x_ref[...] * (1.0 + scale) + shift).astype(BF16)
    pf = _dot(u, wf_ref[...])
    pf_ref[...] = pf
    if cache_mode:
        for o_ref, base in zip(outs[2:4], (F_DIFF_K, F_DIFF_V)):
            if cache_mode == "first":
                for l in range(1, DEPTH):
                    o_ref[l] = jnp.zeros(o_ref.shape[1:], F32)
                o_ref = o_ref.at[0]
            for h in range(DIFF_HEADS):
                o_ref[:, h, :] = pf[:, base + h * LANE:base + (h + 1) * LANE]
    for c0 in range(0, A_COLS, INPROJ_COL_STEP):
        cols = slice(c0, c0 + INPROJ_COL_STEP)
        pa_ref[:, cols] = _dot(u, wa_ref[:, cols]).astype(BF16)


def _inproj(x, mod, wa, wf, layer, row0, tokens_per_row, tm, cache_shape, prev_cache):
    n = x.shape[0]
    mod_row = lambda i: (layer, row0 + (i * tm) // tokens_per_row, 0, 0)
    out_specs = [
        pl.BlockSpec((tm, A_COLS), lambda i: (i, 0)),
        pl.BlockSpec((tm, F_COLS), lambda i: (i, 0)),
    ]
    out_shape = [
        jax.ShapeDtypeStruct((n, A_COLS), BF16),
        jax.ShapeDtypeStruct((n, F_COLS), F32),
    ]
    resident = lambda shape: pl.BlockSpec((None,) + shape, lambda i: (layer, 0, 0), pipeline_mode=pl.Buffered(1))
    in_specs = [
        pl.BlockSpec((tm, D_MODEL), lambda i: (i, 0)),
        pl.BlockSpec((None, None, 1, 6 * D_MODEL), mod_row),
        resident((D_MODEL, A_COLS)),
        resident((D_MODEL, F_COLS)),
    ]
    args = [x, mod, wa, wf]
    cache_mode, aliases = None, {}
    if cache_shape is not None:
        b, t = cache_shape
        assert tm == t
        out_shape += [jax.ShapeDtypeStruct((b, DEPTH, t, DIFF_HEADS, LANE), F32)] * 2
        if prev_cache is None:
            cache_mode = "first"
            out_specs += [pl.BlockSpec((None, DEPTH, t, DIFF_HEADS, LANE), lambda i: (i, 0, 0, 0, 0))] * 2
        else:
            cache_mode = "later"
            out_specs += [pl.BlockSpec((None, None, t, DIFF_HEADS, LANE), lambda i: (i, layer, 0, 0, 0))] * 2
            in_specs += [pl.BlockSpec(memory_space=pl.ANY)] * 2
            args += list(prev_cache)
            aliases = {4: 2, 5: 3}
    return pl.pallas_call(
        functools.partial(_inproj_kernel, cache_mode=cache_mode),
        grid=(n // tm,),
        in_specs=in_specs,
        out_specs=out_specs,
        out_shape=out_shape,
        input_output_aliases=aliases,
        compiler_params=_cparams(("arbitrary",)),
    )(*args)


def _conv_masks(shape, t_len):
    row = lax.broadcasted_iota(jnp.int32, shape, 0)
    half = SHORT_CONV // 2
    return {d: jnp.logical_and(row + d >= 0, row + d < t_len) for d in range(-half, half + 1) if d}


def _short_conv(x, w, t_len, masks):
    half = SHORT_CONV // 2
    acc = x * w[half:half + 1, :]
    for d, valid in masks.items():
        shifted = pltpu.roll(x, (-d) % t_len, axis=0)
        acc = acc + jnp.where(valid, shifted, 0.0) * w[half + d:half + d + 1, :]
    return acc


GDN_PAIRS = 2


def _solve_unit_tri(a_list, rhs_list):
    n = a_list[0].shape[0]
    assert n == 64
    eye = jnp.where(_tri(n, "ge") & _tri(n, "le"), 1.0, 0.0)
    a_b = [a.astype(BF16) for a in a_list]
    bf = lambda xs: [x.astype(BF16) for x in xs]
    mul = lambda xs, ys: [_dot(x, y) for x, y in zip(xs, ys)]
    pair = lambda xs, ys, xb, yb: [x + y + xy for x, y, xy in zip(xs, ys, mul(xb, yb))]
    p0 = [-a for a in a_list]
    p0b = [-a for a in a_b]
    p1 = mul(p0b, p0b)
    p1b = bf(p1)
    p2 = mul(p1b, p1b)
    n01 = pair(p0, p1, p0b, p1b)
    p2b = bf(p2)
    p3 = mul(p2b, p2b)
    p3b = bf(p3)
    p4 = mul(p3b, p3b)
    n23 = pair(p2, p3, p2b, p3b)
    p4b = bf(p4)
    p5 = mul(p4b, p4b)
    n03 = pair(n01, n23, bf(n01), bf(n23))
    p5b = bf(p5)
    n45 = pair(p4, p5, p4b, p5b)
    inv_b = [(eye + x).astype(BF16) for x in pair(n03, n45, bf(n03), bf(n45))]
    sol = [_dot(x, r.astype(BF16)) for x, r in zip(inv_b, rhs_list)]
    resid = [r - s - _dot(a, s.astype(BF16)) for r, s, a in zip(rhs_list, sol, a_b)]
    return [s + _dot(x, r.astype(BF16)) for s, x, r in zip(sol, inv_b, resid)]


def _gdn_kernel(*refs, t_len, has_state, emit_state):
    it = iter(refs)
    qkv_ref, z_ref, misc_ref, cw_ref, alog_ref, dtb_ref, g_ref = (next(it) for _ in range(7))
    s0_ref = next(it) if has_state else None
    if emit_state == "later":
        next(it)
    o_ref = next(it)
    sfin_ref = next(it) if emit_state else None
    (q_s, k_s, v_s, kt_s, bet, gat, gat_t, cumc, cumr, u_s, w_s, qd_s, a_s, ktl_s, gts, o_f, o_b,
     s_s) = (next(it) for _ in range(18))

    c = CHUNK
    n_chunks = t_len // c
    nh = GDN_HEADS
    dk = GDN_DK

    masks = _conv_masks((t_len, dk), t_len)
    for h in range(nh):
        hs = slice(h * dk, (h + 1) * dk)
        q, k, v = (
            _silu(_short_conv(qkv_ref[:, j * GDN_W + h * dk:j * GDN_W + (h + 1) * dk].astype(F32),
                              cw_ref[:, j * GDN_W + h * dk:j * GDN_W + (h + 1) * dk], t_len, masks))
            for j in range(3))
        q = q * lax.rsqrt(jnp.sum(q * q, axis=-1, keepdims=True) + EPS) * (dk ** -0.5)
        k = k * lax.rsqrt(jnp.sum(k * k, axis=-1, keepdims=True) + EPS)
        q_s[:, hs] = q
        k_s[:, hs] = k
        v_s[:, hs] = v
        kt_s[hs, :] = k.T.astype(BF16)
    misc = misc_ref[...]
    bet[...] = _sigmoid(misc)
    g_all = -jnp.exp(alog_ref[...]) * _softplus(misc + dtb_ref[...])
    gat[...] = g_all
    gat_t[...] = g_all.T[MISC_A:MISC_A + 2 * nh, :]
    for d in range(2):
        for h in range(nh):
            s_s[d * nh + h] = s0_ref[d, h] if has_state else jnp.zeros((dk, GDN_DV), F32)

    two = 2 * c
    r2 = lax.broadcasted_iota(jnp.int32, (two, two), 0)
    c2 = lax.broadcasted_iota(jnp.int32, (two, two), 1)
    same = (r2 >= c) == (c2 >= c)
    lo2 = jnp.where(jnp.logical_and(same, r2 >= c2), 1.0, 0.0).astype(BF16)
    up2 = jnp.where(jnp.logical_and(same, r2 <= c2), 1.0, 0.0).astype(BF16)
    incl = (_tri(c, "ge"), _tri(c, "le"))
    strict = (_tri(c, "gt"), _tri(c, "lt"))
    last = (c - 1, 0)

    def cumulate(cp, carry):
        r0 = pl.multiple_of(cp * two, two)
        g_blk = gat[pl.ds(r0, two), :]
        gt_blk = gat_t[:, pl.ds(r0, two)]
        cumc[0, pl.ds(r0, two), :] = _mm_mask_lhs(lo2, g_blk)
        cumc[1, pl.ds(r0, two), :] = _mm_mask_lhs(up2, g_blk)
        cumr[0, :, pl.ds(r0, two)] = _mm_mask_rhs(gt_blk, up2)
        cumr[1, :, pl.ds(r0, two)] = _mm_mask_rhs(gt_blk, lo2)
        return carry

    lax.fori_loop(0, n_chunks // 2, cumulate, 0)

    def phase1(step, carry):
        chains, a_list, rhs_list = [], [], []
        for pp in range(GDN_PAIRS):
            gather_chains(step * GDN_PAIRS + pp, chains, a_list, rhs_list)
        for (idx, rows), sol in zip(chains, _solve_unit_tri(a_list, rhs_list)):
            u_s[idx, rows, :] = sol[:, :GDN_DV]
            w_s[idx, rows, :] = sol[:, GDN_DV:].astype(BF16)
        return carry

    def gather_chains(cp, chains, a_list, rhs_list):
        r0 = pl.multiple_of(cp * two, two)
        b_blk = bet[pl.ds(r0, two), :]
        cum_col = (cumc[0, pl.ds(r0, two), :], cumc[1, pl.ds(r0, two), :])
        cum_row = (cumr[0, :, pl.ds(r0, two)], cumr[1, :, pl.ds(r0, two)])
        for h in range(nh):
            hs = slice(h * dk, (h + 1) * dk)
            kt_pair = kt_s[hs, pl.ds(r0, two)]
            for s in range(2):
                rows = pl.ds(pl.multiple_of(r0 + s * c, c), c)
                q_c = q_s[rows, hs]
                k_c = k_s[rows, hs]
                v_c = v_s[rows, hs]
                kt_c = kt_pair[:, s * c:(s + 1) * c]
                kk = _dot(k_c.astype(BF16), kt_c)
                qk = _dot(q_c.astype(BF16), kt_c)
                for d in range(2):
                    idx = d * nh + h
                    gc = cum_col[d][s * c:(s + 1) * c, MISC_A + idx:MISC_A + idx + 1]
                    gcr = cum_row[d][idx:idx + 1, s * c:(s + 1) * c]
                    beta = b_blk[s * c:(s + 1) * c, MISC_B + idx:MISC_B + idx + 1]
                    decay = jnp.where(incl[d], jnp.exp(jnp.where(incl[d], gc - gcr, 0.0)), 0.0)
                    e_gc = jnp.exp(gc)
                    a_list.append(jnp.where(strict[d], beta * kk * decay, 0.0))
                    rhs_list.append(jnp.concatenate([v_c * beta, k_c * (beta * e_gc)], axis=1))
                    a_s[idx, rows, :] = jnp.where(incl[d], qk * decay, 0.0).astype(BF16)
                    qd_s[idx, rows, :] = (q_c * e_gc).astype(BF16)
                    g_last = gcr[:, last[d]:last[d] + 1]
                    ktl_s[idx, cp * 2 + s] = (kt_c.astype(F32) * jnp.exp(g_last - gcr)).astype(BF16)
                    gts[cp * 2 + s, idx:idx + 1, :] = jnp.broadcast_to(jnp.exp(g_last), (1, LANE))
                    chains.append((idx, rows))

    lax.fori_loop(0, n_chunks // (2 * GDN_PAIRS), phase1, 0)

    def phase2(i, carry):
        chains = []
        for d, out in ((0, o_f), (1, o_b)):
            ci = i if d == 0 else n_chunks - 1 - i
            rows = pl.ds(pl.multiple_of(ci * c, c), c)
            g_tail = gts[ci]
            for h in range(nh):
                chains.append((d * nh + h, h, ci, rows, out, g_tail))
        s_f32 = [s_s[idx] for idx, *_ in chains]
        s_b = [s.astype(BF16) for s in s_f32]
        v_b = [(u_s[idx, rows, :] - _dot(w_s[idx, rows, :], sb)).astype(BF16)
               for (idx, _, _, rows, _, _), sb in zip(chains, s_b)]
        for (idx, h, ci, rows, out, g_tail), s, sb, vb in zip(chains, s_f32, s_b, v_b):
            s_s[idx] = s * g_tail[idx:idx + 1, :] + _dot(ktl_s[idx, ci], vb)
            out[rows, h * GDN_DV:(h + 1) * GDN_DV] = _dot(qd_s[idx, rows, :], sb) + _dot(
                a_s[idx, rows, :], vb)
        return carry

    lax.fori_loop(0, n_chunks, phase2, 0)

    for h in range(nh):
        hs = slice(h * GDN_DV, (h + 1) * GDN_DV)
        o = o_f[:, hs] + o_b[:, hs]
        o_ref[:, hs] = (_rms(o, g_ref[...]) * _silu(z_ref[:, hs].astype(F32))).astype(BF16)
    if emit_state:
        for d in range(2):
            for h in range(nh):
                _store_state(sfin_ref, emit_state, d, h, s_s[d * nh + h])


def _store_state(sfin_ref, mode, d, h, s):
    if mode == "first":
        sfin_ref[0, d, h] = s
        for l in range(1, DEPTH):
            sfin_ref[l, d, h] = jnp.zeros_like(s)
    else:
        sfin_ref[d, h] = s


def _state_output(b, layer, heads, dk, dv, prev, in_specs, args):
    shape = jax.ShapeDtypeStruct((b, DEPTH, 2, heads, dk, dv), F32)
    if prev is None:
        return "first", pl.BlockSpec((None, DEPTH, 2, heads, dk, dv), lambda bi: (bi, 0, 0, 0, 0, 0)), shape, {}
    in_specs.append(pl.BlockSpec(memory_space=pl.ANY))
    args.append(prev)
    spec = pl.BlockSpec((None, None, 2, heads, dk, dv), lambda bi: (bi, layer, 0, 0, 0, 0))
    return "later", spec, shape, {len(args) - 1: 1}


def _gdn(pa3, pf3, conv_w, alog_row, dtb_row, norm_g, s0, layer, emit_state, prev_states):
    b, t, _ = pa3.shape
    full = lambda shape: pl.BlockSpec(shape, lambda bi: (0,) * len(shape))
    in_specs = [
        pl.BlockSpec((None, t, 3 * GDN_W), lambda bi: (bi, 0, A_GDN_QKV // (3 * GDN_W))),
        pl.BlockSpec((None, t, GDN_W), lambda bi: (bi, 0, A_GDN_Z // GDN_W)),
        pl.BlockSpec((None, t, LANE), lambda bi: (bi, 0, F_MISC // LANE)),
        full((SHORT_CONV, 3 * GDN_W)),
        full((1, LANE)),
        full((1, LANE)),
        full((1, GDN_DV)),
    ]
    args = [pa3, pa3, pf3, conv_w, alog_row, dtb_row, norm_g]
    if s0 is not None:
        in_specs.append(pl.BlockSpec((None, None, 2, GDN_HEADS, GDN_DK, GDN_DV),
                                     lambda bi: (bi, layer, 0, 0, 0, 0)))
        args.append(s0)
    out_specs = [pl.BlockSpec((None, t, GDN_W), lambda bi: (bi, 0, 0))]
    out_shape = [jax.ShapeDtypeStruct((b, t, GDN_W), BF16)]
    mode, aliases = None, {}
    if emit_state:
        mode, spec, shape, aliases = _state_output(b, layer, GDN_HEADS, GDN_DK, GDN_DV, prev_states, in_specs, args)
        out_specs.append(spec)
        out_shape.append(shape)
    nc = t // CHUNK
    nd = 2 * GDN_HEADS
    wide = pltpu.VMEM((t, GDN_W), F32)
    scratch = [
        wide, wide, wide,
        pltpu.VMEM((GDN_W, t), BF16),
        pltpu.VMEM((t, LANE), F32), pltpu.VMEM((t, LANE), F32),
        pltpu.VMEM((nd, t), F32),
        pltpu.VMEM((2, t, LANE), F32), pltpu.VMEM((2, nd, t), F32),
        pltpu.VMEM((nd, t, GDN_DV), F32),
        pltpu.VMEM((nd, t, GDN_DK), BF16),
        pltpu.VMEM((nd, t, GDN_DK), BF16),
        pltpu.VMEM((nd, t, CHUNK), BF16),
        pltpu.VMEM((nd, nc, GDN_DK, CHUNK), BF16),
        pltpu.VMEM((nc, nd, LANE), F32),
        wide, wide,
        pltpu.VMEM((nd, GDN_DK, GDN_DV), F32),
    ]
    outs = pl.pallas_call(
        functools.partial(_gdn_kernel, t_len=t, has_state=s0 is not None, emit_state=mode),
        grid=(b,),
        in_specs=in_specs,
        out_specs=out_specs,
        out_shape=out_shape,
        scratch_shapes=scratch,
        input_output_aliases=aliases,
        compiler_params=_cparams(("arbitrary",)),
    )(*args)
    return outs if emit_state else (outs[0], None)


def _gla_kernel(*refs, t_len, has_state, emit_state):
    it = iter(refs)
    q_ref, k_ref, v_ref, r_ref, misc_ref, wg_ref, bg_ref, g_ref = (next(it) for _ in range(8))
    s0_ref = next(it) if has_state else None
    if emit_state == "later":
        next(it)
    o_ref = next(it)
    sfin_ref = next(it) if emit_state else None
    glog, vt_s, o_f, o_b, st = (next(it) for _ in range(5))

    c = CHUNK
    two = 2 * c
    n_pairs = t_len // two
    nh = GLA_HEADS
    misc = misc_ref[...]
    for d in range(2):
        lr = misc[:, MISC_LR + d * GLA_RANK:MISC_LR + (d + 1) * GLA_RANK]
        logits = _mm_hi(lr, wg_ref[d]) + bg_ref[d:d + 1, :]
        glog[d] = -_softplus(-logits) / GLA_TAU
    for h in range(nh):
        hs = slice(h * GLA_DV, (h + 1) * GLA_DV)
        vt_s[hs, :] = v_ref[:, hs].astype(F32).T.astype(BF16)
    for d in range(2):
        for h in range(nh):
            if has_state:
                s0 = jnp.concatenate([s0_ref[d, h], jnp.zeros((GLA_DV - GLA_DK, GLA_DV), F32)], axis=0)
                st[d * nh + h] = s0.T[:, :GLA_DK]
            else:
                st[d * nh + h] = jnp.zeros((GLA_DV, GLA_DK), F32)

    r2 = lax.broadcasted_iota(jnp.int32, (two, two), 0)
    c2 = lax.broadcasted_iota(jnp.int32, (two, two), 1)
    same = (r2 >= c) == (c2 >= c)
    cum_mask = (jnp.where(jnp.logical_and(same, r2 >= c2), 1.0, 0.0).astype(BF16),
                jnp.where(jnp.logical_and(same, r2 <= c2), 1.0, 0.0).astype(BF16))
    incl = (_tri(c, "ge"), _tri(c, "le"))
    last = (c - 1, 0)
    mid = (c // 2, c - 1 - c // 2)
    order = ((0, 1), (1, 0))

    def cumulate(cp, carry):
        r0 = pl.multiple_of(cp * two, two)
        for d in range(2):
            glog[d, pl.ds(r0, two), :] = _mm_mask_lhs(cum_mask[d], glog[d, pl.ds(r0, two), :])
        return carry

    lax.fori_loop(0, n_pairs, cumulate, 0)

    def body(i, carry):
        pieces = []
        for d in range(2):
            r0 = pl.multiple_of((i if d == 0 else n_pairs - 1 - i) * two, two)
            b_all = glog[d, pl.ds(r0, two), :]
            q_all = q_ref[pl.ds(r0, two), :].astype(F32) * (GLA_DK ** -0.5)
            k_all = k_ref[pl.ds(r0, two), :].astype(F32)
            vt_pair = [vt_s[h * GLA_DV:(h + 1) * GLA_DV, pl.ds(r0, two)] for h in range(nh)]
            for s in range(2):
                rs = slice(s * c, (s + 1) * c)
                bq = b_all[rs]
                b_mid = bq[mid[d]:mid[d] + 1, :]
                b_last = bq[last[d]:last[d] + 1, :]
                qe = (q_all[rs] * jnp.exp(bq - b_mid)).astype(BF16)
                ke = (k_all[rs] * jnp.exp(b_mid - bq)).astype(BF16)
                qd = (q_all[rs] * jnp.exp(bq)).astype(BF16)
                kt = (k_all[rs] * jnp.exp(b_last - bq)).astype(BF16)
                g_last = jnp.exp(b_last)
                rows = pl.ds(pl.multiple_of(r0 + s * c, c), c)
                for h in range(nh):
                    ks = slice(h * GLA_DK, (h + 1) * GLA_DK)
                    pieces.append((d, s, h, rows, qe[:, ks], ke[:, ks], qd[:, ks], kt[:, ks], g_last[:, ks],
                                   vt_pair[h][:, rs], v_ref[rows, h * GLA_DV:(h + 1) * GLA_DV]))
        a_intra = [jnp.where(incl[p[0]], _mm_nt(p[4], p[5]), 0.0).astype(BF16) for p in pieces]
        upd = [_dot(p[9], p[7]) for p in pieces]
        intra = [_dot(a, p[10]) for a, p in zip(a_intra, pieces)]
        for step in range(2):
            for p, m, o_in in zip(pieces, upd, intra):
                d, s, h, rows = p[0], p[1], p[2], p[3]
                if s != order[d][step]:
                    continue
                s_t = st[d * nh + h]
                out = o_f if d == 0 else o_b
                out[rows, h * GLA_DV:(h + 1) * GLA_DV] = _mm_nt(p[6], s_t) + o_in
                st[d * nh + h] = s_t * p[8] + m
        return carry

    lax.fori_loop(0, n_pairs, body, 0)

    for h in range(nh):
        hs = slice(h * GLA_DV, (h + 1) * GLA_DV)
        o = o_f[:, hs] + o_b[:, hs]
        o_ref[:, hs] = (_rms(o, g_ref[...]) * _silu(r_ref[:, hs].astype(F32))).astype(BF16)
    if emit_state:
        for d in range(2):
            for h in range(nh):
                s_pad = jnp.concatenate([st[d * nh + h], jnp.zeros((GLA_DV, GLA_DV - GLA_DK), F32)], axis=1)
                _store_state(sfin_ref, emit_state, d, h, s_pad.T[:GLA_DK, :])


def _gla(pa3, pf3, wg, bg, norm_g, s0, layer, emit_state, prev_states):
    b, t, _ = pa3.shape
    full = lambda shape: pl.BlockSpec(shape, lambda bi: (0,) * len(shape))
    in_specs = [
        pl.BlockSpec((None, t, GLA_KW), lambda bi: (bi, 0, A_GLA_Q // GLA_KW)),
        pl.BlockSpec((None, t, GLA_KW), lambda bi: (bi, 0, A_GLA_K // GLA_KW)),
        pl.BlockSpec((None, t, GLA_VW), lambda bi: (bi, 0, A_GLA_V // GLA_VW)),
        pl.BlockSpec((None, t, GLA_VW), lambda bi: (bi, 0, A_GLA_R // GLA_VW)),
        pl.BlockSpec((None, t, LANE), lambda bi: (bi, 0, F_MISC // LANE)),
        full((2, GLA_RANK, GLA_KW)),
        full((2, GLA_KW)),
        full((1, GLA_DV)),
    ]
    args = [pa3, pa3, pa3, pa3, pf3, wg, bg, norm_g]
    if s0 is not None:
        in_specs.append(pl.BlockSpec((None, None, 2, GLA_HEADS, GLA_DK, GLA_DV),
                                     lambda bi: (bi, layer, 0, 0, 0, 0)))
        args.append(s0)
    out_specs = [pl.BlockSpec((None, t, GLA_VW), lambda bi: (bi, 0, 0))]
    out_shape = [jax.ShapeDtypeStruct((b, t, GLA_VW), BF16)]
    mode, aliases = None, {}
    if emit_state:
        mode, spec, shape, aliases = _state_output(b, layer, GLA_HEADS, GLA_DK, GLA_DV, prev_states, in_specs, args)
        out_specs.append(spec)
        out_shape.append(shape)
    scratch = [pltpu.VMEM((2, t, GLA_KW), F32),
               pltpu.VMEM((GLA_VW, t), BF16),
               pltpu.VMEM((t, GLA_VW), F32), pltpu.VMEM((t, GLA_VW), F32),
               pltpu.VMEM((2 * GLA_HEADS, GLA_DV, GLA_DK), F32)]
    outs = pl.pallas_call(
        functools.partial(_gla_kernel, t_len=t, has_state=s0 is not None, emit_state=mode),
        grid=(b,),
        in_specs=in_specs,
        out_specs=out_specs,
        out_shape=out_shape,
        scratch_shapes=scratch,
        input_output_aliases=aliases,
        compiler_params=_cparams(("arbitrary",)),
    )(*args)
    return outs if emit_state else (outs[0], None)


ATT_Q_BLOCK = 256


def _rope_tables(t_len):
    half = DIFF_DH // 2
    quarter = half // 2
    inv = ROPE_THETA ** (-np.arange(0, half, 2, dtype=np.float64) / half)
    tok = np.arange(t_len)
    pos = np.stack([tok // GRID_W, tok % GRID_W], axis=1).astype(np.float64)
    ang = pos[:, :, None] * inv[None, None, :]
    cos = np.concatenate([np.cos(ang), np.cos(ang)], axis=-1).reshape(t_len, DIFF_DH)
    sin = np.concatenate([-np.sin(ang), np.sin(ang)], axis=-1).reshape(t_len, DIFF_DH)
    cos = np.concatenate([cos, cos], axis=-1).astype(np.float32)
    sin = np.concatenate([sin, sin], axis=-1).astype(np.float32)
    first = ((np.arange(2 * DIFF_DH) % half) < quarter).astype(np.float32)[None, :]
    return jnp.asarray(cos), jnp.asarray(sin), jnp.asarray(first), quarter


def _rope(x, cos, sin, first, quarter):
    width = x.shape[-1]
    ahead = pltpu.roll(x, width - quarter, axis=1)
    behind = pltpu.roll(x, quarter, axis=1)
    partner = jnp.where(first > 0.5, ahead, behind)
    return x * cos + partner * sin


def _diff_kernel(*refs, t_len, ctx_len, lam_init):
    it = iter(refs)
    q_ref, k_ref, v_ref, lam_ref, g_ref = (next(it) for _ in range(5))
    if ctx_len:
        ck_ref, cv_ref, cos_ref, sin_ref, first_ref = (next(it) for _ in range(5))
    o_ref = next(it)
    q_sc, k_sc, v_sc = (next(it) for _ in range(3))

    scale = DIFF_DH ** -0.5
    for h in range(DIFF_HEADS):
        hs = slice(h * LANE, (h + 1) * LANE)
        q = q_ref[:, hs].astype(F32)
        k = k_ref[:, hs]
        if ctx_len:
            quarter = DIFF_DH // 4
            q = _rope(q, cos_ref[...], sin_ref[...], first_ref[...], quarter)
            k = _rope(k, cos_ref[...], sin_ref[...], first_ref[...], quarter)
        q_sc[:, hs] = (q * scale).astype(BF16)
        k_sc[0:t_len, hs] = k.astype(BF16)
    v_sc[0:t_len, :] = v_ref[...].astype(BF16)
    if ctx_len:
        k_sc[t_len:t_len + ctx_len, :] = ck_ref[...].astype(BF16)
        v_sc[t_len:t_len + ctx_len, :] = cv_ref[...].astype(BF16)

    lp = lam_ref[...]
    lam = (jnp.exp(jnp.sum(lp[0:1, :] * lp[1:2, :], axis=1, keepdims=True))
           - jnp.exp(jnp.sum(lp[2:3, :] * lp[3:4, :], axis=1, keepdims=True)) + lam_init)

    tq = min(ATT_Q_BLOCK, t_len)
    group = DIFF_HEADS if t_len + ctx_len <= 512 else 2

    def body(i, carry):
        r0 = pl.multiple_of(i * tq, tq)
        for h0 in range(0, DIFF_HEADS, group):
            parts = [(h, half) for h in range(h0, h0 + group) for half in range(2)]
            cols = [slice(h * LANE + half * DIFF_DH, h * LANE + (half + 1) * DIFF_DH) for h, half in parts]
            s = [_mm_nt(q_sc[pl.ds(r0, tq), c], k_sc[:, c]) for c in cols]
            e = [jnp.exp(x - jnp.max(x, axis=-1, keepdims=True)) for x in s]
            inv_l = [1.0 / jnp.sum(x, axis=-1, keepdims=True) for x in e]
            pv = [_dot(x.astype(BF16), v_sc[:, h * LANE:(h + 1) * LANE]) * r
                  for x, r, (h, _) in zip(e, inv_l, parts)]
            for j in range(group):
                h = h0 + j
                o = pv[2 * j] - lam * pv[2 * j + 1]
                o_ref[pl.ds(r0, tq), h * LANE:(h + 1) * LANE] = (
                    _rms(o, g_ref[...]) * (1.0 - lam_init)).astype(BF16)
        return carry

    lax.fori_loop(0, t_len // tq, body, 0)


def _diff(pa3, pf3, lam_p, norm_g, ctx_k, ctx_v, layer, lam_init):
    b, t, _ = pa3.shape
    ctx_len = 0 if ctx_k is None else ctx_k.shape[2]
    full = lambda shape: pl.BlockSpec(shape, lambda bi: (0,) * len(shape))
    in_specs = [
        pl.BlockSpec((None, t, DIFF_QW), lambda bi: (bi, 0, A_DIFF_Q // DIFF_QW)),
        pl.BlockSpec((None, t, DIFF_QW), lambda bi: (bi, 0, F_DIFF_K // DIFF_QW)),
        pl.BlockSpec((None, t, DIFF_VW), lambda bi: (bi, 0, F_DIFF_V // DIFF_VW)),
        full((4, DIFF_DH)),
        full((1, DIFF_VD)),
    ]
    args = [pa3, pf3, pf3, lam_p, norm_g]
    if ctx_len:
        cos, sin, first, _ = _rope_tables(t)
        in_specs += [
            pl.BlockSpec((None, None, ctx_len, DIFF_QW), lambda bi: (bi, layer, 0, 0)),
            pl.BlockSpec((None, None, ctx_len, DIFF_VW), lambda bi: (bi, layer, 0, 0)),
            full((t, LANE)),
            full((t, LANE)),
            full((1, LANE)),
        ]
        args += [ctx_k, ctx_v, cos, sin, first]
    tk = t + ctx_len
    return pl.pallas_call(
        functools.partial(_diff_kernel, t_len=t, ctx_len=ctx_len, lam_init=lam_init),
        grid=(b,),
        in_specs=in_specs,
        out_specs=pl.BlockSpec((None, t, DIFF_VW), lambda bi: (bi, 0, 0)),
        out_shape=jax.ShapeDtypeStruct((b, t, DIFF_VW), BF16),
        scratch_shapes=[pltpu.VMEM((t, DIFF_QW), BF16), pltpu.VMEM((tk, DIFF_QW), BF16),
                        pltpu.VMEM((tk, DIFF_VW), BF16)],
        compiler_params=_cparams(("arbitrary",)),
    )(*args)


def _route(logits_t, bias_col):
    scores = _sigmoid(logits_t)
    biased = scores + bias_col
    rows = [biased[e:e + 1, :] for e in range(N_EXPERTS)]
    grp = []
    for g in range(N_GROUPS):
        a0, a1, a2, a3 = rows[g * GROUP_SIZE:(g + 1) * GROUP_SIZE]
        hi01, lo01 = jnp.maximum(a0, a1), jnp.minimum(a0, a1)
        hi23, lo23 = jnp.maximum(a2, a3), jnp.minimum(a2, a3)
        top1 = jnp.maximum(hi01, hi23)
        top2 = jnp.maximum(jnp.minimum(hi01, hi23), jnp.maximum(lo01, lo23))
        grp.append(top1 + top2)
    best = []
    for g in range(N_GROUPS):
        win = None
        for o in range(N_GROUPS):
            if o == g:
                continue
            cond = grp[g] > grp[o] if o < g else grp[g] >= grp[o]
            win = cond if win is None else jnp.logical_and(win, cond)
        best.append(win)
    sel_rows = []
    for e in range(N_EXPERTS):
        g = e // GROUP_SIZE
        beaten = jnp.zeros_like(rows[e])
        for o in range(g * GROUP_SIZE, (g + 1) * GROUP_SIZE):
            if o == e:
                continue
            ahead = rows[o] >= rows[e] if o < e else rows[o] > rows[e]
            beaten = beaten + jnp.where(ahead, 1.0, 0.0)
        sel_rows.append(jnp.where(jnp.logical_and(best[g], beaten < 1.5), 1.0, 0.0))
    sel = jnp.concatenate(sel_rows, axis=0)
    picked = sel * scores
    return sel, picked / jnp.sum(picked, axis=0, keepdims=True)


PIECE = 16
ROW_TILE = 256
SLOT_PAD = N_EXPERTS * PIECE
HS_COLS = D_MODEL + LANE


def _route_meta(sel, wts):
    n_e, t = sel.shape
    e_col = lax.broadcasted_iota(jnp.int32, (n_e, 1), 0).astype(F32)
    before = jnp.where(_tri(t, "lt"), 1.0, 0.0).astype(BF16)
    rank = _dot(sel.astype(BF16), before)
    cnt = jnp.sum(sel, axis=1, keepdims=True)
    plen = jnp.floor((cnt + (PIECE - 1.0)) * (1.0 / PIECE)) * PIECE
    lower = jnp.where(_tri(n_e, "gt"), 1.0, 0.0).astype(BF16)
    loff = _dot(lower, jnp.broadcast_to(plen, (n_e, LANE)).astype(BF16))[:, 0:1]
    slot = loff + rank
    chosen = sel > 0.5
    e_a = jnp.min(jnp.where(chosen, e_col, 2.0 * n_e), axis=0, keepdims=True)
    e_b = jnp.max(jnp.where(chosen, e_col, -1.0), axis=0, keepdims=True)
    take = lambda e_row, x: jnp.sum(jnp.where(e_col == e_row, x, 0.0), axis=0, keepdims=True)
    rows = jnp.concatenate([take(e_a, slot), take(e_b, slot), take(e_a, wts), take(e_b, wts), e_a, e_b,
                            jnp.zeros((2, t), F32)], axis=0)
    return rows, plen, loff


def _merge_kernel(og_ref, ol_ref, od_ref, lg0_ref, lg1_ref, lg2_ref, x_ref, mod_ref, wb_ref, wo_ref,
                  lng_ref, lnb_ref, rw_ref, rb_ref, x1_ref, h_ref, mrow_ref, mcol_ref, plen_ref, loff_ref):
    gate1 = mod_ref[:, 2 * D_MODEL:3 * D_MODEL]
    shift2 = mod_ref[:, 3 * D_MODEL:4 * D_MODEL]
    scale2 = mod_ref[:, 4 * D_MODEL:5 * D_MODEL]
    tm = x_ref.shape[0]
    halves = [pl.ds(r * (tm // 2), tm // 2) for r in range(2)]
    accs = [None, None]
    for s, (o_ref, lg_ref) in enumerate(((og_ref, lg0_ref), (ol_ref, lg1_ref), (od_ref, lg2_ref))):
        projs = [_dot(o_ref[rows, :], wb_ref[s]) for rows in halves]
        terms = [_sigmoid(lg_ref[rows, :].astype(F32)) * p for rows, p in zip(halves, projs)]
        accs = [t if a is None else a + t for a, t in zip(accs, terms)]
    ms = [_mm(a, wo_ref[...]) for a in accs]
    logit_parts = []
    for rows, m in zip(halves, ms):
        x1 = _layernorm(ALPHA * x_ref[rows, :] + gate1 * m, lng_ref[...], lnb_ref[...])
        x1_ref[rows, :] = x1
        h = x1 * (1.0 + scale2) + shift2
        h_ref[rows, :] = h.astype(BF16)
        logit_parts.append(_mm_hi_nt(rw_ref[...], h))
    sel, gates_t = _route(jnp.concatenate(logit_parts, axis=1), rb_ref[...])
    rows, plen, loff = _route_meta(sel, gates_t)
    mrow_ref[...] = rows
    mcol_ref[...] = jnp.concatenate([rows, jnp.zeros((LANE - rows.shape[0], tm), F32)], axis=0).T
    lane = lax.broadcasted_iota(jnp.int32, (N_EXPERTS, LANE), 1)
    cols = jnp.where(lane == 0, plen, jnp.where(lane == 1, loff, 0.0))
    cols_t = jnp.concatenate([cols, jnp.zeros((LANE - N_EXPERTS, LANE), F32)], axis=0).T
    plen_ref[...] = cols_t[0:1, :]
    loff_ref[...] = cols_t[1:2, :]


def _merge(o_gdn, o_gla, o_diff, pa, x, mod, wb, wo, ln_g, ln_b, rw_t, rb_col, layer, row0,
           tokens_per_row, tm):
    n = x.shape[0]
    mod_row = lambda i: (layer, row0 + (i * tm) // tokens_per_row, 0, 0)
    full = lambda shape: pl.BlockSpec(shape, lambda i: (0,) * len(shape))
    return pl.pallas_call(
        _merge_kernel,
        grid=(n // tm,),
        in_specs=[
            pl.BlockSpec((tm, MIX_W), lambda i: (i, 0)),
            pl.BlockSpec((tm, MIX_W), lambda i: (i, 0)),
            pl.BlockSpec((tm, MIX_W), lambda i: (i, 0)),
            pl.BlockSpec((tm, D_MODEL), lambda i: (i, A_MERGE // D_MODEL)),
            pl.BlockSpec((tm, D_MODEL), lambda i: (i, A_MERGE // D_MODEL + 1)),
            pl.BlockSpec((tm, D_MODEL), lambda i: (i, A_MERGE // D_MODEL + 2)),
            pl.BlockSpec((tm, D_MODEL), lambda i: (i, 0)),
            pl.BlockSpec((None, None, 1, 6 * D_MODEL), mod_row),
            full((N_BRANCH, MIX_W, D_MODEL)),
            full((D_MODEL, D_MODEL)),
            full((1, D_MODEL)),
            full((1, D_MODEL)),
            full((N_EXPERTS, D_MODEL)),
            full((N_EXPERTS, 1)),
        ],
        out_specs=[
            pl.BlockSpec((tm, D_MODEL), lambda i: (i, 0)),
            pl.BlockSpec((tm, D_MODEL), lambda i: (i, 0)),
            pl.BlockSpec((8, tm), lambda i: (0, i)),
            pl.BlockSpec((tm, LANE), lambda i: (i, 0)),
            pl.BlockSpec((None, 1, LANE), lambda i: (i, 0, 0)),
            pl.BlockSpec((None, 1, LANE), lambda i: (i, 0, 0)),
        ],
        out_shape=[
            jax.ShapeDtypeStruct((n, D_MODEL), F32),
            jax.ShapeDtypeStruct((n, D_MODEL), BF16),
            jax.ShapeDtypeStruct((8, n), F32),
            jax.ShapeDtypeStruct((n, LANE), F32),
            jax.ShapeDtypeStruct((n // tm, 1, LANE), F32),
            jax.ShapeDtypeStruct((n // tm, 1, LANE), F32),
        ],
        compiler_params=_cparams(("arbitrary",)),
    )(o_gdn, o_gla, o_diff, pa, pa, pa, x, mod, wb, wo, ln_g, ln_b, rw_t, rb_col)


def _plan_kernel(plen_ref, loff_ref, goff_ref, npc_ref, loffi_ref, tiles_ref, tail_ref, *, n_steps):
    plen = plen_ref[...]
    nb = plen.shape[0]
    earlier = jnp.where(_tri(nb, "gt"), 1.0, 0.0).astype(BF16)
    run = _dot(earlier, plen.astype(BF16))
    gtot = jnp.sum(plen, axis=0, keepdims=True)
    lane = lax.broadcasted_iota(jnp.int32, (1, LANE), 1).astype(F32)
    ntile = jnp.floor((gtot + (ROW_TILE - 1.0)) * (1.0 / ROW_TILE))
    before = jnp.where(_tri(LANE, "lt"), 1.0, 0.0).astype(BF16)
    first = _dot(jnp.broadcast_to(ntile, (8, LANE)).astype(BF16), before)[0:1, :]
    end = first + ntile
    n_used = jnp.sum(ntile, axis=1, keepdims=True)
    goff_ref[...] = (first * ROW_TILE + run).astype(jnp.int32)
    npc_ref[...] = (plen * (1.0 / PIECE)).astype(jnp.int32)
    loffi_ref[...] = loff_ref[...].astype(jnp.int32)
    end_col = jnp.concatenate([end, jnp.zeros((LANE - 1, LANE), F32)], axis=0).T[:, 0:1]
    e_col = lax.broadcasted_iota(jnp.int32, (LANE, 1), 0).astype(F32)
    tile = jnp.minimum(lane, n_used - 1.0)
    done = jnp.logical_and(end_col <= tile, e_col < N_EXPERTS)
    t_exp = jnp.minimum(jnp.sum(jnp.where(done, 1.0, 0.0), axis=0, keepdims=True), N_EXPERTS - 1.0)
    nonempty = jnp.where(ntile > 0.5, 1.0, 0.0)
    run_idx = _dot(jnp.broadcast_to(nonempty, (8, LANE)).astype(BF16), before)[0:1, :]
    owner_after = jnp.logical_and(end_col <= end, e_col < N_EXPERTS)
    nxt = jnp.minimum(jnp.sum(jnp.where(owner_after, 1.0, 0.0), axis=0, keepdims=True), N_EXPERTS - 1.0)
    has_next = jnp.where(end < n_used, 1.0, 0.0)
    by_expert = jnp.concatenate([run_idx - 2.0 * jnp.floor(run_idx * 0.5), nxt, has_next,
                                 jnp.zeros((LANE - 3, LANE), F32)], axis=0).T
    mine = e_col == t_exp
    per_tile = [jnp.sum(jnp.where(mine, by_expert[:, j:j + 1], 0.0), axis=0, keepdims=True) for j in range(3)]
    tiles_ref[...] = jnp.concatenate(
        [t_exp, jnp.broadcast_to(n_used, (1, LANE))] + per_tile + [jnp.zeros((3, LANE), F32)],
        axis=0).astype(jnp.int32)
    tail_ref[...] = jnp.concatenate(
        [first * ROW_TILE + gtot, (ntile * ROW_TILE - gtot) * (1.0 / PIECE),
         jnp.broadcast_to(n_used, (1, LANE)), jnp.broadcast_to(n_steps - n_used, (1, LANE)),
         jnp.zeros((4, LANE), F32)], axis=0).astype(jnp.int32)


def _plan(plen, loff, n_steps):
    nb = plen.shape[0]
    i32 = lambda rows: jax.ShapeDtypeStruct((rows, LANE), jnp.int32)
    return pl.pallas_call(
        functools.partial(_plan_kernel, n_steps=n_steps),
        out_shape=[i32(nb), i32(nb), i32(nb), i32(8), i32(8)],
    )(plen, loff)


def _piece_copy(local, remote, sem, to_remote, lo, go):
    src = local.at[pl.ds(pl.multiple_of(lo, PIECE), PIECE)]
    dst = remote.at[pl.ds(pl.multiple_of(go, PIECE), PIECE)]
    return pltpu.make_async_copy(src, dst, sem) if to_remote else pltpu.make_async_copy(dst, src, sem)


def _start_pieces(b, loff_s, npc_s, goff_s, local, remote, sem, to_remote):
    for e in range(N_EXPERTS):
        lo = loff_s[b, e]
        go = goff_s[b, e]

        def start(j, carry, lo=lo, go=go):
            _piece_copy(local, remote, sem, to_remote, lo + j * PIECE, go + j * PIECE).start()
            return carry

        lax.fori_loop(0, npc_s[b, e], start, 0)


def _wait_pieces(b, npc_s, local, remote, sem, to_remote):
    total = 0
    for e in range(N_EXPERTS):
        total = total + npc_s[b, e]

    def wait(j, carry):
        _piece_copy(local, remote, sem, to_remote, 0, 0).wait()
        return carry

    lax.fori_loop(0, total, wait, 0)


def _dispatch_kernel(loff_s, npc_s, goff_s, tail_s, h_ref, mrow_ref, mcol_ref, hs_hbm, local, zeros, sems,
                     tail_sem, tile_sem, *, slots):
    b = pl.program_id(0)
    last = pl.num_programs(0) - 1
    slot = b % 2
    buf = local.at[slot]
    sem = sems.at[slot]

    @pl.when(b >= 2)
    def _():
        _wait_pieces(b - 2, npc_s, buf, hs_hbm, sem, True)

    tb = h_ref.shape[0]
    mrow = mrow_ref[...]
    mcol = mcol_ref[...]
    slot_id = lax.broadcasted_iota(jnp.int32, (slots, tb), 0).astype(F32)
    s_a = jnp.where(slot_id == mrow[0:1, :], 1.0, 0.0).astype(BF16)
    s_b = jnp.where(slot_id == mrow[1:2, :], 1.0, 0.0).astype(BF16)
    lane = lax.broadcasted_iota(jnp.int32, (tb, LANE), 1)

    def weight_lanes(w):
        hi = w.astype(BF16).astype(F32)
        return jnp.where(lane == 0, hi, jnp.where(lane == 1, w - hi, 0.0)).astype(BF16)

    local[slot, :, :D_MODEL] = _dot(s_a + s_b, h_ref[...]).astype(BF16)
    local[slot, :, D_MODEL:] = (_dot(s_a, weight_lanes(mcol[:, 2:3]))
                                + _dot(s_b, weight_lanes(mcol[:, 3:4]))).astype(BF16)
    _start_pieces(b, loff_s, npc_s, goff_s, buf, hs_hbm, sem, True)

    @pl.when(b == last)
    def _():
        _wait_pieces(b, npc_s, buf, hs_hbm, sem, True)

        @pl.when(b >= 1)
        def _():
            _wait_pieces(b - 1, npc_s, local.at[1 - slot], hs_hbm, sems.at[1 - slot], True)

        zeros[...] = jnp.zeros_like(zeros)
        zero_piece = zeros.at[pl.ds(0, PIECE)]
        total = 0
        for e in range(N_EXPERTS):
            n = tail_s[1, e]
            start_row = tail_s[0, e]

            def start(j, carry, start_row=start_row):
                dst = hs_hbm.at[pl.ds(pl.multiple_of(start_row + j * PIECE, PIECE), PIECE)]
                pltpu.make_async_copy(zero_piece, dst, tail_sem).start()
                return carry

            lax.fori_loop(0, n, start, 0)
            total = total + n

        def wait(j, carry):
            pltpu.make_async_copy(zero_piece, hs_hbm.at[pl.ds(0, PIECE)], tail_sem).wait()
            return carry

        lax.fori_loop(0, total, wait, 0)

        def tile_copy(j):
            row = pl.multiple_of((tail_s[2, 0] + j) * ROW_TILE, ROW_TILE)
            return pltpu.make_async_copy(zeros, hs_hbm.at[pl.ds(row, ROW_TILE)], tile_sem)

        def start_tile(j, carry):
            tile_copy(j).start()
            return carry

        def wait_tile(j, carry):
            tile_copy(j).wait()
            return carry

        lax.fori_loop(0, tail_s[3, 0], start_tile, 0)
        lax.fori_loop(0, tail_s[3, 0], wait_tile, 0)


def _dispatch(loff_i, npc, goff, tail, h, mrow, mcol, tb, n_steps):
    n = h.shape[0]
    slots = 2 * tb + SLOT_PAD
    grid_spec = pltpu.PrefetchScalarGridSpec(
        num_scalar_prefetch=4,
        grid=(n // tb,),
        in_specs=[
            pl.BlockSpec((tb, D_MODEL), lambda i, *_: (i, 0)),
            pl.BlockSpec((8, tb), lambda i, *_: (0, i)),
            pl.BlockSpec((tb, LANE), lambda i, *_: (i, 0)),
        ],
        out_specs=pl.BlockSpec(memory_space=pl.ANY),
        scratch_shapes=[pltpu.VMEM((2, slots, HS_COLS), BF16), pltpu.VMEM((ROW_TILE, HS_COLS), BF16),
                        pltpu.SemaphoreType.DMA((2,)), pltpu.SemaphoreType.DMA, pltpu.SemaphoreType.DMA],
    )
    return pl.pallas_call(
        functools.partial(_dispatch_kernel, slots=slots),
        grid_spec=grid_spec,
        out_shape=jax.ShapeDtypeStruct((n_steps * ROW_TILE, HS_COLS), BF16),
        compiler_params=_cparams(("arbitrary",)),
    )(loff_i, npc, goff, tail, h, mrow, mcol)


def _expert_kernel(tiles_s, hs_ref, wg_hbm, wu_hbm, wd_hbm, y_ref, wg_f, wu_f, wd_f, wg_b, wu_b, wd_b, sems,
                   *, layer):
    i = pl.program_id(0)
    e = tiles_s[0, i]
    slot = tiles_s[2, i]
    first_of_run = jnp.logical_or(i == 0, e != tiles_s[0, jnp.maximum(i - 1, 0)])

    def weight_copies(expert, s):
        return [pltpu.make_async_copy(src.at[layer, expert], dst.at[s], sems.at[s, j])
                for j, (src, dst) in enumerate(((wg_hbm, wg_f), (wu_hbm, wu_f), (wd_hbm, wd_f)))]

    @pl.when(i == 0)
    def _():
        for cp in weight_copies(e, slot):
            cp.start()

    @pl.when(jnp.logical_and(first_of_run, i < tiles_s[1, 0]))
    def _():
        for cp in weight_copies(e, slot):
            cp.wait()

        @pl.when(tiles_s[4, i] > 0)
        def _():
            for cp in weight_copies(tiles_s[3, i], 1 - slot):
                cp.start()

        wg_b[...] = wg_f[slot].astype(BF16)
        wu_b[...] = wu_f[slot].astype(BF16)
        wd_b[...] = wd_f[slot].astype(BF16)

    @pl.when(i < tiles_s[1, 0])
    def _():
        halves = [pl.ds(r * (ROW_TILE // 2), ROW_TILE // 2) for r in range(2)]
        xs = [hs_ref[rows, :D_MODEL] for rows in halves]
        ws = [hs_ref[rows, D_MODEL:D_MODEL + 1].astype(F32) + hs_ref[rows, D_MODEL + 1:D_MODEL + 2].astype(F32)
              for rows in halves]
        gs = [_dot(x, wg_b[...]) for x in xs]
        us = [_dot(x, wu_b[...]) for x in xs]
        acts = [(_silu(g) * u * w).astype(BF16) for g, u, w in zip(gs, us, ws)]
        for rows, act in zip(halves, acts):
            y_ref[rows, :] = _dot(act, wd_b[...]).astype(BF16)

    @pl.when(i >= tiles_s[1, 0])
    def _():
        y_ref[...] = jnp.zeros_like(y_ref)


def _experts(tiles, hs, wg, wu, wd, layer, n_steps):
    grid_spec = pltpu.PrefetchScalarGridSpec(
        num_scalar_prefetch=1,
        grid=(n_steps,),
        in_specs=[
            pl.BlockSpec((ROW_TILE, HS_COLS), lambda i, t: (i, 0)),
            pl.BlockSpec(memory_space=pl.ANY),
            pl.BlockSpec(memory_space=pl.ANY),
            pl.BlockSpec(memory_space=pl.ANY),
        ],
        out_specs=pl.BlockSpec((ROW_TILE, D_MODEL), lambda i, t: (i, 0)),
        scratch_shapes=[pltpu.VMEM((2, D_MODEL, D_FF), F32), pltpu.VMEM((2, D_MODEL, D_FF), F32),
                        pltpu.VMEM((2, D_FF, D_MODEL), F32),
                        pltpu.VMEM((D_MODEL, D_FF), BF16), pltpu.VMEM((D_MODEL, D_FF), BF16),
                        pltpu.VMEM((D_FF, D_MODEL), BF16), pltpu.SemaphoreType.DMA((2, 3))],
    )
    return pl.pallas_call(
        functools.partial(_expert_kernel, layer=layer),
        grid_spec=grid_spec,
        out_shape=jax.ShapeDtypeStruct((hs.shape[0], D_MODEL), BF16),
        compiler_params=_cparams(("arbitrary",)),
    )(tiles, hs, wg, wu, wd)


def _combine_kernel(loff_s, npc_s, goff_s, y_hbm, mcol_ref, x1_ref, mod_ref, lng_ref, lnb_ref, o_ref,
                    local, sems, *, slots):
    b = pl.program_id(0)
    tb = x1_ref.shape[0]
    slot = b % 2

    def fetch(blk, s):
        local[s] = jnp.zeros(local.shape[1:], local.dtype)
        _start_pieces(blk, loff_s, npc_s, goff_s, local.at[s], y_hbm, sems.at[s], False)

    @pl.when(b == 0)
    def _():
        fetch(0, 0)

    _wait_pieces(b, npc_s, local.at[slot], y_hbm, sems.at[slot], False)

    @pl.when(b + 1 < pl.num_programs(0))
    def _():
        fetch(b + 1, 1 - slot)

    mcol = mcol_ref[...]
    slot_id = lax.broadcasted_iota(jnp.int32, (tb, slots), 1).astype(F32)
    pick = jnp.logical_or(slot_id == mcol[:, 0:1], slot_id == mcol[:, 1:2])
    f = _dot(jnp.where(pick, 1.0, 0.0).astype(BF16), local[slot])
    gate2 = mod_ref[:, 5 * D_MODEL:6 * D_MODEL]
    o_ref[...] = _layernorm(ALPHA * x1_ref[...] + gate2 * f, lng_ref[...], lnb_ref[...])


def _combine(loff_i, npc, goff, y, mcol, x1, mod, ln_g, ln_b, layer, row0, tokens_per_row, tb):
    n = x1.shape[0]
    slots = 2 * tb + SLOT_PAD
    mod_row = lambda i, *_: (layer, row0 + (i * tb) // tokens_per_row, 0, 0)
    grid_spec = pltpu.PrefetchScalarGridSpec(
        num_scalar_prefetch=3,
        grid=(n // tb,),
        in_specs=[
            pl.BlockSpec(memory_space=pl.ANY),
            pl.BlockSpec((tb, LANE), lambda i, *_: (i, 0)),
            pl.BlockSpec((tb, D_MODEL), lambda i, *_: (i, 0)),
            pl.BlockSpec((None, None, 1, 6 * D_MODEL), mod_row),
            pl.BlockSpec((1, D_MODEL), lambda i, *_: (0, 0)),
            pl.BlockSpec((1, D_MODEL), lambda i, *_: (0, 0)),
        ],
        out_specs=pl.BlockSpec((tb, D_MODEL), lambda i, *_: (i, 0)),
        scratch_shapes=[pltpu.VMEM((2, slots, D_MODEL), BF16), pltpu.SemaphoreType.DMA((2,))],
    )
    return pl.pallas_call(
        functools.partial(_combine_kernel, slots=slots),
        grid_spec=grid_spec,
        out_shape=jax.ShapeDtypeStruct((n, D_MODEL), F32),
        compiler_params=_cparams(("arbitrary",)),
    )(loff_i, npc, goff, y, mcol, x1, mod, ln_g, ln_b)


def _moe(h, mrow, mcol, plen, loff, wg, wu, wd, x1, mod, ln_g, ln_b, layer, row0, tokens_per_row, tb):
    n = h.shape[0]
    nb = n // tb
    n_steps = (2 * n + nb * N_EXPERTS * (PIECE - 1)) // ROW_TILE + N_EXPERTS
    assert n_steps <= LANE
    goff, npc, loff_i, tiles, tail = _plan(plen.reshape(nb, LANE), loff.reshape(nb, LANE), n_steps)
    hs = _dispatch(loff_i, npc, goff, tail, h, mrow, mcol, tb, n_steps)
    y = _experts(tiles, hs, wg, wu, wd, layer, n_steps)
    return _combine(loff_i, npc, goff, y, mcol, x1, mod, ln_g, ln_b, layer, row0, tokens_per_row, tb)


IN_SIZES = (3 * GDN_W, GDN_W, 2 * GDN_HEADS, 2 * GDN_HEADS, GLA_KW, GLA_KW, GLA_VW, GLA_VW,
            2 * GLA_RANK, DIFF_QW, DIFF_QW, DIFF_VW, N_BRANCH * D_MODEL)
IN_OFFS = tuple(int(v) for v in np.concatenate([[0], np.cumsum(IN_SIZES)]))
D_IN = IN_OFFS[-1]


def _reorder_kernel(wt_ref, wa_ref, wf_ref):
    o = IN_OFFS
    for lo, hi, dst in ((o[0], o[2], A_GDN_QKV), (o[4], o[8], A_GLA_Q), (o[9], o[10], A_DIFF_Q),
                        (o[12], o[13], A_MERGE)):
        wa_ref[:, dst:dst + hi - lo] = wt_ref[lo:hi, :].T.astype(BF16)
    wf_ref[:, F_DIFF_K:F_DIFF_K + o[12] - o[10]] = wt_ref[o[10]:o[12], :].T.astype(BF16)
    gates = wt_ref[o[2]:o[4], :]
    low_rank = wt_ref[o[8]:o[9], :]
    assert MISC_B == 0 and MISC_A == o[3] - o[2] and MISC_LR == o[4] - o[2]
    pad = jnp.zeros((LANE - gates.shape[0] - low_rank.shape[0], gates.shape[1]), F32)
    wf_ref[:, F_MISC:F_MISC + LANE] = jnp.concatenate([gates, low_rank, pad], axis=0).T.astype(BF16)


def _reorder_w_in(w_in):
    tr = 256
    return pl.pallas_call(
        _reorder_kernel,
        grid=(DEPTH, D_MODEL // tr),
        in_specs=[pl.BlockSpec((None, D_IN, tr), lambda l, i: (l, 0, i))],
        out_specs=[pl.BlockSpec((None, tr, A_COLS), lambda l, i: (l, i, 0)),
                   pl.BlockSpec((None, tr, F_COLS), lambda l, i: (l, i, 0))],
        out_shape=[jax.ShapeDtypeStruct((DEPTH, D_MODEL, A_COLS), BF16),
                   jax.ShapeDtypeStruct((DEPTH, D_MODEL, F_COLS), BF16)],
        compiler_params=_cparams(("arbitrary", "arbitrary")),
    )(jnp.swapaxes(w_in, 1, 2))


def _lane_row(vals, offset):
    return jnp.zeros((1, LANE), F32).at[0, offset:offset + vals.shape[0]].set(vals)


def kernel(x_prompt, x_sample, c, state_gdn, state_gla, cache_k, cache_v, c_ctx, w_mod, b_mod, w_in,
           gdn_conv, gdn_a_log, gdn_dt_bias, gdn_norm, gla_w_gate, gla_b_gate, gla_norm, diff_lambda,
           diff_norm, w_branch, w_out, ln_g, ln_b, router_w, router_b, exp_w_gate, exp_w_up, exp_w_down):
    bp, tp, d = x_prompt.shape
    bs, ts, _ = x_sample.shape
    pad_rows = MOD_ROWS - 1 - bs
    cvecs = jnp.concatenate([c_ctx[None, :], c, jnp.zeros((pad_rows, d), F32)], axis=0)
    mod = _modulation(cvecs, w_mod, b_mod).reshape(DEPTH, MOD_ROWS, 1, 6 * d)

    rw_t = router_w.T
    rb_col = router_b.reshape(N_EXPERTS, 1)
    ck = cache_k.reshape(bs, DEPTH, cache_k.shape[2], DIFF_QW)
    cv = cache_v.reshape(bs, DEPTH, cache_v.shape[2], DIFF_VW)

    wa, wf = _reorder_w_in(w_in)
    layer_w = []
    for l in range(DEPTH):
        layer_w.append(dict(
            wa=wa, wf=wf,
            alog=_lane_row(gdn_a_log[l].reshape(-1), MISC_A),
            dtb=_lane_row(gdn_dt_bias[l].reshape(-1), MISC_A),
            wb=w_branch[l].astype(BF16),
            wo=w_out[l].astype(BF16),
        ))

    def layer(x, l, row0, tokens_per_row, s_gdn, s_gla, ctx_k, ctx_v, emit_state, tm, prev_gdn=None,
              prev_gla=None, prev_cache=None):
        b, t, _ = x.shape
        n = b * t
        lw = layer_w[l]
        xf = x.reshape(n, d)
        if emit_state:
            pa, pf, *cache = _inproj(xf, mod, lw["wa"], lw["wf"], l, row0, tokens_per_row, t, (b, t), prev_cache)
        else:
            pa, pf = _inproj(xf, mod, lw["wa"], lw["wf"], l, row0, tokens_per_row, tm, None, None)
            cache = None
        pa3 = pa.reshape(b, t, A_COLS)
        pf3 = pf.reshape(b, t, F_COLS)
        o_gdn, gdn_fin = _gdn(pa3, pf3, gdn_conv[l], lw["alog"], lw["dtb"], gdn_norm[l][None, :], s_gdn, l,
                              emit_state, prev_gdn)
        o_gla, gla_fin = _gla(pa3, pf3, gla_w_gate[l], gla_b_gate[l], gla_norm[l][None, :], s_gla, l,
                              emit_state, prev_gla)
        lam_init = 0.8 - 0.6 * math.exp(-0.3 * l)
        o_diff = _diff(pa3, pf3, diff_lambda[l], diff_norm[l][None, :], ctx_k, ctx_v, l, lam_init)
        x1, h, mrow, mcol, plen, loff = _merge(
            o_gdn.reshape(n, MIX_W), o_gla.reshape(n, MIX_W), o_diff.reshape(n, MIX_W), pa, xf, mod,
            lw["wb"], lw["wo"], ln_g[l, 0][None, :], ln_b[l, 0][None, :], rw_t, rb_col, l, row0,
            tokens_per_row, tm)
        x2 = _moe(h, mrow, mcol, plen, loff, exp_w_gate, exp_w_up, exp_w_down, x1, mod,
                  ln_g[l, 1][None, :], ln_b[l, 1][None, :], l, row0, tokens_per_row, tm)
        return x2.reshape(b, t, d), gdn_fin, gla_fin, cache

    hp = x_prompt
    gdn_states, gla_states, cache = None, None, None
    for l in range(DEPTH):
        hp, gdn_states, gla_states, cache = layer(hp, l, 0, bp * tp, None, None, None, None, True, 512,
                                                  gdn_states, gla_states, cache)

    hs = x_sample
    for l in range(DEPTH):
        hs = layer(hs, l, 1, ts, state_gdn, state_gla, ck, cv, False, 512)[0]

    return hp, hs, gdn_states, gla_states, cache[0], cache[1]
```

```python
import functools
import math

import numpy as np
import jax
import jax.numpy as jnp
from jax import lax
from jax.experimental import pallas as pl
from jax.experimental.pallas import tpu as pltpu

F32 = jnp.float32
BF16 = jnp.bfloat16

D_MODEL = 1024
DEPTH = 2
GRID_W = 64
GDN_HEADS = 4
GDN_DK = 128
GDN_DV = 128
GDN_W = GDN_HEADS * GDN_DV
SHORT_CONV = 5
GLA_HEADS = 4
GLA_DK = 64
GLA_DV = 128
GLA_KW = GLA_HEADS * GLA_DK
GLA_VW = GLA_HEADS * GLA_DV
GLA_RANK = 16
GLA_TAU = 16.0
CHUNK = 64
DIFF_HEADS = 4
DIFF_DH = 64
DIFF_VD = 2 * DIFF_DH
DIFF_QW = DIFF_HEADS * 2 * DIFF_DH
DIFF_VW = DIFF_HEADS * DIFF_VD
ROPE_THETA = 10000.0
MIX_W = 512
N_BRANCH = 3
N_EXPERTS = 16
N_GROUPS = 4
GROUP_SIZE = N_EXPERTS // N_GROUPS
D_FF = 512
ALPHA = (2 * DEPTH) ** 0.25
EPS = 1e-6

LANE = 128
MOD_ROWS = 8
VMEM_LIMIT = 56 * 1024 * 1024
TOKEN_BLOCK = 512

A_GDN_QKV = 0
A_GDN_Z = 1536
A_GLA_Q = 2048
A_GLA_K = 2304
A_GLA_V = 2560
A_GLA_R = 3072
A_DIFF_Q = 3584
A_MERGE = 4096
A_COLS = 7168
F_DIFF_K = 0
F_DIFF_V = 512
F_MISC = 1024
F_COLS = 1152
MISC_B = 0
MISC_A = 8
MISC_LR = 16


def _cparams(sem):
    return pltpu.CompilerParams(dimension_semantics=sem, vmem_limit_bytes=VMEM_LIMIT)


def _dot(a, b):
    return jnp.dot(a, b, preferred_element_type=F32)


def _mm(a, b):
    return _dot(a.astype(BF16), b.astype(BF16))


def _mm_nt(a, b):
    return lax.dot_general(a.astype(BF16), b.astype(BF16), (((1,), (1,)), ((), ())),
                           preferred_element_type=F32)


def _split3(x):
    hi = x.astype(BF16)
    r = x - hi.astype(F32)
    mid = r.astype(BF16)
    lo = (r - mid.astype(F32)).astype(BF16)
    return hi, mid, lo


def _mm_mask_lhs(mask_bf16, x):
    hi, mid, lo = _split3(x)
    return _dot(mask_bf16, hi) + _dot(mask_bf16, mid) + _dot(mask_bf16, lo)


def _mm_mask_rhs(x, mask_bf16):
    hi, mid, lo = _split3(x)
    return _dot(hi, mask_bf16) + _dot(mid, mask_bf16) + _dot(lo, mask_bf16)


def _mm_hi(a, b):
    ah = a.astype(BF16)
    al = (a - ah.astype(F32)).astype(BF16)
    bh = b.astype(BF16)
    bl = (b - bh.astype(F32)).astype(BF16)
    return _dot(ah, bh) + _dot(ah, bl) + _dot(al, bh)


def _mm_hi_nt(a, b):
    ah = a.astype(BF16)
    al = (a - ah.astype(F32)).astype(BF16)
    bh = b.astype(BF16)
    bl = (b - bh.astype(F32)).astype(BF16)
    dot = lambda x, y: lax.dot_general(x, y, (((1,), (1,)), ((), ())), preferred_element_type=F32)
    return dot(ah, bh) + dot(ah, bl) + dot(al, bh)


def _sigmoid(x):
    return 1.0 / (1.0 + jnp.exp(-x))


def _silu(x):
    return x * _sigmoid(x)


def _softplus(x):
    return jnp.maximum(x, 0.0) + jnp.log1p(jnp.exp(-jnp.abs(x)))


def _rms(x, g):
    return x * lax.rsqrt(jnp.mean(x * x, axis=-1, keepdims=True) + EPS) * g


def _layernorm(y, g, b):
    mu = jnp.mean(y, axis=-1, keepdims=True)
    d = y - mu
    var = jnp.mean(d * d, axis=-1, keepdims=True)
    return d * lax.rsqrt(var + EPS) * g + b


def _tri(n, kind):
    r = lax.broadcasted_iota(jnp.int32, (n, n), 0)
    c = lax.broadcasted_iota(jnp.int32, (n, n), 1)
    return {"ge": r >= c, "gt": r > c, "le": r <= c, "lt": r < c}[kind]


def _mod_kernel(c_ref, w_ref, b_ref, o_ref):
    s = _silu(c_ref[...])
    s_hi = s.astype(BF16)
    s_lo = (s - s_hi.astype(F32)).astype(BF16)
    w = w_ref[...].astype(BF16)
    o_ref[...] = _dot(s_hi, w) + _dot(s_lo, w) + b_ref[...]


def _modulation(cvecs, w_mod, b_mod):
    tn = 1536
    return pl.pallas_call(
        _mod_kernel,
        grid=(DEPTH, 6 * D_MODEL // tn),
        in_specs=[
            pl.BlockSpec((MOD_ROWS, D_MODEL), lambda l, j: (0, 0)),
            pl.BlockSpec((None, D_MODEL, tn), lambda l, j: (l, 0, j)),
            pl.BlockSpec((None, 1, tn), lambda l, j: (l, 0, j)),
        ],
        out_specs=pl.BlockSpec((None, MOD_ROWS, tn), lambda l, j: (l, 0, j)),
        out_shape=jax.ShapeDtypeStruct((DEPTH, MOD_ROWS, 6 * D_MODEL), F32),
        compiler_params=_cparams(("arbitrary", "arbitrary")),
    )(cvecs, w_mod, b_mod.reshape(DEPTH, 1, 6 * D_MODEL))


INPROJ_COL_STEP = 1792


def _inproj_kernel(*refs, cache_mode):
    x_ref, mod_ref, wa_ref, wf_ref = refs[:4]
    outs = refs[6:] if cache_mode == "later" else refs[4:]
    pa_ref, pf_ref = outs[:2]
    shift = mod_ref[:, 0:D_MODEL]
    scale = mod_ref[:, D_MODEL:2 * D_MODEL]
    u = (x_ref[...] * (1.0 + scale) + shift).astype(BF16)
    pf = _dot(u, wf_ref[...])
    pf_ref[...] = pf
    if cache_mode:
        for o_ref, base in zip(outs[2:4], (F_DIFF_K, F_DIFF_V)):
            if cache_mode == "first":
                for l in range(1, DEPTH):
                    o_ref[l] = jnp.zeros(o_ref.shape[1:], F32)
                o_ref = o_ref.at[0]
            for h in range(DIFF_HEADS):
                o_ref[:, h, :] = pf[:, base + h * LANE:base + (h + 1) * LANE]
    for c0 in range(0, A_COLS, INPROJ_COL_STEP):
        cols = slice(c0, c0 + INPROJ_COL_STEP)
        pa_ref[:, cols] = _dot(u, wa_ref[:, cols]).astype(BF16)


def _inproj(x, mod, wa, wf, layer, row0, tokens_per_row, tm, cache_shape, prev_cache):
    n = x.shape[0]
    mod_row = lambda i: (layer, row0 + (i * tm) // tokens_per_row, 0, 0)
    out_specs = [
        pl.BlockSpec((tm, A_COLS), lambda i: (i, 0)),
        pl.BlockSpec((tm, F_COLS), lambda i: (i, 0)),
    ]
    out_shape = [
        jax.ShapeDtypeStruct((n, A_COLS), BF16),
        jax.ShapeDtypeStruct((n, F_COLS), F32),
    ]
    resident = lambda shape: pl.BlockSpec((None,) + shape, lambda i: (layer, 0, 0), pipeline_mode=pl.Buffered(1))
    in_specs = [
        pl.BlockSpec((tm, D_MODEL), lambda i: (i, 0)),
        pl.BlockSpec((None, None, 1, 6 * D_MODEL), mod_row),
        resident((D_MODEL, A_COLS)),
        resident((D_MODEL, F_COLS)),
    ]
    args = [x, mod, wa, wf]
    cache_mode, aliases = None, {}
    if cache_shape is not None:
        b, t = cache_shape
        assert tm == t
        out_shape += [jax.ShapeDtypeStruct((b, DEPTH, t, DIFF_HEADS, LANE), F32)] * 2
        if prev_cache is None:
            cache_mode = "first"
            out_specs += [pl.BlockSpec((None, DEPTH, t, DIFF_HEADS, LANE), lambda i: (i, 0, 0, 0, 0))] * 2
        else:
            cache_mode = "later"
            out_specs += [pl.BlockSpec((None, None, t, DIFF_HEADS, LANE), lambda i: (i, layer, 0, 0, 0))] * 2
            in_specs += [pl.BlockSpec(memory_space=pl.ANY)] * 2
            args += list(prev_cache)
            aliases = {4: 2, 5: 3}
    return pl.pallas_call(
        functools.partial(_inproj_kernel, cache_mode=cache_mode),
        grid=(n // tm,),
        in_specs=in_specs,
        out_specs=out_specs,
        out_shape=out_shape,
        input_output_aliases=aliases,
        compiler_params=_cparams(("arbitrary",)),
    )(*args)


def _conv_masks(shape, t_len):
    row = lax.broadcasted_iota(jnp.int32, shape, 0)
    half = SHORT_CONV // 2
    return {d: jnp.logical_and(row + d >= 0, row + d < t_len) for d in range(-half, half + 1) if d}


def _short_conv(x, w, t_len, masks):
    half = SHORT_CONV // 2
    acc = x * w[half:half + 1, :]
    for d, valid in masks.items():
        shifted = pltpu.roll(x, (-d) % t_len, axis=0)
        acc = acc + jnp.where(valid, shifted, 0.0) * w[half + d:half + d + 1, :]
    return acc


GDN_PAIRS = 2


def _solve_unit_tri(a_list, rhs_list):
    n = a_list[0].shape[0]
    assert n == 64
    eye = jnp.where(_tri(n, "ge") & _tri(n, "le"), 1.0, 0.0)
    a_b = [a.astype(BF16) for a in a_list]
    bf = lambda xs: [x.astype(BF16) for x in xs]
    mul = lambda xs, ys: [_dot(x, y) for x, y in zip(xs, ys)]
    pair = lambda xs, ys, xb, yb: [x + y + xy for x, y, xy in zip(xs, ys, mul(xb, yb))]
    p0 = [-a for a in a_list]
    p0b = [-a for a in a_b]
    p1 = mul(p0b, p0b)
    p1b = bf(p1)
    p2 = mul(p1b, p1b)
    n01 = pair(p0, p1, p0b, p1b)
    p2b = bf(p2)
    p3 = mul(p2b, p2b)
    p3b = bf(p3)
    p4 = mul(p3b, p3b)
    n23 = pair(p2, p3, p2b, p3b)
    p4b = bf(p4)
    p5 = mul(p4b, p4b)
    n03 = pair(n01, n23, bf(n01), bf(n23))
    p5b = bf(p5)
    n45 = pair(p4, p5, p4b, p5b)
    inv_b = [(eye + x).astype(BF16) for x in pair(n03, n45, bf(n03), bf(n45))]
    sol = [_dot(x, r.astype(BF16)) for x, r in zip(inv_b, rhs_list)]
    resid = [r - s - _dot(a, s.astype(BF16)) for r, s, a in zip(rhs_list, sol, a_b)]
    return [s + _dot(x, r.astype(BF16)) for s, x, r in zip(sol, inv_b, resid)]


def _gdn_kernel(*refs, t_len, has_state, emit_state):
    it = iter(refs)
    qkv_ref, z_ref, misc_ref, cw_ref, alog_ref, dtb_ref, g_ref = (next(it) for _ in range(7))
    s0_ref = next(it) if has_state else None
    if emit_state == "later":
        next(it)
    o_ref = next(it)
    sfin_ref = next(it) if emit_state else None
    (q_s, k_s, v_s, kt_s, bet, gat, gat_t, cumc, cumr, u_s, w_s, qd_s, a_s, ktl_s, gts, o_f, o_b,
     s_s) = (next(it) for _ in range(18))

    c = CHUNK
    n_chunks = t_len // c
    nh = GDN_HEADS
    dk = GDN_DK

    masks = _conv_masks((t_len, dk), t_len)
    for h in range(nh):
        hs = slice(h * dk, (h + 1) * dk)
        q, k, v = (
            _silu(_short_conv(qkv_ref[:, j * GDN_W + h * dk:j * GDN_W + (h + 1) * dk].astype(F32),
                              cw_ref[:, j * GDN_W + h * dk:j * GDN_W + (h + 1) * dk], t_len, masks))
            for j in range(3))
        q = q * lax.rsqrt(jnp.sum(q * q, axis=-1, keepdims=True) + EPS) * (dk ** -0.5)
        k = k * lax.rsqrt(jnp.sum(k * k, axis=-1, keepdims=True) + EPS)
        q_s[:, hs] = q
        k_s[:, hs] = k
        v_s[:, hs] = v
        kt_s[hs, :] = k.T.astype(BF16)
    misc = misc_ref[...]
    bet[...] = _sigmoid(misc)
    g_all = -jnp.exp(alog_ref[...]) * _softplus(misc + dtb_ref[...])
    gat[...] = g_all
    gat_t[...] = g_all.T[MISC_A:MISC_A + 2 * nh, :]
    for d in range(2):
        for h in range(nh):
            s_s[d * nh + h] = s0_ref[d, h] if has_state else jnp.zeros((dk, GDN_DV), F32)

    two = 2 * c
    r2 = lax.broadcasted_iota(jnp.int32, (two, two), 0)
    c2 = lax.broadcasted_iota(jnp.int32, (two, two), 1)
    same = (r2 >= c) == (c2 >= c)
    lo2 = jnp.where(jnp.logical_and(same, r2 >= c2), 1.0, 0.0).astype(BF16)
    up2 = jnp.where(jnp.logical_and(same, r2 <= c2), 1.0, 0.0).astype(BF16)
    incl = (_tri(c, "ge"), _tri(c, "le"))
    strict = (_tri(c, "gt"), _tri(c, "lt"))
    last = (c - 1, 0)

    def cumulate(cp, carry):
        r0 = pl.multiple_of(cp * two, two)
        g_blk = gat[pl.ds(r0, two), :]
        gt_blk = gat_t[:, pl.ds(r0, two)]
        cumc[0, pl.ds(r0, two), :] = _mm_mask_lhs(lo2, g_blk)
        cumc[1, pl.ds(r0, two), :] = _mm_mask_lhs(up2, g_blk)
        cumr[0, :, pl.ds(r0, two)] = _mm_mask_rhs(gt_blk, up2)
        cumr[1, :, pl.ds(r0, two)] = _mm_mask_rhs(gt_blk, lo2)
        return carry

    lax.fori_loop(0, n_chunks // 2, cumulate, 0)

    def phase1(step, carry):
        chains, a_list, rhs_list = [], [], []
        for pp in range(GDN_PAIRS):
            gather_chains(step * GDN_PAIRS + pp, chains, a_list, rhs_list)
        for (idx, rows), sol in zip(chains, _solve_unit_tri(a_list, rhs_list)):
            u_s[idx, rows, :] = sol[:, :GDN_DV]
            w_s[idx, rows, :] = sol[:, GDN_DV:].astype(BF16)
        return carry

    def gather_chains(cp, chains, a_list, rhs_list):
        r0 = pl.multiple_of(cp * two, two)
        b_blk = bet[pl.ds(r0, two), :]
        cum_col = (cumc[0, pl.ds(r0, two), :], cumc[1, pl.ds(r0, two), :])
        cum_row = (cumr[0, :, pl.ds(r0, two)], cumr[1, :, pl.ds(r0, two)])
        for h in range(nh):
            hs = slice(h * dk, (h + 1) * dk)
            kt_pair = kt_s[hs, pl.ds(r0, two)]
            for s in range(2):
                rows = pl.ds(pl.multiple_of(r0 + s * c, c), c)
                q_c = q_s[rows, hs]
                k_c = k_s[rows, hs]
                v_c = v_s[rows, hs]
                kt_c = kt_pair[:, s * c:(s + 1) * c]
                kk = _dot(k_c.astype(BF16), kt_c)
                qk = _dot(q_c.astype(BF16), kt_c)
                for d in range(2):
                    idx = d * nh + h
                    gc = cum_col[d][s * c:(s + 1) * c, MISC_A + idx:MISC_A + idx + 1]
                    gcr = cum_row[d][idx:idx + 1, s * c:(s + 1) * c]
                    beta = b_blk[s * c:(s + 1) * c, MISC_B + idx:MISC_B + idx + 1]
                    decay = jnp.where(incl[d], jnp.exp(jnp.where(incl[d], gc - gcr, 0.0)), 0.0)
                    e_gc = jnp.exp(gc)
                    a_list.append(jnp.where(strict[d], beta * kk * decay, 0.0))
                    rhs_list.append(jnp.concatenate([v_c * beta, k_c * (beta * e_gc)], axis=1))
                    a_s[idx, rows, :] = jnp.where(incl[d], qk * decay, 0.0).astype(BF16)
                    qd_s[idx, rows, :] = (q_c * e_gc).astype(BF16)
                    g_last = gcr[:, last[d]:last[d] + 1]
                    ktl_s[idx, cp * 2 + s] = (kt_c.astype(F32) * jnp.exp(g_last - gcr)).astype(BF16)
                    gts[cp * 2 + s, idx:idx + 1, :] = jnp.broadcast_to(jnp.exp(g_last), (1, LANE))
                    chains.append((idx, rows))

    lax.fori_loop(0, n_chunks // (2 * GDN_PAIRS), phase1, 0)

    def phase2(i, carry):
        chains = []
        for d, out in ((0, o_f), (1, o_b)):
            ci = i if d == 0 else n_chunks - 1 - i
            rows = pl.ds(pl.multiple_of(ci * c, c), c)
            g_tail = gts[ci]
            for h in range(nh):
                chains.append((d * nh + h, h, ci, rows, out, g_tail))
        s_f32 = [s_s[idx] for idx, *_ in chains]
        s_b = [s.astype(BF16) for s in s_f32]
        v_b = [(u_s[idx, rows, :] - _dot(w_s[idx, rows, :], sb)).astype(BF16)
               for (idx, _, _, rows, _, _), sb in zip(chains, s_b)]
        for (idx, h, ci, rows, out, g_tail), s, sb, vb in zip(chains, s_f32, s_b, v_b):
            s_s[idx] = s * g_tail[idx:idx + 1, :] + _dot(ktl_s[idx, ci], vb)
            out[rows, h * GDN_DV:(h + 1) * GDN_DV] = _dot(qd_s[idx, rows, :], sb) + _dot(
                a_s[idx, rows, :], vb)
        return carry

    lax.fori_loop(0, n_chunks, phase2, 0)

    for h in range(nh):
        hs = slice(h * GDN_DV, (h + 1) * GDN_DV)
        o = o_f[:, hs] + o_b[:, hs]
        o_ref[:, hs] = (_rms(o, g_ref[...]) * _silu(z_ref[:, hs].astype(F32))).astype(BF16)
    if emit_state:
        for d in range(2):
            for h in range(nh):
                _store_state(sfin_ref, emit_state, d, h, s_s[d * nh + h])


def _store_state(sfin_ref, mode, d, h, s):
    if mode == "first":
        sfin_ref[0, d, h] = s
        for l in range(1, DEPTH):
            sfin_ref[l, d, h] = jnp.zeros_like(s)
    else:
        sfin_ref[d, h] = s


def _state_output(b, layer, heads, dk, dv, prev, in_specs, args):
    shape = jax.ShapeDtypeStruct((b, DEPTH, 2, heads, dk, dv), F32)
    if prev is None:
        return "first", pl.BlockSpec((None, DEPTH, 2, heads, dk, dv), lambda bi: (bi, 0, 0, 0, 0, 0)), shape, {}
    in_specs.append(pl.BlockSpec(memory_space=pl.ANY))
    args.append(prev)
    spec = pl.BlockSpec((None, None, 2, heads, dk, dv), lambda bi: (bi, layer, 0, 0, 0, 0))
    return "later", spec, shape, {len(args) - 1: 1}


def _gdn(pa3, pf3, conv_w, alog_row, dtb_row, norm_g, s0, layer, emit_state, prev_states):
    b, t, _ = pa3.shape
    full = lambda shape: pl.BlockSpec(shape, lambda bi: (0,) * len(shape))
    in_specs = [
        pl.BlockSpec((None, t, 3 * GDN_W), lambda bi: (bi, 0, A_GDN_QKV // (3 * GDN_W))),
        pl.BlockSpec((None, t, GDN_W), lambda bi: (bi, 0, A_GDN_Z // GDN_W)),
        pl.BlockSpec((None, t, LANE), lambda bi: (bi, 0, F_MISC // LANE)),
        full((SHORT_CONV, 3 * GDN_W)),
        full((1, LANE)),
        full((1, LANE)),
        full((1, GDN_DV)),
    ]
    args = [pa3, pa3, pf3, conv_w, alog_row, dtb_row, norm_g]
    if s0 is not None:
        in_specs.append(pl.BlockSpec((None, None, 2, GDN_HEADS, GDN_DK, GDN_DV),
                                     lambda bi: (bi, layer, 0, 0, 0, 0)))
        args.append(s0)
    out_specs = [pl.BlockSpec((None, t, GDN_W), lambda bi: (bi, 0, 0))]
    out_shape = [jax.ShapeDtypeStruct((b, t, GDN_W), BF16)]
    mode, aliases = None, {}
    if emit_state:
        mode, spec, shape, aliases = _state_output(b, layer, GDN_HEADS, GDN_DK, GDN_DV, prev_states, in_specs, args)
        out_specs.append(spec)
        out_shape.append(shape)
    nc = t // CHUNK
    nd = 2 * GDN_HEADS
    wide = pltpu.VMEM((t, GDN_W), F32)
    scratch = [
        wide, wide, wide,
        pltpu.VMEM((GDN_W, t), BF16),
        pltpu.VMEM((t, LANE), F32), pltpu.VMEM((t, LANE), F32),
        pltpu.VMEM((nd, t), F32),
        pltpu.VMEM((2, t, LANE), F32), pltpu.VMEM((2, nd, t), F32),
        pltpu.VMEM((nd, t, GDN_DV), F32),
        pltpu.VMEM((nd, t, GDN_DK), BF16),
        pltpu.VMEM((nd, t, GDN_DK), BF16),
        pltpu.VMEM((nd, t, CHUNK), BF16),
        pltpu.VMEM((nd, nc, GDN_DK, CHUNK), BF16),
        pltpu.VMEM((nc, nd, LANE), F32),
        wide, wide,
        pltpu.VMEM((nd, GDN_DK, GDN_DV), F32),
    ]
    outs = pl.pallas_call(
        functools.partial(_gdn_kernel, t_len=t, has_state=s0 is not None, emit_state=mode),
        grid=(b,),
        in_specs=in_specs,
        out_specs=out_specs,
        out_shape=out_shape,
        scratch_shapes=scratch,
        input_output_aliases=aliases,
        compiler_params=_cparams(("arbitrary",)),
    )(*args)
    return outs if emit_state else (outs[0], None)


def _gla_kernel(*refs, t_len, has_state, emit_state):
    it = iter(refs)
    q_ref, k_ref, v_ref, r_ref, misc_ref, wg_ref, bg_ref, g_ref = (next(it) for _ in range(8))
    s0_ref = next(it) if has_state else None
    if emit_state == "later":
        next(it)
    o_ref = next(it)
    sfin_ref = next(it) if emit_state else None
    glog, vt_s, o_f, o_b, st = (next(it) for _ in range(5))

    c = CHUNK
    two = 2 * c
    n_pairs = t_len // two
    nh = GLA_HEADS
    misc = misc_ref[...]
    for d in range(2):
        lr = misc[:, MISC_LR + d * GLA_RANK:MISC_LR + (d + 1) * GLA_RANK]
        logits = _mm_hi(lr, wg_ref[d]) + bg_ref[d:d + 1, :]
        glog[d] = -_softplus(-logits) / GLA_TAU
    for h in range(nh):
        hs = slice(h * GLA_DV, (h + 1) * GLA_DV)
        vt_s[hs, :] = v_ref[:, hs].astype(F32).T.astype(BF16)
    for d in range(2):
        for h in range(nh):
            if has_state:
                s0 = jnp.concatenate([s0_ref[d, h], jnp.zeros((GLA_DV - GLA_DK, GLA_DV), F32)], axis=0)
                st[d * nh + h] = s0.T[:, :GLA_DK]
            else:
                st[d * nh + h] = jnp.zeros((GLA_DV, GLA_DK), F32)

    r2 = lax.broadcasted_iota(jnp.int32, (two, two), 0)
    c2 = lax.broadcasted_iota(jnp.int32, (two, two), 1)
    same = (r2 >= c) == (c2 >= c)
    cum_mask = (jnp.where(jnp.logical_and(same, r2 >= c2), 1.0, 0.0).astype(BF16),
                jnp.where(jnp.logical_and(same, r2 <= c2), 1.0, 0.0).astype(BF16))
    incl = (_tri(c, "ge"), _tri(c, "le"))
    last = (c - 1, 0)
    mid = (c // 2, c - 1 - c // 2)
    order = ((0, 1), (1, 0))

    def cumulate(cp, carry):
        r0 = pl.multiple_of(cp * two, two)
        for d in range(2):
            glog[d, pl.ds(r0, two), :] = _mm_mask_lhs(cum_mask[d], glog[d, pl.ds(r0, two), :])
        return carry

    lax.fori_loop(0, n_pairs, cumulate, 0)

    def body(i, carry):
        pieces = []
        for d in range(2):
            r0 = pl.multiple_of((i if d == 0 else n_pairs - 1 - i) * two, two)
            b_all = glog[d, pl.ds(r0, two), :]
            q_all = q_ref[pl.ds(r0, two), :].astype(F32) * (GLA_DK ** -0.5)
            k_all = k_ref[pl.ds(r0, two), :].astype(F32)
            vt_pair = [vt_s[h * GLA_DV:(h + 1) * GLA_DV, pl.ds(r0, two)] for h in range(nh)]
            for s in range(2):
                rs = slice(s * c, (s + 1) * c)
                bq = b_all[rs]
                b_mid = bq[mid[d]:mid[d] + 1, :]
                b_last = bq[last[d]:last[d] + 1, :]
                qe = (q_all[rs] * jnp.exp(bq - b_mid)).astype(BF16)
                ke = (k_all[rs] * jnp.exp(b_mid - bq)).astype(BF16)
                qd = (q_all[rs] * jnp.exp(bq)).astype(BF16)
                kt = (k_all[rs] * jnp.exp(b_last - bq)).astype(BF16)
                g_last = jnp.exp(b_last)
                rows = pl.ds(pl.multiple_of(r0 + s * c, c), c)
                for h in range(nh):
                    ks = slice(h * GLA_DK, (h + 1) * GLA_DK)
                    pieces.append((d, s, h, rows, qe[:, ks], ke[:, ks], qd[:, ks], kt[:, ks], g_last[:, ks],
                                   vt_pair[h][:, rs], v_ref[rows, h * GLA_DV:(h + 1) * GLA_DV]))
        a_intra = [jnp.where(incl[p[0]], _mm_nt(p[4], p[5]), 0.0).astype(BF16) for p in pieces]
        upd = [_dot(p[9], p[7]) for p in pieces]
        intra = [_dot(a, p[10]) for a, p in zip(a_intra, pieces)]
        for step in range(2):
            for p, m, o_in in zip(pieces, upd, intra):
                d, s, h, rows = p[0], p[1], p[2], p[3]
                if s != order[d][step]:
                    continue
                s_t = st[d * nh + h]
                out = o_f if d == 0 else o_b
                out[rows, h * GLA_DV:(h + 1) * GLA_DV] = _mm_nt(p[6], s_t) + o_in
                st[d * nh + h] = s_t * p[8] + m
        return carry

    lax.fori_loop(0, n_pairs, body, 0)

    for h in range(nh):
        hs = slice(h * GLA_DV, (h + 1) * GLA_DV)
        o = o_f[:, hs] + o_b[:, hs]
        o_ref[:, hs] = (_rms(o, g_ref[...]) * _silu(r_ref[:, hs].astype(F32))).astype(BF16)
    if emit_state:
        for d in range(2):
            for h in range(nh):
                s_pad = jnp.concatenate([st[d * nh + h], jnp.zeros((GLA_DV, GLA_DV - GLA_DK), F32)], axis=1)
                _store_state(sfin_ref, emit_state, d, h, s_pad.T[:GLA_DK, :])


def _gla(pa3, pf3, wg, bg, norm_g, s0, layer, emit_state, prev_states):
    b, t, _ = pa3.shape
    full = lambda shape: pl.BlockSpec(shape, lambda bi: (0,) * len(shape))
    in_specs = [
        pl.BlockSpec((None, t, GLA_KW), lambda bi: (bi, 0, A_GLA_Q // GLA_KW)),
        pl.BlockSpec((None, t, GLA_KW), lambda bi: (bi, 0, A_GLA_K // GLA_KW)),
        pl.BlockSpec((None, t, GLA_VW), lambda bi: (bi, 0, A_GLA_V // GLA_VW)),
        pl.BlockSpec((None, t, GLA_VW), lambda bi: (bi, 0, A_GLA_R // GLA_VW)),
        pl.BlockSpec((None, t, LANE), lambda bi: (bi, 0, F_MISC // LANE)),
        full((2, GLA_RANK, GLA_KW)),
        full((2, GLA_KW)),
        full((1, GLA_DV)),
    ]
    args = [pa3, pa3, pa3, pa3, pf3, wg, bg, norm_g]
    if s0 is not None:
        in_specs.append(pl.BlockSpec((None, None, 2, GLA_HEADS, GLA_DK, GLA_DV),
                                     lambda bi: (bi, layer, 0, 0, 0, 0)))
        args.append(s0)
    out_specs = [pl.BlockSpec((None, t, GLA_VW), lambda bi: (bi, 0, 0))]
    out_shape = [jax.ShapeDtypeStruct((b, t, GLA_VW), BF16)]
    mode, aliases = None, {}
    if emit_state:
        mode, spec, shape, aliases = _state_output(b, layer, GLA_HEADS, GLA_DK, GLA_DV, prev_states, in_specs, args)
        out_specs.append(spec)
        out_shape.append(shape)
    scratch = [pltpu.VMEM((2, t, GLA_KW), F32),
               pltpu.VMEM((GLA_VW, t), BF16),
               pltpu.VMEM((t, GLA_VW), F32), pltpu.VMEM((t, GLA_VW), F32),
               pltpu.VMEM((2 * GLA_HEADS, GLA_DV, GLA_DK), F32)]
    outs = pl.pallas_call(
        functools.partial(_gla_kernel, t_len=t, has_state=s0 is not None, emit_state=mode),
        grid=(b,),
        in_specs=in_specs,
        out_specs=out_specs,
        out_shape=out_shape,
        scratch_shapes=scratch,
        input_output_aliases=aliases,
        compiler_params=_cparams(("arbitrary",)),
    )(*args)
    return outs if emit_state else (outs[0], None)


ATT_Q_BLOCK = 256


def _rope_tables(t_len):
    half = DIFF_DH // 2
    quarter = half // 2
    inv = ROPE_THETA ** (-np.arange(0, half, 2, dtype=np.float64) / half)
    tok = np.arange(t_len)
    pos = np.stack([tok // GRID_W, tok % GRID_W], axis=1).astype(np.float64)
    ang = pos[:, :, None] * inv[None, None, :]
    cos = np.concatenate([np.cos(ang), np.cos(ang)], axis=-1).reshape(t_len, DIFF_DH)
    sin = np.concatenate([-np.sin(ang), np.sin(ang)], axis=-1).reshape(t_len, DIFF_DH)
    cos = np.concatenate([cos, cos], axis=-1).astype(np.float32)
    sin = np.concatenate([sin, sin], axis=-1).astype(np.float32)
    first = ((np.arange(2 * DIFF_DH) % half) < quarter).astype(np.float32)[None, :]
    return jnp.asarray(cos), jnp.asarray(sin), jnp.asarray(first), quarter


def _rope(x, cos, sin, first, quarter):
    width = x.shape[-1]
    ahead = pltpu.roll(x, width - quarter, axis=1)
    behind = pltpu.roll(x, quarter, axis=1)
    partner = jnp.where(first > 0.5, ahead, behind)
    return x * cos + partner * sin


def _diff_kernel(*refs, t_len, ctx_len, lam_init):
    it = iter(refs)
    q_ref, k_ref, v_ref, lam_ref, g_ref = (next(it) for _ in range(5))
    if ctx_len:
        ck_ref, cv_ref, cos_ref, sin_ref, first_ref = (next(it) for _ in range(5))
    o_ref = next(it)
    q_sc, k_sc, v_sc = (next(it) for _ in range(3))

    scale = DIFF_DH ** -0.5
    for h in range(DIFF_HEADS):
        hs = slice(h * LANE, (h + 1) * LANE)
        q = q_ref[:, hs].astype(F32)
        k = k_ref[:, hs]
        if ctx_len:
            quarter = DIFF_DH // 4
            q = _rope(q, cos_ref[...], sin_ref[...], first_ref[...], quarter)
            k = _rope(k, cos_ref[...], sin_ref[...], first_ref[...], quarter)
        q_sc[:, hs] = (q * scale).astype(BF16)
        k_sc[0:t_len, hs] = k.astype(BF16)
    v_sc[0:t_len, :] = v_ref[...].astype(BF16)
    if ctx_len:
        k_sc[t_len:t_len + ctx_len, :] = ck_ref[...].astype(BF16)
        v_sc[t_len:t_len + ctx_len, :] = cv_ref[...].astype(BF16)

    lp = lam_ref[...]
    lam = (jnp.exp(jnp.sum(lp[0:1, :] * lp[1:2, :], axis=1, keepdims=True))
           - jnp.exp(jnp.sum(lp[2:3, :] * lp[3:4, :], axis=1, keepdims=True)) + lam_init)

    tq = min(ATT_Q_BLOCK, t_len)
    group = DIFF_HEADS if t_len + ctx_len <= 512 else 2

    def body(i, carry):
        r0 = pl.multiple_of(i * tq, tq)
        for h0 in range(0, DIFF_HEADS, group):
            parts = [(h, half) for h in range(h0, h0 + group) for half in range(2)]
            cols = [slice(h * LANE + half * DIFF_DH, h * LANE + (half + 1) * DIFF_DH) for h, half in parts]
            s = [_mm_nt(q_sc[pl.ds(r0, tq), c], k_sc[:, c]) for c in cols]
            e = [jnp.exp(x - jnp.max(x, axis=-1, keepdims=True)) for x in s]
            inv_l = [1.0 / jnp.sum(x, axis=-1, keepdims=True) for x in e]
            pv = [_dot(x.astype(BF16), v_sc[:, h * LANE:(h + 1) * LANE]) * r
                  for x, r, (h, _) in zip(e, inv_l, parts)]
            for j in range(group):
                h = h0 + j
                o = pv[2 * j] - lam * pv[2 * j + 1]
                o_ref[pl.ds(r0, tq), h * LANE:(h + 1) * LANE] = (
                    _rms(o, g_ref[...]) * (1.0 - lam_init)).astype(BF16)
        return carry

    lax.fori_loop(0, t_len // tq, body, 0)


def _diff(pa3, pf3, lam_p, norm_g, ctx_k, ctx_v, layer, lam_init):
    b, t, _ = pa3.shape
    ctx_len = 0 if ctx_k is None else ctx_k.shape[2]
    full = lambda shape: pl.BlockSpec(shape, lambda bi: (0,) * len(shape))
    in_specs = [
        pl.BlockSpec((None, t, DIFF_QW), lambda bi: (bi, 0, A_DIFF_Q // DIFF_QW)),
        pl.BlockSpec((None, t, DIFF_QW), lambda bi: (bi, 0, F_DIFF_K // DIFF_QW)),
        pl.BlockSpec((None, t, DIFF_VW), lambda bi: (bi, 0, F_DIFF_V // DIFF_VW)),
        full((4, DIFF_DH)),
        full((1, DIFF_VD)),
    ]
    args = [pa3, pf3, pf3, lam_p, norm_g]
    if ctx_len:
        cos, sin, first, _ = _rope_tables(t)
        in_specs += [
            pl.BlockSpec((None, None, ctx_len, DIFF_QW), lambda bi: (bi, layer, 0, 0)),
            pl.BlockSpec((None, None, ctx_len, DIFF_VW), lambda bi: (bi, layer, 0, 0)),
            full((t, LANE)),
            full((t, LANE)),
            full((1, LANE)),
        ]
        args += [ctx_k, ctx_v, cos, sin, first]
    tk = t + ctx_len
    return pl.pallas_call(
        functools.partial(_diff_kernel, t_len=t, ctx_len=ctx_len, lam_init=lam_init),
        grid=(b,),
        in_specs=in_specs,
        out_specs=pl.BlockSpec((None, t, DIFF_VW), lambda bi: (bi, 0, 0)),
        out_shape=jax.ShapeDtypeStruct((b, t, DIFF_VW), BF16),
        scratch_shapes=[pltpu.VMEM((t, DIFF_QW), BF16), pltpu.VMEM((tk, DIFF_QW), BF16),
                        pltpu.VMEM((tk, DIFF_VW), BF16)],
        compiler_params=_cparams(("arbitrary",)),
    )(*args)


def _route(logits_t, bias_col):
    scores = _sigmoid(logits_t)
    biased = scores + bias_col
    rows = [biased[e:e + 1, :] for e in range(N_EXPERTS)]
    grp = []
    for g in range(N_GROUPS):
        a0, a1, a2, a3 = rows[g * GROUP_SIZE:(g + 1) * GROUP_SIZE]
        hi01, lo01 = jnp.maximum(a0, a1), jnp.minimum(a0, a1)
        hi23, lo23 = jnp.maximum(a2, a3), jnp.minimum(a2, a3)
        top1 = jnp.maximum(hi01, hi23)
        top2 = jnp.maximum(jnp.minimum(hi01, hi23), jnp.maximum(lo01, lo23))
        grp.append(top1 + top2)
    best = []
    for g in range(N_GROUPS):
        win = None
        for o in range(N_GROUPS):
            if o == g:
                continue
            cond = grp[g] > grp[o] if o < g else grp[g] >= grp[o]
            win = cond if win is None else jnp.logical_and(win, cond)
        best.append(win)
    sel_rows = []
    for e in range(N_EXPERTS):
        g = e // GROUP_SIZE
        beaten = jnp.zeros_like(rows[e])
        for o in range(g * GROUP_SIZE, (g + 1) * GROUP_SIZE):
            if o == e:
                continue
            ahead = rows[o] >= rows[e] if o < e else rows[o] > rows[e]
            beaten = beaten + jnp.where(ahead, 1.0, 0.0)
        sel_rows.append(jnp.where(jnp.logical_and(best[g], beaten < 1.5), 1.0, 0.0))
    sel = jnp.concatenate(sel_rows, axis=0)
    picked = sel * scores
    return sel, picked / jnp.sum(picked, axis=0, keepdims=True)


PIECE = 16
ROW_TILE = 256
SLOT_PAD = N_EXPERTS * PIECE
HS_COLS = D_MODEL + LANE


def _route_meta(sel, wts):
    n_e, t = sel.shape
    e_col = lax.broadcasted_iota(jnp.int32, (n_e, 1), 0).astype(F32)
    before = jnp.where(_tri(t, "lt"), 1.0, 0.0).astype(BF16)
    rank = _dot(sel.astype(BF16), before)
    cnt = jnp.sum(sel, axis=1, keepdims=True)
    plen = jnp.floor((cnt + (PIECE - 1.0)) * (1.0 / PIECE)) * PIECE
    lower = jnp.where(_tri(n_e, "gt"), 1.0, 0.0).astype(BF16)
    loff = _dot(lower, jnp.broadcast_to(plen, (n_e, LANE)).astype(BF16))[:, 0:1]
    slot = loff + rank
    chosen = sel > 0.5
    e_a = jnp.min(jnp.where(chosen, e_col, 2.0 * n_e), axis=0, keepdims=True)
    e_b = jnp.max(jnp.where(chosen, e_col, -1.0), axis=0, keepdims=True)
    take = lambda e_row, x: jnp.sum(jnp.where(e_col == e_row, x, 0.0), axis=0, keepdims=True)
    rows = jnp.concatenate([take(e_a, slot), take(e_b, slot), take(e_a, wts), take(e_b, wts), e_a, e_b,
                            jnp.zeros((2, t), F32)], axis=0)
    return rows, plen, loff


def _merge_kernel(og_ref, ol_ref, od_ref, lg0_ref, lg1_ref, lg2_ref, x_ref, mod_ref, wb_ref, wo_ref,
                  lng_ref, lnb_ref, rw_ref, rb_ref, x1_ref, h_ref, mrow_ref, mcol_ref, plen_ref, loff_ref):
    gate1 = mod_ref[:, 2 * D_MODEL:3 * D_MODEL]
    shift2 = mod_ref[:, 3 * D_MODEL:4 * D_MODEL]
    scale2 = mod_ref[:, 4 * D_MODEL:5 * D_MODEL]
    tm = x_ref.shape[0]
    halves = [pl.ds(r * (tm // 2), tm // 2) for r in range(2)]
    accs = [None, None]
    for s, (o_ref, lg_ref) in enumerate(((og_ref, lg0_ref), (ol_ref, lg1_ref), (od_ref, lg2_ref))):
        projs = [_dot(o_ref[rows, :], wb_ref[s]) for rows in halves]
        terms = [_sigmoid(lg_ref[rows, :].astype(F32)) * p for rows, p in zip(halves, projs)]
        accs = [t if a is None else a + t for a, t in zip(accs, terms)]
    ms = [_mm(a, wo_ref[...]) for a in accs]
    logit_parts = []
    for rows, m in zip(halves, ms):
        x1 = _layernorm(ALPHA * x_ref[rows, :] + gate1 * m, lng_ref[...], lnb_ref[...])
        x1_ref[rows, :] = x1
        h = x1 * (1.0 + scale2) + shift2
        h_ref[rows, :] = h.astype(BF16)
        logit_parts.append(_mm_hi_nt(rw_ref[...], h))
    sel, gates_t = _route(jnp.concatenate(logit_parts, axis=1), rb_ref[...])
    rows, plen, loff = _route_meta(sel, gates_t)
    mrow_ref[...] = rows
    mcol_ref[...] = jnp.concatenate([rows, jnp.zeros((LANE - rows.shape[0], tm), F32)], axis=0).T
    lane = lax.broadcasted_iota(jnp.int32, (N_EXPERTS, LANE), 1)
    cols = jnp.where(lane == 0, plen, jnp.where(lane == 1, loff, 0.0))
    cols_t = jnp.concatenate([cols, jnp.zeros((LANE - N_EXPERTS, LANE), F32)], axis=0).T
    plen_ref[...] = cols_t[0:1, :]
    loff_ref[...] = cols_t[1:2, :]


def _merge(o_gdn, o_gla, o_diff, pa, x, mod, wb, wo, ln_g, ln_b, rw_t, rb_col, layer, row0,
           tokens_per_row, tm):
    n = x.shape[0]
    mod_row = lambda i: (layer, row0 + (i * tm) // tokens_per_row, 0, 0)
    full = lambda shape: pl.BlockSpec(shape, lambda i: (0,) * len(shape))
    return pl.pallas_call(
        _merge_kernel,
        grid=(n // tm,),
        in_specs=[
            pl.BlockSpec((tm, MIX_W), lambda i: (i, 0)),
            pl.BlockSpec((tm, MIX_W), lambda i: (i, 0)),
            pl.BlockSpec((tm, MIX_W), lambda i: (i, 0)),
            pl.BlockSpec((tm, D_MODEL), lambda i: (i, A_MERGE // D_MODEL)),
            pl.BlockSpec((tm, D_MODEL), lambda i: (i, A_MERGE // D_MODEL + 1)),
            pl.BlockSpec((tm, D_MODEL), lambda i: (i, A_MERGE // D_MODEL + 2)),
            pl.BlockSpec((tm, D_MODEL), lambda i: (i, 0)),
            pl.BlockSpec((None, None, 1, 6 * D_MODEL), mod_row),
            full((N_BRANCH, MIX_W, D_MODEL)),
            full((D_MODEL, D_MODEL)),
            full((1, D_MODEL)),
            full((1, D_MODEL)),
            full((N_EXPERTS, D_MODEL)),
            full((N_EXPERTS, 1)),
        ],
        out_specs=[
            pl.BlockSpec((tm, D_MODEL), lambda i: (i, 0)),
            pl.BlockSpec((tm, D_MODEL), lambda i: (i, 0)),
            pl.BlockSpec((8, tm), lambda i: (0, i)),
            pl.BlockSpec((tm, LANE), lambda i: (i, 0)),
            pl.BlockSpec((None, 1, LANE), lambda i: (i, 0, 0)),
            pl.BlockSpec((None, 1, LANE), lambda i: (i, 0, 0)),
        ],
        out_shape=[
            jax.ShapeDtypeStruct((n, D_MODEL), F32),
            jax.ShapeDtypeStruct((n, D_MODEL), BF16),
            jax.ShapeDtypeStruct((8, n), F32),
            jax.ShapeDtypeStruct((n, LANE), F32),
            jax.ShapeDtypeStruct((n // tm, 1, LANE), F32),
            jax.ShapeDtypeStruct((n // tm, 1, LANE), F32),
        ],
        compiler_params=_cparams(("arbitrary",)),
    )(o_gdn, o_gla, o_diff, pa, pa, pa, x, mod, wb, wo, ln_g, ln_b, rw_t, rb_col)


def _plan_kernel(plen_ref, loff_ref, goff_ref, npc_ref, loffi_ref, tiles_ref, tail_ref, *, n_steps):
    plen = plen_ref[...]
    nb = plen.shape[0]
    earlier = jnp.where(_tri(nb, "gt"), 1.0, 0.0).astype(BF16)
    run = _dot(earlier, plen.astype(BF16))
    gtot = jnp.sum(plen, axis=0, keepdims=True)
    lane = lax.broadcasted_iota(jnp.int32, (1, LANE), 1).astype(F32)
    ntile = jnp.floor((gtot + (ROW_TILE - 1.0)) * (1.0 / ROW_TILE))
    before = jnp.where(_tri(LANE, "lt"), 1.0, 0.0).astype(BF16)
    first = _dot(jnp.broadcast_to(ntile, (8, LANE)).astype(BF16), before)[0:1, :]
    end = first + ntile
    n_used = jnp.sum(ntile, axis=1, keepdims=True)
    goff_ref[...] = (first * ROW_TILE + run).astype(jnp.int32)
    npc_ref[...] = (plen * (1.0 / PIECE)).astype(jnp.int32)
    loffi_ref[...] = loff_ref[...].astype(jnp.int32)
    end_col = jnp.concatenate([end, jnp.zeros((LANE - 1, LANE), F32)], axis=0).T[:, 0:1]
    e_col = lax.broadcasted_iota(jnp.int32, (LANE, 1), 0).astype(F32)
    tile = jnp.minimum(lane, n_used - 1.0)
    done = jnp.logical_and(end_col <= tile, e_col < N_EXPERTS)
    t_exp = jnp.minimum(jnp.sum(jnp.where(done, 1.0, 0.0), axis=0, keepdims=True), N_EXPERTS - 1.0)
    nonempty = jnp.where(ntile > 0.5, 1.0, 0.0)
    run_idx = _dot(jnp.broadcast_to(nonempty, (8, LANE)).astype(BF16), before)[0:1, :]
    owner_after = jnp.logical_and(end_col <= end, e_col < N_EXPERTS)
    nxt = jnp.minimum(jnp.sum(jnp.where(owner_after, 1.0, 0.0), axis=0, keepdims=True), N_EXPERTS - 1.0)
    has_next = jnp.where(end < n_used, 1.0, 0.0)
    by_expert = jnp.concatenate([run_idx - 2.0 * jnp.floor(run_idx * 0.5), nxt, has_next,
                                 jnp.zeros((LANE - 3, LANE), F32)], axis=0).T
    mine = e_col == t_exp
    per_tile = [jnp.sum(jnp.where(mine, by_expert[:, j:j + 1], 0.0), axis=0, keepdims=True) for j in range(3)]
    tiles_ref[...] = jnp.concatenate(
        [t_exp, jnp.broadcast_to(n_used, (1, LANE))] + per_tile + [jnp.zeros((3, LANE), F32)],
        axis=0).astype(jnp.int32)
    tail_ref[...] = jnp.concatenate(
        [first * ROW_TILE + gtot, (ntile * ROW_TILE - gtot) * (1.0 / PIECE),
         jnp.broadcast_to(n_used, (1, LANE)), jnp.broadcast_to(n_steps - n_used, (1, LANE)),
         jnp.zeros((4, LANE), F32)], axis=0).astype(jnp.int32)


def _plan(plen, loff, n_steps):
    nb = plen.shape[0]
    i32 = lambda rows: jax.ShapeDtypeStruct((rows, LANE), jnp.int32)
    return pl.pallas_call(
        functools.partial(_plan_kernel, n_steps=n_steps),
        out_shape=[i32(nb), i32(nb), i32(nb), i32(8), i32(8)],
    )(plen, loff)


def _piece_copy(local, remote, sem, to_remote, lo, go):
    src = local.at[pl.ds(pl.multiple_of(lo, PIECE), PIECE)]
    dst = remote.at[pl.ds(pl.multiple_of(go, PIECE), PIECE)]
    return pltpu.make_async_copy(src, dst, sem) if to_remote else pltpu.make_async_copy(dst, src, sem)


def _start_pieces(b, loff_s, npc_s, goff_s, local, remote, sem, to_remote):
    for e in range(N_EXPERTS):
        lo = loff_s[b, e]
        go = goff_s[b, e]

        def start(j, carry, lo=lo, go=go):
            _piece_copy(local, remote, sem, to_remote, lo + j * PIECE, go + j * PIECE).start()
            return carry

        lax.fori_loop(0, npc_s[b, e], start, 0)


def _wait_pieces(b, npc_s, local, remote, sem, to_remote):
    total = 0
    for e in range(N_EXPERTS):
        total = total + npc_s[b, e]

    def wait(j, carry):
        _piece_copy(local, remote, sem, to_remote, 0, 0).wait()
        return carry

    lax.fori_loop(0, total, wait, 0)


def _dispatch_kernel(loff_s, npc_s, goff_s, tail_s, h_ref, mrow_ref, mcol_ref, hs_hbm, local, zeros, sems,
                     tail_sem, tile_sem, *, slots):
    b = pl.program_id(0)
    last = pl.num_programs(0) - 1
    slot = b % 2
    buf = local.at[slot]
    sem = sems.at[slot]

    @pl.when(b >= 2)
    def _():
        _wait_pieces(b - 2, npc_s, buf, hs_hbm, sem, True)

    tb = h_ref.shape[0]
    mrow = mrow_ref[...]
    mcol = mcol_ref[...]
    slot_id = lax.broadcasted_iota(jnp.int32, (slots, tb), 0).astype(F32)
    s_a = jnp.where(slot_id == mrow[0:1, :], 1.0, 0.0).astype(BF16)
    s_b = jnp.where(slot_id == mrow[1:2, :], 1.0, 0.0).astype(BF16)
    lane = lax.broadcasted_iota(jnp.int32, (tb, LANE), 1)

    def weight_lanes(w):
        hi = w.astype(BF16).astype(F32)
        return jnp.where(lane == 0, hi, jnp.where(lane == 1, w - hi, 0.0)).astype(BF16)

    local[slot, :, :D_MODEL] = _dot(s_a + s_b, h_ref[...]).astype(BF16)
    local[slot, :, D_MODEL:] = (_dot(s_a, weight_lanes(mcol[:, 2:3]))
                                + _dot(s_b, weight_lanes(mcol[:, 3:4]))).astype(BF16)
    _start_pieces(b, loff_s, npc_s, goff_s, buf, hs_hbm, sem, True)

    @pl.when(b == last)
    def _():
        _wait_pieces(b, npc_s, buf, hs_hbm, sem, True)

        @pl.when(b >= 1)
        def _():
            _wait_pieces(b - 1, npc_s, local.at[1 - slot], hs_hbm, sems.at[1 - slot], True)

        zeros[...] = jnp.zeros_like(zeros)
        zero_piece = zeros.at[pl.ds(0, PIECE)]
        total = 0
        for e in range(N_EXPERTS):
            n = tail_s[1, e]
            start_row = tail_s[0, e]

            def start(j, carry, start_row=start_row):
                dst = hs_hbm.at[pl.ds(pl.multiple_of(start_row + j * PIECE, PIECE), PIECE)]
                pltpu.make_async_copy(zero_piece, dst, tail_sem).start()
                return carry

            lax.fori_loop(0, n, start, 0)
            total = total + n

        def wait(j, carry):
            pltpu.make_async_copy(zero_piece, hs_hbm.at[pl.ds(0, PIECE)], tail_sem).wait()
            return carry

        lax.fori_loop(0, total, wait, 0)

        def tile_copy(j):
            row = pl.multiple_of((tail_s[2, 0] + j) * ROW_TILE, ROW_TILE)
            return pltpu.make_async_copy(zeros, hs_hbm.at[pl.ds(row, ROW_TILE)], tile_sem)

        def start_tile(j, carry):
            tile_copy(j).start()
            return carry

        def wait_tile(j, carry):
            tile_copy(j).wait()
            return carry

        lax.fori_loop(0, tail_s[3, 0], start_tile, 0)
        lax.fori_loop(0, tail_s[3, 0], wait_tile, 0)


def _dispatch(loff_i, npc, goff, tail, h, mrow, mcol, tb, n_steps):
    n = h.shape[0]
    slots = 2 * tb + SLOT_PAD
    grid_spec = pltpu.PrefetchScalarGridSpec(
        num_scalar_prefetch=4,
        grid=(n // tb,),
        in_specs=[
            pl.BlockSpec((tb, D_MODEL), lambda i, *_: (i, 0)),
            pl.BlockSpec((8, tb), lambda i, *_: (0, i)),
            pl.BlockSpec((tb, LANE), lambda i, *_: (i, 0)),
        ],
        out_specs=pl.BlockSpec(memory_space=pl.ANY),
        scratch_shapes=[pltpu.VMEM((2, slots, HS_COLS), BF16), pltpu.VMEM((ROW_TILE, HS_COLS), BF16),
                        pltpu.SemaphoreType.DMA((2,)), pltpu.SemaphoreType.DMA, pltpu.SemaphoreType.DMA],
    )
    return pl.pallas_call(
        functools.partial(_dispatch_kernel, slots=slots),
        grid_spec=grid_spec,
        out_shape=jax.ShapeDtypeStruct((n_steps * ROW_TILE, HS_COLS), BF16),
        compiler_params=_cparams(("arbitrary",)),
    )(loff_i, npc, goff, tail, h, mrow, mcol)


def _expert_kernel(tiles_s, hs_ref, wg_hbm, wu_hbm, wd_hbm, y_ref, wg_f, wu_f, wd_f, wg_b, wu_b, wd_b, sems,
                   *, layer):
    i = pl.program_id(0)
    e = tiles_s[0, i]
    slot = tiles_s[2, i]
    first_of_run = jnp.logical_or(i == 0, e != tiles_s[0, jnp.maximum(i - 1, 0)])

    def weight_copies(expert, s):
        return [pltpu.make_async_copy(src.at[layer, expert], dst.at[s], sems.at[s, j])
                for j, (src, dst) in enumerate(((wg_hbm, wg_f), (wu_hbm, wu_f), (wd_hbm, wd_f)))]

    @pl.when(i == 0)
    def _():
        for cp in weight_copies(e, slot):
            cp.start()

    @pl.when(jnp.logical_and(first_of_run, i < tiles_s[1, 0]))
    def _():
        for cp in weight_copies(e, slot):
            cp.wait()

        @pl.when(tiles_s[4, i] > 0)
        def _():
            for cp in weight_copies(tiles_s[3, i], 1 - slot):
                cp.start()

        wg_b[...] = wg_f[slot].astype(BF16)
        wu_b[...] = wu_f[slot].astype(BF16)
        wd_b[...] = wd_f[slot].astype(BF16)

    @pl.when(i < tiles_s[1, 0])
    def _():
        halves = [pl.ds(r * (ROW_TILE // 2), ROW_TILE // 2) for r in range(2)]
        xs = [hs_ref[rows, :D_MODEL] for rows in halves]
        ws = [hs_ref[rows, D_MODEL:D_MODEL + 1].astype(F32) + hs_ref[rows, D_MODEL + 1:D_MODEL + 2].astype(F32)
              for rows in halves]
        gs = [_dot(x, wg_b[...]) for x in xs]
        us = [_dot(x, wu_b[...]) for x in xs]
        acts = [(_silu(g) * u * w).astype(BF16) for g, u, w in zip(gs, us, ws)]
        for rows, act in zip(halves, acts):
            y_ref[rows, :] = _dot(act, wd_b[...]).astype(BF16)

    @pl.when(i >= tiles_s[1, 0])
    def _():
        y_ref[...] = jnp.zeros_like(y_ref)


def _experts(tiles, hs, wg, wu, wd, layer, n_steps):
    grid_spec = pltpu.PrefetchScalarGridSpec(
        num_scalar_prefetch=1,
        grid=(n_steps,),
        in_specs=[
            pl.BlockSpec((ROW_TILE, HS_COLS), lambda i, t: (i, 0)),
            pl.BlockSpec(memory_space=pl.ANY),
            pl.BlockSpec(memory_space=pl.ANY),
            pl.BlockSpec(memory_space=pl.ANY),
        ],
        out_specs=pl.BlockSpec((ROW_TILE, D_MODEL), lambda i, t: (i, 0)),
        scratch_shapes=[pltpu.VMEM((2, D_MODEL, D_FF), F32), pltpu.VMEM((2, D_MODEL, D_FF), F32),
                        pltpu.VMEM((2, D_FF, D_MODEL), F32),
                        pltpu.VMEM((D_MODEL, D_FF), BF16), pltpu.VMEM((D_MODEL, D_FF), BF16),
                        pltpu.VMEM((D_FF, D_MODEL), BF16), pltpu.SemaphoreType.DMA((2, 3))],
    )
    return pl.pallas_call(
        functools.partial(_expert_kernel, layer=layer),
        grid_spec=grid_spec,
        out_shape=jax.ShapeDtypeStruct((hs.shape[0], D_MODEL), BF16),
        compiler_params=_cparams(("arbitrary",)),
    )(tiles, hs, wg, wu, wd)


def _combine_kernel(loff_s, npc_s, goff_s, y_hbm, mcol_ref, x1_ref, mod_ref, lng_ref, lnb_ref, o_ref,
                    local, sems, *, slots):
    b = pl.program_id(0)
    tb = x1_ref.shape[0]
    slot = b % 2

    def fetch(blk, s):
        local[s] = jnp.zeros(local.shape[1:], local.dtype)
        _start_pieces(blk, loff_s, npc_s, goff_s, local.at[s], y_hbm, sems.at[s], False)

    @pl.when(b == 0)
    def _():
        fetch(0, 0)

    _wait_pieces(b, npc_s, local.at[slot], y_hbm, sems.at[slot], False)

    @pl.when(b + 1 < pl.num_programs(0))
    def _():
        fetch(b + 1, 1 - slot)

    mcol = mcol_ref[...]
    slot_id = lax.broadcasted_iota(jnp.int32, (tb, slots), 1).astype(F32)
    pick = jnp.logical_or(slot_id == mcol[:, 0:1], slot_id == mcol[:, 1:2])
    f = _dot(jnp.where(pick, 1.0, 0.0).astype(BF16), local[slot])
    gate2 = mod_ref[:, 5 * D_MODEL:6 * D_MODEL]
    o_ref[...] = _layernorm(ALPHA * x1_ref[...] + gate2 * f, lng_ref[...], lnb_ref[...])


def _combine(loff_i, npc, goff, y, mcol, x1, mod, ln_g, ln_b, layer, row0, tokens_per_row, tb):
    n = x1.shape[0]
    slots = 2 * tb + SLOT_PAD
    mod_row = lambda i, *_: (layer, row0 + (i * tb) // tokens_per_row, 0, 0)
    grid_spec = pltpu.PrefetchScalarGridSpec(
        num_scalar_prefetch=3,
        grid=(n // tb,),
        in_specs=[
            pl.BlockSpec(memory_space=pl.ANY),
            pl.BlockSpec((tb, LANE), lambda i, *_: (i, 0)),
            pl.BlockSpec((tb, D_MODEL), lambda i, *_: (i, 0)),
            pl.BlockSpec((None, None, 1, 6 * D_MODEL), mod_row),
            pl.BlockSpec((1, D_MODEL), lambda i, *_: (0, 0)),
            pl.BlockSpec((1, D_MODEL), lambda i, *_: (0, 0)),
        ],
        out_specs=pl.BlockSpec((tb, D_MODEL), lambda i, *_: (i, 0)),
        scratch_shapes=[pltpu.VMEM((2, slots, D_MODEL), BF16), pltpu.SemaphoreType.DMA((2,))],
    )
    return pl.pallas_call(
        functools.partial(_combine_kernel, slots=slots),
        grid_spec=grid_spec,
        out_shape=jax.ShapeDtypeStruct((n, D_MODEL), F32),
        compiler_params=_cparams(("arbitrary",)),
    )(loff_i, npc, goff, y, mcol, x1, mod, ln_g, ln_b)


def _moe(h, mrow, mcol, plen, loff, wg, wu, wd, x1, mod, ln_g, ln_b, layer, row0, tokens_per_row, tb):
    n = h.shape[0]
    nb = n // tb
    n_steps = (2 * n + nb * N_EXPERTS * (PIECE - 1)) // ROW_TILE + N_EXPERTS
    assert n_steps <= LANE
    goff, npc, loff_i, tiles, tail = _plan(plen.reshape(nb, LANE), loff.reshape(nb, LANE), n_steps)
    hs = _dispatch(loff_i, npc, goff, tail, h, mrow, mcol, tb, n_steps)
    y = _experts(tiles, hs, wg, wu, wd, layer, n_steps)
    return _combine(loff_i, npc, goff, y, mcol, x1, mod, ln_g, ln_b, layer, row0, tokens_per_row, tb)


IN_SIZES = (3 * GDN_W, GDN_W, 2 * GDN_HEADS, 2 * GDN_HEADS, GLA_KW, GLA_KW, GLA_VW, GLA_VW,
            2 * GLA_RANK, DIFF_QW, DIFF_QW, DIFF_VW, N_BRANCH * D_MODEL)
IN_OFFS = tuple(int(v) for v in np.concatenate([[0], np.cumsum(IN_SIZES)]))
D_IN = IN_OFFS[-1]


def _reorder_kernel(wt_ref, wa_ref, wf_ref):
    o = IN_OFFS
    for lo, hi, dst in ((o[0], o[2], A_GDN_QKV), (o[4], o[8], A_GLA_Q), (o[9], o[10], A_DIFF_Q),
                        (o[12], o[13], A_MERGE)):
        wa_ref[:, dst:dst + hi - lo] = wt_ref[lo:hi, :].T.astype(BF16)
    wf_ref[:, F_DIFF_K:F_DIFF_K + o[12] - o[10]] = wt_ref[o[10]:o[12], :].T.astype(BF16)
    gates = wt_ref[o[2]:o[4], :]
    low_rank = wt_ref[o[8]:o[9], :]
    assert MISC_B == 0 and MISC_A == o[3] - o[2] and MISC_LR == o[4] - o[2]
    pad = jnp.zeros((LANE - gates.shape[0] - low_rank.shape[0], gates.shape[1]), F32)
    wf_ref[:, F_MISC:F_MISC + LANE] = jnp.concatenate([gates, low_rank, pad], axis=0).T.astype(BF16)


def _reorder_w_in(w_in):
    tr = 256
    return pl.pallas_call(
        _reorder_kernel,
        grid=(DEPTH, D_MODEL // tr),
        in_specs=[pl.BlockSpec((None, D_IN, tr), lambda l, i: (l, 0, i))],
        out_specs=[pl.BlockSpec((None, tr, A_COLS), lambda l, i: (l, i, 0)),
                   pl.BlockSpec((None, tr, F_COLS), lambda l, i: (l, i, 0))],
        out_shape=[jax.ShapeDtypeStruct((DEPTH, D_MODEL, A_COLS), BF16),
                   jax.ShapeDtypeStruct((DEPTH, D_MODEL, F_COLS), BF16)],
        compiler_params=_cparams(("arbitrary", "arbitrary")),
    )(jnp.swapaxes(w_in, 1, 2))


def _lane_row(vals, offset):
    return jnp.zeros((1, LANE), F32).at[0, offset:offset + vals.shape[0]].set(vals)


def kernel(x_prompt, x_sample, c, state_gdn, state_gla, cache_k, cache_v, c_ctx, w_mod, b_mod, w_in,
           gdn_conv, gdn_a_log, gdn_dt_bias, gdn_norm, gla_w_gate, gla_b_gate, gla_norm, diff_lambda,
           diff_norm, w_branch, w_out, ln_g, ln_b, router_w, router_b, exp_w_gate, exp_w_up, exp_w_down):
    bp, tp, d = x_prompt.shape
    bs, ts, _ = x_sample.shape
    pad_rows = MOD_ROWS - 1 - bs
    cvecs = jnp.concatenate([c_ctx[None, :], c, jnp.zeros((pad_rows, d), F32)], axis=0)
    mod = _modulation(cvecs, w_mod, b_mod).reshape(DEPTH, MOD_ROWS, 1, 6 * d)

    rw_t = router_w.T
    rb_col = router_b.reshape(N_EXPERTS, 1)
    ck = cache_k.reshape(bs, DEPTH, cache_k.shape[2], DIFF_QW)
    cv = cache_v.reshape(bs, DEPTH, cache_v.shape[2], DIFF_VW)

    wa, wf = _reorder_w_in(w_in)
    layer_w = []
    for l in range(DEPTH):
        layer_w.append(dict(
            wa=wa, wf=wf,
            alog=_lane_row(gdn_a_log[l].reshape(-1), MISC_A),
            dtb=_lane_row(gdn_dt_bias[l].reshape(-1), MISC_A),
            wb=w_branch[l].astype(BF16),
            wo=w_out[l].astype(BF16),
        ))

    def layer(x, l, row0, tokens_per_row, s_gdn, s_gla, ctx_k, ctx_v, emit_state, tm, prev_gdn=None,
              prev_gla=None, prev_cache=None):
        b, t, _ = x.shape
        n = b * t
        lw = layer_w[l]
        xf = x.reshape(n, d)
        if emit_state:
            pa, pf, *cache = _inproj(xf, mod, lw["wa"], lw["wf"], l, row0, tokens_per_row, t, (b, t), prev_cache)
        else:
            pa, pf = _inproj(xf, mod, lw["wa"], lw["wf"], l, row0, tokens_per_row, tm, None, None)
            cache = None
        pa3 = pa.reshape(b, t, A_COLS)
        pf3 = pf.reshape(b, t, F_COLS)
        o_gdn, gdn_fin = _gdn(pa3, pf3, gdn_conv[l], lw["alog"], lw["dtb"], gdn_norm[l][None, :], s_gdn, l,
                              emit_state, prev_gdn)
        o_gla, gla_fin = _gla(pa3, pf3, gla_w_gate[l], gla_b_gate[l], gla_norm[l][None, :], s_gla, l,
                              emit_state, prev_gla)
        lam_init = 0.8 - 0.6 * math.exp(-0.3 * l)
        o_diff = _diff(pa3, pf3, diff_lambda[l], diff_norm[l][None, :], ctx_k, ctx_v, l, lam_init)
        x1, h, mrow, mcol, plen, loff = _merge(
            o_gdn.reshape(n, MIX_W), o_gla.reshape(n, MIX_W), o_diff.reshape(n, MIX_W), pa, xf, mod,
            lw["wb"], lw["wo"], ln_g[l, 0][None, :], ln_b[l, 0][None, :], rw_t, rb_col, l, row0,
            tokens_per_row, tm)
        x2 = _moe(h, mrow, mcol, plen, loff, exp_w_gate, exp_w_up, exp_w_down, x1, mod,
                  ln_g[l, 1][None, :], ln_b[l, 1][None, :], l, row0, tokens_per_row, tm)
        return x2.reshape(b, t, d), gdn_fin, gla_fin, cache

    hp = x_prompt
    gdn_states, gla_states, cache = None, None, None
    for l in range(DEPTH):
        hp, gdn_states, gla_states, cache = layer(hp, l, 0, bp * tp, None, None, None, None, True, TOKEN_BLOCK,
                                                  gdn_states, gla_states, cache)

    hs = x_sample
    for l in range(DEPTH):
        hs = layer(hs, l, 1, ts, state_gdn, state_gla, ck, cv, False, TOKEN_BLOCK)[0]

    return hp, hs, gdn_states, gla_states, cache[0], cache[1]
```
